```python
import jax, jax.numpy as jnp
from jax import lax
import numpy as np

D_MODEL = 2048
BATCH = 4
SEQ = 2048
DEPTH = 1

HEAD_DIM = 64
ATTN_HEADS = D_MODEL // 128
KV_HEADS = ATTN_HEADS // 4
Q_DIM = ATTN_HEADS * HEAD_DIM
KV_DIM = KV_HEADS * HEAD_DIM
WINDOW = 128
ATTN_BLOCK = 128
ROPE_THETA = 10000.0
D_INNER = D_MODEL
SSM_HEAD_DIM = 64
SSM_HEADS = D_INNER // SSM_HEAD_DIM
SSM_GROUPS = 4
D_STATE = 128
CONV_WIDTH = 4
CHUNK = 128
CONV_DIM = D_INNER + 2 * SSM_GROUPS * D_STATE
FFN_HIDDEN = -(-(8 * D_MODEL) // (3 * 256)) * 256
PLE_DIM = 256
IN_DIM = Q_DIM + 2 * KV_DIM + D_INNER + CONV_DIM + SSM_HEADS + 2 * D_MODEL
NORM_EPS = 1e-6
SSM_NORM_EPS = 1e-5

kernel_name = "hybrid_swa_sink_ssd_gated_block"


def rmsnorm(x, g, eps=NORM_EPS):
    xf = x.astype(jnp.float32)
    y = xf * lax.rsqrt(jnp.mean(xf * xf, axis=-1, keepdims=True) + eps)
    return (y * g.astype(jnp.float32)).astype(x.dtype)


def apply_rope(t, positions):
    half = HEAD_DIM // 2
    inv_freq = ROPE_THETA ** (-jnp.arange(half, dtype=jnp.float32) * 2.0 / HEAD_DIM)
    ang = positions.astype(jnp.float32)[..., None] * inv_freq
    cos, sin = jnp.cos(ang)[:, :, None, :], jnp.sin(ang)[:, :, None, :]
    t1, t2 = t[..., :half], t[..., half:]
    return jnp.concatenate([t1 * cos - t2 * sin, t2 * cos + t1 * sin], axis=-1)


def sliding_window_sink_attention(q, k, v, sinks):
    b, s = q.shape[0], q.shape[1]
    nb = s // ATTN_BLOCK
    grp = ATTN_HEADS // KV_HEADS
    qb = q.reshape(b, nb, ATTN_BLOCK, KV_HEADS, grp, HEAD_DIM)

    def banded(t):
        tb = t.reshape(b, nb, ATTN_BLOCK, KV_HEADS, HEAD_DIM)
        prev = jnp.pad(tb, ((0, 0), (1, 0), (0, 0), (0, 0), (0, 0)))[:, :-1]
        return jnp.concatenate([prev, tb], axis=2)

    kw, vw = banded(k), banded(v)
    scores = jnp.einsum('bnqhgd,bnkhd->bnhgqk', qb, kw) * (HEAD_DIM ** -0.5)
    qi = jnp.arange(ATTN_BLOCK)[:, None] + ATTN_BLOCK
    kj = jnp.arange(2 * ATTN_BLOCK)[None, :]
    dist = qi - kj
    key_pos = (jnp.arange(nb) * ATTN_BLOCK)[:, None, None] - ATTN_BLOCK + kj[None]
    valid = (dist >= 0)[None] & (dist < WINDOW)[None] & (key_pos >= 0)
    scores = jnp.where(valid[None, :, None, None], scores, -jnp.inf)
    sink = sinks.astype(jnp.float32).reshape(KV_HEADS, grp)[None, None, :, :, None]
    m = jnp.maximum(scores.max(axis=-1), sink)
    e = jnp.exp(scores - m[..., None])
    probs = e / (e.sum(axis=-1) + jnp.exp(sink - m))[..., None]
    out = jnp.einsum('bnhgqk,bnkhd->bnqhgd', probs, vw)
    return out.reshape(b, s, Q_DIM)


def causal_depthwise_conv(x, w, bias):
    out = lax.conv_general_dilated(
        x, w[:, None, :], window_strides=(1,), padding=[(CONV_WIDTH - 1, 0)],
        dimension_numbers=('NWC', 'WIO', 'NWC'), feature_group_count=x.shape[-1])
    return out + bias


def ssd_chunked(xh, dt, a_neg, bm, cm):
    b, s = xh.shape[0], xh.shape[1]
    nc = s // CHUNK
    e_per = SSM_HEADS // SSM_GROUPS
    xd = (xh * dt[..., None]).reshape(b, nc, CHUNK, SSM_GROUPS, e_per, SSM_HEAD_DIM)
    a = jnp.transpose((dt * a_neg).reshape(b, nc, CHUNK, SSM_GROUPS, e_per), (0, 1, 3, 4, 2))
    a_cs = jnp.cumsum(a, axis=-1)
    bc = bm.reshape(b, nc, CHUNK, SSM_GROUPS, D_STATE)
    cc = cm.reshape(b, nc, CHUNK, SSM_GROUPS, D_STATE)
    tril = jnp.tril(jnp.ones((CHUNK, CHUNK), dtype=bool))
    diff = a_cs[..., :, None] - a_cs[..., None, :]
    decay = jnp.where(tril, jnp.exp(jnp.where(tril, diff, 0.0)), 0.0)
    cb = jnp.einsum('bclgn,bcsgn->bcgls', cc, bc)
    y_diag = jnp.einsum('bcgels,bcsgep->bclgep', cb[:, :, :, None] * decay, xd)
    decay_states = jnp.exp(a_cs[..., -1:] - a_cs)
    states = jnp.einsum('bclgn,bcgel,bclgep->bcgepn', bc, decay_states, xd)
    chunk_decay = jnp.exp(a_cs[..., -1])

    def step(carry, inp):
        st, dec = inp
        return carry * dec[..., None, None] + st, carry

    init = jnp.zeros((b, SSM_GROUPS, e_per, SSM_HEAD_DIM, D_STATE), jnp.float32)
    _, prev = lax.scan(step, init, (jnp.moveaxis(states, 1, 0), jnp.moveaxis(chunk_decay, 1, 0)))
    prev = jnp.moveaxis(prev, 0, 1)
    y_off = jnp.einsum('bclgn,bcgepn,bcgel->bclgep', cc, prev, jnp.exp(a_cs))
    return (y_diag + y_off).reshape(b, s, SSM_HEADS, SSM_HEAD_DIM)


def _dense(key, shape, fan_in):
    return jax.random.normal(key, shape, jnp.float32) * (fan_in ** -0.5)


def setup_inputs(seed: int = 0) -> dict:
    key = jax.random.key(seed)
    ks = jax.random.split(key, 24)
    L = DEPTH
    ones_noise = lambda k, shape: 1.0 + 0.05 * jax.random.normal(k, shape, jnp.float32)
    dt0 = jnp.exp(jax.random.uniform(ks[5], (L, SSM_HEADS), jnp.float32, np.log(1e-3), np.log(1e-1)))
    return {
        "x": jax.random.normal(ks[0], (BATCH, SEQ, D_MODEL), jnp.float32),
        "p": jax.random.normal(ks[1], (DEPTH, BATCH, SEQ, PLE_DIM), jnp.float32),
        "positions": jnp.tile(jnp.arange(SEQ, dtype=jnp.int32)[None], (BATCH, 1)),
        "g_mix": ones_noise(ks[2], (L, D_MODEL)),
        "w_in": _dense(ks[3], (L, D_MODEL, IN_DIM), D_MODEL),
        "conv_w": _dense(ks[4], (L, CONV_WIDTH, CONV_DIM), CONV_WIDTH),
        "conv_b": 0.02 * jax.random.normal(ks[6], (L, CONV_DIM), jnp.float32),
        "dt_bias": dt0 + jnp.log(-jnp.expm1(-dt0)),
        "a_log": jnp.log(jax.random.uniform(ks[7], (L, SSM_HEADS), jnp.float32, 1.0, 16.0)),
        "d_skip": ones_noise(ks[8], (L, SSM_HEADS)),
        "g_ssd": ones_noise(ks[9], (L, D_INNER)),
        "sinks": 0.5 * jax.random.normal(ks[10], (L, ATTN_HEADS), jnp.float32),
        "w_attn_br": _dense(ks[11], (L, Q_DIM, D_MODEL), Q_DIM),
        "w_ssd_br": _dense(ks[12], (L, D_INNER, D_MODEL), D_INNER),
        "w_o": _dense(ks[13], (L, D_MODEL, D_MODEL), D_MODEL),
        "g_ffn": ones_noise(ks[14], (L, D_MODEL)),
        "w_gate": _dense(ks[15], (L, D_MODEL, FFN_HIDDEN), D_MODEL),
        "w_up": _dense(ks[16], (L, D_MODEL, FFN_HIDDEN), D_MODEL),
        "w_down": _dense(ks[17], (L, FFN_HIDDEN, D_MODEL), FFN_HIDDEN),
        "g_ple": ones_noise(ks[18], (L, D_MODEL)),
        "w_ple_gate": _dense(ks[19], (L, D_MODEL, D_MODEL), D_MODEL),
        "w_ple_proj": _dense(ks[20], (L, PLE_DIM, D_MODEL), PLE_DIM),
        "g_final": ones_noise(ks[21], (D_MODEL,)),
    }


def reference(x, p, positions, g_mix, w_in, conv_w, conv_b, dt_bias, a_log, d_skip, g_ssd,
              sinks, w_attn_br, w_ssd_br, w_o, g_ffn, w_gate, w_up, w_down, g_ple,
              w_ple_gate, w_ple_proj, g_final):
    b, s = x.shape[0], x.shape[1]
    f32 = jnp.float32
    sizes = [Q_DIM, KV_DIM, KV_DIM, D_INNER, CONV_DIM, SSM_HEADS, D_MODEL, D_MODEL]
    offsets = [int(o) for o in np.cumsum(sizes)[:-1]]
    h = x
    for i in range(DEPTH):
        u = rmsnorm(h, g_mix[i])
        proj = u @ w_in[i]
        q, k, v, z, xbc, dt_raw, g_a, g_s = jnp.split(proj, offsets, axis=-1)

        q = apply_rope(q.astype(f32).reshape(b, s, ATTN_HEADS, HEAD_DIM), positions)
        k = apply_rope(k.astype(f32).reshape(b, s, KV_HEADS, HEAD_DIM), positions)
        v = v.astype(f32).reshape(b, s, KV_HEADS, HEAD_DIM)
        attn = sliding_window_sink_attention(q, k, v, sinks[i]).astype(x.dtype)
        out_a = attn @ w_attn_br[i]

        xbc = jax.nn.silu(causal_depthwise_conv(xbc, conv_w[i], conv_b[i])).astype(f32)
        xs, bm, cm = jnp.split(xbc, [D_INNER, D_INNER + SSM_GROUPS * D_STATE], axis=-1)
        xh = xs.reshape(b, s, SSM_HEADS, SSM_HEAD_DIM)
        dt = jax.nn.softplus(dt_raw.astype(f32) + dt_bias[i].astype(f32))
        a_neg = -jnp.exp(a_log[i].astype(f32))
        y = ssd_chunked(xh, dt, a_neg,
                        bm.reshape(b, s, SSM_GROUPS, D_STATE),
                        cm.reshape(b, s, SSM_GROUPS, D_STATE))
        y = (y + d_skip[i].astype(f32)[:, None] * xh).reshape(b, s, D_INNER)
        y = rmsnorm(y * jax.nn.silu(z.astype(f32)), g_ssd[i], SSM_NORM_EPS).astype(x.dtype)
        out_s = y @ w_ssd_br[i]

        merged = jax.nn.sigmoid(g_a) * out_a + jax.nn.sigmoid(g_s) * out_s
        h = h + merged @ w_o[i]

        f = rmsnorm(h, g_ffn[i])
        h = h + (jax.nn.silu(f @ w_gate[i]) * (f @ w_up[i])) @ w_down[i]

        gate = jax.nn.sigmoid(rmsnorm(h, g_ple[i]) @ w_ple_gate[i])
        h = h + gate * (p[i] @ w_ple_proj[i])
    return rmsnorm(h, g_final)
```

```python
from functools import partial

import numpy as np
import jax
import jax.numpy as jnp
from jax import lax
from jax.experimental import pallas as pl
from jax.experimental.pallas import tpu as pltpu

F32 = jnp.float32
BF16 = jnp.bfloat16

D_MODEL = 2048
HEAD_DIM = 64
ATTN_HEADS = 16
KV_HEADS = 4
GROUP = ATTN_HEADS // KV_HEADS
Q_DIM = ATTN_HEADS * HEAD_DIM
KV_DIM = KV_HEADS * HEAD_DIM
ATTN_BLOCK = 128
ROPE_THETA = 10000.0
D_INNER = 2048
SSM_HEAD_DIM = 64
SSM_HEADS = 32
SSM_GROUPS = 4
HEADS_PER_GROUP = SSM_HEADS // SSM_GROUPS
D_STATE = 128
CONV_WIDTH = 4
CHUNK = 128
CONV_DIM = D_INNER + 2 * SSM_GROUPS * D_STATE
FFN_HIDDEN = 5632
PLE_DIM = 256
NORM_EPS = 1e-6
SSM_NORM_EPS = 1e-5

LANES = 128
SUBLANES = 8
VMEM_LIMIT_BYTES = 56 * 1024 * 1024

NEG_BIG = -1e30


def _cparams(semantics):
    return pltpu.CompilerParams(dimension_semantics=semantics,
                                vmem_limit_bytes=VMEM_LIMIT_BYTES)


def _resident(shape):
    return pl.BlockSpec(shape, lambda *_: (0,) * len(shape),
                        pipeline_mode=pl.Buffered(1))


def _rms(xf, g, eps):
    var = jnp.mean(xf * xf, axis=-1, keepdims=True)
    return xf * lax.rsqrt(var + eps) * g


def _sigmoid(x):
    return 1.0 / (1.0 + jnp.exp(-x))


def _silu(x):
    return x * _sigmoid(x)


def _dot(a, b):
    return jnp.dot(a, b, preferred_element_type=F32)


def _dot_nt(a, b):
    return lax.dot_general(a, b, (((1,), (1,)), ((), ())),
                           preferred_element_type=F32)


def _split3(v):
    v1 = v.astype(BF16)
    r1 = v - v1.astype(F32)
    v2 = r1.astype(BF16)
    v3 = (r1 - v2.astype(F32)).astype(BF16)
    return v1, v2, v3


def _prep_kernel(x_ref, g_ref, pos_ref, invf_ref, sgn_ref, u_ref, cos_ref, sin_ref):
    u_ref[...] = _rms(x_ref[...], g_ref[...], NORM_EPS).astype(BF16)
    ang = pos_ref[...].astype(F32) * invf_ref[...]
    cos_ref[...] = jnp.cos(ang)
    sin_ref[...] = jnp.sin(ang) * sgn_ref[...]


def _prep(x2, g_mix, pos2, invf, sgn, tm=512):
    t = x2.shape[0]
    row = lambda i: (i, 0)
    return pl.pallas_call(
        _prep_kernel,
        out_shape=(jax.ShapeDtypeStruct((t, D_MODEL), BF16),
                   jax.ShapeDtypeStruct((t, LANES), F32),
                   jax.ShapeDtypeStruct((t, LANES), F32)),
        grid=(t // tm,),
        in_specs=[pl.BlockSpec((tm, D_MODEL), row),
                  _resident((1, D_MODEL)),
                  pl.BlockSpec((tm, 1), row),
                  _resident((1, LANES)),
                  _resident((1, LANES))],
        out_specs=(pl.BlockSpec((tm, D_MODEL), row),
                   pl.BlockSpec((tm, LANES), row),
                   pl.BlockSpec((tm, LANES), row)),
        compiler_params=_cparams(("arbitrary",)),
        name="prep",
    )(x2, g_mix, pos2, invf, sgn)


def _mm_kernel(a_ref, w_ref, o_ref):
    o_ref[...] = _dot(a_ref[...], w_ref[...]).astype(o_ref.dtype)


def _mm(a, w, out_dtype, tm, tn, name):
    m, k = a.shape
    n = w.shape[1]
    return pl.pallas_call(
        _mm_kernel,
        out_shape=jax.ShapeDtypeStruct((m, n), out_dtype),
        grid=(n // tn, m // tm),
        in_specs=[pl.BlockSpec((tm, k), lambda j, i: (i, 0)),
                  pl.BlockSpec((k, tn), lambda j, i: (0, j))],
        out_specs=pl.BlockSpec((tm, tn), lambda j, i: (i, j)),
        compiler_params=_cparams(("arbitrary", "arbitrary")),
        name=name,
    )(a, w)


def _attn_kernel(sinks_ref, q_ref, kc_ref, kp_ref, vc_ref, vp_ref,
                 cc_ref, sc_ref, cp_ref, sp_ref, o_ref):
    n = pl.program_id(1)
    blk = ATTN_BLOCK
    lane = lax.broadcasted_iota(jnp.int32, (blk, LANES), 1)
    first_half = (lane % HEAD_DIM) < (HEAD_DIM // 2)
    low_head = lane < HEAD_DIM

    def rope(t, c, s):
        partner = jnp.where(first_half,
                            pltpu.roll(t, LANES - HEAD_DIM // 2, 1),
                            pltpu.roll(t, HEAD_DIM // 2, 1))
        return t * c + partner * s

    cc, sc, cp, sp = cc_ref[...], sc_ref[...], cp_ref[...], sp_ref[...]

    qi = lax.broadcasted_iota(jnp.int32, (blk, 2 * blk), 0)
    kj = lax.broadcasted_iota(jnp.int32, (blk, 2 * blk), 1)
    rel = kj - qi
    valid = (rel >= 1) & (rel <= blk) & ((kj >= blk) | (n > 0))

    scale = HEAD_DIM ** -0.5
    for half in range(KV_DIM // LANES):
        ksl = slice(half * LANES, (half + 1) * LANES)
        k_win = jnp.concatenate(
            [rope(kp_ref[:, ksl].astype(F32), cp, sp),
             rope(kc_ref[:, ksl].astype(F32), cc, sc)], axis=0).astype(BF16)
        v_win = jnp.concatenate([vp_ref[:, ksl], vc_ref[:, ksl]], axis=0)
        for sub in range(2):
            kh = half * 2 + sub
            k_low = sub == 0
            q_rows = []
            for g in range(GROUP):
                chunk = kh * 2 + g // 2
                qc = rope(q_ref[:, chunk * LANES:(chunk + 1) * LANES].astype(F32), cc, sc)
                if (g % 2 == 0) != k_low:
                    qc = pltpu.roll(qc, HEAD_DIM, 1)
                keep = low_head if k_low else jnp.logical_not(low_head)
                q_rows.append(jnp.where(keep, qc * scale, 0.0).astype(BF16))
            qs = jnp.concatenate(q_rows, axis=0)
            s_all = _dot_nt(qs, k_win)
            p_rows = []
            for g in range(GROUP):
                sink = sinks_ref[0, kh * GROUP + g]
                s = jnp.where(valid, s_all[g * blk:(g + 1) * blk], NEG_BIG)
                m = jnp.maximum(jnp.max(s, axis=-1, keepdims=True), sink)
                e = jnp.exp(s - m)
                den = jnp.sum(e, axis=-1, keepdims=True) + jnp.exp(sink - m)
                p_rows.append((e / den).astype(BF16))
            pv = _dot(jnp.concatenate(p_rows, axis=0), v_win)
            for pair in range(GROUP // 2):
                a = pv[(2 * pair) * blk:(2 * pair + 1) * blk]
                b = pv[(2 * pair + 1) * blk:(2 * pair + 2) * blk]
                if k_low:
                    b = pltpu.roll(b, HEAD_DIM, 1)
                else:
                    a = pltpu.roll(a, HEAD_DIM, 1)
                chunk = kh * 2 + pair
                o_ref[:, chunk * LANES:(chunk + 1) * LANES] = (
                    jnp.where(low_head, a, b).astype(o_ref.dtype))


def _attention(qkv, cos_t, sin_t, sinks, batch, seq):
    t = qkv.shape[0]
    nb = seq // ATTN_BLOCK
    kcol = Q_DIM // KV_DIM
    vcol = kcol + 1
    cur = lambda b, n: b * nb + n
    prev = lambda b, n: jnp.maximum(b * nb + n - 1, 0)
    return pl.pallas_call(
        _attn_kernel,
        out_shape=jax.ShapeDtypeStruct((t, Q_DIM), BF16),
        grid=(batch, nb),
        in_specs=[pl.BlockSpec(memory_space=pltpu.SMEM),
                  pl.BlockSpec((ATTN_BLOCK, Q_DIM), lambda b, n: (cur(b, n), 0)),
                  pl.BlockSpec((ATTN_BLOCK, KV_DIM), lambda b, n: (cur(b, n), kcol)),
                  pl.BlockSpec((ATTN_BLOCK, KV_DIM), lambda b, n: (prev(b, n), kcol)),
                  pl.BlockSpec((ATTN_BLOCK, KV_DIM), lambda b, n: (cur(b, n), vcol)),
                  pl.BlockSpec((ATTN_BLOCK, KV_DIM), lambda b, n: (prev(b, n), vcol)),
                  pl.BlockSpec((ATTN_BLOCK, LANES), lambda b, n: (cur(b, n), 0)),
                  pl.BlockSpec((ATTN_BLOCK, LANES), lambda b, n: (cur(b, n), 0)),
                  pl.BlockSpec((ATTN_BLOCK, LANES), lambda b, n: (prev(b, n), 0)),
                  pl.BlockSpec((ATTN_BLOCK, LANES), lambda b, n: (prev(b, n), 0))],
        out_specs=pl.BlockSpec((ATTN_BLOCK, Q_DIM), lambda b, n: (cur(b, n), 0)),
        compiler_params=_cparams(("arbitrary", "arbitrary")),
        name="attention",
    )(sinks, qkv, qkv, qkv, qkv, qkv, cos_t, sin_t, cos_t, sin_t)


def _ssd_kernel(xbc_ref, z_ref, dt_ref, cw_ref, cb_ref, dtb_ref, alog_ref,
                dskip_ref, g_ref, expand_ref, o_ref, xpad_ref, state_ref, y_ref):
    c = pl.program_id(1)
    L = CHUNK
    tail = SUBLANES

    @pl.when(c == 0)
    def _():
        xpad_ref[0:tail, :] = jnp.zeros((tail, CONV_DIM), F32)
        state_ref[...] = jnp.zeros_like(state_ref)

    @pl.when(c > 0)
    def _():
        xpad_ref[0:tail, :] = xpad_ref[L:L + tail, :]

    xpad_ref[tail:tail + L, :] = xbc_ref[...].astype(F32)

    acc = cb_ref[...] + xpad_ref[pl.ds(tail - (CONV_WIDTH - 1), L), :] * cw_ref[0:1, :]
    for w in range(1, CONV_WIDTH):
        acc = acc + xpad_ref[pl.ds(tail - (CONV_WIDTH - 1) + w, L), :] * cw_ref[w:w + 1, :]
    xc = _silu(acc)
    xs = xc[:, :D_INNER]
    bm = xc[:, D_INNER:D_INNER + SSM_GROUPS * D_STATE]
    cm = xc[:, D_INNER + SSM_GROUPS * D_STATE:]

    v = dt_ref[...] + dtb_ref[...]
    dt = jnp.maximum(v, 0.0) + jnp.log1p(jnp.exp(-jnp.abs(v)))
    a = dt * (-jnp.exp(alog_ref[...]))

    ri = lax.broadcasted_iota(jnp.int32, (L, L), 0)
    ci = lax.broadcasted_iota(jnp.int32, (L, L), 1)
    causal = ri >= ci
    tril = jnp.where(causal, 1.0, 0.0).astype(BF16)
    a1, a2, a3 = _split3(a)
    cs = _dot(tril, a1) + _dot(tril, a2) + _dot(tril, a3)
    cs_t = cs.T

    lane = lax.broadcasted_iota(jnp.int32, (L, LANES), 1)

    def expand(q):
        q1, q2, q3 = _split3(q)
        packed = jnp.where(
            lane < SSM_HEADS, q1.astype(F32),
            jnp.where(lane < 2 * SSM_HEADS, pltpu.roll(q2.astype(F32), SSM_HEADS, 1),
                      jnp.where(lane < 3 * SSM_HEADS,
                                pltpu.roll(q3.astype(F32), 2 * SSM_HEADS, 1), 0.0)))
        return _dot(packed.astype(BF16), expand_ref[...])

    dtx = expand(dt)
    csx = expand(cs)
    cs_last = csx[L - 1:L, :]
    xd = xs * dtx
    xdd = (xd * jnp.exp(cs_last - csx)).astype(BF16)
    xd16 = xd.astype(BF16)
    chunk_decay = jnp.exp(cs_last)

    lane_l = lax.broadcasted_iota(jnp.int32, (L, LANES), 1)
    low_head = lane_l < SSM_HEAD_DIM
    gw = HEADS_PER_GROUP * SSM_HEAD_DIM

    for g in range(SSM_GROUPS):
        bg = bm[:, g * D_STATE:(g + 1) * D_STATE]
        cg = cm[:, g * D_STATE:(g + 1) * D_STATE].astype(BF16)
        bg_t = bg.T.astype(BF16)
        cb = _dot_nt(cg, bg.astype(BF16))
        gsl = slice(g * gw, (g + 1) * gw)
        st_prev = state_ref[:, gsl]
        y_off = _dot(cg, st_prev.astype(BF16))
        state_ref[:, gsl] = st_prev * chunk_decay[:, gsl] + _dot(bg_t, xdd[:, gsl])
        y_ref[:, gsl] = y_off
        for pair in range(HEADS_PER_GROUP // 2):
            psl = slice(g * gw + pair * LANES, g * gw + (pair + 1) * LANES)
            xp = xd16[:, psl]
            y_pair = None
            for sub in range(2):
                h = g * HEADS_PER_GROUP + pair * 2 + sub
                diff = cs[:, h:h + 1] - cs_t[h:h + 1, :]
                decay = jnp.exp(jnp.where(causal, diff, NEG_BIG))
                mh = (cb * decay).astype(BF16)
                keep = low_head if sub == 0 else jnp.logical_not(low_head)
                part = _dot(mh, jnp.where(keep, xp, jnp.zeros_like(xp)))
                y_pair = part if y_pair is None else y_pair + part
            y_ref[:, psl] = y_ref[:, psl] * jnp.exp(csx[:, psl]) + y_pair

    y = y_ref[...] + dskip_ref[...] * xs
    yz = y * _silu(z_ref[...].astype(F32))
    o_ref[...] = _rms(yz, g_ref[...], SSM_NORM_EPS).astype(o_ref.dtype)


def _ssd(xbc, z, dtp, conv_w, conv_b, dtb_p, alog_p, dskip_x, g_ssd, expand, batch, seq):
    t = xbc.shape[0]
    nc = seq // CHUNK
    row = lambda b, c: (b * nc + c, 0)
    return pl.pallas_call(
        _ssd_kernel,
        out_shape=jax.ShapeDtypeStruct((t, D_INNER), BF16),
        grid=(batch, nc),
        in_specs=[pl.BlockSpec((CHUNK, CONV_DIM), row),
                  pl.BlockSpec((CHUNK, D_INNER), row),
                  pl.BlockSpec((CHUNK, LANES), row),
                  _resident((CONV_WIDTH, CONV_DIM)),
                  _resident((1, CONV_DIM)),
                  _resident((1, LANES)),
                  _resident((1, LANES)),
                  _resident((1, D_INNER)),
                  _resident((1, D_INNER)),
                  _resident((LANES, D_INNER))],
        out_specs=pl.BlockSpec((CHUNK, D_INNER), row),
        scratch_shapes=[pltpu.VMEM((CHUNK + 2 * SUBLANES, CONV_DIM), F32),
                        pltpu.VMEM((D_STATE, D_INNER), F32),
                        pltpu.VMEM((CHUNK, D_INNER), F32)],
        compiler_params=_cparams(("arbitrary", "arbitrary")),
        name="ssd",
    )(xbc, z, dtp, conv_w, conv_b, dtb_p, alog_p, dskip_x, g_ssd, expand)


def _merge_kernel(attn_ref, y_ref, ga_ref, gs_ref, wa_ref, ws_ref, o_ref):
    out_a = _dot(attn_ref[...], wa_ref[...])
    out_s = _dot(y_ref[...], ws_ref[...])
    merged = (_sigmoid(ga_ref[...].astype(F32)) * out_a
              + _sigmoid(gs_ref[...].astype(F32)) * out_s)
    o_ref[...] = merged.astype(o_ref.dtype)


def _merge(attn, y, gates, wa, ws, tm=512):
    t = attn.shape[0]
    row = lambda i: (i, 0)
    return pl.pallas_call(
        _merge_kernel,
        out_shape=jax.ShapeDtypeStruct((t, D_MODEL), BF16),
        grid=(t // tm,),
        in_specs=[pl.BlockSpec((tm, Q_DIM), row),
                  pl.BlockSpec((tm, D_INNER), row),
                  pl.BlockSpec((tm, D_MODEL), lambda i: (i, 0)),
                  pl.BlockSpec((tm, D_MODEL), lambda i: (i, 1)),
                  _resident((Q_DIM, D_MODEL)),
                  _resident((D_INNER, D_MODEL))],
        out_specs=pl.BlockSpec((tm, D_MODEL), row),
        compiler_params=_cparams(("arbitrary",)),
        name="merge",
    )(attn, y, gates, gates, wa, ws)


def _oproj_kernel(m_ref, x_ref, w_ref, o_ref):
    o_ref[...] = x_ref[...] + _dot(m_ref[...], w_ref[...])


def _oproj(merged, x2, wo, tm=512):
    t = merged.shape[0]
    row = lambda i: (i, 0)
    return pl.pallas_call(
        _oproj_kernel,
        out_shape=jax.ShapeDtypeStruct((t, D_MODEL), F32),
        grid=(t // tm,),
        in_specs=[pl.BlockSpec((tm, D_MODEL), row),
                  pl.BlockSpec((tm, D_MODEL), row),
                  _resident((D_MODEL, D_MODEL))],
        out_specs=pl.BlockSpec((tm, D_MODEL), row),
        compiler_params=_cparams(("arbitrary",)),
        name="oproj",
    )(merged, x2, wo)


def _ffn_kernel(h_ref, g_ref, wg_ref, wu_ref, wd_ref, o_ref, f_ref):
    j = pl.program_id(1)

    @pl.when(j == 0)
    def _():
        h = h_ref[...]
        f_ref[...] = _rms(h, g_ref[...], NORM_EPS).astype(BF16)
        o_ref[...] = h

    f = f_ref[...]
    act = (_silu(_dot(f, wg_ref[...])) * _dot(f, wu_ref[...])).astype(BF16)
    o_ref[...] += _dot(act, wd_ref[...])


def _ffn(h1, g_ffn, wg, wu, wd, tm=512, th=512):
    t = h1.shape[0]
    return pl.pallas_call(
        _ffn_kernel,
        out_shape=jax.ShapeDtypeStruct((t, D_MODEL), F32),
        grid=(t // tm, FFN_HIDDEN // th),
        in_specs=[pl.BlockSpec((tm, D_MODEL), lambda i, j: (i, 0)),
                  _resident((1, D_MODEL)),
                  pl.BlockSpec((D_MODEL, th), lambda i, j: (0, j)),
                  pl.BlockSpec((D_MODEL, th), lambda i, j: (0, j)),
                  pl.BlockSpec((th, D_MODEL), lambda i, j: (j, 0))],
        out_specs=pl.BlockSpec((tm, D_MODEL), lambda i, j: (i, 0)),
        scratch_shapes=[pltpu.VMEM((tm, D_MODEL), BF16)],
        compiler_params=_cparams(("arbitrary", "arbitrary")),
        name="ffn",
    )(h1, g_ffn, wg, wu, wd)


def _ple_kernel(h_ref, p_ref, gp_ref, gf_ref, wg_ref, wp_ref, o_ref):
    h = h_ref[...]
    r = _rms(h, gp_ref[...], NORM_EPS).astype(BF16)
    gate = _sigmoid(_dot(r, wg_ref[...]))
    h3 = h + gate * _dot(p_ref[...].astype(BF16), wp_ref[...])
    o_ref[...] = _rms(h3, gf_ref[...], NORM_EPS)


def _ple(h2, p2, g_ple, g_final, wpg, wpp, tm=512):
    t = h2.shape[0]
    row = lambda i: (i, 0)
    return pl.pallas_call(
        _ple_kernel,
        out_shape=jax.ShapeDtypeStruct((t, D_MODEL), F32),
        grid=(t // tm,),
        in_specs=[pl.BlockSpec((tm, D_MODEL), row),
                  pl.BlockSpec((tm, PLE_DIM), row),
                  _resident((1, D_MODEL)),
                  _resident((1, D_MODEL)),
                  _resident((D_MODEL, D_MODEL)),
                  _resident((PLE_DIM, D_MODEL))],
        out_specs=pl.BlockSpec((tm, D_MODEL), row),
        compiler_params=_cparams(("arbitrary",)),
        name="ple",
    )(h2, p2, g_ple, g_final, wpg, wpp)


def _rope_constants():
    half = HEAD_DIM // 2
    lane = np.arange(LANES)
    inv_freq = ROPE_THETA ** (-jnp.arange(half, dtype=F32) * 2.0 / HEAD_DIM)
    invf = inv_freq[lane % half][None, :]
    sgn = jnp.asarray(np.where((lane % HEAD_DIM) < half, -1.0, 1.0), F32)[None, :]
    return invf, sgn


def _expand_matrix():
    rows = np.arange(LANES)[:, None]
    cols = np.arange(D_INNER)[None, :]
    hit = (rows < 3 * SSM_HEADS) & ((rows % SSM_HEADS) == (cols // SSM_HEAD_DIM))
    return jnp.asarray(hit, BF16)


def _pad_lanes(v):
    return jnp.pad(v.astype(F32), (0, LANES - v.shape[0]))[None, :]


def kernel(x, p, positions, g_mix, w_in, conv_w, conv_b, dt_bias, a_log, d_skip, g_ssd,
           sinks, w_attn_br, w_ssd_br, w_o, g_ffn, w_gate, w_up, w_down, g_ple,
           w_ple_gate, w_ple_proj, g_final):
    b, s, d = x.shape
    t = b * s
    assert d == D_MODEL and s % CHUNK == 0 and p.shape[0] == 1
    i = 0
    x2 = x.reshape(t, d)
    p2 = p[i].reshape(t, PLE_DIM)
    pos2 = positions.reshape(t, 1)

    o_k = Q_DIM
    o_z = Q_DIM + 2 * KV_DIM
    o_xbc = o_z + D_INNER
    o_dt = o_xbc + CONV_DIM
    o_ga = o_dt + SSM_HEADS
    w = w_in[i]
    w_qkv = w[:, :o_z].astype(BF16)
    w_z = w[:, o_z:o_xbc].astype(BF16)
    w_xbc = w[:, o_xbc:o_dt].astype(BF16)
    w_dt = jnp.pad(w[:, o_dt:o_ga], ((0, 0), (0, LANES - SSM_HEADS))).astype(BF16)
    w_gates = w[:, o_ga:].astype(BF16)

    invf, sgn = _rope_constants()
    u, cos_t, sin_t = _prep(x2, g_mix[i][None, :], pos2, invf, sgn)

    qkv = _mm(u, w_qkv, BF16, 1024, 768, "proj_qkv")
    z = _mm(u, w_z, BF16, 1024, 1024, "proj_z")
    xbc = _mm(u, w_xbc, BF16, 1024, 1024, "proj_xbc")
    gates = _mm(u, w_gates, BF16, 1024, 1024, "proj_gates")
    dtp = _mm(u, w_dt, F32, 1024, LANES, "proj_dt")

    attn = _attention(qkv, cos_t, sin_t, sinks[i][None, :].astype(F32), b, s)

    y = _ssd(xbc, z, dtp, conv_w[i], conv_b[i][None, :], _pad_lanes(dt_bias[i]),
             _pad_lanes(a_log[i]), jnp.repeat(d_skip[i].astype(F32), SSM_HEAD_DIM)[None, :],
             g_ssd[i][None, :], _expand_matrix(), b, s)

    merged = _merge(attn, y, gates, w_attn_br[i].astype(BF16), w_ssd_br[i].astype(BF16))
    h1 = _oproj(merged, x2, w_o[i].astype(BF16))
    h2 = _ffn(h1, g_ffn[i][None, :], w_gate[i].astype(BF16), w_up[i].astype(BF16),
              w_down[i].astype(BF16))
    out = _ple(h2, p2, g_ple[i][None, :], g_final[None, :], w_ple_gate[i].astype(BF16),
               w_ple_proj[i].astype(BF16))
    return out.reshape(b, s, d)
```

```python
from functools import partial

import numpy as np
import jax
import jax.numpy as jnp
from jax import lax
from jax.experimental import pallas as pl
from jax.experimental.pallas import tpu as pltpu

F32 = jnp.float32
BF16 = jnp.bfloat16

D_MODEL = 2048
HEAD_DIM = 64
ATTN_HEADS = 16
KV_HEADS = 4
GROUP = ATTN_HEADS // KV_HEADS
Q_DIM = ATTN_HEADS * HEAD_DIM
KV_DIM = KV_HEADS * HEAD_DIM
ATTN_BLOCK = 128
ROPE_THETA = 10000.0
D_INNER = 2048
SSM_HEAD_DIM = 64
SSM_HEADS = 32
SSM_GROUPS = 4
HEADS_PER_GROUP = SSM_HEADS // SSM_GROUPS
D_STATE = 128
CONV_WIDTH = 4
CHUNK = 128
CONV_DIM = D_INNER + 2 * SSM_GROUPS * D_STATE
FFN_HIDDEN = 5632
PLE_DIM = 256
NORM_EPS = 1e-6
SSM_NORM_EPS = 1e-5

LANES = 128
SUBLANES = 8
VMEM_LIMIT_BYTES = 56 * 1024 * 1024

NEG_BIG = -1e30


def _cparams(semantics):
    return pltpu.CompilerParams(dimension_semantics=semantics,
                                vmem_limit_bytes=VMEM_LIMIT_BYTES)


def _resident(shape):
    return pl.BlockSpec(shape, lambda *_: (0,) * len(shape),
                        pipeline_mode=pl.Buffered(1))


def _rms(xf, g, eps):
    var = jnp.mean(xf * xf, axis=-1, keepdims=True)
    return xf * lax.rsqrt(var + eps) * g


NEG_LOG2E = -1.4426950408889634


def _sigmoid(x):
    return 1.0 / (1.0 + jnp.exp2(x * NEG_LOG2E))


def _silu(x):
    return x * _sigmoid(x)


def _dot(a, b):
    return jnp.dot(a, b, preferred_element_type=F32)


def _dot_nt(a, b):
    return lax.dot_general(a, b, (((1,), (1,)), ((), ())),
                           preferred_element_type=F32)


def _split3(v):
    v1 = v.astype(BF16)
    r1 = v - v1.astype(F32)
    v2 = r1.astype(BF16)
    v3 = (r1 - v2.astype(F32)).astype(BF16)
    return v1, v2, v3


def _prep_kernel(x_ref, g_ref, pos_ref, invf_ref, sgn_ref, u_ref, cos_ref, sin_ref):
    u_ref[...] = _rms(x_ref[...], g_ref[...], NORM_EPS).astype(BF16)
    ang = pos_ref[...].astype(F32) * invf_ref[...]
    cos_ref[...] = jnp.cos(ang)
    sin_ref[...] = jnp.sin(ang) * sgn_ref[...]


def _prep(x2, g_mix, pos2, invf, sgn, tm=512):
    t = x2.shape[0]
    row = lambda i: (i, 0)
    return pl.pallas_call(
        _prep_kernel,
        out_shape=(jax.ShapeDtypeStruct((t, D_MODEL), BF16),
                   jax.ShapeDtypeStruct((t, LANES), F32),
                   jax.ShapeDtypeStruct((t, LANES), F32)),
        grid=(t // tm,),
        in_specs=[pl.BlockSpec((tm, D_MODEL), row),
                  _resident((1, D_MODEL)),
                  pl.BlockSpec((tm, 1), row),
                  _resident((1, LANES)),
                  _resident((1, LANES))],
        out_specs=(pl.BlockSpec((tm, D_MODEL), row),
                   pl.BlockSpec((tm, LANES), row),
                   pl.BlockSpec((tm, LANES), row)),
        compiler_params=_cparams(("arbitrary",)),
        name="prep",
    )(x2, g_mix, pos2, invf, sgn)


def _inproj_kernel(a_ref, wt_ref, o_ref, wbf_ref):
    @pl.when(pl.program_id(1) == 0)
    def _():
        wbf_ref[...] = wt_ref[...].astype(BF16)

    o_ref[...] = _dot_nt(a_ref[...], wbf_ref[...]).astype(o_ref.dtype)


def _inproj(a, wt, row0, n, out_dtype, tm, tn, name):
    m, k = a.shape
    assert n % tn == 0 and row0 % SUBLANES == 0
    return pl.pallas_call(
        _inproj_kernel,
        out_shape=jax.ShapeDtypeStruct((m, n), out_dtype),
        grid=(n // tn, m // tm),
        in_specs=[pl.BlockSpec((tm, k), lambda j, i: (i, 0)),
                  pl.BlockSpec((pl.Element(tn), pl.Element(k)),
                               lambda j, i: (pl.multiple_of(row0 + j * tn, SUBLANES), 0))],
        out_specs=pl.BlockSpec((tm, tn), lambda j, i: (i, j)),
        scratch_shapes=[pltpu.VMEM((tn, k), BF16)],
        compiler_params=_cparams(("arbitrary", "arbitrary")),
        name=name,
    )(a, wt)


def _attn_kernel(sinks_ref, q_ref, kc_ref, kp_ref, vc_ref, vp_ref,
                 cc_ref, sc_ref, cp_ref, sp_ref, o_ref):
    n = pl.program_id(1)
    blk = ATTN_BLOCK
    lane = lax.broadcasted_iota(jnp.int32, (blk, LANES), 1)
    first_half = (lane % HEAD_DIM) < (HEAD_DIM // 2)
    low_head = lane < HEAD_DIM

    def rope(t, c, s):
        partner = jnp.where(first_half,
                            pltpu.roll(t, LANES - HEAD_DIM // 2, 1),
                            pltpu.roll(t, HEAD_DIM // 2, 1))
        return t * c + partner * s

    cc, sc, cp, sp = cc_ref[...], sc_ref[...], cp_ref[...], sp_ref[...]

    qi = lax.broadcasted_iota(jnp.int32, (blk, 2 * blk), 0)
    kj = lax.broadcasted_iota(jnp.int32, (blk, 2 * blk), 1)
    rel = kj - qi
    valid = (rel >= 1) & (rel <= blk) & ((kj >= blk) | (n > 0))

    scale = HEAD_DIM ** -0.5
    for half in range(KV_DIM // LANES):
        ksl = slice(half * LANES, (half + 1) * LANES)
        k_win = jnp.concatenate(
            [rope(kp_ref[:, ksl].astype(F32), cp, sp),
             rope(kc_ref[:, ksl].astype(F32), cc, sc)], axis=0).astype(BF16)
        v_win = jnp.concatenate([vp_ref[:, ksl], vc_ref[:, ksl]], axis=0)
        for sub in range(2):
            kh = half * 2 + sub
            k_low = sub == 0
            q_rows = []
            for g in range(GROUP):
                chunk = kh * 2 + g // 2
                qc = rope(q_ref[:, chunk * LANES:(chunk + 1) * LANES].astype(F32), cc, sc)
                if (g % 2 == 0) != k_low:
                    qc = pltpu.roll(qc, HEAD_DIM, 1)
                keep = low_head if k_low else jnp.logical_not(low_head)
                q_rows.append(jnp.where(keep, qc * scale, 0.0).astype(BF16))
            qs = jnp.concatenate(q_rows, axis=0)
            s_all = _dot_nt(qs, k_win)
            p_rows = []
            for g in range(GROUP):
                sink = sinks_ref[0, kh * GROUP + g]
                s = jnp.where(valid, s_all[g * blk:(g + 1) * blk], NEG_BIG)
                m = jnp.maximum(jnp.max(s, axis=-1, keepdims=True), sink)
                e = jnp.exp(s - m)
                den = jnp.sum(e, axis=-1, keepdims=True) + jnp.exp(sink - m)
                p_rows.append((e / den).astype(BF16))
            pv = _dot(jnp.concatenate(p_rows, axis=0), v_win)
            for pair in range(GROUP // 2):
                a = pv[(2 * pair) * blk:(2 * pair + 1) * blk]
                b = pv[(2 * pair + 1) * blk:(2 * pair + 2) * blk]
                if k_low:
                    b = pltpu.roll(b, HEAD_DIM, 1)
                else:
                    a = pltpu.roll(a, HEAD_DIM, 1)
                chunk = kh * 2 + pair
                o_ref[:, chunk * LANES:(chunk + 1) * LANES] = (
                    jnp.where(low_head, a, b).astype(o_ref.dtype))


def _attention(qkv, cos_t, sin_t, sinks, batch, seq):
    t = qkv.shape[0]
    nb = seq // ATTN_BLOCK
    kcol = Q_DIM // KV_DIM
    vcol = kcol + 1
    cur = lambda b, n: b * nb + n
    prev = lambda b, n: jnp.maximum(b * nb + n - 1, 0)
    return pl.pallas_call(
        _attn_kernel,
        out_shape=jax.ShapeDtypeStruct((t, Q_DIM), BF16),
        grid=(batch, nb),
        in_specs=[pl.BlockSpec(memory_space=pltpu.SMEM),
                  pl.BlockSpec((ATTN_BLOCK, Q_DIM), lambda b, n: (cur(b, n), 0)),
                  pl.BlockSpec((ATTN_BLOCK, KV_DIM), lambda b, n: (cur(b, n), kcol)),
                  pl.BlockSpec((ATTN_BLOCK, KV_DIM), lambda b, n: (prev(b, n), kcol)),
                  pl.BlockSpec((ATTN_BLOCK, KV_DIM), lambda b, n: (cur(b, n), vcol)),
                  pl.BlockSpec((ATTN_BLOCK, KV_DIM), lambda b, n: (prev(b, n), vcol)),
                  pl.BlockSpec((ATTN_BLOCK, LANES), lambda b, n: (cur(b, n), 0)),
                  pl.BlockSpec((ATTN_BLOCK, LANES), lambda b, n: (cur(b, n), 0)),
                  pl.BlockSpec((ATTN_BLOCK, LANES), lambda b, n: (prev(b, n), 0)),
                  pl.BlockSpec((ATTN_BLOCK, LANES), lambda b, n: (prev(b, n), 0))],
        out_specs=pl.BlockSpec((ATTN_BLOCK, Q_DIM), lambda b, n: (cur(b, n), 0)),
        compiler_params=_cparams(("arbitrary", "arbitrary")),
        name="attention",
    )(sinks, qkv, qkv, qkv, qkv, qkv, cos_t, sin_t, cos_t, sin_t)


def _ssd_kernel(xbc_ref, xprev_ref, z_ref, dt_ref, shift_ref, cw_ref, cb_ref, dtb_ref, alog_ref,
                dskip_ref, g_ref, expand_ref, o_ref, state_ref, y_ref):
    c = pl.program_id(1)
    L = CHUNK

    @pl.when(c == 0)
    def _():
        state_ref[...] = jnp.zeros_like(state_ref)

    sel_col = lax.broadcasted_iota(jnp.int32, (CONV_WIDTH * L, 2 * L), 1)
    sel = shift_ref[...]
    sel = jnp.where((sel_col < L) & (c == 0), jnp.zeros_like(sel), sel)
    xwin = jnp.concatenate([xprev_ref[...], xbc_ref[...]], axis=0)
    taps = _dot(sel, xwin)
    acc = cb_ref[...] + taps[0:L] * cw_ref[0:1, :]
    for w in range(1, CONV_WIDTH):
        acc = acc + taps[w * L:(w + 1) * L] * cw_ref[w:w + 1, :]
    xc = _silu(acc)
    xs = xc[:, :D_INNER]
    bm = xc[:, D_INNER:D_INNER + SSM_GROUPS * D_STATE]
    cm = xc[:, D_INNER + SSM_GROUPS * D_STATE:]

    v = dt_ref[...] + dtb_ref[...]
    dt = jnp.maximum(v, 0.0) + jnp.log1p(jnp.exp(-jnp.abs(v)))
    a = dt * (-jnp.exp(alog_ref[...]))

    ri = lax.broadcasted_iota(jnp.int32, (L, L), 0)
    ci = lax.broadcasted_iota(jnp.int32, (L, L), 1)
    causal = ri >= ci
    tril = jnp.where(causal, 1.0, 0.0).astype(BF16)
    ones = jnp.ones((L, L), BF16)
    a1, a2, a3 = _split3(a)
    cs = _dot(tril, a1) + _dot(tril, a2) + _dot(tril, a3)
    tot = _dot(ones, a1) + _dot(ones, a2) + _dot(ones, a3)
    cs_t = cs.T

    lane = lax.broadcasted_iota(jnp.int32, (L, LANES), 1)

    def expand(q):
        q1, q2, q3 = _split3(q)
        packed = jnp.where(
            lane < SSM_HEADS, q1.astype(F32),
            jnp.where(lane < 2 * SSM_HEADS, pltpu.roll(q2.astype(F32), SSM_HEADS, 1),
                      jnp.where(lane < 3 * SSM_HEADS,
                                pltpu.roll(q3.astype(F32), 2 * SSM_HEADS, 1), 0.0)))
        return _dot(packed.astype(BF16), expand_ref[...])

    dtx = expand(dt)
    csx = expand(cs)
    totx = expand(tot)
    xd = xs * dtx
    xdd = (xd * jnp.exp(totx - csx)).astype(BF16)
    xd16 = xd.astype(BF16)
    chunk_decay = jnp.exp(totx)

    low_head = lane < SSM_HEAD_DIM
    gw = HEADS_PER_GROUP * SSM_HEAD_DIM

    for g in range(SSM_GROUPS):
        bg = bm[:, g * D_STATE:(g + 1) * D_STATE]
        cg = cm[:, g * D_STATE:(g + 1) * D_STATE].astype(BF16)
        bg_t = bg.T.astype(BF16)
        cb = _dot_nt(cg, bg.astype(BF16))
        gsl = slice(g * gw, (g + 1) * gw)
        st_prev = state_ref[:, gsl]
        y_off = _dot(cg, st_prev.astype(BF16))
        state_ref[:, gsl] = st_prev * chunk_decay[:, gsl] + _dot(bg_t, xdd[:, gsl])
        y_ref[:, gsl] = y_off
        for pair in range(HEADS_PER_GROUP // 2):
            psl = slice(g * gw + pair * LANES, g * gw + (pair + 1) * LANES)
            xp = xd16[:, psl]
            y_pair = None
            for sub in range(2):
                h = g * HEADS_PER_GROUP + pair * 2 + sub
                diff = cs[:, h:h + 1] - cs_t[h:h + 1, :]
                decay = jnp.exp(jnp.where(causal, diff, NEG_BIG))
                mh = (cb * decay).astype(BF16)
                keep = low_head if sub == 0 else jnp.logical_not(low_head)
                part = _dot(mh, jnp.where(keep, xp, jnp.zeros_like(xp)))
                y_pair = part if y_pair is None else y_pair + part
            y_ref[:, psl] = y_ref[:, psl] * jnp.exp(csx[:, psl]) + y_pair

    y = y_ref[...] + dskip_ref[...] * xs
    yz = y * _silu(z_ref[...].astype(F32))
    o_ref[...] = _rms(yz, g_ref[...], SSM_NORM_EPS).astype(o_ref.dtype)


def _ssd(xbc, z, dtp, shift, conv_w, conv_b, dtb_p, alog_p, dskip_x, g_ssd, expand, batch, seq):
    t = xbc.shape[0]
    nc = seq // CHUNK
    row = lambda b, c: (b * nc + c, 0)
    prev = lambda b, c: (jnp.maximum(b * nc + c - 1, 0), 0)
    return pl.pallas_call(
        _ssd_kernel,
        out_shape=jax.ShapeDtypeStruct((t, D_INNER), BF16),
        grid=(batch, nc),
        in_specs=[pl.BlockSpec((CHUNK, CONV_DIM), row),
                  pl.BlockSpec((CHUNK, CONV_DIM), prev),
                  pl.BlockSpec((CHUNK, D_INNER), row),
                  pl.BlockSpec((CHUNK, LANES), row),
                  _resident((CONV_WIDTH * CHUNK, 2 * CHUNK)),
                  _resident((CONV_WIDTH, CONV_DIM)),
                  _resident((1, CONV_DIM)),
                  _resident((1, LANES)),
                  _resident((1, LANES)),
                  _resident((1, D_INNER)),
                  _resident((1, D_INNER)),
                  _resident((LANES, D_INNER))],
        out_specs=pl.BlockSpec((CHUNK, D_INNER), row),
        scratch_shapes=[pltpu.VMEM((D_STATE, D_INNER), F32),
                        pltpu.VMEM((CHUNK, D_INNER), F32)],
        compiler_params=_cparams(("arbitrary", "arbitrary")),
        name="ssd",
    )(xbc, xbc, z, dtp, shift, conv_w, conv_b, dtb_p, alog_p, dskip_x, g_ssd, expand)


def _merge_kernel(attn_ref, y_ref, ga_ref, gs_ref, wa_ref, ws_ref, o_ref):
    out_a = _dot(attn_ref[...], wa_ref[...])
    out_s = _dot(y_ref[...], ws_ref[...])
    merged = (_sigmoid(ga_ref[...].astype(F32)) * out_a
              + _sigmoid(gs_ref[...].astype(F32)) * out_s)
    o_ref[...] = merged.astype(o_ref.dtype)


def _merge(attn, y, gates, wa, ws, tm=512):
    t = attn.shape[0]
    row = lambda i: (i, 0)
    return pl.pallas_call(
        _merge_kernel,
        out_shape=jax.ShapeDtypeStruct((t, D_MODEL), BF16),
        grid=(t // tm,),
        in_specs=[pl.BlockSpec((tm, Q_DIM), row),
                  pl.BlockSpec((tm, D_INNER), row),
                  pl.BlockSpec((tm, D_MODEL), lambda i: (i, 0)),
                  pl.BlockSpec((tm, D_MODEL), lambda i: (i, 1)),
                  _resident((Q_DIM, D_MODEL)),
                  _resident((D_INNER, D_MODEL))],
        out_specs=pl.BlockSpec((tm, D_MODEL), row),
        compiler_params=_cparams(("arbitrary",)),
        name="merge",
    )(attn, y, gates, gates, wa, ws)


def _oproj_kernel(m_ref, x_ref, w_ref, o_ref):
    o_ref[...] = x_ref[...] + _dot(m_ref[...], w_ref[...])


def _oproj(merged, x2, wo, tm=512):
    t = merged.shape[0]
    row = lambda i: (i, 0)
    return pl.pallas_call(
        _oproj_kernel,
        out_shape=jax.ShapeDtypeStruct((t, D_MODEL), F32),
        grid=(t // tm,),
        in_specs=[pl.BlockSpec((tm, D_MODEL), row),
                  pl.BlockSpec((tm, D_MODEL), row),
                  _resident((D_MODEL, D_MODEL))],
        out_specs=pl.BlockSpec((tm, D_MODEL), row),
        compiler_params=_cparams(("arbitrary",)),
        name="oproj",
    )(merged, x2, wo)


def _ffn_kernel(h_ref, g_ref, wg_ref, wu_ref, wd_ref, o_ref, f_ref):
    j = pl.program_id(1)

    @pl.when(j == 0)
    def _():
        h = h_ref[...]
        f_ref[...] = _rms(h, g_ref[...], NORM_EPS).astype(BF16)
        o_ref[...] = h

    f = f_ref[...]
    act = (_silu(_dot(f, wg_ref[...])) * _dot(f, wu_ref[...])).astype(BF16)
    o_ref[...] += _dot(act, wd_ref[...])


def _ffn(h1, g_ffn, wg, wu, wd, tm=512, th=512):
    t = h1.shape[0]
    return pl.pallas_call(
        _ffn_kernel,
        out_shape=jax.ShapeDtypeStruct((t, D_MODEL), F32),
        grid=(t // tm, FFN_HIDDEN // th),
        in_specs=[pl.BlockSpec((tm, D_MODEL), lambda i, j: (i, 0)),
                  _resident((1, D_MODEL)),
                  pl.BlockSpec((D_MODEL, th), lambda i, j: (0, j)),
                  pl.BlockSpec((D_MODEL, th), lambda i, j: (0, j)),
                  pl.BlockSpec((th, D_MODEL), lambda i, j: (j, 0))],
        out_specs=pl.BlockSpec((tm, D_MODEL), lambda i, j: (i, 0)),
        scratch_shapes=[pltpu.VMEM((tm, D_MODEL), BF16)],
        compiler_params=_cparams(("arbitrary", "arbitrary")),
        name="ffn",
    )(h1, g_ffn, wg, wu, wd)


def _ple_kernel(h_ref, p_ref, gp_ref, gf_ref, wg_ref, wp_ref, o_ref):
    h = h_ref[...]
    r = _rms(h, gp_ref[...], NORM_EPS).astype(BF16)
    gate = _sigmoid(_dot(r, wg_ref[...]))
    h3 = h + gate * _dot(p_ref[...].astype(BF16), wp_ref[...])
    o_ref[...] = _rms(h3, gf_ref[...], NORM_EPS)


def _ple(h2, p2, g_ple, g_final, wpg, wpp, tm=512):
    t = h2.shape[0]
    row = lambda i: (i, 0)
    return pl.pallas_call(
        _ple_kernel,
        out_shape=jax.ShapeDtypeStruct((t, D_MODEL), F32),
        grid=(t // tm,),
        in_specs=[pl.BlockSpec((tm, D_MODEL), row),
                  pl.BlockSpec((tm, PLE_DIM), row),
                  _resident((1, D_MODEL)),
                  _resident((1, D_MODEL)),
                  _resident((D_MODEL, D_MODEL)),
                  _resident((PLE_DIM, D_MODEL))],
        out_specs=pl.BlockSpec((tm, D_MODEL), row),
        compiler_params=_cparams(("arbitrary",)),
        name="ple",
    )(h2, p2, g_ple, g_final, wpg, wpp)


def _rope_constants():
    half = HEAD_DIM // 2
    lane = np.arange(LANES)
    inv_freq = ROPE_THETA ** (-jnp.arange(half, dtype=F32) * 2.0 / HEAD_DIM)
    invf = inv_freq[lane % half][None, :]
    sgn = jnp.asarray(np.where((lane % HEAD_DIM) < half, -1.0, 1.0), F32)[None, :]
    return invf, sgn


def _expand_matrix():
    rows = np.arange(LANES)[:, None]
    cols = np.arange(D_INNER)[None, :]
    hit = (rows < 3 * SSM_HEADS) & ((rows % SSM_HEADS) == (cols // SSM_HEAD_DIM))
    return jnp.asarray(hit, BF16)


def _shift_matrix():
    rows = np.arange(CONV_WIDTH * CHUNK)[:, None]
    cols = np.arange(2 * CHUNK)[None, :]
    w, t = rows // CHUNK, rows % CHUNK
    return jnp.asarray(cols == CHUNK + t - (CONV_WIDTH - 1) + w, BF16)


def _pad_lanes(v):
    return jnp.pad(v.astype(F32), (0, LANES - v.shape[0]))[None, :]


def kernel(x, p, positions, g_mix, w_in, conv_w, conv_b, dt_bias, a_log, d_skip, g_ssd,
           sinks, w_attn_br, w_ssd_br, w_o, g_ffn, w_gate, w_up, w_down, g_ple,
           w_ple_gate, w_ple_proj, g_final):
    b, s, d = x.shape
    t = b * s
    assert d == D_MODEL and s % CHUNK == 0 and p.shape[0] == 1
    i = 0
    x2 = x.reshape(t, d)
    p2 = p[i].reshape(t, PLE_DIM)
    pos2 = positions.reshape(t, 1)

    o_k = Q_DIM
    o_z = Q_DIM + 2 * KV_DIM
    o_xbc = o_z + D_INNER
    o_dt = o_xbc + CONV_DIM
    o_ga = o_dt + SSM_HEADS
    wt = jnp.swapaxes(w_in[i], 0, 1)

    invf, sgn = _rope_constants()
    u, cos_t, sin_t = _prep(x2, g_mix[i][None, :], pos2, invf, sgn)

    qkv = _inproj(u, wt, 0, o_z, BF16, 1024, 768, "proj_qkv")
    z = _inproj(u, wt, o_z, D_INNER, BF16, 1024, 1024, "proj_z")
    xbc = _inproj(u, wt, o_xbc, CONV_DIM, BF16, 1024, 1024, "proj_xbc")
    gates = _inproj(u, wt, o_ga, 2 * D_MODEL, BF16, 1024, 1024, "proj_gates")
    dtp = _inproj(u, wt, o_dt, LANES, F32, 1024, LANES, "proj_dt")

    attn = _attention(qkv, cos_t, sin_t, sinks[i][None, :].astype(F32), b, s)

    y = _ssd(xbc, z, dtp, _shift_matrix(), conv_w[i], conv_b[i][None, :], _pad_lanes(dt_bias[i]),
             _pad_lanes(a_log[i]), jnp.repeat(d_skip[i].astype(F32), SSM_HEAD_DIM)[None, :],
             g_ssd[i][None, :], _expand_matrix(), b, s)

    merged = _merge(attn, y, gates, w_attn_br[i].astype(BF16), w_ssd_br[i].astype(BF16))
    h1 = _oproj(merged, x2, w_o[i].astype(BF16))
    h2 = _ffn(h1, g_ffn[i][None, :], w_gate[i].astype(BF16), w_up[i].astype(BF16),
              w_down[i].astype(BF16))
    out = _ple(h2, p2, g_ple[i][None, :], g_final[None, :], w_ple_gate[i].astype(BF16),
               w_ple_proj[i].astype(BF16))
    return out.reshape(b, s, d)
```

```python
from functools import partial

import numpy as np
import jax
import jax.numpy as jnp
from jax import lax
from jax.experimental import pallas as pl
from jax.experimental.pallas import tpu as pltpu

F32 = jnp.float32
BF16 = jnp.bfloat16

D_MODEL = 2048
HEAD_DIM = 64
ATTN_HEADS = 16
KV_HEADS = 4
GROUP = ATTN_HEADS // KV_HEADS
Q_DIM = ATTN_HEADS * HEAD_DIM
KV_DIM = KV_HEADS * HEAD_DIM
ATTN_BLOCK = 128
ROPE_THETA = 10000.0
D_INNER = 2048
SSM_HEAD_DIM = 64
SSM_HEADS = 32
SSM_GROUPS = 4
HEADS_PER_GROUP = SSM_HEADS // SSM_GROUPS
D_STATE = 128
CONV_WIDTH = 4
CHUNK = 128
CONV_DIM = D_INNER + 2 * SSM_GROUPS * D_STATE
FFN_HIDDEN = 5632
PLE_DIM = 256
NORM_EPS = 1e-6
SSM_NORM_EPS = 1e-5

LANES = 128
SUBLANES = 8
VMEM_LIMIT_BYTES = 56 * 1024 * 1024

NEG_BIG = -1e30


def _cparams(semantics):
    return pltpu.CompilerParams(dimension_semantics=semantics,
                                vmem_limit_bytes=VMEM_LIMIT_BYTES)


def _resident(shape):
    return pl.BlockSpec(shape, lambda *_: (0,) * len(shape),
                        pipeline_mode=pl.Buffered(1))


def _rms(xf, g, eps):
    var = jnp.mean(xf * xf, axis=-1, keepdims=True)
    return xf * lax.rsqrt(var + eps) * g


NEG_LOG2E = -1.4426950408889634


def _sigmoid(x):
    return 1.0 / (1.0 + jnp.exp2(x * NEG_LOG2E))


def _silu(x):
    return x * _sigmoid(x)


def _dot(a, b):
    return jnp.dot(a, b, preferred_element_type=F32)


def _dot_nt(a, b):
    return lax.dot_general(a, b, (((1,), (1,)), ((), ())),
                           preferred_element_type=F32)


def _split3(v):
    v1 = v.astype(BF16)
    r1 = v - v1.astype(F32)
    v2 = r1.astype(BF16)
    v3 = (r1 - v2.astype(F32)).astype(BF16)
    return v1, v2, v3


def _prep_kernel(x_ref, g_ref, pos_ref, invf_ref, sgn_ref, u_ref, cos_ref, sin_ref):
    u_ref[...] = _rms(x_ref[...], g_ref[...], NORM_EPS).astype(BF16)
    ang = pos_ref[...].astype(F32) * invf_ref[...]
    cos_ref[...] = jnp.cos(ang)
    sin_ref[...] = jnp.sin(ang) * sgn_ref[...]


def _prep(x2, g_mix, pos2, invf, sgn, tm=512):
    t = x2.shape[0]
    row = lambda i: (i, 0)
    return pl.pallas_call(
        _prep_kernel,
        out_shape=(jax.ShapeDtypeStruct((t, D_MODEL), BF16),
                   jax.ShapeDtypeStruct((t, LANES), F32),
                   jax.ShapeDtypeStruct((t, LANES), F32)),
        grid=(t // tm,),
        in_specs=[pl.BlockSpec((tm, D_MODEL), row),
                  _resident((1, D_MODEL)),
                  pl.BlockSpec((tm, 1), row),
                  _resident((1, LANES)),
                  _resident((1, LANES))],
        out_specs=(pl.BlockSpec((tm, D_MODEL), row),
                   pl.BlockSpec((tm, LANES), row),
                   pl.BlockSpec((tm, LANES), row)),
        compiler_params=_cparams(("arbitrary",)),
        name="prep",
    )(x2, g_mix, pos2, invf, sgn)


def _inproj_kernel(a_ref, wt_ref, o_ref, wbf_ref):
    @pl.when(pl.program_id(1) == 0)
    def _():
        wbf_ref[...] = wt_ref[...].astype(BF16)

    o_ref[...] = _dot_nt(a_ref[...], wbf_ref[...]).astype(o_ref.dtype)


def _inproj(a, wt, row0, n, out_dtype, tm, tn, name):
    m, k = a.shape
    assert n % tn == 0 and row0 % SUBLANES == 0
    return pl.pallas_call(
        _inproj_kernel,
        out_shape=jax.ShapeDtypeStruct((m, n), out_dtype),
        grid=(n // tn, m // tm),
        in_specs=[pl.BlockSpec((tm, k), lambda j, i: (i, 0)),
                  pl.BlockSpec((pl.Element(tn), pl.Element(k)),
                               lambda j, i: (pl.multiple_of(row0 + j * tn, SUBLANES), 0))],
        out_specs=pl.BlockSpec((tm, tn), lambda j, i: (i, j)),
        scratch_shapes=[pltpu.VMEM((tn, k), BF16)],
        compiler_params=_cparams(("arbitrary", "arbitrary")),
        name=name,
    )(a, wt)


def _attn_kernel(sinks_ref, q_ref, kc_ref, kp_ref, vc_ref, vp_ref,
                 cc_ref, sc_ref, cp_ref, sp_ref, o_ref):
    n = pl.program_id(1)
    blk = ATTN_BLOCK
    lane = lax.broadcasted_iota(jnp.int32, (blk, LANES), 1)
    first_half = (lane % HEAD_DIM) < (HEAD_DIM // 2)
    low_head = lane < HEAD_DIM

    def rope(t, c, s):
        partner = jnp.where(first_half,
                            pltpu.roll(t, LANES - HEAD_DIM // 2, 1),
                            pltpu.roll(t, HEAD_DIM // 2, 1))
        return t * c + partner * s

    cc, sc, cp, sp = cc_ref[...], sc_ref[...], cp_ref[...], sp_ref[...]

    qi = lax.broadcasted_iota(jnp.int32, (blk, 2 * blk), 0)
    kj = lax.broadcasted_iota(jnp.int32, (blk, 2 * blk), 1)
    rel = kj - qi
    valid = (rel >= 1) & (rel <= blk) & ((kj >= blk) | (n > 0))

    scale = HEAD_DIM ** -0.5
    for half in range(KV_DIM // LANES):
        ksl = slice(half * LANES, (half + 1) * LANES)
        k_win = jnp.concatenate(
            [rope(kp_ref[:, ksl].astype(F32), cp, sp),
             rope(kc_ref[:, ksl].astype(F32), cc, sc)], axis=0).astype(BF16)
        v_win = jnp.concatenate([vp_ref[:, ksl], vc_ref[:, ksl]], axis=0)
        for sub in range(2):
            kh = half * 2 + sub
            k_low = sub == 0
            q_rows = []
            for g in range(GROUP):
                chunk = kh * 2 + g // 2
                qc = rope(q_ref[:, chunk * LANES:(chunk + 1) * LANES].astype(F32), cc, sc)
                if (g % 2 == 0) != k_low:
                    qc = pltpu.roll(qc, HEAD_DIM, 1)
                keep = low_head if k_low else jnp.logical_not(low_head)
                q_rows.append(jnp.where(keep, qc * scale, 0.0).astype(BF16))
            qs = jnp.concatenate(q_rows, axis=0)
            s_all = _dot_nt(qs, k_win)
            p_rows = []
            for g in range(GROUP):
                sink = sinks_ref[0, kh * GROUP + g]
                s = jnp.where(valid, s_all[g * blk:(g + 1) * blk], NEG_BIG)
                m = jnp.maximum(jnp.max(s, axis=-1, keepdims=True), sink)
                e = jnp.exp(s - m)
                den = jnp.sum(e, axis=-1, keepdims=True) + jnp.exp(sink - m)
                p_rows.append((e / den).astype(BF16))
            pv = _dot(jnp.concatenate(p_rows, axis=0), v_win)
            for pair in range(GROUP // 2):
                a = pv[(2 * pair) * blk:(2 * pair + 1) * blk]
                b = pv[(2 * pair + 1) * blk:(2 * pair + 2) * blk]
                if k_low:
                    b = pltpu.roll(b, HEAD_DIM, 1)
                else:
                    a = pltpu.roll(a, HEAD_DIM, 1)
                chunk = kh * 2 + pair
                o_ref[:, chunk * LANES:(chunk + 1) * LANES] = (
                    jnp.where(low_head, a, b).astype(o_ref.dtype))


def _attention(qkv, cos_t, sin_t, sinks, batch, seq):
    t = qkv.shape[0]
    nb = seq // ATTN_BLOCK
    kcol = Q_DIM // KV_DIM
    vcol = kcol + 1
    cur = lambda b, n: b * nb + n
    prev = lambda b, n: jnp.maximum(b * nb + n - 1, 0)
    return pl.pallas_call(
        _attn_kernel,
        out_shape=jax.ShapeDtypeStruct((t, Q_DIM), BF16),
        grid=(batch, nb),
        in_specs=[pl.BlockSpec(memory_space=pltpu.SMEM),
                  pl.BlockSpec((ATTN_BLOCK, Q_DIM), lambda b, n: (cur(b, n), 0)),
                  pl.BlockSpec((ATTN_BLOCK, KV_DIM), lambda b, n: (cur(b, n), kcol)),
                  pl.BlockSpec((ATTN_BLOCK, KV_DIM), lambda b, n: (prev(b, n), kcol)),
                  pl.BlockSpec((ATTN_BLOCK, KV_DIM), lambda b, n: (cur(b, n), vcol)),
                  pl.BlockSpec((ATTN_BLOCK, KV_DIM), lambda b, n: (prev(b, n), vcol)),
                  pl.BlockSpec((ATTN_BLOCK, LANES), lambda b, n: (cur(b, n), 0)),
                  pl.BlockSpec((ATTN_BLOCK, LANES), lambda b, n: (cur(b, n), 0)),
                  pl.BlockSpec((ATTN_BLOCK, LANES), lambda b, n: (prev(b, n), 0)),
                  pl.BlockSpec((ATTN_BLOCK, LANES), lambda b, n: (prev(b, n), 0))],
        out_specs=pl.BlockSpec((ATTN_BLOCK, Q_DIM), lambda b, n: (cur(b, n), 0)),
        compiler_params=_cparams(("arbitrary", "arbitrary")),
        name="attention",
    )(sinks, qkv, qkv, qkv, qkv, qkv, cos_t, sin_t, cos_t, sin_t)


def _ssd_kernel(xbc_ref, xprev_ref, z_ref, dt_ref, shift_ref, cw_ref, cb_ref, dtb_ref, alog_ref,
                dskip_ref, g_ref, expand_ref, o_ref, state_ref, y_ref):
    c = pl.program_id(1)
    L = CHUNK

    @pl.when(c == 0)
    def _():
        state_ref[...] = jnp.zeros_like(state_ref)

    sel_col = lax.broadcasted_iota(jnp.int32, (CONV_WIDTH * L, 2 * L), 1)
    sel = shift_ref[...]
    sel = jnp.where((sel_col < L) & (c == 0), jnp.zeros_like(sel), sel)
    xwin = jnp.concatenate([xprev_ref[...], xbc_ref[...]], axis=0)
    taps = _dot(sel, xwin)
    acc = cb_ref[...] + taps[0:L] * cw_ref[0:1, :]
    for w in range(1, CONV_WIDTH):
        acc = acc + taps[w * L:(w + 1) * L] * cw_ref[w:w + 1, :]
    xc = _silu(acc)
    xs = xc[:, :D_INNER]
    bm = xc[:, D_INNER:D_INNER + SSM_GROUPS * D_STATE]
    cm = xc[:, D_INNER + SSM_GROUPS * D_STATE:]

    v = dt_ref[...] + dtb_ref[...]
    dt = jnp.maximum(v, 0.0) + jnp.log1p(jnp.exp(-jnp.abs(v)))
    a = dt * (-jnp.exp(alog_ref[...]))

    ri = lax.broadcasted_iota(jnp.int32, (L, L), 0)
    ci = lax.broadcasted_iota(jnp.int32, (L, L), 1)
    causal = ri >= ci
    tril = jnp.where(causal, 1.0, 0.0).astype(BF16)
    ones = jnp.ones((L, L), BF16)
    a1, a2, a3 = _split3(a)
    cs = _dot(tril, a1) + _dot(tril, a2) + _dot(tril, a3)
    tot = _dot(ones, a1) + _dot(ones, a2) + _dot(ones, a3)
    cs_t = cs.T

    lane = lax.broadcasted_iota(jnp.int32, (L, LANES), 1)

    def expand(q):
        q1, q2, q3 = _split3(q)
        packed = jnp.where(
            lane < SSM_HEADS, q1.astype(F32),
            jnp.where(lane < 2 * SSM_HEADS, pltpu.roll(q2.astype(F32), SSM_HEADS, 1),
                      jnp.where(lane < 3 * SSM_HEADS,
                                pltpu.roll(q3.astype(F32), 2 * SSM_HEADS, 1), 0.0)))
        return _dot(packed.astype(BF16), expand_ref[...])

    dtx = expand(dt)
    csx = expand(cs)
    totx = expand(tot)
    xd = xs * dtx
    xdd = (xd * jnp.exp(totx - csx)).astype(BF16)
    xd16 = xd.astype(BF16)
    chunk_decay = jnp.exp(totx)

    low_head = lane < SSM_HEAD_DIM
    gw = HEADS_PER_GROUP * SSM_HEAD_DIM

    for g in range(SSM_GROUPS):
        bg = bm[:, g * D_STATE:(g + 1) * D_STATE]
        cg = cm[:, g * D_STATE:(g + 1) * D_STATE].astype(BF16)
        bg_t = bg.T.astype(BF16)
        cb = _dot_nt(cg, bg.astype(BF16))
        gsl = slice(g * gw, (g + 1) * gw)
        st_prev = state_ref[:, gsl]
        y_off = _dot(cg, st_prev.astype(BF16))
        state_ref[:, gsl] = st_prev * chunk_decay[:, gsl] + _dot(bg_t, xdd[:, gsl])
        y_ref[:, gsl] = y_off
        for pair in range(HEADS_PER_GROUP // 2):
            psl = slice(g * gw + pair * LANES, g * gw + (pair + 1) * LANES)
            xp = xd16[:, psl]
            y_pair = None
            for sub in range(2):
                h = g * HEADS_PER_GROUP + pair * 2 + sub
                diff = cs[:, h:h + 1] - cs_t[h:h + 1, :]
                decay = jnp.exp(jnp.where(causal, diff, NEG_BIG))
                mh = (cb * decay).astype(BF16)
                keep = low_head if sub == 0 else jnp.logical_not(low_head)
                part = _dot(mh, jnp.where(keep, xp, jnp.zeros_like(xp)))
                y_pair = part if y_pair is None else y_pair + part
            y_ref[:, psl] = y_ref[:, psl] * jnp.exp(csx[:, psl]) + y_pair

    y = y_ref[...] + dskip_ref[...] * xs
    yz = y * _silu(z_ref[...].astype(F32))
    o_ref[...] = _rms(yz, g_ref[...], SSM_NORM_EPS).astype(o_ref.dtype)


def _ssd(xbc, z, dtp, shift, conv_w, conv_b, dtb_p, alog_p, dskip_x, g_ssd, expand, batch, seq):
    t = xbc.shape[0]
    nc = seq // CHUNK
    row = lambda b, c: (b * nc + c, 0)
    prev = lambda b, c: (jnp.maximum(b * nc + c - 1, 0), 0)
    return pl.pallas_call(
        _ssd_kernel,
        out_shape=jax.ShapeDtypeStruct((t, D_INNER), BF16),
        grid=(batch, nc),
        in_specs=[pl.BlockSpec((CHUNK, CONV_DIM), row),
                  pl.BlockSpec((CHUNK, CONV_DIM), prev),
                  pl.BlockSpec((CHUNK, D_INNER), row),
                  pl.BlockSpec((CHUNK, LANES), row),
                  _resident((CONV_WIDTH * CHUNK, 2 * CHUNK)),
                  _resident((CONV_WIDTH, CONV_DIM)),
                  _resident((1, CONV_DIM)),
                  _resident((1, LANES)),
                  _resident((1, LANES)),
                  _resident((1, D_INNER)),
                  _resident((1, D_INNER)),
                  _resident((LANES, D_INNER))],
        out_specs=pl.BlockSpec((CHUNK, D_INNER), row),
        scratch_shapes=[pltpu.VMEM((D_STATE, D_INNER), F32),
                        pltpu.VMEM((CHUNK, D_INNER), F32)],
        compiler_params=_cparams(("arbitrary", "arbitrary")),
        name="ssd",
    )(xbc, xbc, z, dtp, shift, conv_w, conv_b, dtb_p, alog_p, dskip_x, g_ssd, expand)


def _merge_kernel(attn_ref, y_ref, ga_ref, gs_ref, wa_ref, ws_ref, o_ref):
    out_a = _dot(attn_ref[...], wa_ref[...])
    out_s = _dot(y_ref[...], ws_ref[...])
    merged = (_sigmoid(ga_ref[...].astype(F32)) * out_a
              + _sigmoid(gs_ref[...].astype(F32)) * out_s)
    o_ref[...] = merged.astype(o_ref.dtype)


def _merge(attn, y, gates, wa, ws, tm=512):
    t = attn.shape[0]
    row = lambda i: (i, 0)
    return pl.pallas_call(
        _merge_kernel,
        out_shape=jax.ShapeDtypeStruct((t, D_MODEL), BF16),
        grid=(t // tm,),
        in_specs=[pl.BlockSpec((tm, Q_DIM), row),
                  pl.BlockSpec((tm, D_INNER), row),
                  pl.BlockSpec((tm, D_MODEL), lambda i: (i, 0)),
                  pl.BlockSpec((tm, D_MODEL), lambda i: (i, 1)),
                  _resident((Q_DIM, D_MODEL)),
                  _resident((D_INNER, D_MODEL))],
        out_specs=pl.BlockSpec((tm, D_MODEL), row),
        compiler_params=_cparams(("arbitrary",)),
        name="merge",
    )(attn, y, gates, gates, wa, ws)


def _oproj_kernel(m_ref, x_ref, w_ref, g_ref, h_ref, f_ref):
    h = x_ref[...] + _dot(m_ref[...], w_ref[...])
    h_ref[...] = h
    f_ref[...] = _rms(h, g_ref[...], NORM_EPS).astype(BF16)


def _oproj(merged, x2, wo, g_ffn, tm=512):
    t = merged.shape[0]
    row = lambda i: (i, 0)
    return pl.pallas_call(
        _oproj_kernel,
        out_shape=(jax.ShapeDtypeStruct((t, D_MODEL), F32),
                   jax.ShapeDtypeStruct((t, D_MODEL), BF16)),
        grid=(t // tm,),
        in_specs=[pl.BlockSpec((tm, D_MODEL), row),
                  pl.BlockSpec((tm, D_MODEL), row),
                  _resident((D_MODEL, D_MODEL)),
                  _resident((1, D_MODEL))],
        out_specs=(pl.BlockSpec((tm, D_MODEL), row),
                   pl.BlockSpec((tm, D_MODEL), row)),
        compiler_params=_cparams(("arbitrary",)),
        name="oproj",
    )(merged, x2, wo, g_ffn)


def _ffn_up_kernel(f_ref, wg_ref, wu_ref, o_ref, wg16_ref, wu16_ref):
    @pl.when(pl.program_id(1) == 0)
    def _():
        wg16_ref[...] = wg_ref[...].astype(BF16)
        wu16_ref[...] = wu_ref[...].astype(BF16)

    f = f_ref[...]
    o_ref[...] = (_silu(_dot(f, wg16_ref[...])) * _dot(f, wu16_ref[...])).astype(o_ref.dtype)


def _ffn_up(f, wg, wu, tm=1024, th=512):
    t = f.shape[0]
    return pl.pallas_call(
        _ffn_up_kernel,
        out_shape=jax.ShapeDtypeStruct((t, FFN_HIDDEN), BF16),
        grid=(FFN_HIDDEN // th, t // tm),
        in_specs=[pl.BlockSpec((tm, D_MODEL), lambda j, i: (i, 0)),
                  pl.BlockSpec((D_MODEL, th), lambda j, i: (0, j)),
                  pl.BlockSpec((D_MODEL, th), lambda j, i: (0, j))],
        out_specs=pl.BlockSpec((tm, th), lambda j, i: (i, j)),
        scratch_shapes=[pltpu.VMEM((D_MODEL, th), BF16),
                        pltpu.VMEM((D_MODEL, th), BF16)],
        compiler_params=_cparams(("arbitrary", "arbitrary")),
        name="ffn_up",
    )(f, wg, wu)


def _ffn_down_kernel(a_ref, h_ref, wd_ref, o_ref, wd16_ref):
    @pl.when(pl.program_id(1) == 0)
    def _():
        wd16_ref[...] = wd_ref[...].astype(BF16)

    o_ref[...] = h_ref[...] + _dot(a_ref[...], wd16_ref[...])


def _ffn_down(act, h1, wd, tm=512, tn=512):
    t = act.shape[0]
    return pl.pallas_call(
        _ffn_down_kernel,
        out_shape=jax.ShapeDtypeStruct((t, D_MODEL), F32),
        grid=(D_MODEL // tn, t // tm),
        in_specs=[pl.BlockSpec((tm, FFN_HIDDEN), lambda j, i: (i, 0)),
                  pl.BlockSpec((tm, tn), lambda j, i: (i, j)),
                  pl.BlockSpec((FFN_HIDDEN, tn), lambda j, i: (0, j))],
        out_specs=pl.BlockSpec((tm, tn), lambda j, i: (i, j)),
        scratch_shapes=[pltpu.VMEM((FFN_HIDDEN, tn), BF16)],
        compiler_params=_cparams(("arbitrary", "arbitrary")),
        name="ffn_down",
    )(act, h1, wd)


def _ple_kernel(h_ref, p_ref, gp_ref, gf_ref, wg_ref, wp_ref, o_ref):
    h = h_ref[...]
    r = _rms(h, gp_ref[...], NORM_EPS).astype(BF16)
    gate = _sigmoid(_dot(r, wg_ref[...]))
    h3 = h + gate * _dot(p_ref[...].astype(BF16), wp_ref[...])
    o_ref[...] = _rms(h3, gf_ref[...], NORM_EPS)


def _ple(h2, p2, g_ple, g_final, wpg, wpp, tm=512):
    t = h2.shape[0]
    row = lambda i: (i, 0)
    return pl.pallas_call(
        _ple_kernel,
        out_shape=jax.ShapeDtypeStruct((t, D_MODEL), F32),
        grid=(t // tm,),
        in_specs=[pl.BlockSpec((tm, D_MODEL), row),
                  pl.BlockSpec((tm, PLE_DIM), row),
                  _resident((1, D_MODEL)),
                  _resident((1, D_MODEL)),
                  _resident((D_MODEL, D_MODEL)),
                  _resident((PLE_DIM, D_MODEL))],
        out_specs=pl.BlockSpec((tm, D_MODEL), row),
        compiler_params=_cparams(("arbitrary",)),
        name="ple",
    )(h2, p2, g_ple, g_final, wpg, wpp)


def _rope_constants():
    half = HEAD_DIM // 2
    lane = np.arange(LANES)
    inv_freq = ROPE_THETA ** (-jnp.arange(half, dtype=F32) * 2.0 / HEAD_DIM)
    invf = inv_freq[lane % half][None, :]
    sgn = jnp.asarray(np.where((lane % HEAD_DIM) < half, -1.0, 1.0), F32)[None, :]
    return invf, sgn


def _expand_matrix():
    rows = np.arange(LANES)[:, None]
    cols = np.arange(D_INNER)[None, :]
    hit = (rows < 3 * SSM_HEADS) & ((rows % SSM_HEADS) == (cols // SSM_HEAD_DIM))
    return jnp.asarray(hit, BF16)


def _shift_matrix():
    rows = np.arange(CONV_WIDTH * CHUNK)[:, None]
    cols = np.arange(2 * CHUNK)[None, :]
    w, t = rows // CHUNK, rows % CHUNK
    return jnp.asarray(cols == CHUNK + t - (CONV_WIDTH - 1) + w, BF16)


def _pad_lanes(v):
    return jnp.pad(v.astype(F32), (0, LANES - v.shape[0]))[None, :]


def kernel(x, p, positions, g_mix, w_in, conv_w, conv_b, dt_bias, a_log, d_skip, g_ssd,
           sinks, w_attn_br, w_ssd_br, w_o, g_ffn, w_gate, w_up, w_down, g_ple,
           w_ple_gate, w_ple_proj, g_final):
    b, s, d = x.shape
    t = b * s
    assert d == D_MODEL and s % CHUNK == 0 and p.shape[0] == 1
    i = 0
    x2 = x.reshape(t, d)
    p2 = p[i].reshape(t, PLE_DIM)
    pos2 = positions.reshape(t, 1)

    o_k = Q_DIM
    o_z = Q_DIM + 2 * KV_DIM
    o_xbc = o_z + D_INNER
    o_dt = o_xbc + CONV_DIM
    o_ga = o_dt + SSM_HEADS
    wt = jnp.swapaxes(w_in[i], 0, 1)

    invf, sgn = _rope_constants()
    u, cos_t, sin_t = _prep(x2, g_mix[i][None, :], pos2, invf, sgn)

    qkv = _inproj(u, wt, 0, o_z, BF16, 1024, 768, "proj_qkv")
    z = _inproj(u, wt, o_z, D_INNER, BF16, 1024, 1024, "proj_z")
    xbc = _inproj(u, wt, o_xbc, CONV_DIM, BF16, 1024, 1024, "proj_xbc")
    gates = _inproj(u, wt, o_ga, 2 * D_MODEL, BF16, 1024, 1024, "proj_gates")
    dtp = _inproj(u, wt, o_dt, LANES, F32, 1024, LANES, "proj_dt")

    attn = _attention(qkv, cos_t, sin_t, sinks[i][None, :].astype(F32), b, s)

    y = _ssd(xbc, z, dtp, _shift_matrix(), conv_w[i], conv_b[i][None, :], _pad_lanes(dt_bias[i]),
             _pad_lanes(a_log[i]), jnp.repeat(d_skip[i].astype(F32), SSM_HEAD_DIM)[None, :],
             g_ssd[i][None, :], _expand_matrix(), b, s)

    merged = _merge(attn, y, gates, w_attn_br[i].astype(BF16), w_ssd_br[i].astype(BF16))
    h1, f = _oproj(merged, x2, w_o[i].astype(BF16), g_ffn[i][None, :])
    act = _ffn_up(f, w_gate[i], w_up[i])
    h2 = _ffn_down(act, h1, w_down[i])
    out = _ple(h2, p2, g_ple[i][None, :], g_final[None, :], w_ple_gate[i].astype(BF16),
               w_ple_proj[i].astype(BF16))
    return out.reshape(b, s, d)
```

```python
from functools import partial

import numpy as np
import jax
import jax.numpy as jnp
from jax import lax
from jax.experimental import pallas as pl
from jax.experimental.pallas import tpu as pltpu

F32 = jnp.float32
BF16 = jnp.bfloat16

D_MODEL = 2048
HEAD_DIM = 64
ATTN_HEADS = 16
KV_HEADS = 4
GROUP = ATTN_HEADS // KV_HEADS
Q_DIM = ATTN_HEADS * HEAD_DIM
KV_DIM = KV_HEADS * HEAD_DIM
ATTN_BLOCK = 128
ROPE_THETA = 10000.0
D_INNER = 2048
SSM_HEAD_DIM = 64
SSM_HEADS = 32
SSM_GROUPS = 4
HEADS_PER_GROUP = SSM_HEADS // SSM_GROUPS
D_STATE = 128
CONV_WIDTH = 4
CHUNK = 128
CONV_DIM = D_INNER + 2 * SSM_GROUPS * D_STATE
FFN_HIDDEN = 5632
PLE_DIM = 256
NORM_EPS = 1e-6
SSM_NORM_EPS = 1e-5

LANES = 128
SUBLANES = 8
VMEM_LIMIT_BYTES = 56 * 1024 * 1024

NEG_BIG = -1e30


def _cparams(semantics):
    return pltpu.CompilerParams(dimension_semantics=semantics,
                                vmem_limit_bytes=VMEM_LIMIT_BYTES)


def _resident(shape):
    return pl.BlockSpec(shape, lambda *_: (0,) * len(shape),
                        pipeline_mode=pl.Buffered(1))


def _rms(xf, g, eps):
    var = jnp.mean(xf * xf, axis=-1, keepdims=True)
    return xf * lax.rsqrt(var + eps) * g


NEG_LOG2E = -1.4426950408889634


def _sigmoid(x):
    return 1.0 / (1.0 + jnp.exp2(x * NEG_LOG2E))


def _silu(x):
    return x * _sigmoid(x)


def _dot(a, b):
    return jnp.dot(a, b, preferred_element_type=F32)


def _dot_nt(a, b):
    return lax.dot_general(a, b, (((1,), (1,)), ((), ())),
                           preferred_element_type=F32)


def _split3(v):
    v1 = v.astype(BF16)
    r1 = v - v1.astype(F32)
    v2 = r1.astype(BF16)
    v3 = (r1 - v2.astype(F32)).astype(BF16)
    return v1, v2, v3


def _prep_kernel(x_ref, g_ref, pos_ref, invf_ref, sgn_ref, wdt_ref, u_ref, cos_ref, sin_ref, dt_ref):
    u = _rms(x_ref[...], g_ref[...], NORM_EPS).astype(BF16)
    u_ref[...] = u
    ang = pos_ref[...].astype(F32) * invf_ref[...]
    cos_ref[...] = jnp.cos(ang)
    sin_ref[...] = jnp.sin(ang) * sgn_ref[...]
    dt_ref[...] = _dot_nt(u, wdt_ref[...].astype(BF16))


def _prep(x2, g_mix, pos2, invf, sgn, wt, dt_row0, tm=512):
    t = x2.shape[0]
    row = lambda i: (i, 0)
    assert dt_row0 % SUBLANES == 0
    return pl.pallas_call(
        _prep_kernel,
        out_shape=(jax.ShapeDtypeStruct((t, D_MODEL), BF16),
                   jax.ShapeDtypeStruct((t, LANES), F32),
                   jax.ShapeDtypeStruct((t, LANES), F32),
                   jax.ShapeDtypeStruct((t, LANES), F32)),
        grid=(t // tm,),
        in_specs=[pl.BlockSpec((tm, D_MODEL), row),
                  _resident((1, D_MODEL)),
                  pl.BlockSpec((tm, 1), row),
                  _resident((1, LANES)),
                  _resident((1, LANES)),
                  pl.BlockSpec((pl.Element(LANES), pl.Element(D_MODEL)),
                               lambda i: (dt_row0, 0), pipeline_mode=pl.Buffered(1))],
        out_specs=(pl.BlockSpec((tm, D_MODEL), row),
                   pl.BlockSpec((tm, LANES), row),
                   pl.BlockSpec((tm, LANES), row),
                   pl.BlockSpec((tm, LANES), row)),
        compiler_params=_cparams(("arbitrary",)),
        name="prep",
    )(x2, g_mix, pos2, invf, sgn, wt)


def _inproj_kernel(a_ref, wt_ref, o_ref, wbf_ref):
    @pl.when(pl.program_id(1) == 0)
    def _():
        wbf_ref[...] = wt_ref[...].astype(BF16)

    o_ref[...] = _dot_nt(a_ref[...], wbf_ref[...]).astype(o_ref.dtype)


def _inproj(a, wt, row0, n, out_dtype, tm, tn, name):
    m, k = a.shape
    assert n % tn == 0 and row0 % SUBLANES == 0
    return pl.pallas_call(
        _inproj_kernel,
        out_shape=jax.ShapeDtypeStruct((m, n), out_dtype),
        grid=(n // tn, m // tm),
        in_specs=[pl.BlockSpec((tm, k), lambda j, i: (i, 0)),
                  pl.BlockSpec((pl.Element(tn), pl.Element(k)),
                               lambda j, i: (pl.multiple_of(row0 + j * tn, SUBLANES), 0))],
        out_specs=pl.BlockSpec((tm, tn), lambda j, i: (i, j)),
        scratch_shapes=[pltpu.VMEM((tn, k), BF16)],
        compiler_params=_cparams(("arbitrary", "arbitrary")),
        name=name,
    )(a, wt)


def _attn_kernel(sinks_ref, q_ref, kc_ref, kp_ref, vc_ref, vp_ref,
                 cc_ref, sc_ref, cp_ref, sp_ref, o_ref):
    n = pl.program_id(1)
    blk = ATTN_BLOCK
    lane = lax.broadcasted_iota(jnp.int32, (blk, LANES), 1)
    first_half = (lane % HEAD_DIM) < (HEAD_DIM // 2)
    low_head = lane < HEAD_DIM

    def rope(t, c, s):
        partner = jnp.where(first_half,
                            pltpu.roll(t, LANES - HEAD_DIM // 2, 1),
                            pltpu.roll(t, HEAD_DIM // 2, 1))
        return t * c + partner * s

    cc, sc, cp, sp = cc_ref[...], sc_ref[...], cp_ref[...], sp_ref[...]

    qi = lax.broadcasted_iota(jnp.int32, (blk, 2 * blk), 0)
    kj = lax.broadcasted_iota(jnp.int32, (blk, 2 * blk), 1)
    rel = kj - qi
    valid = (rel >= 1) & (rel <= blk) & ((kj >= blk) | (n > 0))

    scale = HEAD_DIM ** -0.5
    n_half = KV_DIM // LANES
    rows_per_half = 2 * GROUP * blk
    s_rows, sink_rows, v_wins = [], [], []
    for a in range(n_half):
        ksl = slice(a * LANES, (a + 1) * LANES)
        k_win = jnp.concatenate(
            [rope(kp_ref[:, ksl].astype(F32), cp, sp),
             rope(kc_ref[:, ksl].astype(F32), cc, sc)], axis=0).astype(BF16)
        v_wins.append(jnp.concatenate([vp_ref[:, ksl], vc_ref[:, ksl]], axis=0))
        qc = [rope(q_ref[:, (a * GROUP + g) * LANES:(a * GROUP + g + 1) * LANES].astype(F32), cc, sc)
              * scale for g in range(GROUP)]
        q_rows = []
        for sub in range(2):
            keep = low_head if sub == 0 else jnp.logical_not(low_head)
            for g in range(GROUP):
                q_rows.append(jnp.where(keep, qc[g], 0.0).astype(BF16))
                sink_rows.append(
                    jnp.full((blk, LANES), sinks_ref[0, (2 * a + sub) * GROUP + g], F32))
        s_rows.append(_dot_nt(jnp.concatenate(q_rows, axis=0), k_win))
    valid_all = jnp.concatenate([valid] * (n_half * 2 * GROUP), axis=0)
    s = jnp.where(valid_all, jnp.concatenate(s_rows, axis=0), NEG_BIG)
    sink = jnp.concatenate(sink_rows, axis=0)
    s0, s1 = s[:, :LANES], s[:, LANES:]
    m = jnp.maximum(jnp.max(jnp.maximum(s0, s1), axis=-1, keepdims=True), sink)
    e0 = jnp.exp(s0 - m)
    e1 = jnp.exp(s1 - m)
    den = jnp.sum(e0 + e1, axis=-1, keepdims=True) + jnp.exp(sink - m)
    r = 1.0 / den
    p = jnp.concatenate([e0 * r, e1 * r], axis=1).astype(BF16)
    for a in range(n_half):
        pv = _dot(p[a * rows_per_half:(a + 1) * rows_per_half], v_wins[a])
        for g in range(GROUP):
            lo = pv[g * blk:(g + 1) * blk]
            hi = pv[(GROUP + g) * blk:(GROUP + g + 1) * blk]
            chunk = a * GROUP + g
            o_ref[:, chunk * LANES:(chunk + 1) * LANES] = (
                jnp.where(low_head, lo, hi).astype(o_ref.dtype))


def _attention(q, kv, cos_t, sin_t, sinks, batch, seq):
    t = q.shape[0]
    nb = seq // ATTN_BLOCK
    kcol, vcol = 0, 1
    cur = lambda b, n: b * nb + n
    prev = lambda b, n: jnp.maximum(b * nb + n - 1, 0)
    return pl.pallas_call(
        _attn_kernel,
        out_shape=jax.ShapeDtypeStruct((t, Q_DIM), BF16),
        grid=(batch, nb),
        in_specs=[pl.BlockSpec(memory_space=pltpu.SMEM),
                  pl.BlockSpec((ATTN_BLOCK, Q_DIM), lambda b, n: (cur(b, n), 0)),
                  pl.BlockSpec((ATTN_BLOCK, KV_DIM), lambda b, n: (cur(b, n), kcol)),
                  pl.BlockSpec((ATTN_BLOCK, KV_DIM), lambda b, n: (prev(b, n), kcol)),
                  pl.BlockSpec((ATTN_BLOCK, KV_DIM), lambda b, n: (cur(b, n), vcol)),
                  pl.BlockSpec((ATTN_BLOCK, KV_DIM), lambda b, n: (prev(b, n), vcol)),
                  pl.BlockSpec((ATTN_BLOCK, LANES), lambda b, n: (cur(b, n), 0)),
                  pl.BlockSpec((ATTN_BLOCK, LANES), lambda b, n: (cur(b, n), 0)),
                  pl.BlockSpec((ATTN_BLOCK, LANES), lambda b, n: (prev(b, n), 0)),
                  pl.BlockSpec((ATTN_BLOCK, LANES), lambda b, n: (prev(b, n), 0))],
        out_specs=pl.BlockSpec((ATTN_BLOCK, Q_DIM), lambda b, n: (cur(b, n), 0)),
        compiler_params=_cparams(("arbitrary", "arbitrary")),
        name="attention",
    )(sinks, q, kv, kv, kv, kv, cos_t, sin_t, cos_t, sin_t)


def _ssd_kernel(xbc_ref, xprev_ref, z_ref, dt_ref, shift_ref, cw_ref, cb_ref, dtb_ref,
                alog_ref, dskip_ref, g_ref, expand_ref, o_ref, state_ref, y_ref):
    c = pl.program_id(1)
    L = CHUNK

    @pl.when(c == 0)
    def _():
        state_ref[...] = jnp.zeros_like(state_ref)

    sel_col = lax.broadcasted_iota(jnp.int32, (CONV_WIDTH * L, 2 * L), 1)
    sel = shift_ref[...]
    sel = jnp.where((sel_col < L) & (c == 0), jnp.zeros_like(sel), sel)
    xwin = jnp.concatenate([xprev_ref[...], xbc_ref[...]], axis=0)
    taps = _dot(sel, xwin)
    acc = cb_ref[...] + taps[0:L] * cw_ref[0:1, :]
    for w in range(1, CONV_WIDTH):
        acc = acc + taps[w * L:(w + 1) * L] * cw_ref[w:w + 1, :]
    xc = _silu(acc)
    xs = xc[:, :D_INNER]
    bm = xc[:, D_INNER:D_INNER + SSM_GROUPS * D_STATE]
    cm = xc[:, D_INNER + SSM_GROUPS * D_STATE:]

    v = dt_ref[...] + dtb_ref[...]
    dt = jnp.maximum(v, 0.0) + jnp.log1p(jnp.exp(-jnp.abs(v)))
    a = dt * (-jnp.exp(alog_ref[...]))

    ri = lax.broadcasted_iota(jnp.int32, (L, L), 0)
    ci = lax.broadcasted_iota(jnp.int32, (L, L), 1)
    causal = ri >= ci
    tril = jnp.where(causal, 1.0, 0.0).astype(BF16)
    ones = jnp.ones((L, L), BF16)
    a1, a2, a3 = _split3(a)
    cs = _dot(tril, a1) + _dot(tril, a2) + _dot(tril, a3)
    tot = _dot(ones, a1) + _dot(ones, a2) + _dot(ones, a3)
    cs_t = cs.T

    lane = lax.broadcasted_iota(jnp.int32, (L, LANES), 1)

    def expand(q):
        q1, q2, q3 = _split3(q)
        packed = jnp.where(
            lane < SSM_HEADS, q1.astype(F32),
            jnp.where(lane < 2 * SSM_HEADS, pltpu.roll(q2.astype(F32), SSM_HEADS, 1),
                      jnp.where(lane < 3 * SSM_HEADS,
                                pltpu.roll(q3.astype(F32), 2 * SSM_HEADS, 1), 0.0)))
        return _dot(packed.astype(BF16), expand_ref[...])

    dtx = expand(dt)
    csx = expand(cs)
    totx = expand(tot)
    xd = xs * dtx
    xdd = (xd * jnp.exp(totx - csx)).astype(BF16)
    xd16 = xd.astype(BF16)
    chunk_decay = jnp.exp(totx)

    low_head = lane < SSM_HEAD_DIM
    gw = HEADS_PER_GROUP * SSM_HEAD_DIM

    for g in range(SSM_GROUPS):
        bg = bm[:, g * D_STATE:(g + 1) * D_STATE]
        cg = cm[:, g * D_STATE:(g + 1) * D_STATE].astype(BF16)
        bg_t = bg.T.astype(BF16)
        cb = _dot_nt(cg, bg.astype(BF16))
        gsl = slice(g * gw, (g + 1) * gw)
        st_prev = state_ref[:, gsl]
        y_off = _dot(cg, st_prev.astype(BF16))
        state_ref[:, gsl] = st_prev * chunk_decay[:, gsl] + _dot(bg_t, xdd[:, gsl])
        y_ref[:, gsl] = y_off
        for pair in range(HEADS_PER_GROUP // 2):
            psl = slice(g * gw + pair * LANES, g * gw + (pair + 1) * LANES)
            xp = xd16[:, psl]
            y_pair = None
            for sub in range(2):
                h = g * HEADS_PER_GROUP + pair * 2 + sub
                diff = cs[:, h:h + 1] - cs_t[h:h + 1, :]
                decay = jnp.exp(jnp.where(causal, diff, NEG_BIG))
                mh = (cb * decay).astype(BF16)
                keep = low_head if sub == 0 else jnp.logical_not(low_head)
                part = _dot(mh, jnp.where(keep, xp, jnp.zeros_like(xp)))
                y_pair = part if y_pair is None else y_pair + part
            y_ref[:, psl] = y_ref[:, psl] * jnp.exp(csx[:, psl]) + y_pair

    y = y_ref[...] + dskip_ref[...] * xs
    yz = y * _silu(z_ref[...].astype(F32))
    o_ref[...] = _rms(yz, g_ref[...], SSM_NORM_EPS).astype(o_ref.dtype)


def _ssd(xbc, z, dtp, shift, conv_w, conv_b, dtb_p, alog_p, dskip_x, g_ssd, expand, batch, seq):
    t = xbc.shape[0]
    nc = seq // CHUNK
    row = lambda b, c: (b * nc + c, 0)
    prev = lambda b, c: (jnp.maximum(b * nc + c - 1, 0), 0)
    return pl.pallas_call(
        _ssd_kernel,
        out_shape=jax.ShapeDtypeStruct((t, D_INNER), BF16),
        grid=(batch, nc),
        in_specs=[pl.BlockSpec((CHUNK, CONV_DIM), row),
                  pl.BlockSpec((CHUNK, CONV_DIM), prev),
                  pl.BlockSpec((CHUNK, D_INNER), row),
                  pl.BlockSpec((CHUNK, LANES), row),
                  _resident((CONV_WIDTH * CHUNK, 2 * CHUNK)),
                  _resident((CONV_WIDTH, CONV_DIM)),
                  _resident((1, CONV_DIM)),
                  _resident((1, LANES)),
                  _resident((1, LANES)),
                  _resident((1, D_INNER)),
                  _resident((1, D_INNER)),
                  _resident((LANES, D_INNER))],
        out_specs=pl.BlockSpec((CHUNK, D_INNER), row),
        scratch_shapes=[pltpu.VMEM((D_STATE, D_INNER), F32),
                        pltpu.VMEM((CHUNK, D_INNER), F32)],
        compiler_params=_cparams(("arbitrary", "arbitrary")),
        name="ssd",
    )(xbc, xbc, z, dtp, shift, conv_w, conv_b, dtb_p, alog_p, dskip_x, g_ssd, expand)


def _merge_kernel(attn_ref, y_ref, ga_ref, gs_ref, wa_ref, ws_ref, o_ref):
    out_a = _dot(attn_ref[...], wa_ref[...])
    out_s = _dot(y_ref[...], ws_ref[...])
    merged = (_sigmoid(ga_ref[...].astype(F32)) * out_a
              + _sigmoid(gs_ref[...].astype(F32)) * out_s)
    o_ref[...] = merged.astype(o_ref.dtype)


def _merge(attn, y, gates, wa, ws, tm=512):
    t = attn.shape[0]
    row = lambda i: (i, 0)
    return pl.pallas_call(
        _merge_kernel,
        out_shape=jax.ShapeDtypeStruct((t, D_MODEL), BF16),
        grid=(t // tm,),
        in_specs=[pl.BlockSpec((tm, Q_DIM), row),
                  pl.BlockSpec((tm, D_INNER), row),
                  pl.BlockSpec((tm, D_MODEL), lambda i: (i, 0)),
                  pl.BlockSpec((tm, D_MODEL), lambda i: (i, 1)),
                  _resident((Q_DIM, D_MODEL)),
                  _resident((D_INNER, D_MODEL))],
        out_specs=pl.BlockSpec((tm, D_MODEL), row),
        compiler_params=_cparams(("arbitrary",)),
        name="merge",
    )(attn, y, gates, gates, wa, ws)


def _oproj_kernel(m_ref, x_ref, w_ref, g_ref, h_ref, f_ref):
    h = x_ref[...] + _dot(m_ref[...], w_ref[...])
    h_ref[...] = h
    f_ref[...] = _rms(h, g_ref[...], NORM_EPS).astype(BF16)


def _oproj(merged, x2, wo, g_ffn, tm=512):
    t = merged.shape[0]
    row = lambda i: (i, 0)
    return pl.pallas_call(
        _oproj_kernel,
        out_shape=(jax.ShapeDtypeStruct((t, D_MODEL), F32),
                   jax.ShapeDtypeStruct((t, D_MODEL), BF16)),
        grid=(t // tm,),
        in_specs=[pl.BlockSpec((tm, D_MODEL), row),
                  pl.BlockSpec((tm, D_MODEL), row),
                  _resident((D_MODEL, D_MODEL)),
                  _resident((1, D_MODEL))],
        out_specs=(pl.BlockSpec((tm, D_MODEL), row),
                   pl.BlockSpec((tm, D_MODEL), row)),
        compiler_params=_cparams(("arbitrary",)),
        name="oproj",
    )(merged, x2, wo, g_ffn)


def _ffn_up_kernel(f_ref, wg_ref, wu_ref, o_ref, wg16_ref, wu16_ref):
    @pl.when(pl.program_id(1) == 0)
    def _():
        wg16_ref[...] = wg_ref[...].astype(BF16)
        wu16_ref[...] = wu_ref[...].astype(BF16)

    f = f_ref[...]
    o_ref[...] = (_silu(_dot(f, wg16_ref[...])) * _dot(f, wu16_ref[...])).astype(o_ref.dtype)


def _ffn_up(f, wg, wu, tm=1024, th=512):
    t = f.shape[0]
    return pl.pallas_call(
        _ffn_up_kernel,
        out_shape=jax.ShapeDtypeStruct((t, FFN_HIDDEN), BF16),
        grid=(FFN_HIDDEN // th, t // tm),
        in_specs=[pl.BlockSpec((tm, D_MODEL), lambda j, i: (i, 0)),
                  pl.BlockSpec((D_MODEL, th), lambda j, i: (0, j)),
                  pl.BlockSpec((D_MODEL, th), lambda j, i: (0, j))],
        out_specs=pl.BlockSpec((tm, th), lambda j, i: (i, j)),
        scratch_shapes=[pltpu.VMEM((D_MODEL, th), BF16),
                        pltpu.VMEM((D_MODEL, th), BF16)],
        compiler_params=_cparams(("arbitrary", "arbitrary")),
        name="ffn_up",
    )(f, wg, wu)


def _ffn_down_kernel(a_ref, h_ref, wd_ref, o_ref, wd16_ref):
    @pl.when(pl.program_id(1) == 0)
    def _():
        wd16_ref[...] = wd_ref[...].astype(BF16)

    o_ref[...] = h_ref[...] + _dot(a_ref[...], wd16_ref[...])


def _ffn_down(act, h1, wd, tm=1024, tn=512):
    t = act.shape[0]
    return pl.pallas_call(
        _ffn_down_kernel,
        out_shape=jax.ShapeDtypeStruct((t, D_MODEL), F32),
        grid=(D_MODEL // tn, t // tm),
        in_specs=[pl.BlockSpec((tm, FFN_HIDDEN), lambda j, i: (i, 0)),
                  pl.BlockSpec((tm, tn), lambda j, i: (i, j)),
                  pl.BlockSpec((FFN_HIDDEN, tn), lambda j, i: (0, j),
                               pipeline_mode=pl.Buffered(1))],
        out_specs=pl.BlockSpec((tm, tn), lambda j, i: (i, j)),
        scratch_shapes=[pltpu.VMEM((FFN_HIDDEN, tn), BF16)],
        compiler_params=_cparams(("arbitrary", "arbitrary")),
        name="ffn_down",
    )(act, h1, wd)


def _ple_kernel(h_ref, p_ref, gp_ref, gf_ref, wg_ref, wp_ref, o_ref):
    h = h_ref[...]
    r = _rms(h, gp_ref[...], NORM_EPS).astype(BF16)
    gate = _sigmoid(_dot(r, wg_ref[...]))
    h3 = h + gate * _dot(p_ref[...].astype(BF16), wp_ref[...])
    o_ref[...] = _rms(h3, gf_ref[...], NORM_EPS)


def _ple(h2, p2, g_ple, g_final, wpg, wpp, tm=512):
    t = h2.shape[0]
    row = lambda i: (i, 0)
    return pl.pallas_call(
        _ple_kernel,
        out_shape=jax.ShapeDtypeStruct((t, D_MODEL), F32),
        grid=(t // tm,),
        in_specs=[pl.BlockSpec((tm, D_MODEL), row),
                  pl.BlockSpec((tm, PLE_DIM), row),
                  _resident((1, D_MODEL)),
                  _resident((1, D_MODEL)),
                  _resident((D_MODEL, D_MODEL)),
                  _resident((PLE_DIM, D_MODEL))],
        out_specs=pl.BlockSpec((tm, D_MODEL), row),
        compiler_params=_cparams(("arbitrary",)),
        name="ple",
    )(h2, p2, g_ple, g_final, wpg, wpp)


def _rope_constants():
    half = HEAD_DIM // 2
    lane = np.arange(LANES)
    inv_freq = ROPE_THETA ** (-jnp.arange(half, dtype=F32) * 2.0 / HEAD_DIM)
    invf = inv_freq[lane % half][None, :]
    sgn = jnp.asarray(np.where((lane % HEAD_DIM) < half, -1.0, 1.0), F32)[None, :]
    return invf, sgn


def _expand_matrix():
    rows = np.arange(LANES)[:, None]
    cols = np.arange(D_INNER)[None, :]
    hit = (rows < 3 * SSM_HEADS) & ((rows % SSM_HEADS) == (cols // SSM_HEAD_DIM))
    return jnp.asarray(hit, BF16)


def _shift_matrix():
    rows = np.arange(CONV_WIDTH * CHUNK)[:, None]
    cols = np.arange(2 * CHUNK)[None, :]
    w, t = rows // CHUNK, rows % CHUNK
    return jnp.asarray(cols == CHUNK + t - (CONV_WIDTH - 1) + w, BF16)


def _pair_heads(w):
    rest = w.shape[1:]
    w = w.reshape(KV_HEADS // 2, 2, GROUP, HEAD_DIM, *rest)
    return jnp.swapaxes(w, 1, 2).reshape(Q_DIM, *rest)


def _pad_lanes(v):
    return jnp.pad(v.astype(F32), (0, LANES - v.shape[0]))[None, :]


def kernel(x, p, positions, g_mix, w_in, conv_w, conv_b, dt_bias, a_log, d_skip, g_ssd,
           sinks, w_attn_br, w_ssd_br, w_o, g_ffn, w_gate, w_up, w_down, g_ple,
           w_ple_gate, w_ple_proj, g_final):
    b, s, d = x.shape
    t = b * s
    assert d == D_MODEL and s % CHUNK == 0 and p.shape[0] == 1
    i = 0
    x2 = x.reshape(t, d)
    p2 = p[i].reshape(t, PLE_DIM)
    pos2 = positions.reshape(t, 1)

    o_k = Q_DIM
    o_z = Q_DIM + 2 * KV_DIM
    o_xbc = o_z + D_INNER
    o_dt = o_xbc + CONV_DIM
    o_ga = o_dt + SSM_HEADS
    wt = jnp.swapaxes(w_in[i], 0, 1)

    invf, sgn = _rope_constants()
    u, cos_t, sin_t, dtp = _prep(x2, g_mix[i][None, :], pos2, invf, sgn, wt, o_dt)

    q = _inproj(u, _pair_heads(wt[:Q_DIM]), 0, Q_DIM, BF16, 1024, 1024, "proj_q")
    kv = _inproj(u, wt, o_k, 2 * KV_DIM, BF16, 1024, 2 * KV_DIM, "proj_kv")
    z = _inproj(u, wt, o_z, D_INNER, BF16, 1024, 1024, "proj_z")
    xbc = _inproj(u, wt, o_xbc, CONV_DIM, BF16, 1024, 1024, "proj_xbc")
    gates = _inproj(u, wt, o_ga, 2 * D_MODEL, BF16, 1024, 1024, "proj_gates")

    attn = _attention(q, kv, cos_t, sin_t, sinks[i][None, :].astype(F32), b, s)

    y = _ssd(xbc, z, dtp, _shift_matrix(), conv_w[i], conv_b[i][None, :], _pad_lanes(dt_bias[i]),
             _pad_lanes(a_log[i]), jnp.repeat(d_skip[i].astype(F32), SSM_HEAD_DIM)[None, :],
             g_ssd[i][None, :], _expand_matrix(), b, s)

    merged = _merge(attn, y, gates, _pair_heads(w_attn_br[i]).astype(BF16),
                    w_ssd_br[i].astype(BF16))
    h1, f = _oproj(merged, x2, w_o[i].astype(BF16), g_ffn[i][None, :])
    act = _ffn_up(f, w_gate[i], w_up[i])
    h2 = _ffn_down(act, h1, w_down[i])
    out = _ple(h2, p2, g_ple[i][None, :], g_final[None, :], w_ple_gate[i].astype(BF16),
               w_ple_proj[i].astype(BF16))
    return out.reshape(b, s, d)
```

```python
from functools import partial

import numpy as np
import jax
import jax.numpy as jnp
from jax import lax
from jax.experimental import pallas as pl
from jax.experimental.pallas import tpu as pltpu

F32 = jnp.float32
BF16 = jnp.bfloat16

D_MODEL = 2048
HEAD_DIM = 64
ATTN_HEADS = 16
KV_HEADS = 4
GROUP = ATTN_HEADS // KV_HEADS
Q_DIM = ATTN_HEADS * HEAD_DIM
KV_DIM = KV_HEADS * HEAD_DIM
ATTN_BLOCK = 128
ROPE_THETA = 10000.0
D_INNER = 2048
SSM_HEAD_DIM = 64
SSM_HEADS = 32
SSM_GROUPS = 4
HEADS_PER_GROUP = SSM_HEADS // SSM_GROUPS
D_STATE = 128
CONV_WIDTH = 4
CHUNK = 128
CONV_DIM = D_INNER + 2 * SSM_GROUPS * D_STATE
FFN_HIDDEN = 5632
PLE_DIM = 256
NORM_EPS = 1e-6
SSM_NORM_EPS = 1e-5

LANES = 128
SUBLANES = 8
VMEM_LIMIT_BYTES = 56 * 1024 * 1024

NEG_BIG = -1e30


def _cparams(semantics):
    return pltpu.CompilerParams(dimension_semantics=semantics,
                                vmem_limit_bytes=VMEM_LIMIT_BYTES)


def _resident(shape):
    return pl.BlockSpec(shape, lambda *_: (0,) * len(shape),
                        pipeline_mode=pl.Buffered(1))


def _rms(xf, g, eps):
    var = jnp.mean(xf * xf, axis=-1, keepdims=True)
    return xf * lax.rsqrt(var + eps) * g


NEG_LOG2E = -1.4426950408889634


def _sigmoid(x):
    return 1.0 / (1.0 + jnp.exp2(x * NEG_LOG2E))


def _silu(x):
    return x * _sigmoid(x)


def _dot(a, b):
    return jnp.dot(a, b, preferred_element_type=F32)


def _dot_nt(a, b):
    return lax.dot_general(a, b, (((1,), (1,)), ((), ())),
                           preferred_element_type=F32)


def _split3(v):
    v1 = v.astype(BF16)
    r1 = v - v1.astype(F32)
    v2 = r1.astype(BF16)
    v3 = (r1 - v2.astype(F32)).astype(BF16)
    return v1, v2, v3


def _prep_kernel(x_ref, g_ref, pos_ref, invf_ref, sgn_ref, wdt_ref, u_ref, cos_ref, sin_ref, dt_ref):
    u = _rms(x_ref[...], g_ref[...], NORM_EPS).astype(BF16)
    u_ref[...] = u
    ang = pos_ref[...].astype(F32) * invf_ref[...]
    cos_ref[...] = jnp.cos(ang)
    sin_ref[...] = jnp.sin(ang) * sgn_ref[...]
    dt_ref[...] = _dot_nt(u, wdt_ref[...].astype(BF16))


def _prep(x2, g_mix, pos2, invf, sgn, wt, dt_row0, tm=512):
    t = x2.shape[0]
    row = lambda i: (i, 0)
    assert dt_row0 % SUBLANES == 0
    return pl.pallas_call(
        _prep_kernel,
        out_shape=(jax.ShapeDtypeStruct((t, D_MODEL), BF16),
                   jax.ShapeDtypeStruct((t, LANES), F32),
                   jax.ShapeDtypeStruct((t, LANES), F32),
                   jax.ShapeDtypeStruct((t, LANES), F32)),
        grid=(t // tm,),
        in_specs=[pl.BlockSpec((tm, D_MODEL), row),
                  _resident((1, D_MODEL)),
                  pl.BlockSpec((tm, 1), row),
                  _resident((1, LANES)),
                  _resident((1, LANES)),
                  pl.BlockSpec((pl.Element(LANES), pl.Element(D_MODEL)),
                               lambda i: (dt_row0, 0), pipeline_mode=pl.Buffered(1))],
        out_specs=(pl.BlockSpec((tm, D_MODEL), row),
                   pl.BlockSpec((tm, LANES), row),
                   pl.BlockSpec((tm, LANES), row),
                   pl.BlockSpec((tm, LANES), row)),
        compiler_params=_cparams(("arbitrary",)),
        name="prep",
    )(x2, g_mix, pos2, invf, sgn, wt)


def _inproj_kernel(a_ref, wt_ref, o_ref, wbf_ref):
    @pl.when(pl.program_id(1) == 0)
    def _():
        wbf_ref[...] = wt_ref[...].astype(BF16)

    o_ref[...] = _dot_nt(a_ref[...], wbf_ref[...]).astype(o_ref.dtype)


def _inproj(a, wt, row0, n, out_dtype, tm, tn, name):
    m, k = a.shape
    assert n % tn == 0 and row0 % SUBLANES == 0
    return pl.pallas_call(
        _inproj_kernel,
        out_shape=jax.ShapeDtypeStruct((m, n), out_dtype),
        grid=(n // tn, m // tm),
        in_specs=[pl.BlockSpec((tm, k), lambda j, i: (i, 0)),
                  pl.BlockSpec((pl.Element(tn), pl.Element(k)),
                               lambda j, i: (pl.multiple_of(row0 + j * tn, SUBLANES), 0))],
        out_specs=pl.BlockSpec((tm, tn), lambda j, i: (i, j)),
        scratch_shapes=[pltpu.VMEM((tn, k), BF16)],
        compiler_params=_cparams(("arbitrary", "arbitrary")),
        name=name,
    )(a, wt)


def _attn_kernel(sinks_ref, q_ref, kc_ref, kp_ref, vc_ref, vp_ref,
                 cc_ref, sc_ref, cp_ref, sp_ref, o_ref):
    n = pl.program_id(1)
    blk = ATTN_BLOCK
    lane = lax.broadcasted_iota(jnp.int32, (blk, LANES), 1)
    first_half = (lane % HEAD_DIM) < (HEAD_DIM // 2)
    low_head = lane < HEAD_DIM

    def rope(t, c, s):
        partner = jnp.where(first_half,
                            pltpu.roll(t, LANES - HEAD_DIM // 2, 1),
                            pltpu.roll(t, HEAD_DIM // 2, 1))
        return t * c + partner * s

    cc, sc, cp, sp = cc_ref[...], sc_ref[...], cp_ref[...], sp_ref[...]

    qi = lax.broadcasted_iota(jnp.int32, (blk, 2 * blk), 0)
    kj = lax.broadcasted_iota(jnp.int32, (blk, 2 * blk), 1)
    rel = kj - qi
    valid = (rel >= 1) & (rel <= blk) & ((kj >= blk) | (n > 0))

    scale = HEAD_DIM ** -0.5
    n_half = KV_DIM // LANES
    rows_per_half = 2 * GROUP * blk
    s_rows, sink_rows, v_wins = [], [], []
    for a in range(n_half):
        ksl = slice(a * LANES, (a + 1) * LANES)
        k_win = jnp.concatenate(
            [rope(kp_ref[:, ksl].astype(F32), cp, sp),
             rope(kc_ref[:, ksl].astype(F32), cc, sc)], axis=0).astype(BF16)
        v_wins.append(jnp.concatenate([vp_ref[:, ksl], vc_ref[:, ksl]], axis=0))
        qc = [rope(q_ref[:, (a * GROUP + g) * LANES:(a * GROUP + g + 1) * LANES].astype(F32), cc, sc)
              * scale for g in range(GROUP)]
        q_rows = []
        for sub in range(2):
            keep = low_head if sub == 0 else jnp.logical_not(low_head)
            for g in range(GROUP):
                q_rows.append(jnp.where(keep, qc[g], 0.0).astype(BF16))
                sink_rows.append(
                    jnp.full((blk, LANES), sinks_ref[0, (2 * a + sub) * GROUP + g], F32))
        s_rows.append(_dot_nt(jnp.concatenate(q_rows, axis=0), k_win))
    valid_all = jnp.concatenate([valid] * (n_half * 2 * GROUP), axis=0)
    s = jnp.where(valid_all, jnp.concatenate(s_rows, axis=0), NEG_BIG)
    sink = jnp.concatenate(sink_rows, axis=0)
    s0, s1 = s[:, :LANES], s[:, LANES:]
    m = jnp.maximum(jnp.max(jnp.maximum(s0, s1), axis=-1, keepdims=True), sink)
    e0 = jnp.exp(s0 - m)
    e1 = jnp.exp(s1 - m)
    den = jnp.sum(e0 + e1, axis=-1, keepdims=True) + jnp.exp(sink - m)
    r = 1.0 / den
    p = jnp.concatenate([e0 * r, e1 * r], axis=1).astype(BF16)
    for a in range(n_half):
        pv = _dot(p[a * rows_per_half:(a + 1) * rows_per_half], v_wins[a])
        for g in range(GROUP):
            lo = pv[g * blk:(g + 1) * blk]
            hi = pv[(GROUP + g) * blk:(GROUP + g + 1) * blk]
            chunk = a * GROUP + g
            o_ref[:, chunk * LANES:(chunk + 1) * LANES] = (
                jnp.where(low_head, lo, hi).astype(o_ref.dtype))


def _attention(q, kv, cos_t, sin_t, sinks, batch, seq):
    t = q.shape[0]
    nb = seq // ATTN_BLOCK
    kcol, vcol = 0, 1
    cur = lambda b, n: b * nb + n
    prev = lambda b, n: jnp.maximum(b * nb + n - 1, 0)
    return pl.pallas_call(
        _attn_kernel,
        out_shape=jax.ShapeDtypeStruct((t, Q_DIM), BF16),
        grid=(batch, nb),
        in_specs=[pl.BlockSpec(memory_space=pltpu.SMEM),
                  pl.BlockSpec((ATTN_BLOCK, Q_DIM), lambda b, n: (cur(b, n), 0)),
                  pl.BlockSpec((ATTN_BLOCK, KV_DIM), lambda b, n: (cur(b, n), kcol)),
                  pl.BlockSpec((ATTN_BLOCK, KV_DIM), lambda b, n: (prev(b, n), kcol)),
                  pl.BlockSpec((ATTN_BLOCK, KV_DIM), lambda b, n: (cur(b, n), vcol)),
                  pl.BlockSpec((ATTN_BLOCK, KV_DIM), lambda b, n: (prev(b, n), vcol)),
                  pl.BlockSpec((ATTN_BLOCK, LANES), lambda b, n: (cur(b, n), 0)),
                  pl.BlockSpec((ATTN_BLOCK, LANES), lambda b, n: (cur(b, n), 0)),
                  pl.BlockSpec((ATTN_BLOCK, LANES), lambda b, n: (prev(b, n), 0)),
                  pl.BlockSpec((ATTN_BLOCK, LANES), lambda b, n: (prev(b, n), 0))],
        out_specs=pl.BlockSpec((ATTN_BLOCK, Q_DIM), lambda b, n: (cur(b, n), 0)),
        compiler_params=_cparams(("arbitrary", "arbitrary")),
        name="attention",
    )(sinks, q, kv, kv, kv, kv, cos_t, sin_t, cos_t, sin_t)


SSD_STRIP = 256


def _ssd_kernel(xbc_ref, xprev_ref, z_ref, dt_ref, shift_ref, cw_ref, cb_ref, dtb_ref,
                alog_ref, dskip_ref, g_ref, expand_ref, o_ref,
                state_ref, xs_ref, b16_ref, c16_ref, xdd_ref, xd16_ref, yz_ref):
    c = pl.program_id(1)
    L = CHUNK
    W = SSD_STRIP
    bc_dim = SSM_GROUPS * D_STATE

    @pl.when(c == 0)
    def _():
        state_ref[...] = jnp.zeros_like(state_ref)

    n_shift = CONV_WIDTH - 1
    sel_col = lax.broadcasted_iota(jnp.int32, (n_shift * L, 2 * L), 1)
    sel = shift_ref[...]
    sel = jnp.where((sel_col < L) & (c == 0), jnp.zeros_like(sel), sel)
    for s in range(CONV_DIM // W):
        sl = slice(s * W, (s + 1) * W)
        x_cur = xbc_ref[:, sl]
        xwin = jnp.concatenate([xprev_ref[:, sl], x_cur], axis=0)
        taps = _dot(sel, xwin)
        acc = cb_ref[:, sl] + x_cur.astype(F32) * cw_ref[n_shift:CONV_WIDTH, sl]
        for w in range(n_shift):
            acc = acc + taps[w * L:(w + 1) * L] * cw_ref[w:w + 1, sl]
        xc = _silu(acc)
        if s * W < D_INNER:
            xs_ref[:, sl] = xc
        elif s * W < D_INNER + bc_dim:
            b16_ref[:, s * W - D_INNER:(s + 1) * W - D_INNER] = xc.astype(BF16)
        else:
            c16_ref[:, s * W - D_INNER - bc_dim:(s + 1) * W - D_INNER - bc_dim] = xc.astype(BF16)

    v = dt_ref[...] + dtb_ref[...]
    dt = jnp.maximum(v, 0.0) + jnp.log1p(jnp.exp(-jnp.abs(v)))
    a = dt * (jnp.exp(alog_ref[...]) * NEG_LOG2E)

    ri = lax.broadcasted_iota(jnp.int32, (L, L), 0)
    ci = lax.broadcasted_iota(jnp.int32, (L, L), 1)
    causal = ri >= ci
    tril = jnp.where(causal, 1.0, 0.0).astype(BF16)
    ones = jnp.ones((L, L), BF16)
    a1, a2, a3 = _split3(a)
    cs = _dot(tril, a1) + _dot(tril, a2) + _dot(tril, a3)
    tot = _dot(ones, a1) + _dot(ones, a2) + _dot(ones, a3)
    cs_t = cs.T

    lane = lax.broadcasted_iota(jnp.int32, (L, LANES), 1)

    def pack(q):
        q1, q2, q3 = _split3(q)
        packed = jnp.where(
            lane < SSM_HEADS, q1.astype(F32),
            jnp.where(lane < 2 * SSM_HEADS, pltpu.roll(q2.astype(F32), SSM_HEADS, 1),
                      jnp.where(lane < 3 * SSM_HEADS,
                                pltpu.roll(q3.astype(F32), 2 * SSM_HEADS, 1), 0.0)))
        return packed.astype(BF16)

    pk_all = jnp.concatenate([pack(dt), pack(cs), pack(tot)], axis=0)
    pk_ct = pk_all[L:]

    for s in range(D_INNER // W):
        sl = slice(s * W, (s + 1) * W)
        ex = _dot(pk_all, expand_ref[:, sl])
        xd = xs_ref[:, sl] * ex[0:L]
        xd16_ref[:, sl] = xd.astype(BF16)
        xdd_ref[:, sl] = (xd * jnp.exp2(ex[2 * L:3 * L] - ex[L:2 * L])).astype(BF16)

    low_head = lane < SSM_HEAD_DIM
    heads_per_strip = W // SSM_HEAD_DIM
    strips_per_group = HEADS_PER_GROUP // heads_per_strip
    ssq = jnp.zeros((L, LANES), F32)

    for g in range(SSM_GROUPS):
        nsl = slice(g * D_STATE, (g + 1) * D_STATE)
        bg = b16_ref[:, nsl]
        cg = c16_ref[:, nsl]
        bg_t = bg.astype(F32).T.astype(BF16)
        cb = _dot_nt(cg, bg)
        for k in range(strips_per_group):
            s = g * strips_per_group + k
            sl = slice(s * W, (s + 1) * W)
            ex = _dot(pk_ct, expand_ref[:, sl])
            st_prev = state_ref[:, sl]
            y_off = _dot(cg, st_prev.astype(BF16))
            state_ref[:, sl] = st_prev * jnp.exp2(ex[L:2 * L]) + _dot(bg_t, xdd_ref[:, sl])
            y_pairs = []
            for pair in range(W // LANES):
                psl = slice(s * W + pair * LANES, s * W + (pair + 1) * LANES)
                xp = xd16_ref[:, psl]
                y_pair = None
                for sub in range(2):
                    h = s * heads_per_strip + pair * 2 + sub
                    diff = cs[:, h:h + 1] - cs_t[h:h + 1, :]
                    decay = jnp.exp2(jnp.where(causal, diff, NEG_BIG))
                    mh = (cb * decay).astype(BF16)
                    keep = low_head if sub == 0 else jnp.logical_not(low_head)
                    part = _dot(mh, jnp.where(keep, xp, jnp.zeros_like(xp)))
                    y_pair = part if y_pair is None else y_pair + part
                y_pairs.append(y_pair)
            y = (jnp.concatenate(y_pairs, axis=1) + y_off * jnp.exp2(ex[0:L])
                 + dskip_ref[:, sl] * xs_ref[:, sl])
            yz = y * _silu(z_ref[:, sl].astype(F32))
            yz_ref[:, sl] = yz
            for pair in range(W // LANES):
                part = yz[:, pair * LANES:(pair + 1) * LANES]
                ssq = ssq + part * part

    var = jnp.sum(ssq, axis=-1, keepdims=True) * (1.0 / D_INNER)
    rs = lax.rsqrt(var + SSM_NORM_EPS)
    for s in range(D_INNER // W):
        sl = slice(s * W, (s + 1) * W)
        o_ref[:, sl] = (yz_ref[:, sl] * rs * g_ref[:, sl]).astype(o_ref.dtype)


def _ssd(xbc, z, dtp, shift, conv_w, conv_b, dtb_p, alog_p, dskip_x, g_ssd, expand, batch, seq):
    t = xbc.shape[0]
    nc = seq // CHUNK
    row = lambda b, c: (b * nc + c, 0)
    prev = lambda b, c: (jnp.maximum(b * nc + c - 1, 0), 0)
    return pl.pallas_call(
        _ssd_kernel,
        out_shape=jax.ShapeDtypeStruct((t, D_INNER), BF16),
        grid=(batch, nc),
        in_specs=[pl.BlockSpec((CHUNK, CONV_DIM), row),
                  pl.BlockSpec((CHUNK, CONV_DIM), prev),
                  pl.BlockSpec((CHUNK, D_INNER), row),
                  pl.BlockSpec((CHUNK, LANES), row),
                  _resident(((CONV_WIDTH - 1) * CHUNK, 2 * CHUNK)),
                  _resident((CONV_WIDTH, CONV_DIM)),
                  _resident((1, CONV_DIM)),
                  _resident((1, LANES)),
                  _resident((1, LANES)),
                  _resident((1, D_INNER)),
                  _resident((1, D_INNER)),
                  _resident((LANES, D_INNER))],
        out_specs=pl.BlockSpec((CHUNK, D_INNER), row),
        scratch_shapes=[pltpu.VMEM((D_STATE, D_INNER), F32),
                        pltpu.VMEM((CHUNK, D_INNER), F32),
                        pltpu.VMEM((CHUNK, SSM_GROUPS * D_STATE), BF16),
                        pltpu.VMEM((CHUNK, SSM_GROUPS * D_STATE), BF16),
                        pltpu.VMEM((CHUNK, D_INNER), BF16),
                        pltpu.VMEM((CHUNK, D_INNER), BF16),
                        pltpu.VMEM((CHUNK, D_INNER), F32)],
        compiler_params=_cparams(("arbitrary", "arbitrary")),
        name="ssd",
    )(xbc, xbc, z, dtp, shift, conv_w, conv_b, dtb_p, alog_p, dskip_x, g_ssd, expand)


def _merge_kernel(attn_ref, y_ref, ga_ref, gs_ref, wa_ref, ws_ref, o_ref, wa16_ref, ws16_ref):
    @pl.when(pl.program_id(1) == 0)
    def _():
        wa16_ref[...] = wa_ref[...].astype(BF16)
        ws16_ref[...] = ws_ref[...].astype(BF16)

    out_a = _dot(attn_ref[...], wa16_ref[...])
    out_s = _dot(y_ref[...], ws16_ref[...])
    merged = (_sigmoid(ga_ref[...].astype(F32)) * out_a
              + _sigmoid(gs_ref[...].astype(F32)) * out_s)
    o_ref[...] = merged.astype(o_ref.dtype)


def _merge(attn, y, gates, wa, ws, tm=512, tn=1024):
    t = attn.shape[0]
    nj = D_MODEL // tn
    return pl.pallas_call(
        _merge_kernel,
        out_shape=jax.ShapeDtypeStruct((t, D_MODEL), BF16),
        grid=(nj, t // tm),
        in_specs=[pl.BlockSpec((tm, Q_DIM), lambda j, i: (i, 0)),
                  pl.BlockSpec((tm, D_INNER), lambda j, i: (i, 0)),
                  pl.BlockSpec((tm, tn), lambda j, i: (i, j)),
                  pl.BlockSpec((tm, tn), lambda j, i: (i, nj + j)),
                  pl.BlockSpec((Q_DIM, tn), lambda j, i: (0, j)),
                  pl.BlockSpec((D_INNER, tn), lambda j, i: (0, j))],
        out_specs=pl.BlockSpec((tm, tn), lambda j, i: (i, j)),
        scratch_shapes=[pltpu.VMEM((Q_DIM, tn), BF16),
                        pltpu.VMEM((D_INNER, tn), BF16)],
        compiler_params=_cparams(("arbitrary", "arbitrary")),
        name="merge",
    )(attn, y, gates, gates, wa, ws)


def _oproj_kernel(m_ref, x_ref, w_ref, g_ref, h_ref, f_ref, w16_ref):
    @pl.when(pl.program_id(0) == 0)
    def _():
        w16_ref[...] = w_ref[...].astype(BF16)

    h = x_ref[...] + _dot(m_ref[...], w16_ref[...])
    h_ref[...] = h
    f_ref[...] = _rms(h, g_ref[...], NORM_EPS).astype(BF16)


def _oproj(merged, x2, wo, g_ffn, tm=512):
    t = merged.shape[0]
    row = lambda i: (i, 0)
    return pl.pallas_call(
        _oproj_kernel,
        out_shape=(jax.ShapeDtypeStruct((t, D_MODEL), F32),
                   jax.ShapeDtypeStruct((t, D_MODEL), BF16)),
        grid=(t // tm,),
        in_specs=[pl.BlockSpec((tm, D_MODEL), row),
                  pl.BlockSpec((tm, D_MODEL), row),
                  _resident((D_MODEL, D_MODEL)),
                  _resident((1, D_MODEL))],
        out_specs=(pl.BlockSpec((tm, D_MODEL), row),
                   pl.BlockSpec((tm, D_MODEL), row)),
        scratch_shapes=[pltpu.VMEM((D_MODEL, D_MODEL), BF16)],
        compiler_params=_cparams(("arbitrary",)),
        name="oproj",
    )(merged, x2, wo, g_ffn)


def _ffn_up_kernel(f_ref, wg_ref, wu_ref, o_ref, wg16_ref, wu16_ref):
    @pl.when(pl.program_id(1) == 0)
    def _():
        wg16_ref[...] = wg_ref[...].astype(BF16)
        wu16_ref[...] = wu_ref[...].astype(BF16)

    f = f_ref[...]
    o_ref[...] = (_silu(_dot(f, wg16_ref[...])) * _dot(f, wu16_ref[...])).astype(o_ref.dtype)


def _ffn_up(f, wg, wu, tm=2048, th=512):
    t = f.shape[0]
    return pl.pallas_call(
        _ffn_up_kernel,
        out_shape=jax.ShapeDtypeStruct((t, FFN_HIDDEN), BF16),
        grid=(FFN_HIDDEN // th, t // tm),
        in_specs=[pl.BlockSpec((tm, D_MODEL), lambda j, i: (i, 0)),
                  pl.BlockSpec((D_MODEL, th), lambda j, i: (0, j)),
                  pl.BlockSpec((D_MODEL, th), lambda j, i: (0, j))],
        out_specs=pl.BlockSpec((tm, th), lambda j, i: (i, j)),
        scratch_shapes=[pltpu.VMEM((D_MODEL, th), BF16),
                        pltpu.VMEM((D_MODEL, th), BF16)],
        compiler_params=_cparams(("arbitrary", "arbitrary")),
        name="ffn_up",
    )(f, wg, wu)


def _ffn_down_kernel(a_ref, h_ref, wd_ref, o_ref, wd16_ref):
    @pl.when(pl.program_id(1) == 0)
    def _():
        wd16_ref[...] = wd_ref[...].astype(BF16)

    o_ref[...] = h_ref[...] + _dot(a_ref[...], wd16_ref[...])


def _ffn_down(act, h1, wd, tm=1024, tn=512):
    t = act.shape[0]
    return pl.pallas_call(
        _ffn_down_kernel,
        out_shape=jax.ShapeDtypeStruct((t, D_MODEL), F32),
        grid=(D_MODEL // tn, t // tm),
        in_specs=[pl.BlockSpec((tm, FFN_HIDDEN), lambda j, i: (i, 0)),
                  pl.BlockSpec((tm, tn), lambda j, i: (i, j)),
                  pl.BlockSpec((FFN_HIDDEN, tn), lambda j, i: (0, j),
                               pipeline_mode=pl.Buffered(1))],
        out_specs=pl.BlockSpec((tm, tn), lambda j, i: (i, j)),
        scratch_shapes=[pltpu.VMEM((FFN_HIDDEN, tn), BF16)],
        compiler_params=_cparams(("arbitrary", "arbitrary")),
        name="ffn_down",
    )(act, h1, wd)


def _ple_kernel(h_ref, p_ref, gp_ref, gf_ref, wg_ref, wp_ref, o_ref, wg16_ref, wp16_ref):
    @pl.when(pl.program_id(0) == 0)
    def _():
        wg16_ref[...] = wg_ref[...].astype(BF16)
        wp16_ref[...] = wp_ref[...].astype(BF16)

    h = h_ref[...]
    r = _rms(h, gp_ref[...], NORM_EPS).astype(BF16)
    gate = _sigmoid(_dot(r, wg16_ref[...]))
    h3 = h + gate * _dot(p_ref[...].astype(BF16), wp16_ref[...])
    o_ref[...] = _rms(h3, gf_ref[...], NORM_EPS)


def _ple(h2, p2, g_ple, g_final, wpg, wpp, tm=512):
    t = h2.shape[0]
    row = lambda i: (i, 0)
    return pl.pallas_call(
        _ple_kernel,
        out_shape=jax.ShapeDtypeStruct((t, D_MODEL), F32),
        grid=(t // tm,),
        in_specs=[pl.BlockSpec((tm, D_MODEL), row),
                  pl.BlockSpec((tm, PLE_DIM), row),
                  _resident((1, D_MODEL)),
                  _resident((1, D_MODEL)),
                  _resident((D_MODEL, D_MODEL)),
                  _resident((PLE_DIM, D_MODEL))],
        out_specs=pl.BlockSpec((tm, D_MODEL), row),
        scratch_shapes=[pltpu.VMEM((D_MODEL, D_MODEL), BF16),
                        pltpu.VMEM((PLE_DIM, D_MODEL), BF16)],
        compiler_params=_cparams(("arbitrary",)),
        name="ple",
    )(h2, p2, g_ple, g_final, wpg, wpp)


def _rope_constants():
    half = HEAD_DIM // 2
    lane = np.arange(LANES)
    inv_freq = ROPE_THETA ** (-jnp.arange(half, dtype=F32) * 2.0 / HEAD_DIM)
    invf = inv_freq[lane % half][None, :]
    sgn = jnp.asarray(np.where((lane % HEAD_DIM) < half, -1.0, 1.0), F32)[None, :]
    return invf, sgn


def _expand_matrix():
    rows = np.arange(LANES)[:, None]
    cols = np.arange(D_INNER)[None, :]
    hit = (rows < 3 * SSM_HEADS) & ((rows % SSM_HEADS) == (cols // SSM_HEAD_DIM))
    return jnp.asarray(hit, BF16)


def _shift_matrix():
    rows = np.arange((CONV_WIDTH - 1) * CHUNK)[:, None]
    cols = np.arange(2 * CHUNK)[None, :]
    w, t = rows // CHUNK, rows % CHUNK
    return jnp.asarray(cols == CHUNK + t - (CONV_WIDTH - 1) + w, BF16)


def _pair_heads(w):
    rest = w.shape[1:]
    w = w.reshape(KV_HEADS // 2, 2, GROUP, HEAD_DIM, *rest)
    return jnp.swapaxes(w, 1, 2).reshape(Q_DIM, *rest)


def _pad_lanes(v):
    return jnp.pad(v.astype(F32), (0, LANES - v.shape[0]))[None, :]


def kernel(x, p, positions, g_mix, w_in, conv_w, conv_b, dt_bias, a_log, d_skip, g_ssd,
           sinks, w_attn_br, w_ssd_br, w_o, g_ffn, w_gate, w_up, w_down, g_ple,
           w_ple_gate, w_ple_proj, g_final):
    b, s, d = x.shape
    t = b * s
    assert d == D_MODEL and s % CHUNK == 0 and p.shape[0] == 1
    i = 0
    x2 = x.reshape(t, d)
    p2 = p[i].reshape(t, PLE_DIM)
    pos2 = positions.reshape(t, 1)

    o_k = Q_DIM
    o_z = Q_DIM + 2 * KV_DIM
    o_xbc = o_z + D_INNER
    o_dt = o_xbc + CONV_DIM
    o_ga = o_dt + SSM_HEADS
    wt = jnp.swapaxes(w_in[i], 0, 1)

    invf, sgn = _rope_constants()
    u, cos_t, sin_t, dtp = _prep(x2, g_mix[i][None, :], pos2, invf, sgn, wt, o_dt)

    q = _inproj(u, _pair_heads(wt[:Q_DIM]), 0, Q_DIM, BF16, 2048, 1024, "proj_q")
    kv = _inproj(u, wt, o_k, 2 * KV_DIM, BF16, 2048, 2 * KV_DIM, "proj_kv")
    z = _inproj(u, wt, o_z, D_INNER, BF16, 2048, 1024, "proj_z")
    xbc = _inproj(u, wt, o_xbc, CONV_DIM, BF16, 2048, 1024, "proj_xbc")
    gates = _inproj(u, wt, o_ga, 2 * D_MODEL, BF16, 2048, 1024, "proj_gates")

    attn = _attention(q, kv, cos_t, sin_t, sinks[i][None, :].astype(F32), b, s)

    y = _ssd(xbc, z, dtp, _shift_matrix(), conv_w[i], conv_b[i][None, :], _pad_lanes(dt_bias[i]),
             _pad_lanes(a_log[i]), jnp.repeat(d_skip[i].astype(F32), SSM_HEAD_DIM)[None, :],
             g_ssd[i][None, :], _expand_matrix(), b, s)

    merged = _merge(attn, y, gates, _pair_heads(w_attn_br[i]), w_ssd_br[i])
    h1, f = _oproj(merged, x2, w_o[i], g_ffn[i][None, :])
    act = _ffn_up(f, w_gate[i], w_up[i])
    h2 = _ffn_down(act, h1, w_down[i])
    out = _ple(h2, p2, g_ple[i][None, :], g_final[None, :], w_ple_gate[i], w_ple_proj[i])
    return out.reshape(b, s, d)
```

```python
from functools import partial

import numpy as np
import jax
import jax.numpy as jnp
from jax import lax
from jax.experimental import pallas as pl
from jax.experimental.pallas import tpu as pltpu

F32 = jnp.float32
BF16 = jnp.bfloat16

D_MODEL = 2048
HEAD_DIM = 64
ATTN_HEADS = 16
KV_HEADS = 4
GROUP = ATTN_HEADS // KV_HEADS
Q_DIM = ATTN_HEADS * HEAD_DIM
KV_DIM = KV_HEADS * HEAD_DIM
ATTN_BLOCK = 128
ROPE_THETA = 10000.0
D_INNER = 2048
SSM_HEAD_DIM = 64
SSM_HEADS = 32
SSM_GROUPS = 4
HEADS_PER_GROUP = SSM_HEADS // SSM_GROUPS
D_STATE = 128
CONV_WIDTH = 4
CHUNK = 128
CONV_DIM = D_INNER + 2 * SSM_GROUPS * D_STATE
FFN_HIDDEN = 5632
PLE_DIM = 256
NORM_EPS = 1e-6
SSM_NORM_EPS = 1e-5

LANES = 128
SUBLANES = 8
VMEM_LIMIT_BYTES = 56 * 1024 * 1024

NEG_BIG = -1e30


def _cparams(semantics):
    return pltpu.CompilerParams(dimension_semantics=semantics,
                                vmem_limit_bytes=VMEM_LIMIT_BYTES)


def _resident(shape):
    return pl.BlockSpec(shape, lambda *_: (0,) * len(shape),
                        pipeline_mode=pl.Buffered(1))


def _rms(xf, g, eps):
    var = jnp.mean(xf * xf, axis=-1, keepdims=True)
    return xf * lax.rsqrt(var + eps) * g


NEG_LOG2E = -1.4426950408889634


def _sigmoid(x):
    return 1.0 / (1.0 + jnp.exp2(x * NEG_LOG2E))


def _silu(x):
    return x * _sigmoid(x)


def _dot(a, b):
    return jnp.dot(a, b, preferred_element_type=F32)


def _dot_nt(a, b):
    return lax.dot_general(a, b, (((1,), (1,)), ((), ())),
                           preferred_element_type=F32)


def _split3(v):
    v1 = v.astype(BF16)
    r1 = v - v1.astype(F32)
    v2 = r1.astype(BF16)
    v3 = (r1 - v2.astype(F32)).astype(BF16)
    return v1, v2, v3


def _prep_kernel(x_ref, g_ref, pos_ref, invf_ref, sgn_ref, wdt_ref, u_ref, cos_ref, sin_ref, dt_ref):
    u = _rms(x_ref[...], g_ref[...], NORM_EPS).astype(BF16)
    u_ref[...] = u
    ang = pos_ref[...].astype(F32) * invf_ref[...]
    cos_ref[...] = jnp.cos(ang)
    sin_ref[...] = jnp.sin(ang) * sgn_ref[...]
    dt_ref[...] = _dot_nt(u, wdt_ref[...].astype(BF16))


def _prep(x2, g_mix, pos2, invf, sgn, wt, dt_row0, tm=512):
    t = x2.shape[0]
    row = lambda i: (i, 0)
    assert dt_row0 % SUBLANES == 0
    return pl.pallas_call(
        _prep_kernel,
        out_shape=(jax.ShapeDtypeStruct((t, D_MODEL), BF16),
                   jax.ShapeDtypeStruct((t, LANES), F32),
                   jax.ShapeDtypeStruct((t, LANES), F32),
                   jax.ShapeDtypeStruct((t, LANES), F32)),
        grid=(t // tm,),
        in_specs=[pl.BlockSpec((tm, D_MODEL), row),
                  _resident((1, D_MODEL)),
                  pl.BlockSpec((tm, 1), row),
                  _resident((1, LANES)),
                  _resident((1, LANES)),
                  pl.BlockSpec((pl.Element(LANES), pl.Element(D_MODEL)),
                               lambda i: (dt_row0, 0), pipeline_mode=pl.Buffered(1))],
        out_specs=(pl.BlockSpec((tm, D_MODEL), row),
                   pl.BlockSpec((tm, LANES), row),
                   pl.BlockSpec((tm, LANES), row),
                   pl.BlockSpec((tm, LANES), row)),
        compiler_params=_cparams(("arbitrary",)),
        name="prep",
    )(x2, g_mix, pos2, invf, sgn, wt)


PAIR_ROWS = 2 * GROUP * HEAD_DIM


def _paired_head_blocks():
    return [(g * 2 + sub, sub * GROUP + g) for g in range(GROUP) for sub in range(2)]


def _cast_rows(dst_ref, src_ref, paired_heads):
    if not paired_heads:
        dst_ref[...] = src_ref[...].astype(BF16)
        return
    assert dst_ref.shape[0] % PAIR_ROWS == 0
    for base in range(0, dst_ref.shape[0], PAIR_ROWS):
        for dst, src in _paired_head_blocks():
            dst_ref[base + dst * HEAD_DIM:base + (dst + 1) * HEAD_DIM, :] = (
                src_ref[base + src * HEAD_DIM:base + (src + 1) * HEAD_DIM, :].astype(BF16))


def _inproj_kernel(a_ref, wt_ref, o_ref, wbf_ref, *, paired_tiles):
    j = pl.program_id(0)

    @pl.when((pl.program_id(1) == 0) & (j >= paired_tiles))
    def _():
        _cast_rows(wbf_ref, wt_ref, False)

    if paired_tiles:
        @pl.when((pl.program_id(1) == 0) & (j < paired_tiles))
        def _():
            _cast_rows(wbf_ref, wt_ref, True)

    o_ref[...] = _dot_nt(a_ref[...], wbf_ref[...]).astype(o_ref.dtype)


def _inproj(a, wt, row0, n, out_dtype, tm, tn, name, paired_tiles=0):
    m, k = a.shape
    assert n % tn == 0 and row0 % SUBLANES == 0
    assert not paired_tiles or tn % PAIR_ROWS == 0
    return pl.pallas_call(
        partial(_inproj_kernel, paired_tiles=paired_tiles),
        out_shape=jax.ShapeDtypeStruct((m, n), out_dtype),
        grid=(n // tn, m // tm),
        in_specs=[pl.BlockSpec((tm, k), lambda j, i: (i, 0)),
                  pl.BlockSpec((pl.Element(tn), pl.Element(k)),
                               lambda j, i: (pl.multiple_of(row0 + j * tn, SUBLANES), 0))],
        out_specs=pl.BlockSpec((tm, tn), lambda j, i: (i, j)),
        scratch_shapes=[pltpu.VMEM((tn, k), BF16)],
        compiler_params=_cparams(("arbitrary", "arbitrary")),
        name=name,
    )(a, wt)


def _attn_kernel(sinks_ref, q_ref, kc_ref, kp_ref, vc_ref, vp_ref,
                 cc_ref, sc_ref, cp_ref, sp_ref, o_ref):
    n = pl.program_id(1)
    blk = ATTN_BLOCK
    lane = lax.broadcasted_iota(jnp.int32, (blk, LANES), 1)
    first_half = (lane % HEAD_DIM) < (HEAD_DIM // 2)
    low_head = lane < HEAD_DIM

    def rope(t, c, s):
        partner = jnp.where(first_half,
                            pltpu.roll(t, LANES - HEAD_DIM // 2, 1),
                            pltpu.roll(t, HEAD_DIM // 2, 1))
        return t * c + partner * s

    cc, sc, cp, sp = cc_ref[...], sc_ref[...], cp_ref[...], sp_ref[...]

    qi = lax.broadcasted_iota(jnp.int32, (blk, 2 * blk), 0)
    kj = lax.broadcasted_iota(jnp.int32, (blk, 2 * blk), 1)
    rel = kj - qi
    valid = (rel >= 1) & (rel <= blk) & ((kj >= blk) | (n > 0))

    scale = HEAD_DIM ** -0.5
    n_half = KV_DIM // LANES
    rows_per_half = 2 * GROUP * blk
    s_rows, sink_rows, v_wins = [], [], []
    for a in range(n_half):
        ksl = slice(a * LANES, (a + 1) * LANES)
        k_win = jnp.concatenate(
            [rope(kp_ref[:, ksl].astype(F32), cp, sp),
             rope(kc_ref[:, ksl].astype(F32), cc, sc)], axis=0).astype(BF16)
        v_wins.append(jnp.concatenate([vp_ref[:, ksl], vc_ref[:, ksl]], axis=0))
        qc = [rope(q_ref[:, (a * GROUP + g) * LANES:(a * GROUP + g + 1) * LANES].astype(F32), cc, sc)
              * scale for g in range(GROUP)]
        q_rows = []
        for sub in range(2):
            keep = low_head if sub == 0 else jnp.logical_not(low_head)
            for g in range(GROUP):
                q_rows.append(jnp.where(keep, qc[g], 0.0).astype(BF16))
                sink_rows.append(
                    jnp.full((blk, LANES), sinks_ref[0, (2 * a + sub) * GROUP + g], F32))
        s_rows.append(_dot_nt(jnp.concatenate(q_rows, axis=0), k_win))
    valid_all = jnp.concatenate([valid] * (n_half * 2 * GROUP), axis=0)
    s = jnp.where(valid_all, jnp.concatenate(s_rows, axis=0), NEG_BIG)
    sink = jnp.concatenate(sink_rows, axis=0)
    s0, s1 = s[:, :LANES], s[:, LANES:]
    m = jnp.maximum(jnp.max(jnp.maximum(s0, s1), axis=-1, keepdims=True), sink)
    e0 = jnp.exp(s0 - m)
    e1 = jnp.exp(s1 - m)
    den = jnp.sum(e0 + e1, axis=-1, keepdims=True) + jnp.exp(sink - m)
    r = 1.0 / den
    p = jnp.concatenate([e0 * r, e1 * r], axis=1).astype(BF16)
    for a in range(n_half):
        pv = _dot(p[a * rows_per_half:(a + 1) * rows_per_half], v_wins[a])
        for g in range(GROUP):
            lo = pv[g * blk:(g + 1) * blk]
            hi = pv[(GROUP + g) * blk:(GROUP + g + 1) * blk]
            chunk = a * GROUP + g
            o_ref[:, chunk * LANES:(chunk + 1) * LANES] = (
                jnp.where(low_head, lo, hi).astype(o_ref.dtype))


def _attention(qkv, cos_t, sin_t, sinks, batch, seq):
    t = qkv.shape[0]
    nb = seq // ATTN_BLOCK
    kcol = Q_DIM // KV_DIM
    vcol = kcol + 1
    cur = lambda b, n: b * nb + n
    prev = lambda b, n: jnp.maximum(b * nb + n - 1, 0)
    return pl.pallas_call(
        _attn_kernel,
        out_shape=jax.ShapeDtypeStruct((t, Q_DIM), BF16),
        grid=(batch, nb),
        in_specs=[pl.BlockSpec(memory_space=pltpu.SMEM),
                  pl.BlockSpec((ATTN_BLOCK, Q_DIM), lambda b, n: (cur(b, n), 0)),
                  pl.BlockSpec((ATTN_BLOCK, KV_DIM), lambda b, n: (cur(b, n), kcol)),
                  pl.BlockSpec((ATTN_BLOCK, KV_DIM), lambda b, n: (prev(b, n), kcol)),
                  pl.BlockSpec((ATTN_BLOCK, KV_DIM), lambda b, n: (cur(b, n), vcol)),
                  pl.BlockSpec((ATTN_BLOCK, KV_DIM), lambda b, n: (prev(b, n), vcol)),
                  pl.BlockSpec((ATTN_BLOCK, LANES), lambda b, n: (cur(b, n), 0)),
                  pl.BlockSpec((ATTN_BLOCK, LANES), lambda b, n: (cur(b, n), 0)),
                  pl.BlockSpec((ATTN_BLOCK, LANES), lambda b, n: (prev(b, n), 0)),
                  pl.BlockSpec((ATTN_BLOCK, LANES), lambda b, n: (prev(b, n), 0))],
        out_specs=pl.BlockSpec((ATTN_BLOCK, Q_DIM), lambda b, n: (cur(b, n), 0)),
        compiler_params=_cparams(("arbitrary", "arbitrary")),
        name="attention",
    )(sinks, qkv, qkv, qkv, qkv, qkv, cos_t, sin_t, cos_t, sin_t)


SSD_STRIP = 256


def _ssd_kernel(z_ref, xin_ref, bcin_ref, xin_prev_ref, bcin_prev_ref, dt_ref, shift_ref, cw_ref,
                cb_ref, dtb_ref, alog_ref, dskip_ref, g_ref, expand_ref, o_ref,
                state_ref, xs_ref, b16_ref, c16_ref, xdd_ref, xd16_ref, yz_ref):
    c = pl.program_id(1)
    L = CHUNK
    W = SSD_STRIP
    bc_dim = SSM_GROUPS * D_STATE

    @pl.when(c == 0)
    def _():
        state_ref[...] = jnp.zeros_like(state_ref)

    n_shift = CONV_WIDTH - 1
    sel_col = lax.broadcasted_iota(jnp.int32, (n_shift * L, 2 * L), 1)
    sel = shift_ref[...]
    sel = jnp.where((sel_col < L) & (c == 0), jnp.zeros_like(sel), sel)
    for s in range(CONV_DIM // W):
        sl = slice(s * W, (s + 1) * W)
        if s * W < D_INNER:
            cur_ref, prev_ref, src = xin_ref, xin_prev_ref, sl
        else:
            cur_ref, prev_ref = bcin_ref, bcin_prev_ref
            src = slice(s * W - D_INNER, (s + 1) * W - D_INNER)
        x_cur = cur_ref[:, src]
        xwin = jnp.concatenate([prev_ref[:, src], x_cur], axis=0)
        taps = _dot(sel, xwin)
        acc = cb_ref[:, sl] + x_cur.astype(F32) * cw_ref[n_shift:CONV_WIDTH, sl]
        for w in range(n_shift):
            acc = acc + taps[w * L:(w + 1) * L] * cw_ref[w:w + 1, sl]
        xc = _silu(acc)
        if s * W < D_INNER:
            xs_ref[:, sl] = xc
        elif s * W < D_INNER + bc_dim:
            b16_ref[:, s * W - D_INNER:(s + 1) * W - D_INNER] = xc.astype(BF16)
        else:
            c16_ref[:, s * W - D_INNER - bc_dim:(s + 1) * W - D_INNER - bc_dim] = xc.astype(BF16)

    v = dt_ref[...] + dtb_ref[...]
    dt = jnp.maximum(v, 0.0) + jnp.log1p(jnp.exp(-jnp.abs(v)))
    a = dt * (jnp.exp(alog_ref[...]) * NEG_LOG2E)

    ri = lax.broadcasted_iota(jnp.int32, (L, L), 0)
    ci = lax.broadcasted_iota(jnp.int32, (L, L), 1)
    causal = ri >= ci
    tril = jnp.where(causal, 1.0, 0.0).astype(BF16)
    ones = jnp.ones((L, L), BF16)
    a1, a2, a3 = _split3(a)
    cs = _dot(tril, a1) + _dot(tril, a2) + _dot(tril, a3)
    tot = _dot(ones, a1) + _dot(ones, a2) + _dot(ones, a3)
    cs_t = cs.T

    lane = lax.broadcasted_iota(jnp.int32, (L, LANES), 1)

    def pack(q):
        q1, q2, q3 = _split3(q)
        packed = jnp.where(
            lane < SSM_HEADS, q1.astype(F32),
            jnp.where(lane < 2 * SSM_HEADS, pltpu.roll(q2.astype(F32), SSM_HEADS, 1),
                      jnp.where(lane < 3 * SSM_HEADS,
                                pltpu.roll(q3.astype(F32), 2 * SSM_HEADS, 1), 0.0)))
        return packed.astype(BF16)

    pk_all = jnp.concatenate([pack(dt), pack(cs), pack(tot)], axis=0)
    pk_ct = pk_all[L:]

    for s in range(D_INNER // W):
        sl = slice(s * W, (s + 1) * W)
        ex = _dot(pk_all, expand_ref[:, sl])
        xd = xs_ref[:, sl] * ex[0:L]
        xd16_ref[:, sl] = xd.astype(BF16)
        xdd_ref[:, sl] = (xd * jnp.exp2(ex[2 * L:3 * L] - ex[L:2 * L])).astype(BF16)

    low_head = lane < SSM_HEAD_DIM
    heads_per_strip = W // SSM_HEAD_DIM
    strips_per_group = HEADS_PER_GROUP // heads_per_strip
    ssq = jnp.zeros((L, LANES), F32)

    for g in range(SSM_GROUPS):
        nsl = slice(g * D_STATE, (g + 1) * D_STATE)
        bg = b16_ref[:, nsl]
        cg = c16_ref[:, nsl]
        bg_t = bg.astype(F32).T.astype(BF16)
        cb = _dot_nt(cg, bg)
        for k in range(strips_per_group):
            s = g * strips_per_group + k
            sl = slice(s * W, (s + 1) * W)
            ex = _dot(pk_ct, expand_ref[:, sl])
            st_prev = state_ref[:, sl]
            y_off = _dot(cg, st_prev.astype(BF16))
            state_ref[:, sl] = st_prev * jnp.exp2(ex[L:2 * L]) + _dot(bg_t, xdd_ref[:, sl])
            y_pairs = []
            for pair in range(W // LANES):
                psl = slice(s * W + pair * LANES, s * W + (pair + 1) * LANES)
                xp = xd16_ref[:, psl]
                y_pair = None
                for sub in range(2):
                    h = s * heads_per_strip + pair * 2 + sub
                    diff = cs[:, h:h + 1] - cs_t[h:h + 1, :]
                    decay = jnp.exp2(jnp.where(causal, diff, NEG_BIG))
                    mh = (cb * decay).astype(BF16)
                    keep = low_head if sub == 0 else jnp.logical_not(low_head)
                    part = _dot(mh, jnp.where(keep, xp, jnp.zeros_like(xp)))
                    y_pair = part if y_pair is None else y_pair + part
                y_pairs.append(y_pair)
            y = (jnp.concatenate(y_pairs, axis=1) + y_off * jnp.exp2(ex[0:L])
                 + dskip_ref[:, sl] * xs_ref[:, sl])
            yz = y * _silu(z_ref[:, sl].astype(F32))
            yz_ref[:, sl] = yz
            for pair in range(W // LANES):
                part = yz[:, pair * LANES:(pair + 1) * LANES]
                ssq = ssq + part * part

    var = jnp.sum(ssq, axis=-1, keepdims=True) * (1.0 / D_INNER)
    rs = lax.rsqrt(var + SSM_NORM_EPS)
    for s in range(D_INNER // W):
        sl = slice(s * W, (s + 1) * W)
        o_ref[:, sl] = (yz_ref[:, sl] * rs * g_ref[:, sl]).astype(o_ref.dtype)


def _ssd(zx, dtp, shift, conv_w, conv_b, dtb_p, alog_p, dskip_x, g_ssd, expand, batch, seq):
    t = zx.shape[0]
    nc = seq // CHUNK
    bc_dim = 2 * SSM_GROUPS * D_STATE
    row = lambda b, c: (b * nc + c, 0)
    cur = lambda b, c: b * nc + c
    prev = lambda b, c: jnp.maximum(b * nc + c - 1, 0)
    x_col = 1
    bc_col = 2 * D_INNER // bc_dim
    return pl.pallas_call(
        _ssd_kernel,
        out_shape=jax.ShapeDtypeStruct((t, D_INNER), BF16),
        grid=(batch, nc),
        in_specs=[pl.BlockSpec((CHUNK, D_INNER), row),
                  pl.BlockSpec((CHUNK, D_INNER), lambda b, c: (cur(b, c), x_col)),
                  pl.BlockSpec((CHUNK, bc_dim), lambda b, c: (cur(b, c), bc_col)),
                  pl.BlockSpec((CHUNK, D_INNER), lambda b, c: (prev(b, c), x_col)),
                  pl.BlockSpec((CHUNK, bc_dim), lambda b, c: (prev(b, c), bc_col)),
                  pl.BlockSpec((CHUNK, LANES), row),
                  _resident(((CONV_WIDTH - 1) * CHUNK, 2 * CHUNK)),
                  _resident((CONV_WIDTH, CONV_DIM)),
                  _resident((1, CONV_DIM)),
                  _resident((1, LANES)),
                  _resident((1, LANES)),
                  _resident((1, D_INNER)),
                  _resident((1, D_INNER)),
                  _resident((LANES, D_INNER))],
        out_specs=pl.BlockSpec((CHUNK, D_INNER), row),
        scratch_shapes=[pltpu.VMEM((D_STATE, D_INNER), F32),
                        pltpu.VMEM((CHUNK, D_INNER), F32),
                        pltpu.VMEM((CHUNK, SSM_GROUPS * D_STATE), BF16),
                        pltpu.VMEM((CHUNK, SSM_GROUPS * D_STATE), BF16),
                        pltpu.VMEM((CHUNK, D_INNER), BF16),
                        pltpu.VMEM((CHUNK, D_INNER), BF16),
                        pltpu.VMEM((CHUNK, D_INNER), F32)],
        compiler_params=_cparams(("arbitrary", "arbitrary")),
        name="ssd",
    )(zx, zx, zx, zx, zx, dtp, shift, conv_w, conv_b, dtb_p, alog_p, dskip_x, g_ssd, expand)


def _merge_kernel(attn_ref, y_ref, ga_ref, gs_ref, wa_ref, ws_ref, o_ref, wa16_ref, ws16_ref):
    @pl.when(pl.program_id(1) == 0)
    def _():
        _cast_rows(wa16_ref, wa_ref, True)
        ws16_ref[...] = ws_ref[...].astype(BF16)

    out_a = _dot(attn_ref[...], wa16_ref[...])
    out_s = _dot(y_ref[...], ws16_ref[...])
    merged = (_sigmoid(ga_ref[...].astype(F32)) * out_a
              + _sigmoid(gs_ref[...].astype(F32)) * out_s)
    o_ref[...] = merged.astype(o_ref.dtype)


def _merge(attn, y, gates, wa, ws, tm=512, tn=1024):
    t = attn.shape[0]
    nj = D_MODEL // tn
    return pl.pallas_call(
        _merge_kernel,
        out_shape=jax.ShapeDtypeStruct((t, D_MODEL), BF16),
        grid=(nj, t // tm),
        in_specs=[pl.BlockSpec((tm, Q_DIM), lambda j, i: (i, 0)),
                  pl.BlockSpec((tm, D_INNER), lambda j, i: (i, 0)),
                  pl.BlockSpec((tm, tn), lambda j, i: (i, j)),
                  pl.BlockSpec((tm, tn), lambda j, i: (i, nj + j)),
                  pl.BlockSpec((Q_DIM, tn), lambda j, i: (0, j)),
                  pl.BlockSpec((D_INNER, tn), lambda j, i: (0, j))],
        out_specs=pl.BlockSpec((tm, tn), lambda j, i: (i, j)),
        scratch_shapes=[pltpu.VMEM((Q_DIM, tn), BF16),
                        pltpu.VMEM((D_INNER, tn), BF16)],
        compiler_params=_cparams(("arbitrary", "arbitrary")),
        name="merge",
    )(attn, y, gates, gates, wa, ws)


def _oproj_kernel(m_ref, x_ref, w_ref, g_ref, h_ref, f_ref, w16_ref):
    @pl.when(pl.program_id(0) == 0)
    def _():
        w16_ref[...] = w_ref[...].astype(BF16)

    h = x_ref[...] + _dot(m_ref[...], w16_ref[...])
    h_ref[...] = h
    f_ref[...] = _rms(h, g_ref[...], NORM_EPS).astype(BF16)


def _oproj(merged, x2, wo, g_ffn, tm=512):
    t = merged.shape[0]
    row = lambda i: (i, 0)
    return pl.pallas_call(
        _oproj_kernel,
        out_shape=(jax.ShapeDtypeStruct((t, D_MODEL), F32),
                   jax.ShapeDtypeStruct((t, D_MODEL), BF16)),
        grid=(t // tm,),
        in_specs=[pl.BlockSpec((tm, D_MODEL), row),
                  pl.BlockSpec((tm, D_MODEL), row),
                  _resident((D_MODEL, D_MODEL)),
                  _resident((1, D_MODEL))],
        out_specs=(pl.BlockSpec((tm, D_MODEL), row),
                   pl.BlockSpec((tm, D_MODEL), row)),
        scratch_shapes=[pltpu.VMEM((D_MODEL, D_MODEL), BF16)],
        compiler_params=_cparams(("arbitrary",)),
        name="oproj",
    )(merged, x2, wo, g_ffn)


def _ffn_up_kernel(f_ref, wg_ref, wu_ref, o_ref, wg16_ref, wu16_ref):
    @pl.when(pl.program_id(1) == 0)
    def _():
        wg16_ref[...] = wg_ref[...].astype(BF16)
        wu16_ref[...] = wu_ref[...].astype(BF16)

    f = f_ref[...]
    o_ref[...] = (_silu(_dot(f, wg16_ref[...])) * _dot(f, wu16_ref[...])).astype(o_ref.dtype)


def _ffn_up(f, wg, wu, tm=2048, th=512):
    t = f.shape[0]
    return pl.pallas_call(
        _ffn_up_kernel,
        out_shape=jax.ShapeDtypeStruct((t, FFN_HIDDEN), BF16),
        grid=(FFN_HIDDEN // th, t // tm),
        in_specs=[pl.BlockSpec((tm, D_MODEL), lambda j, i: (i, 0)),
                  pl.BlockSpec((D_MODEL, th), lambda j, i: (0, j)),
                  pl.BlockSpec((D_MODEL, th), lambda j, i: (0, j))],
        out_specs=pl.BlockSpec((tm, th), lambda j, i: (i, j)),
        scratch_shapes=[pltpu.VMEM((D_MODEL, th), BF16),
                        pltpu.VMEM((D_MODEL, th), BF16)],
        compiler_params=_cparams(("arbitrary", "arbitrary")),
        name="ffn_up",
    )(f, wg, wu)


def _ffn_down_kernel(a_ref, h_ref, wd_ref, o_ref, wd16_ref):
    @pl.when(pl.program_id(1) == 0)
    def _():
        wd16_ref[...] = wd_ref[...].astype(BF16)

    o_ref[...] = h_ref[...] + _dot(a_ref[...], wd16_ref[...])


def _ffn_down(act, h1, wd, tm=1024, tn=512):
    t = act.shape[0]
    return pl.pallas_call(
        _ffn_down_kernel,
        out_shape=jax.ShapeDtypeStruct((t, D_MODEL), F32),
        grid=(D_MODEL // tn, t // tm),
        in_specs=[pl.BlockSpec((tm, FFN_HIDDEN), lambda j, i: (i, 0)),
                  pl.BlockSpec((tm, tn), lambda j, i: (i, j)),
                  pl.BlockSpec((FFN_HIDDEN, tn), lambda j, i: (0, j),
                               pipeline_mode=pl.Buffered(1))],
        out_specs=pl.BlockSpec((tm, tn), lambda j, i: (i, j)),
        scratch_shapes=[pltpu.VMEM((FFN_HIDDEN, tn), BF16)],
        compiler_params=_cparams(("arbitrary", "arbitrary")),
        name="ffn_down",
    )(act, h1, wd)


def _ple_kernel(h_ref, p_ref, gp_ref, gf_ref, wg_ref, wp_ref, o_ref, wg16_ref, wp16_ref):
    @pl.when(pl.program_id(0) == 0)
    def _():
        wg16_ref[...] = wg_ref[...].astype(BF16)
        wp16_ref[...] = wp_ref[...].astype(BF16)

    h = h_ref[...]
    r = _rms(h, gp_ref[...], NORM_EPS).astype(BF16)
    gate = _sigmoid(_dot(r, wg16_ref[...]))
    h3 = h + gate * _dot(p_ref[...].astype(BF16), wp16_ref[...])
    o_ref[...] = _rms(h3, gf_ref[...], NORM_EPS)


def _ple(h2, p2, g_ple, g_final, wpg, wpp, tm=512):
    t = h2.shape[0]
    row = lambda i: (i, 0)
    return pl.pallas_call(
        _ple_kernel,
        out_shape=jax.ShapeDtypeStruct((t, D_MODEL), F32),
        grid=(t // tm,),
        in_specs=[pl.BlockSpec((tm, D_MODEL), row),
                  pl.BlockSpec((tm, PLE_DIM), row),
                  _resident((1, D_MODEL)),
                  _resident((1, D_MODEL)),
                  _resident((D_MODEL, D_MODEL)),
                  _resident((PLE_DIM, D_MODEL))],
        out_specs=pl.BlockSpec((tm, D_MODEL), row),
        scratch_shapes=[pltpu.VMEM((D_MODEL, D_MODEL), BF16),
                        pltpu.VMEM((PLE_DIM, D_MODEL), BF16)],
        compiler_params=_cparams(("arbitrary",)),
        name="ple",
    )(h2, p2, g_ple, g_final, wpg, wpp)


def _rope_constants():
    half = HEAD_DIM // 2
    lane = np.arange(LANES)
    inv_freq = ROPE_THETA ** (-jnp.arange(half, dtype=F32) * 2.0 / HEAD_DIM)
    invf = inv_freq[lane % half][None, :]
    sgn = jnp.asarray(np.where((lane % HEAD_DIM) < half, -1.0, 1.0), F32)[None, :]
    return invf, sgn


def _expand_matrix():
    rows = np.arange(LANES)[:, None]
    cols = np.arange(D_INNER)[None, :]
    hit = (rows < 3 * SSM_HEADS) & ((rows % SSM_HEADS) == (cols // SSM_HEAD_DIM))
    return jnp.asarray(hit, BF16)


def _shift_matrix():
    rows = np.arange((CONV_WIDTH - 1) * CHUNK)[:, None]
    cols = np.arange(2 * CHUNK)[None, :]
    w, t = rows // CHUNK, rows % CHUNK
    return jnp.asarray(cols == CHUNK + t - (CONV_WIDTH - 1) + w, BF16)


def _pad_lanes(v):
    return jnp.pad(v.astype(F32), (0, LANES - v.shape[0]))[None, :]


def kernel(x, p, positions, g_mix, w_in, conv_w, conv_b, dt_bias, a_log, d_skip, g_ssd,
           sinks, w_attn_br, w_ssd_br, w_o, g_ffn, w_gate, w_up, w_down, g_ple,
           w_ple_gate, w_ple_proj, g_final):
    b, s, d = x.shape
    t = b * s
    assert d == D_MODEL and s % CHUNK == 0 and p.shape[0] == 1
    i = 0
    x2 = x.reshape(t, d)
    p2 = p[i].reshape(t, PLE_DIM)
    pos2 = positions.reshape(t, 1)

    o_k = Q_DIM
    o_z = Q_DIM + 2 * KV_DIM
    o_xbc = o_z + D_INNER
    o_dt = o_xbc + CONV_DIM
    o_ga = o_dt + SSM_HEADS
    wt = jnp.swapaxes(w_in[i], 0, 1)

    invf, sgn = _rope_constants()
    u, cos_t, sin_t, dtp = _prep(x2, g_mix[i][None, :], pos2, invf, sgn, wt, o_dt)

    qkv = _inproj(u, wt, 0, o_z, BF16, 2048, PAIR_ROWS, "proj_qkv", paired_tiles=Q_DIM // PAIR_ROWS)
    zx = _inproj(u, wt, o_z, D_INNER + CONV_DIM, BF16, 2048, 1024, "proj_zx")
    gates = _inproj(u, wt, o_ga, 2 * D_MODEL, BF16, 2048, 1024, "proj_gates")

    attn = _attention(qkv, cos_t, sin_t, sinks[i][None, :].astype(F32), b, s)

    y = _ssd(zx, dtp, _shift_matrix(), conv_w[i], conv_b[i][None, :], _pad_lanes(dt_bias[i]),
             _pad_lanes(a_log[i]), jnp.repeat(d_skip[i].astype(F32), SSM_HEAD_DIM)[None, :],
             g_ssd[i][None, :], _expand_matrix(), b, s)

    merged = _merge(attn, y, gates, w_attn_br[i], w_ssd_br[i])
    h1, f = _oproj(merged, x2, w_o[i], g_ffn[i][None, :])
    act = _ffn_up(f, w_gate[i], w_up[i])
    h2 = _ffn_down(act, h1, w_down[i])
    out = _ple(h2, p2, g_ple[i][None, :], g_final[None, :], w_ple_gate[i], w_ple_proj[i])
    return out.reshape(b, s, d)
```

```python
from functools import partial

import numpy as np
import jax
import jax.numpy as jnp
from jax import lax
from jax.experimental import pallas as pl
from jax.experimental.pallas import tpu as pltpu

F32 = jnp.float32
BF16 = jnp.bfloat16

D_MODEL = 2048
HEAD_DIM = 64
ATTN_HEADS = 16
KV_HEADS = 4
GROUP = ATTN_HEADS // KV_HEADS
Q_DIM = ATTN_HEADS * HEAD_DIM
KV_DIM = KV_HEADS * HEAD_DIM
ATTN_BLOCK = 128
ROPE_THETA = 10000.0
D_INNER = 2048
SSM_HEAD_DIM = 64
SSM_HEADS = 32
SSM_GROUPS = 4
HEADS_PER_GROUP = SSM_HEADS // SSM_GROUPS
D_STATE = 128
CONV_WIDTH = 4
CHUNK = 128
CONV_DIM = D_INNER + 2 * SSM_GROUPS * D_STATE
FFN_HIDDEN = 5632
PLE_DIM = 256
NORM_EPS = 1e-6
SSM_NORM_EPS = 1e-5

LANES = 128
SUBLANES = 8
VMEM_LIMIT_BYTES = 56 * 1024 * 1024

NEG_BIG = -1e30


def _cparams(semantics):
    return pltpu.CompilerParams(dimension_semantics=semantics,
                                vmem_limit_bytes=VMEM_LIMIT_BYTES)


def _resident(shape):
    return pl.BlockSpec(shape, lambda *_: (0,) * len(shape),
                        pipeline_mode=pl.Buffered(1))


def _rms(xf, g, eps):
    var = jnp.mean(xf * xf, axis=-1, keepdims=True)
    return xf * lax.rsqrt(var + eps) * g


NEG_LOG2E = -1.4426950408889634


def _sigmoid(x):
    return 1.0 / (1.0 + jnp.exp2(x * NEG_LOG2E))


def _silu(x):
    return x * _sigmoid(x)


def _dot(a, b):
    return jnp.dot(a, b, preferred_element_type=F32)


def _dot_nt(a, b):
    return lax.dot_general(a, b, (((1,), (1,)), ((), ())),
                           preferred_element_type=F32)


def _split3(v):
    v1 = v.astype(BF16)
    r1 = v - v1.astype(F32)
    v2 = r1.astype(BF16)
    v3 = (r1 - v2.astype(F32)).astype(BF16)
    return v1, v2, v3


def _prep_kernel(x_ref, g_ref, pos_ref, invf_ref, sgn_ref, wdt_ref, u_ref, cos_ref, sin_ref, dt_ref):
    u = _rms(x_ref[...], g_ref[...], NORM_EPS).astype(BF16)
    u_ref[...] = u
    ang = pos_ref[...].astype(F32) * invf_ref[...]
    cos_ref[...] = jnp.cos(ang)
    sin_ref[...] = jnp.sin(ang) * sgn_ref[...]
    dt_ref[...] = _dot_nt(u, wdt_ref[...].astype(BF16))


def _prep(x2, g_mix, pos2, invf, sgn, wt, dt_row0, tm=1024):
    t = x2.shape[0]
    row = lambda i: (i, 0)
    assert dt_row0 % SUBLANES == 0
    return pl.pallas_call(
        _prep_kernel,
        out_shape=(jax.ShapeDtypeStruct((t, D_MODEL), BF16),
                   jax.ShapeDtypeStruct((t, LANES), F32),
                   jax.ShapeDtypeStruct((t, LANES), F32),
                   jax.ShapeDtypeStruct((t, LANES), F32)),
        grid=(t // tm,),
        in_specs=[pl.BlockSpec((tm, D_MODEL), row),
                  _resident((1, D_MODEL)),
                  pl.BlockSpec((tm, 1), row),
                  _resident((1, LANES)),
                  _resident((1, LANES)),
                  pl.BlockSpec((pl.Element(LANES), pl.Element(D_MODEL)),
                               lambda i: (dt_row0, 0), pipeline_mode=pl.Buffered(1))],
        out_specs=(pl.BlockSpec((tm, D_MODEL), row),
                   pl.BlockSpec((tm, LANES), row),
                   pl.BlockSpec((tm, LANES), row),
                   pl.BlockSpec((tm, LANES), row)),
        compiler_params=_cparams(("arbitrary",)),
        name="prep",
    )(x2, g_mix, pos2, invf, sgn, wt)


PAIR_ROWS = 2 * GROUP * HEAD_DIM


def _paired_head_blocks():
    return [(g * 2 + sub, sub * GROUP + g) for g in range(GROUP) for sub in range(2)]


def _cast_rows(dst_ref, src_ref, paired_heads):
    if not paired_heads:
        dst_ref[...] = src_ref[...].astype(BF16)
        return
    assert dst_ref.shape[0] % PAIR_ROWS == 0
    for base in range(0, dst_ref.shape[0], PAIR_ROWS):
        for dst, src in _paired_head_blocks():
            dst_ref[base + dst * HEAD_DIM:base + (dst + 1) * HEAD_DIM, :] = (
                src_ref[base + src * HEAD_DIM:base + (src + 1) * HEAD_DIM, :].astype(BF16))


def _inproj_kernel(a_ref, wt_ref, o_ref, wbf_ref, *, paired_tiles):
    j = pl.program_id(0)

    @pl.when((pl.program_id(1) == 0) & (j >= paired_tiles))
    def _():
        _cast_rows(wbf_ref, wt_ref, False)

    if paired_tiles:
        @pl.when((pl.program_id(1) == 0) & (j < paired_tiles))
        def _():
            _cast_rows(wbf_ref, wt_ref, True)

    o_ref[...] = _dot_nt(a_ref[...], wbf_ref[...]).astype(o_ref.dtype)


def _inproj(a, wt, row0, n, out_dtype, tm, tn, name, paired_tiles=0):
    m, k = a.shape
    assert n % tn == 0 and row0 % SUBLANES == 0
    assert not paired_tiles or tn % PAIR_ROWS == 0
    return pl.pallas_call(
        partial(_inproj_kernel, paired_tiles=paired_tiles),
        out_shape=jax.ShapeDtypeStruct((m, n), out_dtype),
        grid=(n // tn, m // tm),
        in_specs=[pl.BlockSpec((tm, k), lambda j, i: (i, 0)),
                  pl.BlockSpec((pl.Element(tn), pl.Element(k)),
                               lambda j, i: (pl.multiple_of(row0 + j * tn, SUBLANES), 0))],
        out_specs=pl.BlockSpec((tm, tn), lambda j, i: (i, j)),
        scratch_shapes=[pltpu.VMEM((tn, k), BF16)],
        compiler_params=_cparams(("arbitrary", "arbitrary")),
        name=name,
    )(a, wt)


def _attn_kernel(sinks_ref, q_ref, kc_ref, kp_ref, vc_ref, vp_ref,
                 cc_ref, sc_ref, cp_ref, sp_ref, o_ref):
    n = pl.program_id(1)
    blk = ATTN_BLOCK
    lane = lax.broadcasted_iota(jnp.int32, (blk, LANES), 1)
    first_half = (lane % HEAD_DIM) < (HEAD_DIM // 2)
    low_head = lane < HEAD_DIM

    def rope(t, c, s):
        partner = jnp.where(first_half,
                            pltpu.roll(t, LANES - HEAD_DIM // 2, 1),
                            pltpu.roll(t, HEAD_DIM // 2, 1))
        return t * c + partner * s

    cc, sc, cp, sp = cc_ref[...], sc_ref[...], cp_ref[...], sp_ref[...]

    qi = lax.broadcasted_iota(jnp.int32, (blk, 2 * blk), 0)
    kj = lax.broadcasted_iota(jnp.int32, (blk, 2 * blk), 1)
    rel = kj - qi
    valid = (rel >= 1) & (rel <= blk) & ((kj >= blk) | (n > 0))

    scale = HEAD_DIM ** -0.5
    n_half = KV_DIM // LANES
    rows_per_half = 2 * GROUP * blk
    s_rows, sink_rows, v_wins = [], [], []
    for a in range(n_half):
        ksl = slice(a * LANES, (a + 1) * LANES)
        k_win = jnp.concatenate(
            [rope(kp_ref[:, ksl].astype(F32), cp, sp),
             rope(kc_ref[:, ksl].astype(F32), cc, sc)], axis=0).astype(BF16)
        v_wins.append(jnp.concatenate([vp_ref[:, ksl], vc_ref[:, ksl]], axis=0))
        qc = [rope(q_ref[:, (a * GROUP + g) * LANES:(a * GROUP + g + 1) * LANES].astype(F32), cc, sc)
              * scale for g in range(GROUP)]
        q_rows = []
        for sub in range(2):
            keep = low_head if sub == 0 else jnp.logical_not(low_head)
            for g in range(GROUP):
                q_rows.append(jnp.where(keep, qc[g], 0.0).astype(BF16))
                sink_rows.append(
                    jnp.full((blk, LANES), sinks_ref[0, (2 * a + sub) * GROUP + g], F32))
        s_rows.append(_dot_nt(jnp.concatenate(q_rows, axis=0), k_win))
    valid_all = jnp.concatenate([valid] * (n_half * 2 * GROUP), axis=0)
    s = jnp.where(valid_all, jnp.concatenate(s_rows, axis=0), NEG_BIG)
    sink = jnp.concatenate(sink_rows, axis=0)
    s0, s1 = s[:, :LANES], s[:, LANES:]
    m = jnp.maximum(jnp.max(jnp.maximum(s0, s1), axis=-1, keepdims=True), sink)
    e0 = jnp.exp(s0 - m)
    e1 = jnp.exp(s1 - m)
    den = jnp.sum(e0 + e1, axis=-1, keepdims=True) + jnp.exp(sink - m)
    r = 1.0 / den
    p = jnp.concatenate([e0 * r, e1 * r], axis=1).astype(BF16)
    for a in range(n_half):
        pv = _dot(p[a * rows_per_half:(a + 1) * rows_per_half], v_wins[a])
        for g in range(GROUP):
            lo = pv[g * blk:(g + 1) * blk]
            hi = pv[(GROUP + g) * blk:(GROUP + g + 1) * blk]
            chunk = a * GROUP + g
            o_ref[:, chunk * LANES:(chunk + 1) * LANES] = (
                jnp.where(low_head, lo, hi).astype(o_ref.dtype))


def _attention(qkv, cos_t, sin_t, sinks, batch, seq):
    t = qkv.shape[0]
    nb = seq // ATTN_BLOCK
    kcol = Q_DIM // KV_DIM
    vcol = kcol + 1
    cur = lambda b, n: b * nb + n
    prev = lambda b, n: jnp.maximum(b * nb + n - 1, 0)
    return pl.pallas_call(
        _attn_kernel,
        out_shape=jax.ShapeDtypeStruct((t, Q_DIM), BF16),
        grid=(batch, nb),
        in_specs=[pl.BlockSpec(memory_space=pltpu.SMEM),
                  pl.BlockSpec((ATTN_BLOCK, Q_DIM), lambda b, n: (cur(b, n), 0)),
                  pl.BlockSpec((ATTN_BLOCK, KV_DIM), lambda b, n: (cur(b, n), kcol)),
                  pl.BlockSpec((ATTN_BLOCK, KV_DIM), lambda b, n: (prev(b, n), kcol)),
                  pl.BlockSpec((ATTN_BLOCK, KV_DIM), lambda b, n: (cur(b, n), vcol)),
                  pl.BlockSpec((ATTN_BLOCK, KV_DIM), lambda b, n: (prev(b, n), vcol)),
                  pl.BlockSpec((ATTN_BLOCK, LANES), lambda b, n: (cur(b, n), 0)),
                  pl.BlockSpec((ATTN_BLOCK, LANES), lambda b, n: (cur(b, n), 0)),
                  pl.BlockSpec((ATTN_BLOCK, LANES), lambda b, n: (prev(b, n), 0)),
                  pl.BlockSpec((ATTN_BLOCK, LANES), lambda b, n: (prev(b, n), 0))],
        out_specs=pl.BlockSpec((ATTN_BLOCK, Q_DIM), lambda b, n: (cur(b, n), 0)),
        compiler_params=_cparams(("arbitrary", "arbitrary")),
        name="attention",
    )(sinks, qkv, qkv, qkv, qkv, qkv, cos_t, sin_t, cos_t, sin_t)


SSD_STRIP = 256


def _ssd_kernel(z_ref, xin_ref, bcin_ref, xin_prev_ref, bcin_prev_ref, dt_ref, shift_ref, cw_ref,
                cb_ref, dtb_ref, alog_ref, dskip_ref, g_ref, expand_ref, o_ref,
                state_ref, xs_ref, b16_ref, c16_ref, xdd_ref, xd16_ref, yz_ref):
    c = pl.program_id(1)
    L = CHUNK
    W = SSD_STRIP
    bc_dim = SSM_GROUPS * D_STATE

    @pl.when(c == 0)
    def _():
        state_ref[...] = jnp.zeros_like(state_ref)

    n_shift = CONV_WIDTH - 1
    sel_col = lax.broadcasted_iota(jnp.int32, (n_shift * L, 2 * L), 1)
    sel = shift_ref[...]
    sel = jnp.where((sel_col < L) & (c == 0), jnp.zeros_like(sel), sel)
    for s in range(CONV_DIM // W):
        sl = slice(s * W, (s + 1) * W)
        if s * W < D_INNER:
            cur_ref, prev_ref, src = xin_ref, xin_prev_ref, sl
        else:
            cur_ref, prev_ref = bcin_ref, bcin_prev_ref
            src = slice(s * W - D_INNER, (s + 1) * W - D_INNER)
        x_cur = cur_ref[:, src]
        xwin = jnp.concatenate([prev_ref[:, src], x_cur], axis=0)
        taps = _dot(sel, xwin)
        acc = cb_ref[:, sl] + x_cur.astype(F32) * cw_ref[n_shift:CONV_WIDTH, sl]
        for w in range(n_shift):
            acc = acc + taps[w * L:(w + 1) * L] * cw_ref[w:w + 1, sl]
        xc = _silu(acc)
        if s * W < D_INNER:
            xs_ref[:, sl] = xc
        elif s * W < D_INNER + bc_dim:
            b16_ref[:, s * W - D_INNER:(s + 1) * W - D_INNER] = xc.astype(BF16)
        else:
            c16_ref[:, s * W - D_INNER - bc_dim:(s + 1) * W - D_INNER - bc_dim] = xc.astype(BF16)

    v = dt_ref[...] + dtb_ref[...]
    dt = jnp.maximum(v, 0.0) + jnp.log1p(jnp.exp(-jnp.abs(v)))
    a = dt * (jnp.exp(alog_ref[...]) * NEG_LOG2E)

    ri = lax.broadcasted_iota(jnp.int32, (L, L), 0)
    ci = lax.broadcasted_iota(jnp.int32, (L, L), 1)
    causal = ri >= ci
    tril = jnp.where(causal, 1.0, 0.0).astype(BF16)
    ones = jnp.ones((L, L), BF16)
    a1, a2, a3 = _split3(a)
    cs = _dot(tril, a1) + _dot(tril, a2) + _dot(tril, a3)
    tot = _dot(ones, a1) + _dot(ones, a2) + _dot(ones, a3)
    cs_t = cs.T[0:SSM_HEADS]

    lane = lax.broadcasted_iota(jnp.int32, (L, LANES), 1)

    def pack(q):
        q1, q2, q3 = _split3(q)
        packed = jnp.where(
            lane < SSM_HEADS, q1.astype(F32),
            jnp.where(lane < 2 * SSM_HEADS, pltpu.roll(q2.astype(F32), SSM_HEADS, 1),
                      jnp.where(lane < 3 * SSM_HEADS,
                                pltpu.roll(q3.astype(F32), 2 * SSM_HEADS, 1), 0.0)))
        return packed.astype(BF16)

    pk_all = jnp.concatenate([pack(dt), pack(cs), pack(tot)], axis=0)
    pk_ct = pk_all[L:]

    for s in range(D_INNER // W):
        sl = slice(s * W, (s + 1) * W)
        ex = _dot(pk_all, expand_ref[:, sl])
        xd = xs_ref[:, sl] * ex[0:L]
        xd16_ref[:, sl] = xd.astype(BF16)
        xdd_ref[:, sl] = (xd * jnp.exp2(ex[2 * L:3 * L] - ex[L:2 * L])).astype(BF16)

    low_head = lane < SSM_HEAD_DIM
    heads_per_strip = W // SSM_HEAD_DIM
    strips_per_group = HEADS_PER_GROUP // heads_per_strip
    ssq = jnp.zeros((L, LANES), F32)

    for g in range(SSM_GROUPS):
        nsl = slice(g * D_STATE, (g + 1) * D_STATE)
        bg = b16_ref[:, nsl]
        cg = c16_ref[:, nsl]
        bg_t = bg.astype(F32).T.astype(BF16)
        cb = jnp.where(causal, _dot_nt(cg, bg), 0.0)
        for k in range(strips_per_group):
            s = g * strips_per_group + k
            sl = slice(s * W, (s + 1) * W)
            ex = _dot(pk_ct, expand_ref[:, sl])
            st_prev = state_ref[:, sl]
            y_off = _dot(cg, st_prev.astype(BF16))
            state_ref[:, sl] = st_prev * jnp.exp2(ex[L:2 * L]) + _dot(bg_t, xdd_ref[:, sl])
            y_pairs = []
            for pair in range(W // LANES):
                psl = slice(s * W + pair * LANES, s * W + (pair + 1) * LANES)
                xp = xd16_ref[:, psl]
                mhs, xps = [], []
                for sub in range(2):
                    h = s * heads_per_strip + pair * 2 + sub
                    diff = cs[:, h:h + 1] - cs_t[h:h + 1, :]
                    decay = jnp.exp2(jnp.minimum(diff, 0.0))
                    mhs.append((cb * decay).astype(BF16))
                    keep = low_head if sub == 0 else jnp.logical_not(low_head)
                    xps.append(jnp.where(keep, xp, jnp.zeros_like(xp)))
                y_pairs.append(_dot(jnp.concatenate(mhs, axis=1), jnp.concatenate(xps, axis=0)))
            y = (jnp.concatenate(y_pairs, axis=1) + y_off * jnp.exp2(ex[0:L])
                 + dskip_ref[:, sl] * xs_ref[:, sl])
            yz = y * _silu(z_ref[:, sl].astype(F32))
            yz_ref[:, sl] = yz
            for pair in range(W // LANES):
                part = yz[:, pair * LANES:(pair + 1) * LANES]
                ssq = ssq + part * part

    var = jnp.sum(ssq, axis=-1, keepdims=True) * (1.0 / D_INNER)
    rs = lax.rsqrt(var + SSM_NORM_EPS)
    for s in range(D_INNER // W):
        sl = slice(s * W, (s + 1) * W)
        o_ref[:, sl] = (yz_ref[:, sl] * rs * g_ref[:, sl]).astype(o_ref.dtype)


def _ssd(zx, dtp, shift, conv_w, conv_b, dtb_p, alog_p, dskip_x, g_ssd, expand, batch, seq):
    t = zx.shape[0]
    nc = seq // CHUNK
    bc_dim = 2 * SSM_GROUPS * D_STATE
    row = lambda b, c: (b * nc + c, 0)
    cur = lambda b, c: b * nc + c
    prev = lambda b, c: jnp.maximum(b * nc + c - 1, 0)
    x_col = 1
    bc_col = 2 * D_INNER // bc_dim
    return pl.pallas_call(
        _ssd_kernel,
        out_shape=jax.ShapeDtypeStruct((t, D_INNER), BF16),
        grid=(batch, nc),
        in_specs=[pl.BlockSpec((CHUNK, D_INNER), row),
                  pl.BlockSpec((CHUNK, D_INNER), lambda b, c: (cur(b, c), x_col)),
                  pl.BlockSpec((CHUNK, bc_dim), lambda b, c: (cur(b, c), bc_col)),
                  pl.BlockSpec((CHUNK, D_INNER), lambda b, c: (prev(b, c), x_col)),
                  pl.BlockSpec((CHUNK, bc_dim), lambda b, c: (prev(b, c), bc_col)),
                  pl.BlockSpec((CHUNK, LANES), row),
                  _resident(((CONV_WIDTH - 1) * CHUNK, 2 * CHUNK)),
                  _resident((CONV_WIDTH, CONV_DIM)),
                  _resident((1, CONV_DIM)),
                  _resident((1, LANES)),
                  _resident((1, LANES)),
                  _resident((1, D_INNER)),
                  _resident((1, D_INNER)),
                  _resident((LANES, D_INNER))],
        out_specs=pl.BlockSpec((CHUNK, D_INNER), row),
        scratch_shapes=[pltpu.VMEM((D_STATE, D_INNER), F32),
                        pltpu.VMEM((CHUNK, D_INNER), F32),
                        pltpu.VMEM((CHUNK, SSM_GROUPS * D_STATE), BF16),
                        pltpu.VMEM((CHUNK, SSM_GROUPS * D_STATE), BF16),
                        pltpu.VMEM((CHUNK, D_INNER), BF16),
                        pltpu.VMEM((CHUNK, D_INNER), BF16),
                        pltpu.VMEM((CHUNK, D_INNER), F32)],
        compiler_params=_cparams(("arbitrary", "arbitrary")),
        name="ssd",
    )(zx, zx, zx, zx, zx, dtp, shift, conv_w, conv_b, dtb_p, alog_p, dskip_x, g_ssd, expand)


def _merge_kernel(attn_ref, y_ref, ga_ref, gs_ref, wa_ref, ws_ref, o_ref, wa16_ref, ws16_ref):
    @pl.when(pl.program_id(1) == 0)
    def _():
        _cast_rows(wa16_ref, wa_ref, True)
        ws16_ref[...] = ws_ref[...].astype(BF16)

    out_a = _dot(attn_ref[...], wa16_ref[...])
    out_s = _dot(y_ref[...], ws16_ref[...])
    merged = (_sigmoid(ga_ref[...].astype(F32)) * out_a
              + _sigmoid(gs_ref[...].astype(F32)) * out_s)
    o_ref[...] = merged.astype(o_ref.dtype)


def _merge(attn, y, gates, wa, ws, tm=512, tn=1024):
    t = attn.shape[0]
    nj = D_MODEL // tn
    return pl.pallas_call(
        _merge_kernel,
        out_shape=jax.ShapeDtypeStruct((t, D_MODEL), BF16),
        grid=(nj, t // tm),
        in_specs=[pl.BlockSpec((tm, Q_DIM), lambda j, i: (i, 0)),
                  pl.BlockSpec((tm, D_INNER), lambda j, i: (i, 0)),
                  pl.BlockSpec((tm, tn), lambda j, i: (i, j)),
                  pl.BlockSpec((tm, tn), lambda j, i: (i, nj + j)),
                  pl.BlockSpec((Q_DIM, tn), lambda j, i: (0, j)),
                  pl.BlockSpec((D_INNER, tn), lambda j, i: (0, j))],
        out_specs=pl.BlockSpec((tm, tn), lambda j, i: (i, j)),
        scratch_shapes=[pltpu.VMEM((Q_DIM, tn), BF16),
                        pltpu.VMEM((D_INNER, tn), BF16)],
        compiler_params=_cparams(("arbitrary", "arbitrary")),
        name="merge",
    )(attn, y, gates, gates, wa, ws)


def _oproj_kernel(m_ref, x_ref, w_ref, g_ref, h_ref, f_ref, w16_ref):
    @pl.when(pl.program_id(0) == 0)
    def _():
        w16_ref[...] = w_ref[...].astype(BF16)

    h = x_ref[...] + _dot(m_ref[...], w16_ref[...])
    h_ref[...] = h
    f_ref[...] = _rms(h, g_ref[...], NORM_EPS).astype(BF16)


def _oproj(merged, x2, wo, g_ffn, tm=512):
    t = merged.shape[0]
    row = lambda i: (i, 0)
    return pl.pallas_call(
        _oproj_kernel,
        out_shape=(jax.ShapeDtypeStruct((t, D_MODEL), F32),
                   jax.ShapeDtypeStruct((t, D_MODEL), BF16)),
        grid=(t // tm,),
        in_specs=[pl.BlockSpec((tm, D_MODEL), row),
                  pl.BlockSpec((tm, D_MODEL), row),
                  _resident((D_MODEL, D_MODEL)),
                  _resident((1, D_MODEL))],
        out_specs=(pl.BlockSpec((tm, D_MODEL), row),
                   pl.BlockSpec((tm, D_MODEL), row)),
        scratch_shapes=[pltpu.VMEM((D_MODEL, D_MODEL), BF16)],
        compiler_params=_cparams(("arbitrary",)),
        name="oproj",
    )(merged, x2, wo, g_ffn)


def _ffn_up_kernel(f_ref, wg_ref, wu_ref, o_ref, wg16_ref, wu16_ref):
    @pl.when(pl.program_id(1) == 0)
    def _():
        wg16_ref[...] = wg_ref[...].astype(BF16)
        wu16_ref[...] = wu_ref[...].astype(BF16)

    f = f_ref[...]
    o_ref[...] = (_silu(_dot(f, wg16_ref[...])) * _dot(f, wu16_ref[...])).astype(o_ref.dtype)


def _ffn_up(f, wg, wu, tm=2048, th=512):
    t = f.shape[0]
    return pl.pallas_call(
        _ffn_up_kernel,
        out_shape=jax.ShapeDtypeStruct((t, FFN_HIDDEN), BF16),
        grid=(FFN_HIDDEN // th, t // tm),
        in_specs=[pl.BlockSpec((tm, D_MODEL), lambda j, i: (i, 0)),
                  pl.BlockSpec((D_MODEL, th), lambda j, i: (0, j)),
                  pl.BlockSpec((D_MODEL, th), lambda j, i: (0, j))],
        out_specs=pl.BlockSpec((tm, th), lambda j, i: (i, j)),
        scratch_shapes=[pltpu.VMEM((D_MODEL, th), BF16),
                        pltpu.VMEM((D_MODEL, th), BF16)],
        compiler_params=_cparams(("arbitrary", "arbitrary")),
        name="ffn_up",
    )(f, wg, wu)


def _ffn_down_kernel(a_ref, h_ref, wd_ref, o_ref, wd16_ref):
    @pl.when(pl.program_id(1) == 0)
    def _():
        wd16_ref[...] = wd_ref[...].astype(BF16)

    o_ref[...] = h_ref[...] + _dot(a_ref[...], wd16_ref[...])


def _ffn_down(act, h1, wd, tm=1024, tn=512):
    t = act.shape[0]
    return pl.pallas_call(
        _ffn_down_kernel,
        out_shape=jax.ShapeDtypeStruct((t, D_MODEL), F32),
        grid=(D_MODEL // tn, t // tm),
        in_specs=[pl.BlockSpec((tm, FFN_HIDDEN), lambda j, i: (i, 0)),
                  pl.BlockSpec((tm, tn), lambda j, i: (i, j)),
                  pl.BlockSpec((FFN_HIDDEN, tn), lambda j, i: (0, j),
                               pipeline_mode=pl.Buffered(1))],
        out_specs=pl.BlockSpec((tm, tn), lambda j, i: (i, j)),
        scratch_shapes=[pltpu.VMEM((FFN_HIDDEN, tn), BF16)],
        compiler_params=_cparams(("arbitrary", "arbitrary")),
        name="ffn_down",
    )(act, h1, wd)


def _ple_kernel(h_ref, p_ref, gp_ref, gf_ref, wg_ref, wp_ref, o_ref, wg16_ref, wp16_ref):
    @pl.when(pl.program_id(0) == 0)
    def _():
        wg16_ref[...] = wg_ref[...].astype(BF16)
        wp16_ref[...] = wp_ref[...].astype(BF16)

    h = h_ref[...]
    r = _rms(h, gp_ref[...], NORM_EPS).astype(BF16)
    gate = _sigmoid(_dot(r, wg16_ref[...]))
    h3 = h + gate * _dot(p_ref[...].astype(BF16), wp16_ref[...])
    o_ref[...] = _rms(h3, gf_ref[...], NORM_EPS)


def _ple(h2, p2, g_ple, g_final, wpg, wpp, tm=512):
    t = h2.shape[0]
    row = lambda i: (i, 0)
    return pl.pallas_call(
        _ple_kernel,
        out_shape=jax.ShapeDtypeStruct((t, D_MODEL), F32),
        grid=(t // tm,),
        in_specs=[pl.BlockSpec((tm, D_MODEL), row),
                  pl.BlockSpec((tm, PLE_DIM), row),
                  _resident((1, D_MODEL)),
                  _resident((1, D_MODEL)),
                  _resident((D_MODEL, D_MODEL)),
                  _resident((PLE_DIM, D_MODEL))],
        out_specs=pl.BlockSpec((tm, D_MODEL), row),
        scratch_shapes=[pltpu.VMEM((D_MODEL, D_MODEL), BF16),
                        pltpu.VMEM((PLE_DIM, D_MODEL), BF16)],
        compiler_params=_cparams(("arbitrary",)),
        name="ple",
    )(h2, p2, g_ple, g_final, wpg, wpp)


def _rope_constants():
    half = HEAD_DIM // 2
    lane = np.arange(LANES)
    inv_freq = ROPE_THETA ** (-jnp.arange(half, dtype=F32) * 2.0 / HEAD_DIM)
    invf = inv_freq[lane % half][None, :]
    sgn = jnp.asarray(np.where((lane % HEAD_DIM) < half, -1.0, 1.0), F32)[None, :]
    return invf, sgn


def _expand_matrix():
    rows = np.arange(LANES)[:, None]
    cols = np.arange(D_INNER)[None, :]
    hit = (rows < 3 * SSM_HEADS) & ((rows % SSM_HEADS) == (cols // SSM_HEAD_DIM))
    return jnp.asarray(hit, BF16)


def _shift_matrix():
    rows = np.arange((CONV_WIDTH - 1) * CHUNK)[:, None]
    cols = np.arange(2 * CHUNK)[None, :]
    w, t = rows // CHUNK, rows % CHUNK
    return jnp.asarray(cols == CHUNK + t - (CONV_WIDTH - 1) + w, BF16)


def _pad_lanes(v):
    return jnp.pad(v.astype(F32), (0, LANES - v.shape[0]))[None, :]


def kernel(x, p, positions, g_mix, w_in, conv_w, conv_b, dt_bias, a_log, d_skip, g_ssd,
           sinks, w_attn_br, w_ssd_br, w_o, g_ffn, w_gate, w_up, w_down, g_ple,
           w_ple_gate, w_ple_proj, g_final):
    b, s, d = x.shape
    t = b * s
    assert d == D_MODEL and s % CHUNK == 0 and p.shape[0] == 1
    i = 0
    x2 = x.reshape(t, d)
    p2 = p[i].reshape(t, PLE_DIM)
    pos2 = positions.reshape(t, 1)

    o_k = Q_DIM
    o_z = Q_DIM + 2 * KV_DIM
    o_xbc = o_z + D_INNER
    o_dt = o_xbc + CONV_DIM
    o_ga = o_dt + SSM_HEADS
    wt = jnp.swapaxes(w_in[i], 0, 1)

    invf, sgn = _rope_constants()
    u, cos_t, sin_t, dtp = _prep(x2, g_mix[i][None, :], pos2, invf, sgn, wt, o_dt)

    qkv = _inproj(u, wt, 0, o_z, BF16, 2048, PAIR_ROWS, "proj_qkv", paired_tiles=Q_DIM // PAIR_ROWS)
    zx = _inproj(u, wt, o_z, D_INNER + CONV_DIM, BF16, 2048, 1024, "proj_zx")
    gates = _inproj(u, wt, o_ga, 2 * D_MODEL, BF16, 2048, 1024, "proj_gates")

    attn = _attention(qkv, cos_t, sin_t, sinks[i][None, :].astype(F32), b, s)

    y = _ssd(zx, dtp, _shift_matrix(), conv_w[i], conv_b[i][None, :], _pad_lanes(dt_bias[i]),
             _pad_lanes(a_log[i]), jnp.repeat(d_skip[i].astype(F32), SSM_HEAD_DIM)[None, :],
             g_ssd[i][None, :], _expand_matrix(), b, s)

    merged = _merge(attn, y, gates, w_attn_br[i], w_ssd_br[i])
    h1, f = _oproj(merged, x2, w_o[i], g_ffn[i][None, :])
    act = _ffn_up(f, w_gate[i], w_up[i])
    h2 = _ffn_down(act, h1, w_down[i])
    out = _ple(h2, p2, g_ple[i][None, :], g_final[None, :], w_ple_gate[i], w_ple_proj[i])
    return out.reshape(b, s, d)
```

```python
from functools import partial

import numpy as np
import jax
import jax.numpy as jnp
from jax import lax
from jax.experimental import pallas as pl
from jax.experimental.pallas import tpu as pltpu

F32 = jnp.float32
BF16 = jnp.bfloat16

D_MODEL = 2048
HEAD_DIM = 64
ATTN_HEADS = 16
KV_HEADS = 4
GROUP = ATTN_HEADS // KV_HEADS
Q_DIM = ATTN_HEADS * HEAD_DIM
KV_DIM = KV_HEADS * HEAD_DIM
ATTN_BLOCK = 128
ROPE_THETA = 10000.0
D_INNER = 2048
SSM_HEAD_DIM = 64
SSM_HEADS = 32
SSM_GROUPS = 4
HEADS_PER_GROUP = SSM_HEADS // SSM_GROUPS
D_STATE = 128
CONV_WIDTH = 4
CHUNK = 128
CONV_DIM = D_INNER + 2 * SSM_GROUPS * D_STATE
FFN_HIDDEN = 5632
PLE_DIM = 256
NORM_EPS = 1e-6
SSM_NORM_EPS = 1e-5

LANES = 128
SUBLANES = 8
VMEM_LIMIT_BYTES = 56 * 1024 * 1024

NEG_BIG = -1e30


def _cparams(semantics):
    return pltpu.CompilerParams(dimension_semantics=semantics,
                                vmem_limit_bytes=VMEM_LIMIT_BYTES)


def _resident(shape):
    return pl.BlockSpec(shape, lambda *_: (0,) * len(shape),
                        pipeline_mode=pl.Buffered(1))


def _rms(xf, g, eps):
    var = jnp.mean(xf * xf, axis=-1, keepdims=True)
    return xf * lax.rsqrt(var + eps) * g


NEG_LOG2E = -1.4426950408889634


def _sigmoid(x):
    return 1.0 / (1.0 + jnp.exp2(x * NEG_LOG2E))


def _silu(x):
    return x * _sigmoid(x)


def _dot(a, b):
    return jnp.dot(a, b, preferred_element_type=F32)


def _dot_nt(a, b):
    return lax.dot_general(a, b, (((1,), (1,)), ((), ())),
                           preferred_element_type=F32)


def _split3(v):
    v1 = v.astype(BF16)
    r1 = v - v1.astype(F32)
    v2 = r1.astype(BF16)
    v3 = (r1 - v2.astype(F32)).astype(BF16)
    return v1, v2, v3


ROPE_FREQS = HEAD_DIM // 2
POS_PER_ROW = LANES // ROPE_FREQS


def _prep_kernel(x_ref, g_ref, pos_ref, invf_ref, sgn_ref, wdt_ref, u_ref, cos_ref, sin_ref, dt_ref):
    u = _rms(x_ref[...], g_ref[...], NORM_EPS).astype(BF16)
    u_ref[...] = u
    rows = pos_ref.shape[0]
    ang = pos_ref[...].astype(F32) * invf_ref[...]
    cos_c = jnp.cos(ang)
    sin_c = jnp.sin(ang)
    group = lax.broadcasted_iota(jnp.int32, ang.shape, 1) // ROPE_FREQS
    for q in range(POS_PER_ROW):
        for src, dst_ref, sign in ((cos_c, cos_ref, None), (sin_c, sin_ref, sgn_ref[...])):
            only = jnp.where(group == q, src, 0.0)
            rep = only
            for k in range(1, POS_PER_ROW):
                rep = rep + pltpu.roll(only, k * ROPE_FREQS, 1)
            if sign is not None:
                rep = rep * sign
            dst_ref[pl.ds(q, rows, stride=POS_PER_ROW), :] = rep
    dt_ref[...] = _dot_nt(u, wdt_ref[...].astype(BF16))


def _prep(x2, g_mix, pos_c, invf, sgn, wt, dt_row0, tm=1024):
    t = x2.shape[0]
    row = lambda i: (i, 0)
    assert dt_row0 % SUBLANES == 0 and tm % (POS_PER_ROW * SUBLANES) == 0
    return pl.pallas_call(
        _prep_kernel,
        out_shape=(jax.ShapeDtypeStruct((t, D_MODEL), BF16),
                   jax.ShapeDtypeStruct((t, LANES), F32),
                   jax.ShapeDtypeStruct((t, LANES), F32),
                   jax.ShapeDtypeStruct((t, LANES), F32)),
        grid=(t // tm,),
        in_specs=[pl.BlockSpec((tm, D_MODEL), row),
                  _resident((1, D_MODEL)),
                  pl.BlockSpec((tm // POS_PER_ROW, LANES), row),
                  _resident((1, LANES)),
                  _resident((1, LANES)),
                  pl.BlockSpec((pl.Element(LANES), pl.Element(D_MODEL)),
                               lambda i: (dt_row0, 0), pipeline_mode=pl.Buffered(1))],
        out_specs=(pl.BlockSpec((tm, D_MODEL), row),
                   pl.BlockSpec((tm, LANES), row),
                   pl.BlockSpec((tm, LANES), row),
                   pl.BlockSpec((tm, LANES), row)),
        compiler_params=_cparams(("arbitrary",)),
        name="prep",
    )(x2, g_mix, pos_c, invf, sgn, wt)


PAIR_ROWS = 2 * GROUP * HEAD_DIM


def _paired_head_blocks():
    return [(g * 2 + sub, sub * GROUP + g) for g in range(GROUP) for sub in range(2)]


def _cast_rows(dst_ref, src_ref, paired_heads):
    if not paired_heads:
        dst_ref[...] = src_ref[...].astype(BF16)
        return
    assert dst_ref.shape[0] % PAIR_ROWS == 0
    for base in range(0, dst_ref.shape[0], PAIR_ROWS):
        for dst, src in _paired_head_blocks():
            dst_ref[base + dst * HEAD_DIM:base + (dst + 1) * HEAD_DIM, :] = (
                src_ref[base + src * HEAD_DIM:base + (src + 1) * HEAD_DIM, :].astype(BF16))


def _inproj_kernel(a_ref, wt_ref, o_ref, wbf_ref, *, paired_tiles):
    j = pl.program_id(0)

    @pl.when((pl.program_id(1) == 0) & (j >= paired_tiles))
    def _():
        _cast_rows(wbf_ref, wt_ref, False)

    if paired_tiles:
        @pl.when((pl.program_id(1) == 0) & (j < paired_tiles))
        def _():
            _cast_rows(wbf_ref, wt_ref, True)

    o_ref[...] = _dot_nt(a_ref[...], wbf_ref[...]).astype(o_ref.dtype)


def _inproj(a, wt, row0, n, out_dtype, tm, tn, name, paired_tiles=0):
    m, k = a.shape
    assert n % tn == 0 and row0 % SUBLANES == 0
    assert not paired_tiles or tn % PAIR_ROWS == 0
    return pl.pallas_call(
        partial(_inproj_kernel, paired_tiles=paired_tiles),
        out_shape=jax.ShapeDtypeStruct((m, n), out_dtype),
        grid=(n // tn, m // tm),
        in_specs=[pl.BlockSpec((tm, k), lambda j, i: (i, 0)),
                  pl.BlockSpec((pl.Element(tn), pl.Element(k)),
                               lambda j, i: (pl.multiple_of(row0 + j * tn, SUBLANES), 0))],
        out_specs=pl.BlockSpec((tm, tn), lambda j, i: (i, j)),
        scratch_shapes=[pltpu.VMEM((tn, k), BF16)],
        compiler_params=_cparams(("arbitrary", "arbitrary")),
        name=name,
    )(a, wt)


ATTN_STEP_BLOCKS = 2


def _attn_kernel(sinks_ref, q_ref, kc_ref, kp_ref, vc_ref, vp_ref,
                 cc_ref, sc_ref, cp_ref, sp_ref, o_ref):
    n = pl.program_id(1)
    blk = ATTN_BLOCK
    nblk = ATTN_STEP_BLOCKS
    lane = lax.broadcasted_iota(jnp.int32, (blk, LANES), 1)
    first_half = (lane % HEAD_DIM) < (HEAD_DIM // 2)
    low_head = lane < HEAD_DIM

    def rope(t, c, s):
        partner = jnp.where(first_half,
                            pltpu.roll(t, LANES - HEAD_DIM // 2, 1),
                            pltpu.roll(t, HEAD_DIM // 2, 1))
        return t * c + partner * s

    def rows(j):
        return slice(j * blk, (j + 1) * blk)

    cp, sp = cp_ref[...], sp_ref[...]
    cc = [cc_ref[rows(j), :] for j in range(nblk)]
    sc = [sc_ref[rows(j), :] for j in range(nblk)]

    qi = lax.broadcasted_iota(jnp.int32, (blk, 2 * blk), 0)
    kj = lax.broadcasted_iota(jnp.int32, (blk, 2 * blk), 1)
    rel = kj - qi
    band = (rel >= 1) & (rel <= blk)
    valid = [band & ((kj >= blk) | (n > 0))] + [band] * (nblk - 1)

    scale = HEAD_DIM ** -0.5
    n_half = KV_DIM // LANES
    rows_per_half = 2 * GROUP * blk
    s_rows, sink_rows, valid_rows, v_wins = [], [], [], []
    for a in range(n_half):
        ksl = slice(a * LANES, (a + 1) * LANES)
        k_blocks = [rope(kp_ref[:, ksl].astype(F32), cp, sp).astype(BF16)]
        v_blocks = [vp_ref[:, ksl]]
        for j in range(nblk):
            k_blocks.append(rope(kc_ref[rows(j), ksl].astype(F32), cc[j], sc[j]).astype(BF16))
            v_blocks.append(vc_ref[rows(j), ksl])
        for j in range(nblk):
            k_win = jnp.concatenate(k_blocks[j:j + 2], axis=0)
            v_wins.append(jnp.concatenate(v_blocks[j:j + 2], axis=0))
            qc = [rope(q_ref[rows(j), (a * GROUP + g) * LANES:(a * GROUP + g + 1) * LANES]
                       .astype(F32), cc[j] * scale, sc[j] * scale) for g in range(GROUP)]
            q_rows = []
            for sub in range(2):
                keep = low_head if sub == 0 else jnp.logical_not(low_head)
                for g in range(GROUP):
                    q_rows.append(jnp.where(keep, qc[g], 0.0).astype(BF16))
                    sink_rows.append(
                        jnp.full((blk, LANES), sinks_ref[0, (2 * a + sub) * GROUP + g], F32))
            s_rows.append(_dot_nt(jnp.concatenate(q_rows, axis=0), k_win))
            valid_rows.extend([valid[j]] * (2 * GROUP))
    s = jnp.where(jnp.concatenate(valid_rows, axis=0), jnp.concatenate(s_rows, axis=0), NEG_BIG)
    sink = jnp.concatenate(sink_rows, axis=0)
    s0, s1 = s[:, :LANES], s[:, LANES:]
    m = jnp.maximum(jnp.max(jnp.maximum(s0, s1), axis=-1, keepdims=True), sink)
    e0 = jnp.exp(s0 - m)
    e1 = jnp.exp(s1 - m)
    den = jnp.sum(e0 + e1, axis=-1, keepdims=True) + jnp.exp(sink - m)
    r = 1.0 / den
    p = jnp.concatenate([e0 * r, e1 * r], axis=1).astype(BF16)
    for a in range(n_half):
        for j in range(nblk):
            w = a * nblk + j
            pv = _dot(p[w * rows_per_half:(w + 1) * rows_per_half], v_wins[w])
            for g in range(GROUP):
                lo = pv[g * blk:(g + 1) * blk]
                hi = pv[(GROUP + g) * blk:(GROUP + g + 1) * blk]
                chunk = a * GROUP + g
                o_ref[rows(j), chunk * LANES:(chunk + 1) * LANES] = (
                    jnp.where(low_head, lo, hi).astype(o_ref.dtype))


def _attention(qkv, cos_t, sin_t, sinks, batch, seq):
    t = qkv.shape[0]
    step = ATTN_STEP_BLOCKS * ATTN_BLOCK
    assert seq % step == 0
    ns = seq // step
    kcol = Q_DIM // KV_DIM
    vcol = kcol + 1
    cur = lambda b, n: b * ns + n
    prev = lambda b, n: jnp.maximum((b * ns + n) * ATTN_STEP_BLOCKS - 1, 0)
    return pl.pallas_call(
        _attn_kernel,
        out_shape=jax.ShapeDtypeStruct((t, Q_DIM), BF16),
        grid=(batch, ns),
        in_specs=[pl.BlockSpec(memory_space=pltpu.SMEM),
                  pl.BlockSpec((step, Q_DIM), lambda b, n: (cur(b, n), 0)),
                  pl.BlockSpec((step, KV_DIM), lambda b, n: (cur(b, n), kcol)),
                  pl.BlockSpec((ATTN_BLOCK, KV_DIM), lambda b, n: (prev(b, n), kcol)),
                  pl.BlockSpec((step, KV_DIM), lambda b, n: (cur(b, n), vcol)),
                  pl.BlockSpec((ATTN_BLOCK, KV_DIM), lambda b, n: (prev(b, n), vcol)),
                  pl.BlockSpec((step, LANES), lambda b, n: (cur(b, n), 0)),
                  pl.BlockSpec((step, LANES), lambda b, n: (cur(b, n), 0)),
                  pl.BlockSpec((ATTN_BLOCK, LANES), lambda b, n: (prev(b, n), 0)),
                  pl.BlockSpec((ATTN_BLOCK, LANES), lambda b, n: (prev(b, n), 0))],
        out_specs=pl.BlockSpec((step, Q_DIM), lambda b, n: (cur(b, n), 0)),
        compiler_params=_cparams(("arbitrary", "arbitrary")),
        name="attention",
    )(sinks, qkv, qkv, qkv, qkv, qkv, cos_t, sin_t, cos_t, sin_t)


SSD_STRIP = 256


def _ssd_kernel(z_ref, xin_ref, bcin_ref, xin_prev_ref, bcin_prev_ref, dt_ref, shift_ref, cw_ref,
                cb_ref, dtb_ref, alog_ref, dskip_ref, g_ref, expand_ref, o_ref,
                state_ref, xs_ref, b16_ref, c16_ref, xdd_ref, xd16_ref, yz_ref):
    c = pl.program_id(1)
    L = CHUNK
    W = SSD_STRIP
    bc_dim = SSM_GROUPS * D_STATE

    @pl.when(c == 0)
    def _():
        state_ref[...] = jnp.zeros_like(state_ref)

    n_shift = CONV_WIDTH - 1
    sel_col = lax.broadcasted_iota(jnp.int32, (n_shift * L, 2 * L), 1)
    sel = shift_ref[...]
    sel = jnp.where((sel_col < L) & (c == 0), jnp.zeros_like(sel), sel)
    for s in range(CONV_DIM // W):
        sl = slice(s * W, (s + 1) * W)
        if s * W < D_INNER:
            cur_ref, prev_ref, src = xin_ref, xin_prev_ref, sl
        else:
            cur_ref, prev_ref = bcin_ref, bcin_prev_ref
            src = slice(s * W - D_INNER, (s + 1) * W - D_INNER)
        x_cur = cur_ref[:, src]
        xwin = jnp.concatenate([prev_ref[:, src], x_cur], axis=0)
        taps = _dot(sel, xwin)
        acc = cb_ref[:, sl] + x_cur.astype(F32) * cw_ref[n_shift:CONV_WIDTH, sl]
        for w in range(n_shift):
            acc = acc + taps[w * L:(w + 1) * L] * cw_ref[w:w + 1, sl]
        xc = _silu(acc)
        if s * W < D_INNER:
            xs_ref[:, sl] = xc
        elif s * W < D_INNER + bc_dim:
            b16_ref[:, s * W - D_INNER:(s + 1) * W - D_INNER] = xc.astype(BF16)
        else:
            c16_ref[:, s * W - D_INNER - bc_dim:(s + 1) * W - D_INNER - bc_dim] = xc.astype(BF16)

    v = dt_ref[...] + dtb_ref[...]
    dt = jnp.maximum(v, 0.0) + jnp.log1p(jnp.exp(-jnp.abs(v)))
    a = dt * (jnp.exp(alog_ref[...]) * NEG_LOG2E)

    ri = lax.broadcasted_iota(jnp.int32, (L, L), 0)
    ci = lax.broadcasted_iota(jnp.int32, (L, L), 1)
    causal = ri >= ci
    tril = jnp.where(causal, 1.0, 0.0).astype(BF16)
    ones = jnp.ones((L, L), BF16)
    a1, a2, a3 = _split3(a)
    cs = _dot(tril, a1) + _dot(tril, a2) + _dot(tril, a3)
    tot = _dot(ones, a1) + _dot(ones, a2) + _dot(ones, a3)
    cs_t = cs.T[0:SSM_HEADS]

    lane = lax.broadcasted_iota(jnp.int32, (L, LANES), 1)

    def pack(q):
        q1, q2, q3 = _split3(q)
        packed = jnp.where(
            lane < SSM_HEADS, q1.astype(F32),
            jnp.where(lane < 2 * SSM_HEADS, pltpu.roll(q2.astype(F32), SSM_HEADS, 1),
                      jnp.where(lane < 3 * SSM_HEADS,
                                pltpu.roll(q3.astype(F32), 2 * SSM_HEADS, 1), 0.0)))
        return packed.astype(BF16)

    pk_all = jnp.concatenate([pack(dt), pack(cs), pack(tot)], axis=0)
    pk_ct = pk_all[L:]

    for s in range(D_INNER // W):
        sl = slice(s * W, (s + 1) * W)
        ex = _dot(pk_all, expand_ref[:, sl])
        xd = xs_ref[:, sl] * ex[0:L]
        xd16_ref[:, sl] = xd.astype(BF16)
        xdd_ref[:, sl] = (xd * jnp.exp2(ex[2 * L:3 * L] - ex[L:2 * L])).astype(BF16)

    low_head = lane < SSM_HEAD_DIM
    heads_per_strip = W // SSM_HEAD_DIM
    strips_per_group = HEADS_PER_GROUP // heads_per_strip
    ssq = jnp.zeros((L, LANES), F32)

    for g in range(SSM_GROUPS):
        nsl = slice(g * D_STATE, (g + 1) * D_STATE)
        bg = b16_ref[:, nsl]
        cg = c16_ref[:, nsl]
        bg_t = bg.astype(F32).T.astype(BF16)
        cb = jnp.where(causal, _dot_nt(cg, bg), 0.0)
        for k in range(strips_per_group):
            s = g * strips_per_group + k
            sl = slice(s * W, (s + 1) * W)
            ex = _dot(pk_ct, expand_ref[:, sl])
            st_prev = state_ref[:, sl]
            y_off = _dot(cg, st_prev.astype(BF16))
            state_ref[:, sl] = st_prev * jnp.exp2(ex[L:2 * L]) + _dot(bg_t, xdd_ref[:, sl])
            y_pairs = []
            for pair in range(W // LANES):
                psl = slice(s * W + pair * LANES, s * W + (pair + 1) * LANES)
                xp = xd16_ref[:, psl]
                mhs, xps = [], []
                for sub in range(2):
                    h = s * heads_per_strip + pair * 2 + sub
                    diff = cs[:, h:h + 1] - cs_t[h:h + 1, :]
                    decay = jnp.exp2(jnp.minimum(diff, 0.0))
                    mhs.append((cb * decay).astype(BF16))
                    keep = low_head if sub == 0 else jnp.logical_not(low_head)
                    xps.append(jnp.where(keep, xp, jnp.zeros_like(xp)))
                y_pairs.append(_dot(jnp.concatenate(mhs, axis=1), jnp.concatenate(xps, axis=0)))
            y = (jnp.concatenate(y_pairs, axis=1) + y_off * jnp.exp2(ex[0:L])
                 + dskip_ref[:, sl] * xs_ref[:, sl])
            yz = y * _silu(z_ref[:, sl].astype(F32))
            yz_ref[:, sl] = yz
            for pair in range(W // LANES):
                part = yz[:, pair * LANES:(pair + 1) * LANES]
                ssq = ssq + part * part

    var = jnp.sum(ssq, axis=-1, keepdims=True) * (1.0 / D_INNER)
    rs = lax.rsqrt(var + SSM_NORM_EPS)
    for s in range(D_INNER // W):
        sl = slice(s * W, (s + 1) * W)
        o_ref[:, sl] = (yz_ref[:, sl] * rs * g_ref[:, sl]).astype(o_ref.dtype)


def _ssd(zx, dtp, shift, conv_w, conv_b, dtb_p, alog_p, dskip_x, g_ssd, expand, batch, seq):
    t = zx.shape[0]
    nc = seq // CHUNK
    bc_dim = 2 * SSM_GROUPS * D_STATE
    row = lambda b, c: (b * nc + c, 0)
    cur = lambda b, c: b * nc + c
    prev = lambda b, c: jnp.maximum(b * nc + c - 1, 0)
    x_col = 1
    bc_col = 2 * D_INNER // bc_dim
    return pl.pallas_call(
        _ssd_kernel,
        out_shape=jax.ShapeDtypeStruct((t, D_INNER), BF16),
        grid=(batch, nc),
        in_specs=[pl.BlockSpec((CHUNK, D_INNER), row),
                  pl.BlockSpec((CHUNK, D_INNER), lambda b, c: (cur(b, c), x_col)),
                  pl.BlockSpec((CHUNK, bc_dim), lambda b, c: (cur(b, c), bc_col)),
                  pl.BlockSpec((CHUNK, D_INNER), lambda b, c: (prev(b, c), x_col)),
                  pl.BlockSpec((CHUNK, bc_dim), lambda b, c: (prev(b, c), bc_col)),
                  pl.BlockSpec((CHUNK, LANES), row),
                  _resident(((CONV_WIDTH - 1) * CHUNK, 2 * CHUNK)),
                  _resident((CONV_WIDTH, CONV_DIM)),
                  _resident((1, CONV_DIM)),
                  _resident((1, LANES)),
                  _resident((1, LANES)),
                  _resident((1, D_INNER)),
                  _resident((1, D_INNER)),
                  _resident((LANES, D_INNER))],
        out_specs=pl.BlockSpec((CHUNK, D_INNER), row),
        scratch_shapes=[pltpu.VMEM((D_STATE, D_INNER), F32),
                        pltpu.VMEM((CHUNK, D_INNER), F32),
                        pltpu.VMEM((CHUNK, SSM_GROUPS * D_STATE), BF16),
                        pltpu.VMEM((CHUNK, SSM_GROUPS * D_STATE), BF16),
                        pltpu.VMEM((CHUNK, D_INNER), BF16),
                        pltpu.VMEM((CHUNK, D_INNER), BF16),
                        pltpu.VMEM((CHUNK, D_INNER), F32)],
        compiler_params=_cparams(("arbitrary", "arbitrary")),
        name="ssd",
    )(zx, zx, zx, zx, zx, dtp, shift, conv_w, conv_b, dtb_p, alog_p, dskip_x, g_ssd, expand)


def _merge_kernel(attn_ref, y_ref, ga_ref, gs_ref, wa_ref, ws_ref, o_ref, wa16_ref, ws16_ref):
    @pl.when(pl.program_id(1) == 0)
    def _():
        _cast_rows(wa16_ref, wa_ref, True)
        ws16_ref[...] = ws_ref[...].astype(BF16)

    out_a = _dot(attn_ref[...], wa16_ref[...])
    out_s = _dot(y_ref[...], ws16_ref[...])
    merged = (_sigmoid(ga_ref[...].astype(F32)) * out_a
              + _sigmoid(gs_ref[...].astype(F32)) * out_s)
    o_ref[...] = merged.astype(o_ref.dtype)


def _merge(attn, y, gates, wa, ws, tm=512, tn=1024):
    t = attn.shape[0]
    nj = D_MODEL // tn
    return pl.pallas_call(
        _merge_kernel,
        out_shape=jax.ShapeDtypeStruct((t, D_MODEL), BF16),
        grid=(nj, t // tm),
        in_specs=[pl.BlockSpec((tm, Q_DIM), lambda j, i: (i, 0)),
                  pl.BlockSpec((tm, D_INNER), lambda j, i: (i, 0)),
                  pl.BlockSpec((tm, tn), lambda j, i: (i, j)),
                  pl.BlockSpec((tm, tn), lambda j, i: (i, nj + j)),
                  pl.BlockSpec((Q_DIM, tn), lambda j, i: (0, j)),
                  pl.BlockSpec((D_INNER, tn), lambda j, i: (0, j))],
        out_specs=pl.BlockSpec((tm, tn), lambda j, i: (i, j)),
        scratch_shapes=[pltpu.VMEM((Q_DIM, tn), BF16),
                        pltpu.VMEM((D_INNER, tn), BF16)],
        compiler_params=_cparams(("arbitrary", "arbitrary")),
        name="merge",
    )(attn, y, gates, gates, wa, ws)


def _oproj_kernel(m_ref, x_ref, w_ref, g_ref, h_ref, f_ref, w16_ref):
    @pl.when(pl.program_id(0) == 0)
    def _():
        w16_ref[...] = w_ref[...].astype(BF16)

    h = x_ref[...] + _dot(m_ref[...], w16_ref[...])
    h_ref[...] = h
    f_ref[...] = _rms(h, g_ref[...], NORM_EPS).astype(BF16)


def _oproj(merged, x2, wo, g_ffn, tm=512):
    t = merged.shape[0]
    row = lambda i: (i, 0)
    return pl.pallas_call(
        _oproj_kernel,
        out_shape=(jax.ShapeDtypeStruct((t, D_MODEL), F32),
                   jax.ShapeDtypeStruct((t, D_MODEL), BF16)),
        grid=(t // tm,),
        in_specs=[pl.BlockSpec((tm, D_MODEL), row),
                  pl.BlockSpec((tm, D_MODEL), row),
                  _resident((D_MODEL, D_MODEL)),
                  _resident((1, D_MODEL))],
        out_specs=(pl.BlockSpec((tm, D_MODEL), row),
                   pl.BlockSpec((tm, D_MODEL), row)),
        scratch_shapes=[pltpu.VMEM((D_MODEL, D_MODEL), BF16)],
        compiler_params=_cparams(("arbitrary",)),
        name="oproj",
    )(merged, x2, wo, g_ffn)


def _ffn_up_kernel(f_ref, wg_ref, wu_ref, o_ref, wg16_ref, wu16_ref):
    @pl.when(pl.program_id(1) == 0)
    def _():
        wg16_ref[...] = wg_ref[...].astype(BF16)
        wu16_ref[...] = wu_ref[...].astype(BF16)

    f = f_ref[...]
    o_ref[...] = (_silu(_dot(f, wg16_ref[...])) * _dot(f, wu16_ref[...])).astype(o_ref.dtype)


def _ffn_up(f, wg, wu, tm=2048, th=512):
    t = f.shape[0]
    return pl.pallas_call(
        _ffn_up_kernel,
        out_shape=jax.ShapeDtypeStruct((t, FFN_HIDDEN), BF16),
        grid=(FFN_HIDDEN // th, t // tm),
        in_specs=[pl.BlockSpec((tm, D_MODEL), lambda j, i: (i, 0)),
                  pl.BlockSpec((D_MODEL, th), lambda j, i: (0, j)),
                  pl.BlockSpec((D_MODEL, th), lambda j, i: (0, j))],
        out_specs=pl.BlockSpec((tm, th), lambda j, i: (i, j)),
        scratch_shapes=[pltpu.VMEM((D_MODEL, th), BF16),
                        pltpu.VMEM((D_MODEL, th), BF16)],
        compiler_params=_cparams(("arbitrary", "arbitrary")),
        name="ffn_up",
    )(f, wg, wu)


def _ffn_down_kernel(a_ref, h_ref, wd_ref, o_ref, wd16_ref):
    @pl.when(pl.program_id(1) == 0)
    def _():
        wd16_ref[...] = wd_ref[...].astype(BF16)

    o_ref[...] = h_ref[...] + _dot(a_ref[...], wd16_ref[...])


def _ffn_down(act, h1, wd, tm=1024, tn=512):
    t = act.shape[0]
    return pl.pallas_call(
        _ffn_down_kernel,
        out_shape=jax.ShapeDtypeStruct((t, D_MODEL), F32),
        grid=(D_MODEL // tn, t // tm),
        in_specs=[pl.BlockSpec((tm, FFN_HIDDEN), lambda j, i: (i, 0)),
                  pl.BlockSpec((tm, tn), lambda j, i: (i, j)),
                  pl.BlockSpec((FFN_HIDDEN, tn), lambda j, i: (0, j),
                               pipeline_mode=pl.Buffered(1))],
        out_specs=pl.BlockSpec((tm, tn), lambda j, i: (i, j)),
        scratch_shapes=[pltpu.VMEM((FFN_HIDDEN, tn), BF16)],
        compiler_params=_cparams(("arbitrary", "arbitrary")),
        name="ffn_down",
    )(act, h1, wd)


def _ple_kernel(h_ref, p_ref, gp_ref, gf_ref, wg_ref, wp_ref, o_ref, wg16_ref, wp16_ref):
    @pl.when(pl.program_id(0) == 0)
    def _():
        wg16_ref[...] = wg_ref[...].astype(BF16)
        wp16_ref[...] = wp_ref[...].astype(BF16)

    h = h_ref[...]
    r = _rms(h, gp_ref[...], NORM_EPS).astype(BF16)
    gate = _sigmoid(_dot(r, wg16_ref[...]))
    h3 = h + gate * _dot(p_ref[...].astype(BF16), wp16_ref[...])
    o_ref[...] = _rms(h3, gf_ref[...], NORM_EPS)


def _ple(h2, p2, g_ple, g_final, wpg, wpp, tm=512):
    t = h2.shape[0]
    row = lambda i: (i, 0)
    return pl.pallas_call(
        _ple_kernel,
        out_shape=jax.ShapeDtypeStruct((t, D_MODEL), F32),
        grid=(t // tm,),
        in_specs=[pl.BlockSpec((tm, D_MODEL), row),
                  pl.BlockSpec((tm, PLE_DIM), row),
                  _resident((1, D_MODEL)),
                  _resident((1, D_MODEL)),
                  _resident((D_MODEL, D_MODEL)),
                  _resident((PLE_DIM, D_MODEL))],
        out_specs=pl.BlockSpec((tm, D_MODEL), row),
        scratch_shapes=[pltpu.VMEM((D_MODEL, D_MODEL), BF16),
                        pltpu.VMEM((PLE_DIM, D_MODEL), BF16)],
        compiler_params=_cparams(("arbitrary",)),
        name="ple",
    )(h2, p2, g_ple, g_final, wpg, wpp)


def _rope_constants():
    half = HEAD_DIM // 2
    lane = np.arange(LANES)
    inv_freq = ROPE_THETA ** (-jnp.arange(half, dtype=F32) * 2.0 / HEAD_DIM)
    invf = inv_freq[lane % half][None, :]
    sgn = jnp.asarray(np.where((lane % HEAD_DIM) < half, -1.0, 1.0), F32)[None, :]
    return invf, sgn


def _expand_matrix():
    rows = np.arange(LANES)[:, None]
    cols = np.arange(D_INNER)[None, :]
    hit = (rows < 3 * SSM_HEADS) & ((rows % SSM_HEADS) == (cols // SSM_HEAD_DIM))
    return jnp.asarray(hit, BF16)


def _shift_matrix():
    rows = np.arange((CONV_WIDTH - 1) * CHUNK)[:, None]
    cols = np.arange(2 * CHUNK)[None, :]
    w, t = rows // CHUNK, rows % CHUNK
    return jnp.asarray(cols == CHUNK + t - (CONV_WIDTH - 1) + w, BF16)


def _pad_lanes(v):
    return jnp.pad(v.astype(F32), (0, LANES - v.shape[0]))[None, :]


def kernel(x, p, positions, g_mix, w_in, conv_w, conv_b, dt_bias, a_log, d_skip, g_ssd,
           sinks, w_attn_br, w_ssd_br, w_o, g_ffn, w_gate, w_up, w_down, g_ple,
           w_ple_gate, w_ple_proj, g_final):
    b, s, d = x.shape
    t = b * s
    assert d == D_MODEL and s % CHUNK == 0 and p.shape[0] == 1
    i = 0
    x2 = x.reshape(t, d)
    p2 = p[i].reshape(t, PLE_DIM)
    pos_c = jnp.repeat(positions.reshape(t // POS_PER_ROW, POS_PER_ROW), ROPE_FREQS, axis=1)

    o_k = Q_DIM
    o_z = Q_DIM + 2 * KV_DIM
    o_xbc = o_z + D_INNER
    o_dt = o_xbc + CONV_DIM
    o_ga = o_dt + SSM_HEADS
    wt = jnp.swapaxes(w_in[i], 0, 1)

    invf, sgn = _rope_constants()
    u, cos_t, sin_t, dtp = _prep(x2, g_mix[i][None, :], pos_c, invf, sgn, wt, o_dt)

    qkv = _inproj(u, wt, 0, o_z, BF16, 2048, PAIR_ROWS, "proj_qkv", paired_tiles=Q_DIM // PAIR_ROWS)
    zx = _inproj(u, wt, o_z, D_INNER + CONV_DIM, BF16, 2048, 1024, "proj_zx")
    gates = _inproj(u, wt, o_ga, 2 * D_MODEL, BF16, 2048, 1024, "proj_gates")

    attn = _attention(qkv, cos_t, sin_t, sinks[i][None, :].astype(F32), b, s)

    y = _ssd(zx, dtp, _shift_matrix(), conv_w[i], conv_b[i][None, :], _pad_lanes(dt_bias[i]),
             _pad_lanes(a_log[i]), jnp.repeat(d_skip[i].astype(F32), SSM_HEAD_DIM)[None, :],
             g_ssd[i][None, :], _expand_matrix(), b, s)

    merged = _merge(attn, y, gates, w_attn_br[i], w_ssd_br[i])
    h1, f = _oproj(merged, x2, w_o[i], g_ffn[i][None, :])
    act = _ffn_up(f, w_gate[i], w_up[i])
    h2 = _ffn_down(act, h1, w_down[i])
    out = _ple(h2, p2, g_ple[i][None, :], g_final[None, :], w_ple_gate[i], w_ple_proj[i])
    return out.reshape(b, s, d)
```

```python
from functools import partial

import numpy as np
import jax
import jax.numpy as jnp
from jax import lax
from jax.experimental import pallas as pl
from jax.experimental.pallas import tpu as pltpu

F32 = jnp.float32
BF16 = jnp.bfloat16

D_MODEL = 2048
HEAD_DIM = 64
ATTN_HEADS = 16
KV_HEADS = 4
GROUP = ATTN_HEADS // KV_HEADS
Q_DIM = ATTN_HEADS * HEAD_DIM
KV_DIM = KV_HEADS * HEAD_DIM
ATTN_BLOCK = 128
ROPE_THETA = 10000.0
D_INNER = 2048
SSM_HEAD_DIM = 64
SSM_HEADS = 32
SSM_GROUPS = 4
HEADS_PER_GROUP = SSM_HEADS // SSM_GROUPS
D_STATE = 128
CONV_WIDTH = 4
CHUNK = 128
CONV_DIM = D_INNER + 2 * SSM_GROUPS * D_STATE
FFN_HIDDEN = 5632
PLE_DIM = 256
NORM_EPS = 1e-6
SSM_NORM_EPS = 1e-5

LANES = 128
SUBLANES = 8
VMEM_LIMIT_BYTES = 56 * 1024 * 1024

NEG_BIG = -1e30


def _cparams(semantics):
    return pltpu.CompilerParams(dimension_semantics=semantics,
                                vmem_limit_bytes=VMEM_LIMIT_BYTES)


def _resident(shape):
    return pl.BlockSpec(shape, lambda *_: (0,) * len(shape),
                        pipeline_mode=pl.Buffered(1))


def _rms(xf, g, eps):
    var = jnp.mean(xf * xf, axis=-1, keepdims=True)
    return xf * lax.rsqrt(var + eps) * g


NEG_LOG2E = -1.4426950408889634


def _sigmoid(x):
    return 1.0 / (1.0 + jnp.exp2(x * NEG_LOG2E))


def _silu(x):
    return x * _sigmoid(x)


def _dot(a, b):
    return jnp.dot(a, b, preferred_element_type=F32)


def _dot_nt(a, b):
    return lax.dot_general(a, b, (((1,), (1,)), ((), ())),
                           preferred_element_type=F32)


def _split3(v):
    v1 = v.astype(BF16)
    r1 = v - v1.astype(F32)
    v2 = r1.astype(BF16)
    v3 = (r1 - v2.astype(F32)).astype(BF16)
    return v1, v2, v3


ROPE_FREQS = HEAD_DIM // 2
POS_PER_ROW = LANES // ROPE_FREQS


def _prep_kernel(x_ref, g_ref, pos_ref, invf_ref, sgn_ref, wdt_ref, u_ref, cos_ref, sin_ref, dt_ref):
    u = _rms(x_ref[...], g_ref[...], NORM_EPS).astype(BF16)
    u_ref[...] = u
    rows = pos_ref.shape[0]
    ang = pos_ref[...].astype(F32) * invf_ref[...]
    cos_c = jnp.cos(ang)
    sin_c = jnp.sin(ang)
    group = lax.broadcasted_iota(jnp.int32, ang.shape, 1) // ROPE_FREQS
    for q in range(POS_PER_ROW):
        for src, dst_ref, sign in ((cos_c, cos_ref, None), (sin_c, sin_ref, sgn_ref[...])):
            only = jnp.where(group == q, src, 0.0)
            rep = only
            for k in range(1, POS_PER_ROW):
                rep = rep + pltpu.roll(only, k * ROPE_FREQS, 1)
            if sign is not None:
                rep = rep * sign
            dst_ref[pl.ds(q, rows, stride=POS_PER_ROW), :] = rep
    dt_ref[...] = _dot_nt(u, wdt_ref[...].astype(BF16))


def _prep(x2, g_mix, pos_c, invf, sgn, wt, dt_row0, tm=1024):
    t = x2.shape[0]
    row = lambda i: (i, 0)
    assert dt_row0 % SUBLANES == 0 and tm % (POS_PER_ROW * SUBLANES) == 0
    return pl.pallas_call(
        _prep_kernel,
        out_shape=(jax.ShapeDtypeStruct((t, D_MODEL), BF16),
                   jax.ShapeDtypeStruct((t, LANES), F32),
                   jax.ShapeDtypeStruct((t, LANES), F32),
                   jax.ShapeDtypeStruct((t, LANES), F32)),
        grid=(t // tm,),
        in_specs=[pl.BlockSpec((tm, D_MODEL), row),
                  _resident((1, D_MODEL)),
                  pl.BlockSpec((tm // POS_PER_ROW, LANES), row),
                  _resident((1, LANES)),
                  _resident((1, LANES)),
                  pl.BlockSpec((pl.Element(LANES), pl.Element(D_MODEL)),
                               lambda i: (dt_row0, 0), pipeline_mode=pl.Buffered(1))],
        out_specs=(pl.BlockSpec((tm, D_MODEL), row),
                   pl.BlockSpec((tm, LANES), row),
                   pl.BlockSpec((tm, LANES), row),
                   pl.BlockSpec((tm, LANES), row)),
        compiler_params=_cparams(("arbitrary",)),
        name="prep",
    )(x2, g_mix, pos_c, invf, sgn, wt)


PAIR_ROWS = 2 * GROUP * HEAD_DIM


def _paired_head_blocks():
    return [(g * 2 + sub, sub * GROUP + g) for g in range(GROUP) for sub in range(2)]


def _cast_rows(dst_ref, src_ref, paired_heads):
    if not paired_heads:
        dst_ref[...] = src_ref[...].astype(BF16)
        return
    assert dst_ref.shape[0] % PAIR_ROWS == 0
    for base in range(0, dst_ref.shape[0], PAIR_ROWS):
        for dst, src in _paired_head_blocks():
            dst_ref[base + dst * HEAD_DIM:base + (dst + 1) * HEAD_DIM, :] = (
                src_ref[base + src * HEAD_DIM:base + (src + 1) * HEAD_DIM, :].astype(BF16))


def _inproj_kernel(a_ref, wt_ref, o_ref, wbf_ref, *, paired_tiles):
    j = pl.program_id(0)

    @pl.when((pl.program_id(1) == 0) & (j >= paired_tiles))
    def _():
        _cast_rows(wbf_ref, wt_ref, False)

    if paired_tiles:
        @pl.when((pl.program_id(1) == 0) & (j < paired_tiles))
        def _():
            _cast_rows(wbf_ref, wt_ref, True)

    o_ref[...] = _dot_nt(a_ref[...], wbf_ref[...]).astype(o_ref.dtype)


def _inproj(a, wt, row0, n, out_dtype, tm, tn, name, paired_tiles=0):
    m, k = a.shape
    assert n % tn == 0 and row0 % SUBLANES == 0
    assert not paired_tiles or tn % PAIR_ROWS == 0
    return pl.pallas_call(
        partial(_inproj_kernel, paired_tiles=paired_tiles),
        out_shape=jax.ShapeDtypeStruct((m, n), out_dtype),
        grid=(n // tn, m // tm),
        in_specs=[pl.BlockSpec((tm, k), lambda j, i: (i, 0)),
                  pl.BlockSpec((pl.Element(tn), pl.Element(k)),
                               lambda j, i: (pl.multiple_of(row0 + j * tn, SUBLANES), 0))],
        out_specs=pl.BlockSpec((tm, tn), lambda j, i: (i, j)),
        scratch_shapes=[pltpu.VMEM((tn, k), BF16)],
        compiler_params=_cparams(("arbitrary", "arbitrary")),
        name=name,
    )(a, wt)


ATTN_STEP_BLOCKS = 2


def _attn_kernel(sinks_ref, q_ref, kc_ref, kp_ref, vc_ref, vp_ref,
                 cc_ref, sc_ref, cp_ref, sp_ref, o_ref):
    n = pl.program_id(1)
    blk = ATTN_BLOCK
    nblk = ATTN_STEP_BLOCKS
    lane = lax.broadcasted_iota(jnp.int32, (blk, LANES), 1)
    first_half = (lane % HEAD_DIM) < (HEAD_DIM // 2)
    low_head = lane < HEAD_DIM

    def rope(t, c, s):
        partner = jnp.where(first_half,
                            pltpu.roll(t, LANES - HEAD_DIM // 2, 1),
                            pltpu.roll(t, HEAD_DIM // 2, 1))
        return t * c + partner * s

    def rows(j):
        return slice(j * blk, (j + 1) * blk)

    cp, sp = cp_ref[...], sp_ref[...]
    cc = [cc_ref[rows(j), :] for j in range(nblk)]
    sc = [sc_ref[rows(j), :] for j in range(nblk)]

    qi = lax.broadcasted_iota(jnp.int32, (blk, 2 * blk), 0)
    kj = lax.broadcasted_iota(jnp.int32, (blk, 2 * blk), 1)
    rel = kj - qi
    band = (rel >= 1) & (rel <= blk)
    valid = [band & ((kj >= blk) | (n > 0))] + [band] * (nblk - 1)

    scale = HEAD_DIM ** -0.5
    n_half = KV_DIM // LANES
    rows_per_half = 2 * GROUP * blk
    s_rows, sink_rows, valid_rows, v_wins = [], [], [], []
    for a in range(n_half):
        ksl = slice(a * LANES, (a + 1) * LANES)
        k_blocks = [rope(kp_ref[:, ksl].astype(F32), cp, sp).astype(BF16)]
        v_blocks = [vp_ref[:, ksl]]
        for j in range(nblk):
            k_blocks.append(rope(kc_ref[rows(j), ksl].astype(F32), cc[j], sc[j]).astype(BF16))
            v_blocks.append(vc_ref[rows(j), ksl])
        for j in range(nblk):
            k_win = jnp.concatenate(k_blocks[j:j + 2], axis=0)
            v_wins.append(jnp.concatenate(v_blocks[j:j + 2], axis=0))
            qc = [rope(q_ref[rows(j), (a * GROUP + g) * LANES:(a * GROUP + g + 1) * LANES]
                       .astype(F32), cc[j] * scale, sc[j] * scale) for g in range(GROUP)]
            q_rows = []
            for sub in range(2):
                keep = low_head if sub == 0 else jnp.logical_not(low_head)
                for g in range(GROUP):
                    q_rows.append(jnp.where(keep, qc[g], 0.0).astype(BF16))
                    sink_rows.append(
                        jnp.full((blk, LANES), sinks_ref[0, (2 * a + sub) * GROUP + g], F32))
            s_rows.append(_dot_nt(jnp.concatenate(q_rows, axis=0), k_win))
            valid_rows.extend([valid[j]] * (2 * GROUP))
    s = jnp.where(jnp.concatenate(valid_rows, axis=0), jnp.concatenate(s_rows, axis=0), NEG_BIG)
    sink = jnp.concatenate(sink_rows, axis=0)
    s0, s1 = s[:, :LANES], s[:, LANES:]
    m = jnp.maximum(jnp.max(jnp.maximum(s0, s1), axis=-1, keepdims=True), sink)
    e0 = jnp.exp(s0 - m)
    e1 = jnp.exp(s1 - m)
    den = jnp.sum(e0 + e1, axis=-1, keepdims=True) + jnp.exp(sink - m)
    r = 1.0 / den
    p = jnp.concatenate([e0 * r, e1 * r], axis=1).astype(BF16)
    for a in range(n_half):
        for j in range(nblk):
            w = a * nblk + j
            pv = _dot(p[w * rows_per_half:(w + 1) * rows_per_half], v_wins[w])
            for g in range(GROUP):
                lo = pv[g * blk:(g + 1) * blk]
                hi = pv[(GROUP + g) * blk:(GROUP + g + 1) * blk]
                chunk = a * GROUP + g
                o_ref[rows(j), chunk * LANES:(chunk + 1) * LANES] = (
                    jnp.where(low_head, lo, hi).astype(o_ref.dtype))


def _attention(qkv, cos_t, sin_t, sinks, batch, seq):
    t = qkv.shape[0]
    step = ATTN_STEP_BLOCKS * ATTN_BLOCK
    assert seq % step == 0
    ns = seq // step
    kcol = Q_DIM // KV_DIM
    vcol = kcol + 1
    cur = lambda b, n: b * ns + n
    prev = lambda b, n: jnp.maximum((b * ns + n) * ATTN_STEP_BLOCKS - 1, 0)
    return pl.pallas_call(
        _attn_kernel,
        out_shape=jax.ShapeDtypeStruct((t, Q_DIM), BF16),
        grid=(batch, ns),
        in_specs=[pl.BlockSpec(memory_space=pltpu.SMEM),
                  pl.BlockSpec((step, Q_DIM), lambda b, n: (cur(b, n), 0)),
                  pl.BlockSpec((step, KV_DIM), lambda b, n: (cur(b, n), kcol)),
                  pl.BlockSpec((ATTN_BLOCK, KV_DIM), lambda b, n: (prev(b, n), kcol)),
                  pl.BlockSpec((step, KV_DIM), lambda b, n: (cur(b, n), vcol)),
                  pl.BlockSpec((ATTN_BLOCK, KV_DIM), lambda b, n: (prev(b, n), vcol)),
                  pl.BlockSpec((step, LANES), lambda b, n: (cur(b, n), 0)),
                  pl.BlockSpec((step, LANES), lambda b, n: (cur(b, n), 0)),
                  pl.BlockSpec((ATTN_BLOCK, LANES), lambda b, n: (prev(b, n), 0)),
                  pl.BlockSpec((ATTN_BLOCK, LANES), lambda b, n: (prev(b, n), 0))],
        out_specs=pl.BlockSpec((step, Q_DIM), lambda b, n: (cur(b, n), 0)),
        compiler_params=_cparams(("arbitrary", "arbitrary")),
        name="attention",
    )(sinks, qkv, qkv, qkv, qkv, qkv, cos_t, sin_t, cos_t, sin_t)


SSD_STRIP = 256


def _ssd_kernel(z_ref, xin_ref, bcin_ref, xin_prev_ref, bcin_prev_ref, dt_ref, shift_ref, cw_ref,
                cb_ref, dtb_ref, alog_ref, dskip_ref, g_ref, expand_ref, o_ref,
                state_ref, xs_ref, b16_ref, c16_ref, xdd_ref, xd16_ref, yz_ref):
    c = pl.program_id(1)
    L = CHUNK
    W = SSD_STRIP
    bc_dim = SSM_GROUPS * D_STATE

    @pl.when(c == 0)
    def _():
        state_ref[...] = jnp.zeros_like(state_ref)

    n_shift = CONV_WIDTH - 1
    sel_col = lax.broadcasted_iota(jnp.int32, (n_shift * L, 2 * L), 1)
    sel = shift_ref[...]
    sel = jnp.where((sel_col < L) & (c == 0), jnp.zeros_like(sel), sel)
    for s in range(CONV_DIM // W):
        sl = slice(s * W, (s + 1) * W)
        if s * W < D_INNER:
            cur_ref, prev_ref, src = xin_ref, xin_prev_ref, sl
        else:
            cur_ref, prev_ref = bcin_ref, bcin_prev_ref
            src = slice(s * W - D_INNER, (s + 1) * W - D_INNER)
        x_cur = cur_ref[:, src]
        xwin = jnp.concatenate([prev_ref[:, src], x_cur], axis=0)
        taps = _dot(sel, xwin)
        acc = cb_ref[:, sl] + x_cur.astype(F32) * cw_ref[n_shift:CONV_WIDTH, sl]
        for w in range(n_shift):
            acc = acc + taps[w * L:(w + 1) * L] * cw_ref[w:w + 1, sl]
        xc = _silu(acc)
        if s * W < D_INNER:
            xs_ref[:, sl] = xc
        elif s * W < D_INNER + bc_dim:
            b16_ref[:, s * W - D_INNER:(s + 1) * W - D_INNER] = xc.astype(BF16)
        else:
            c16_ref[:, s * W - D_INNER - bc_dim:(s + 1) * W - D_INNER - bc_dim] = xc.astype(BF16)

    v = dt_ref[...] + dtb_ref[...]
    dt = jnp.maximum(v, 0.0) + jnp.log1p(jnp.exp(-jnp.abs(v)))
    a = dt * (jnp.exp(alog_ref[...]) * NEG_LOG2E)

    ri = lax.broadcasted_iota(jnp.int32, (L, L), 0)
    ci = lax.broadcasted_iota(jnp.int32, (L, L), 1)
    causal = ri >= ci
    tril = jnp.where(causal, 1.0, 0.0).astype(BF16)
    ones = jnp.ones((L, L), BF16)
    a1, a2, a3 = _split3(a)
    cs = _dot(tril, a1) + _dot(tril, a2) + _dot(tril, a3)
    tot = _dot(ones, a1) + _dot(ones, a2) + _dot(ones, a3)
    cs_t = cs.T[0:SSM_HEADS]

    lane = lax.broadcasted_iota(jnp.int32, (L, LANES), 1)

    def pack(q):
        q1, q2, q3 = _split3(q)
        packed = jnp.where(
            lane < SSM_HEADS, q1.astype(F32),
            jnp.where(lane < 2 * SSM_HEADS, pltpu.roll(q2.astype(F32), SSM_HEADS, 1),
                      jnp.where(lane < 3 * SSM_HEADS,
                                pltpu.roll(q3.astype(F32), 2 * SSM_HEADS, 1), 0.0)))
        return packed.astype(BF16)

    pk_all = jnp.concatenate([pack(dt), pack(cs), pack(tot)], axis=0)
    pk_ct = pk_all[L:]

    for s in range(D_INNER // W):
        sl = slice(s * W, (s + 1) * W)
        ex = _dot(pk_all, expand_ref[:, sl])
        xd = xs_ref[:, sl] * ex[0:L]
        xd16_ref[:, sl] = xd.astype(BF16)
        xdd_ref[:, sl] = (xd * jnp.exp2(ex[2 * L:3 * L] - ex[L:2 * L])).astype(BF16)

    low_head = lane < SSM_HEAD_DIM
    heads_per_strip = W // SSM_HEAD_DIM
    strips_per_group = HEADS_PER_GROUP // heads_per_strip
    ssq = jnp.zeros((L, LANES), F32)

    for g in range(SSM_GROUPS):
        nsl = slice(g * D_STATE, (g + 1) * D_STATE)
        bg = b16_ref[:, nsl]
        cg = c16_ref[:, nsl]
        bg_t = bg.astype(F32).T.astype(BF16)
        cb = jnp.where(causal, _dot_nt(cg, bg), 0.0)
        for k in range(strips_per_group):
            s = g * strips_per_group + k
            sl = slice(s * W, (s + 1) * W)
            ex = _dot(pk_ct, expand_ref[:, sl])
            st_prev = state_ref[:, sl]
            y_off = _dot(cg, st_prev.astype(BF16))
            state_ref[:, sl] = st_prev * jnp.exp2(ex[L:2 * L]) + _dot(bg_t, xdd_ref[:, sl])
            y_pairs = []
            for pair in range(W // LANES):
                psl = slice(s * W + pair * LANES, s * W + (pair + 1) * LANES)
                xp = xd16_ref[:, psl]
                mhs, xps = [], []
                for sub in range(2):
                    h = s * heads_per_strip + pair * 2 + sub
                    diff = cs[:, h:h + 1] - cs_t[h:h + 1, :]
                    decay = jnp.exp2(jnp.minimum(diff, 0.0))
                    mhs.append((cb * decay).astype(BF16))
                    keep = low_head if sub == 0 else jnp.logical_not(low_head)
                    xps.append(jnp.where(keep, xp, jnp.zeros_like(xp)))
                y_pairs.append(_dot(jnp.concatenate(mhs, axis=1), jnp.concatenate(xps, axis=0)))
            y = (jnp.concatenate(y_pairs, axis=1) + y_off * jnp.exp2(ex[0:L])
                 + dskip_ref[:, sl] * xs_ref[:, sl])
            yz = y * _silu(z_ref[:, sl].astype(F32))
            yz_ref[:, sl] = yz
            for pair in range(W // LANES):
                part = yz[:, pair * LANES:(pair + 1) * LANES]
                ssq = ssq + part * part

    var = jnp.sum(ssq, axis=-1, keepdims=True) * (1.0 / D_INNER)
    rs = lax.rsqrt(var + SSM_NORM_EPS)
    for s in range(D_INNER // W):
        sl = slice(s * W, (s + 1) * W)
        o_ref[:, sl] = (yz_ref[:, sl] * rs * g_ref[:, sl]).astype(o_ref.dtype)


def _ssd(zx, dtp, shift, conv_w, conv_b, dtb_p, alog_p, dskip_x, g_ssd, expand, batch, seq):
    t = zx.shape[0]
    nc = seq // CHUNK
    bc_dim = 2 * SSM_GROUPS * D_STATE
    row = lambda b, c: (b * nc + c, 0)
    cur = lambda b, c: b * nc + c
    prev = lambda b, c: jnp.maximum(b * nc + c - 1, 0)
    x_col = 1
    bc_col = 2 * D_INNER // bc_dim
    return pl.pallas_call(
        _ssd_kernel,
        out_shape=jax.ShapeDtypeStruct((t, D_INNER), BF16),
        grid=(batch, nc),
        in_specs=[pl.BlockSpec((CHUNK, D_INNER), row),
                  pl.BlockSpec((CHUNK, D_INNER), lambda b, c: (cur(b, c), x_col)),
                  pl.BlockSpec((CHUNK, bc_dim), lambda b, c: (cur(b, c), bc_col)),
                  pl.BlockSpec((CHUNK, D_INNER), lambda b, c: (prev(b, c), x_col)),
                  pl.BlockSpec((CHUNK, bc_dim), lambda b, c: (prev(b, c), bc_col)),
                  pl.BlockSpec((CHUNK, LANES), row),
                  _resident(((CONV_WIDTH - 1) * CHUNK, 2 * CHUNK)),
                  _resident((CONV_WIDTH, CONV_DIM)),
                  _resident((1, CONV_DIM)),
                  _resident((1, LANES)),
                  _resident((1, LANES)),
                  _resident((1, D_INNER)),
                  _resident((1, D_INNER)),
                  _resident((LANES, D_INNER))],
        out_specs=pl.BlockSpec((CHUNK, D_INNER), row),
        scratch_shapes=[pltpu.VMEM((D_STATE, D_INNER), F32),
                        pltpu.VMEM((CHUNK, D_INNER), F32),
                        pltpu.VMEM((CHUNK, SSM_GROUPS * D_STATE), BF16),
                        pltpu.VMEM((CHUNK, SSM_GROUPS * D_STATE), BF16),
                        pltpu.VMEM((CHUNK, D_INNER), BF16),
                        pltpu.VMEM((CHUNK, D_INNER), BF16),
                        pltpu.VMEM((CHUNK, D_INNER), F32)],
        compiler_params=_cparams(("arbitrary", "arbitrary")),
        name="ssd",
    )(zx, zx, zx, zx, zx, dtp, shift, conv_w, conv_b, dtb_p, alog_p, dskip_x, g_ssd, expand)


def _merge_kernel(attn_ref, y_ref, ga_ref, gs_ref, wa_ref, ws_ref, o_ref, wa16_ref, ws16_ref):
    @pl.when(pl.program_id(1) == 0)
    def _():
        _cast_rows(wa16_ref, wa_ref, True)
        ws16_ref[...] = ws_ref[...].astype(BF16)

    out_a = _dot(attn_ref[...], wa16_ref[...])
    out_s = _dot(y_ref[...], ws16_ref[...])
    merged = (_sigmoid(ga_ref[...].astype(F32)) * out_a
              + _sigmoid(gs_ref[...].astype(F32)) * out_s)
    o_ref[...] = merged.astype(o_ref.dtype)


def _merge(attn, y, gates, wa, ws, tm=512, tn=1024):
    t = attn.shape[0]
    nj = D_MODEL // tn
    return pl.pallas_call(
        _merge_kernel,
        out_shape=jax.ShapeDtypeStruct((t, D_MODEL), BF16),
        grid=(nj, t // tm),
        in_specs=[pl.BlockSpec((tm, Q_DIM), lambda j, i: (i, 0)),
                  pl.BlockSpec((tm, D_INNER), lambda j, i: (i, 0)),
                  pl.BlockSpec((tm, tn), lambda j, i: (i, j)),
                  pl.BlockSpec((tm, tn), lambda j, i: (i, nj + j)),
                  pl.BlockSpec((Q_DIM, tn), lambda j, i: (0, j)),
                  pl.BlockSpec((D_INNER, tn), lambda j, i: (0, j))],
        out_specs=pl.BlockSpec((tm, tn), lambda j, i: (i, j)),
        scratch_shapes=[pltpu.VMEM((Q_DIM, tn), BF16),
                        pltpu.VMEM((D_INNER, tn), BF16)],
        compiler_params=_cparams(("arbitrary", "arbitrary")),
        name="merge",
    )(attn, y, gates, gates, wa, ws)


def _oproj_kernel(m_ref, x_ref, w_ref, g_ref, h_ref, f_ref, w16_ref):
    @pl.when(pl.program_id(0) == 0)
    def _():
        w16_ref[...] = w_ref[...].astype(BF16)

    h = x_ref[...] + _dot(m_ref[...], w16_ref[...])
    h_ref[...] = h
    f_ref[...] = _rms(h, g_ref[...], NORM_EPS).astype(BF16)


def _oproj(merged, x2, wo, g_ffn, tm=512):
    t = merged.shape[0]
    row = lambda i: (i, 0)
    return pl.pallas_call(
        _oproj_kernel,
        out_shape=(jax.ShapeDtypeStruct((t, D_MODEL), F32),
                   jax.ShapeDtypeStruct((t, D_MODEL), BF16)),
        grid=(t // tm,),
        in_specs=[pl.BlockSpec((tm, D_MODEL), row),
                  pl.BlockSpec((tm, D_MODEL), row),
                  _resident((D_MODEL, D_MODEL)),
                  _resident((1, D_MODEL))],
        out_specs=(pl.BlockSpec((tm, D_MODEL), row),
                   pl.BlockSpec((tm, D_MODEL), row)),
        scratch_shapes=[pltpu.VMEM((D_MODEL, D_MODEL), BF16)],
        compiler_params=_cparams(("arbitrary",)),
        name="oproj",
    )(merged, x2, wo, g_ffn)


def _ffn_up_kernel(f_ref, wg_ref, wu_ref, o_ref, wg16_ref, wu16_ref):
    @pl.when(pl.program_id(1) == 0)
    def _():
        wg16_ref[...] = wg_ref[...].astype(BF16)
        wu16_ref[...] = wu_ref[...].astype(BF16)

    f = f_ref[...]
    o_ref[...] = (_silu(_dot(f, wg16_ref[...])) * _dot(f, wu16_ref[...])).astype(o_ref.dtype)


def _ffn_up(f, wg, wu, tm=2048, th=512):
    t = f.shape[0]
    return pl.pallas_call(
        _ffn_up_kernel,
        out_shape=jax.ShapeDtypeStruct((t, FFN_HIDDEN), BF16),
        grid=(FFN_HIDDEN // th, t // tm),
        in_specs=[pl.BlockSpec((tm, D_MODEL), lambda j, i: (i, 0)),
                  pl.BlockSpec((D_MODEL, th), lambda j, i: (0, j)),
                  pl.BlockSpec((D_MODEL, th), lambda j, i: (0, j))],
        out_specs=pl.BlockSpec((tm, th), lambda j, i: (i, j)),
        scratch_shapes=[pltpu.VMEM((D_MODEL, th), BF16),
                        pltpu.VMEM((D_MODEL, th), BF16)],
        compiler_params=_cparams(("arbitrary", "arbitrary")),
        name="ffn_up",
    )(f, wg, wu)


def _ffn_down_kernel(a_ref, h_ref, wd_hbm, o_ref, wd16_ref, wf32_ref, sem, *, n_col_tiles):
    j = pl.program_id(0)
    i = pl.program_id(1)
    tn = wf32_ref.shape[1]

    def copy(col_tile):
        return pltpu.make_async_copy(wd_hbm.at[:, pl.ds(col_tile * tn, tn)], wf32_ref, sem)

    @pl.when((j == 0) & (i == 0))
    def _():
        copy(0).start()

    @pl.when(i == 0)
    def _():
        copy(j).wait()
        wd16_ref[...] = wf32_ref[...].astype(BF16)

    @pl.when((i == 0) & (j + 1 < n_col_tiles))
    def _():
        copy(j + 1).start()

    o_ref[...] = h_ref[...] + _dot(a_ref[...], wd16_ref[...])


def _ffn_down(act, h1, wd, tm=1024, tn=512):
    t = act.shape[0]
    n_col_tiles = D_MODEL // tn
    return pl.pallas_call(
        partial(_ffn_down_kernel, n_col_tiles=n_col_tiles),
        out_shape=jax.ShapeDtypeStruct((t, D_MODEL), F32),
        grid=(n_col_tiles, t // tm),
        in_specs=[pl.BlockSpec((tm, FFN_HIDDEN), lambda j, i: (i, 0)),
                  pl.BlockSpec((tm, tn), lambda j, i: (i, j)),
                  pl.BlockSpec(memory_space=pl.ANY)],
        out_specs=pl.BlockSpec((tm, tn), lambda j, i: (i, j)),
        scratch_shapes=[pltpu.VMEM((FFN_HIDDEN, tn), BF16),
                        pltpu.VMEM((FFN_HIDDEN, tn), F32),
                        pltpu.SemaphoreType.DMA],
        compiler_params=_cparams(("arbitrary", "arbitrary")),
        name="ffn_down",
    )(act, h1, wd)


def _ple_kernel(h_ref, p_ref, gp_ref, gf_ref, wg_ref, wp_ref, o_ref, wg16_ref, wp16_ref):
    @pl.when(pl.program_id(0) == 0)
    def _():
        wg16_ref[...] = wg_ref[...].astype(BF16)
        wp16_ref[...] = wp_ref[...].astype(BF16)

    h = h_ref[...]
    r = _rms(h, gp_ref[...], NORM_EPS).astype(BF16)
    gate = _sigmoid(_dot(r, wg16_ref[...]))
    h3 = h + gate * _dot(p_ref[...].astype(BF16), wp16_ref[...])
    o_ref[...] = _rms(h3, gf_ref[...], NORM_EPS)


def _ple(h2, p2, g_ple, g_final, wpg, wpp, tm=512):
    t = h2.shape[0]
    row = lambda i: (i, 0)
    return pl.pallas_call(
        _ple_kernel,
        out_shape=jax.ShapeDtypeStruct((t, D_MODEL), F32),
        grid=(t // tm,),
        in_specs=[pl.BlockSpec((tm, D_MODEL), row),
                  pl.BlockSpec((tm, PLE_DIM), row),
                  _resident((1, D_MODEL)),
                  _resident((1, D_MODEL)),
                  _resident((D_MODEL, D_MODEL)),
                  _resident((PLE_DIM, D_MODEL))],
        out_specs=pl.BlockSpec((tm, D_MODEL), row),
        scratch_shapes=[pltpu.VMEM((D_MODEL, D_MODEL), BF16),
                        pltpu.VMEM((PLE_DIM, D_MODEL), BF16)],
        compiler_params=_cparams(("arbitrary",)),
        name="ple",
    )(h2, p2, g_ple, g_final, wpg, wpp)


def _rope_constants():
    half = HEAD_DIM // 2
    lane = np.arange(LANES)
    inv_freq = ROPE_THETA ** (-jnp.arange(half, dtype=F32) * 2.0 / HEAD_DIM)
    invf = inv_freq[lane % half][None, :]
    sgn = jnp.asarray(np.where((lane % HEAD_DIM) < half, -1.0, 1.0), F32)[None, :]
    return invf, sgn


def _expand_matrix():
    rows = np.arange(LANES)[:, None]
    cols = np.arange(D_INNER)[None, :]
    hit = (rows < 3 * SSM_HEADS) & ((rows % SSM_HEADS) == (cols // SSM_HEAD_DIM))
    return jnp.asarray(hit, BF16)


def _shift_matrix():
    rows = np.arange((CONV_WIDTH - 1) * CHUNK)[:, None]
    cols = np.arange(2 * CHUNK)[None, :]
    w, t = rows // CHUNK, rows % CHUNK
    return jnp.asarray(cols == CHUNK + t - (CONV_WIDTH - 1) + w, BF16)


def _pad_lanes(v):
    return jnp.pad(v.astype(F32), (0, LANES - v.shape[0]))[None, :]


def kernel(x, p, positions, g_mix, w_in, conv_w, conv_b, dt_bias, a_log, d_skip, g_ssd,
           sinks, w_attn_br, w_ssd_br, w_o, g_ffn, w_gate, w_up, w_down, g_ple,
           w_ple_gate, w_ple_proj, g_final):
    b, s, d = x.shape
    t = b * s
    assert d == D_MODEL and s % CHUNK == 0 and p.shape[0] == 1
    i = 0
    x2 = x.reshape(t, d)
    p2 = p[i].reshape(t, PLE_DIM)
    pos_c = jnp.repeat(positions.reshape(t // POS_PER_ROW, POS_PER_ROW), ROPE_FREQS, axis=1)

    o_k = Q_DIM
    o_z = Q_DIM + 2 * KV_DIM
    o_xbc = o_z + D_INNER
    o_dt = o_xbc + CONV_DIM
    o_ga = o_dt + SSM_HEADS
    wt = jnp.swapaxes(w_in[i], 0, 1)

    invf, sgn = _rope_constants()
    u, cos_t, sin_t, dtp = _prep(x2, g_mix[i][None, :], pos_c, invf, sgn, wt, o_dt)

    qkv = _inproj(u, wt, 0, o_z, BF16, 2048, PAIR_ROWS, "proj_qkv", paired_tiles=Q_DIM // PAIR_ROWS)
    zx = _inproj(u, wt, o_z, D_INNER + CONV_DIM, BF16, 2048, 1024, "proj_zx")
    gates = _inproj(u, wt, o_ga, 2 * D_MODEL, BF16, 2048, 1024, "proj_gates")

    attn = _attention(qkv, cos_t, sin_t, sinks[i][None, :].astype(F32), b, s)

    y = _ssd(zx, dtp, _shift_matrix(), conv_w[i], conv_b[i][None, :], _pad_lanes(dt_bias[i]),
             _pad_lanes(a_log[i]), jnp.repeat(d_skip[i].astype(F32), SSM_HEAD_DIM)[None, :],
             g_ssd[i][None, :], _expand_matrix(), b, s)

    merged = _merge(attn, y, gates, w_attn_br[i], w_ssd_br[i])
    h1, f = _oproj(merged, x2, w_o[i], g_ffn[i][None, :])
    act = _ffn_up(f, w_gate[i], w_up[i])
    h2 = _ffn_down(act, h1, w_down[i])
    out = _ple(h2, p2, g_ple[i][None, :], g_final[None, :], w_ple_gate[i], w_ple_proj[i])
    return out.reshape(b, s, d)
```

```python
from functools import partial

import numpy as np
import jax
import jax.numpy as jnp
from jax import lax
from jax.experimental import pallas as pl
from jax.experimental.pallas import tpu as pltpu

F32 = jnp.float32
BF16 = jnp.bfloat16

D_MODEL = 2048
HEAD_DIM = 64
ATTN_HEADS = 16
KV_HEADS = 4
GROUP = ATTN_HEADS // KV_HEADS
Q_DIM = ATTN_HEADS * HEAD_DIM
KV_DIM = KV_HEADS * HEAD_DIM
ATTN_BLOCK = 128
ROPE_THETA = 10000.0
D_INNER = 2048
SSM_HEAD_DIM = 64
SSM_HEADS = 32
SSM_GROUPS = 4
HEADS_PER_GROUP = SSM_HEADS // SSM_GROUPS
D_STATE = 128
CONV_WIDTH = 4
CHUNK = 128
CONV_DIM = D_INNER + 2 * SSM_GROUPS * D_STATE
FFN_HIDDEN = 5632
PLE_DIM = 256
NORM_EPS = 1e-6
SSM_NORM_EPS = 1e-5

LANES = 128
SUBLANES = 8
VMEM_LIMIT_BYTES = 56 * 1024 * 1024

NEG_BIG = -1e30


def _cparams(semantics):
    return pltpu.CompilerParams(dimension_semantics=semantics,
                                vmem_limit_bytes=VMEM_LIMIT_BYTES)


def _resident(shape):
    return pl.BlockSpec(shape, lambda *_: (0,) * len(shape),
                        pipeline_mode=pl.Buffered(1))


def _rms(xf, g, eps):
    var = jnp.mean(xf * xf, axis=-1, keepdims=True)
    return xf * lax.rsqrt(var + eps) * g


NEG_LOG2E = -1.4426950408889634


def _sigmoid(x):
    return 1.0 / (1.0 + jnp.exp2(x * NEG_LOG2E))


def _silu(x):
    return x * _sigmoid(x)


def _dot(a, b):
    return jnp.dot(a, b, preferred_element_type=F32)


def _dot_nt(a, b):
    return lax.dot_general(a, b, (((1,), (1,)), ((), ())),
                           preferred_element_type=F32)


def _split3(v):
    v1 = v.astype(BF16)
    r1 = v - v1.astype(F32)
    v2 = r1.astype(BF16)
    v3 = (r1 - v2.astype(F32)).astype(BF16)
    return v1, v2, v3


ROPE_FREQS = HEAD_DIM // 2
POS_PER_ROW = LANES // ROPE_FREQS


def _prep_kernel(x_ref, g_ref, pos_ref, invf_ref, sgn_ref, wdt_ref, u_ref, cos_ref, sin_ref, dt_ref):
    u = _rms(x_ref[...], g_ref[...], NORM_EPS).astype(BF16)
    u_ref[...] = u
    rows = pos_ref.shape[0]
    ang = pos_ref[...].astype(F32) * invf_ref[...]
    cos_c = jnp.cos(ang)
    sin_c = jnp.sin(ang)
    group = lax.broadcasted_iota(jnp.int32, ang.shape, 1) // ROPE_FREQS
    for q in range(POS_PER_ROW):
        for src, dst_ref, sign in ((cos_c, cos_ref, None), (sin_c, sin_ref, sgn_ref[...])):
            only = jnp.where(group == q, src, 0.0)
            rep = only
            for k in range(1, POS_PER_ROW):
                rep = rep + pltpu.roll(only, k * ROPE_FREQS, 1)
            if sign is not None:
                rep = rep * sign
            dst_ref[pl.ds(q, rows, stride=POS_PER_ROW), :] = rep
    dt_ref[...] = _dot_nt(u, wdt_ref[...].astype(BF16))


def _prep(x2, g_mix, pos_c, invf, sgn, wt, dt_row0, tm=1024):
    t = x2.shape[0]
    row = lambda i: (i, 0)
    assert dt_row0 % SUBLANES == 0 and tm % (POS_PER_ROW * SUBLANES) == 0
    return pl.pallas_call(
        _prep_kernel,
        out_shape=(jax.ShapeDtypeStruct((t, D_MODEL), BF16),
                   jax.ShapeDtypeStruct((t, LANES), F32),
                   jax.ShapeDtypeStruct((t, LANES), F32),
                   jax.ShapeDtypeStruct((t, LANES), F32)),
        grid=(t // tm,),
        in_specs=[pl.BlockSpec((tm, D_MODEL), row),
                  _resident((1, D_MODEL)),
                  pl.BlockSpec((tm // POS_PER_ROW, LANES), row),
                  _resident((1, LANES)),
                  _resident((1, LANES)),
                  pl.BlockSpec((pl.Element(LANES), pl.Element(D_MODEL)),
                               lambda i: (dt_row0, 0), pipeline_mode=pl.Buffered(1))],
        out_specs=(pl.BlockSpec((tm, D_MODEL), row),
                   pl.BlockSpec((tm, LANES), row),
                   pl.BlockSpec((tm, LANES), row),
                   pl.BlockSpec((tm, LANES), row)),
        compiler_params=_cparams(("arbitrary",)),
        name="prep",
    )(x2, g_mix, pos_c, invf, sgn, wt)


PAIR_ROWS = 2 * GROUP * HEAD_DIM
INPROJ_ROW_TILE = 2048
INPROJ_COL_TILE = 1024


def _paired_head_blocks():
    return [(g * 2 + sub, sub * GROUP + g) for g in range(GROUP) for sub in range(2)]


def _cast_rows(dst_ref, src_ref, paired_heads):
    if not paired_heads:
        dst_ref[...] = src_ref[...].astype(BF16)
        return
    assert dst_ref.shape[0] % PAIR_ROWS == 0
    for base in range(0, dst_ref.shape[0], PAIR_ROWS):
        for dst, src in _paired_head_blocks():
            dst_ref[base + dst * HEAD_DIM:base + (dst + 1) * HEAD_DIM, :] = (
                src_ref[base + src * HEAD_DIM:base + (src + 1) * HEAD_DIM, :].astype(BF16))


def _inproj_kernel(a_ref, wt_ref, o_ref, wbf_ref, *, paired_tiles):
    j = pl.program_id(0)

    @pl.when((pl.program_id(1) == 0) & (j >= paired_tiles))
    def _():
        _cast_rows(wbf_ref, wt_ref, False)

    if paired_tiles:
        @pl.when((pl.program_id(1) == 0) & (j < paired_tiles))
        def _():
            _cast_rows(wbf_ref, wt_ref, True)

    o_ref[...] = _dot_nt(a_ref[...], wbf_ref[...]).astype(o_ref.dtype)


def _inproj(a, wt, segments, out_dtype, tm, tn, name, paired_tiles=0):
    m, k = a.shape
    assert all(seg_n % tn == 0 and row0 % SUBLANES == 0 for row0, seg_n in segments)
    assert not paired_tiles or tn % PAIR_ROWS == 0
    n = sum(seg_n for _, seg_n in segments)

    def weight_row(j):
        row, first_tile = None, 0
        for row0, seg_n in segments:
            here = row0 + (j - first_tile) * tn
            row = here if row is None else jnp.where(j >= first_tile, here, row)
            first_tile += seg_n // tn
        return pl.multiple_of(row, SUBLANES)

    return pl.pallas_call(
        partial(_inproj_kernel, paired_tiles=paired_tiles),
        out_shape=jax.ShapeDtypeStruct((m, n), out_dtype),
        grid=(n // tn, m // tm),
        in_specs=[pl.BlockSpec((tm, k), lambda j, i: (i, 0)),
                  pl.BlockSpec((pl.Element(tn), pl.Element(k)), lambda j, i: (weight_row(j), 0))],
        out_specs=pl.BlockSpec((tm, tn), lambda j, i: (i, j)),
        scratch_shapes=[pltpu.VMEM((tn, k), BF16)],
        compiler_params=_cparams(("arbitrary", "arbitrary")),
        name=name,
    )(a, wt)


ATTN_STEP_BLOCKS = 2


def _attn_kernel(sinks_ref, q_ref, kc_ref, kp_ref, vc_ref, vp_ref,
                 cc_ref, sc_ref, cp_ref, sp_ref, o_ref):
    n = pl.program_id(1)
    blk = ATTN_BLOCK
    nblk = ATTN_STEP_BLOCKS
    lane = lax.broadcasted_iota(jnp.int32, (blk, LANES), 1)
    first_half = (lane % HEAD_DIM) < (HEAD_DIM // 2)
    low_head = lane < HEAD_DIM

    def rope(t, c, s):
        partner = jnp.where(first_half,
                            pltpu.roll(t, LANES - HEAD_DIM // 2, 1),
                            pltpu.roll(t, HEAD_DIM // 2, 1))
        return t * c + partner * s

    def rows(j):
        return slice(j * blk, (j + 1) * blk)

    cp, sp = cp_ref[...], sp_ref[...]
    cc = [cc_ref[rows(j), :] for j in range(nblk)]
    sc = [sc_ref[rows(j), :] for j in range(nblk)]

    qi = lax.broadcasted_iota(jnp.int32, (blk, 2 * blk), 0)
    kj = lax.broadcasted_iota(jnp.int32, (blk, 2 * blk), 1)
    rel = kj - qi
    band = (rel >= 1) & (rel <= blk)
    valid = [band & ((kj >= blk) | (n > 0))] + [band] * (nblk - 1)

    scale = HEAD_DIM ** -0.5
    n_half = KV_DIM // LANES
    rows_per_half = 2 * GROUP * blk
    s_rows, sink_rows, valid_rows, v_wins = [], [], [], []
    for a in range(n_half):
        ksl = slice(a * LANES, (a + 1) * LANES)
        k_blocks = [rope(kp_ref[:, ksl].astype(F32), cp, sp).astype(BF16)]
        v_blocks = [vp_ref[:, ksl]]
        for j in range(nblk):
            k_blocks.append(rope(kc_ref[rows(j), ksl].astype(F32), cc[j], sc[j]).astype(BF16))
            v_blocks.append(vc_ref[rows(j), ksl])
        for j in range(nblk):
            k_win = jnp.concatenate(k_blocks[j:j + 2], axis=0)
            v_wins.append(jnp.concatenate(v_blocks[j:j + 2], axis=0))
            qc = [rope(q_ref[rows(j), (a * GROUP + g) * LANES:(a * GROUP + g + 1) * LANES]
                       .astype(F32), cc[j] * scale, sc[j] * scale) for g in range(GROUP)]
            q_rows = []
            for sub in range(2):
                keep = low_head if sub == 0 else jnp.logical_not(low_head)
                for g in range(GROUP):
                    q_rows.append(jnp.where(keep, qc[g], 0.0).astype(BF16))
                    sink_rows.append(
                        jnp.full((blk, LANES), sinks_ref[0, (2 * a + sub) * GROUP + g], F32))
            s_rows.append(_dot_nt(jnp.concatenate(q_rows, axis=0), k_win))
            valid_rows.extend([valid[j]] * (2 * GROUP))
    s = jnp.where(jnp.concatenate(valid_rows, axis=0), jnp.concatenate(s_rows, axis=0), NEG_BIG)
    sink = jnp.concatenate(sink_rows, axis=0)
    s0, s1 = s[:, :LANES], s[:, LANES:]
    m = jnp.maximum(jnp.max(jnp.maximum(s0, s1), axis=-1, keepdims=True), sink)
    e0 = jnp.exp(s0 - m)
    e1 = jnp.exp(s1 - m)
    den = jnp.sum(e0 + e1, axis=-1, keepdims=True) + jnp.exp(sink - m)
    r = 1.0 / den
    p = jnp.concatenate([e0 * r, e1 * r], axis=1).astype(BF16)
    for a in range(n_half):
        for j in range(nblk):
            w = a * nblk + j
            pv = _dot(p[w * rows_per_half:(w + 1) * rows_per_half], v_wins[w])
            for g in range(GROUP):
                lo = pv[g * blk:(g + 1) * blk]
                hi = pv[(GROUP + g) * blk:(GROUP + g + 1) * blk]
                chunk = a * GROUP + g
                o_ref[rows(j), chunk * LANES:(chunk + 1) * LANES] = (
                    jnp.where(low_head, lo, hi).astype(o_ref.dtype))


def _attention(qkv, cos_t, sin_t, sinks, batch, seq):
    t = qkv.shape[0]
    step = ATTN_STEP_BLOCKS * ATTN_BLOCK
    assert seq % step == 0
    ns = seq // step
    kcol = Q_DIM // KV_DIM
    vcol = kcol + 1
    cur = lambda b, n: b * ns + n
    prev = lambda b, n: jnp.maximum((b * ns + n) * ATTN_STEP_BLOCKS - 1, 0)
    return pl.pallas_call(
        _attn_kernel,
        out_shape=jax.ShapeDtypeStruct((t, Q_DIM), BF16),
        grid=(batch, ns),
        in_specs=[pl.BlockSpec(memory_space=pltpu.SMEM),
                  pl.BlockSpec((step, Q_DIM), lambda b, n: (cur(b, n), 0)),
                  pl.BlockSpec((step, KV_DIM), lambda b, n: (cur(b, n), kcol)),
                  pl.BlockSpec((ATTN_BLOCK, KV_DIM), lambda b, n: (prev(b, n), kcol)),
                  pl.BlockSpec((step, KV_DIM), lambda b, n: (cur(b, n), vcol)),
                  pl.BlockSpec((ATTN_BLOCK, KV_DIM), lambda b, n: (prev(b, n), vcol)),
                  pl.BlockSpec((step, LANES), lambda b, n: (cur(b, n), 0)),
                  pl.BlockSpec((step, LANES), lambda b, n: (cur(b, n), 0)),
                  pl.BlockSpec((ATTN_BLOCK, LANES), lambda b, n: (prev(b, n), 0)),
                  pl.BlockSpec((ATTN_BLOCK, LANES), lambda b, n: (prev(b, n), 0))],
        out_specs=pl.BlockSpec((step, Q_DIM), lambda b, n: (cur(b, n), 0)),
        compiler_params=_cparams(("arbitrary", "arbitrary")),
        name="attention",
    )(sinks, qkv, qkv, qkv, qkv, qkv, cos_t, sin_t, cos_t, sin_t)


SSD_STRIP = 256


def _ssd_kernel(z_ref, xin_ref, bcin_ref, xin_prev_ref, bcin_prev_ref, dt_ref, shift_ref, cw_ref,
                cb_ref, dtb_ref, alog_ref, dskip_ref, g_ref, expand_ref, o_ref,
                state_ref, xs_ref, b16_ref, c16_ref, xdd_ref, xd16_ref, yz_ref):
    c = pl.program_id(1)
    L = CHUNK
    W = SSD_STRIP
    bc_dim = SSM_GROUPS * D_STATE

    @pl.when(c == 0)
    def _():
        state_ref[...] = jnp.zeros_like(state_ref)

    n_shift = CONV_WIDTH - 1
    sel_col = lax.broadcasted_iota(jnp.int32, (n_shift * L, 2 * L), 1)
    sel = shift_ref[...]
    sel = jnp.where((sel_col < L) & (c == 0), jnp.zeros_like(sel), sel)
    for s in range(CONV_DIM // W):
        sl = slice(s * W, (s + 1) * W)
        if s * W < D_INNER:
            cur_ref, prev_ref, src = xin_ref, xin_prev_ref, sl
        else:
            cur_ref, prev_ref = bcin_ref, bcin_prev_ref
            src = slice(s * W - D_INNER, (s + 1) * W - D_INNER)
        x_cur = cur_ref[:, src]
        xwin = jnp.concatenate([prev_ref[:, src], x_cur], axis=0)
        taps = _dot(sel, xwin)
        acc = cb_ref[:, sl] + x_cur.astype(F32) * cw_ref[n_shift:CONV_WIDTH, sl]
        for w in range(n_shift):
            acc = acc + taps[w * L:(w + 1) * L] * cw_ref[w:w + 1, sl]
        xc = _silu(acc)
        if s * W < D_INNER:
            xs_ref[:, sl] = xc
        elif s * W < D_INNER + bc_dim:
            b16_ref[:, s * W - D_INNER:(s + 1) * W - D_INNER] = xc.astype(BF16)
        else:
            c16_ref[:, s * W - D_INNER - bc_dim:(s + 1) * W - D_INNER - bc_dim] = xc.astype(BF16)

    v = dt_ref[...] + dtb_ref[...]
    dt = jnp.maximum(v, 0.0) + jnp.log1p(jnp.exp(-jnp.abs(v)))
    a = dt * (jnp.exp(alog_ref[...]) * NEG_LOG2E)

    ri = lax.broadcasted_iota(jnp.int32, (L, L), 0)
    ci = lax.broadcasted_iota(jnp.int32, (L, L), 1)
    causal = ri >= ci
    tril = jnp.where(causal, 1.0, 0.0).astype(BF16)
    ones = jnp.ones((L, L), BF16)
    a1, a2, a3 = _split3(a)
    cs = _dot(tril, a1) + _dot(tril, a2) + _dot(tril, a3)
    tot = _dot(ones, a1) + _dot(ones, a2) + _dot(ones, a3)
    cs_t = cs.T[0:SSM_HEADS]

    lane = lax.broadcasted_iota(jnp.int32, (L, LANES), 1)

    def pack(q):
        q1, q2, q3 = _split3(q)
        packed = jnp.where(
            lane < SSM_HEADS, q1.astype(F32),
            jnp.where(lane < 2 * SSM_HEADS, pltpu.roll(q2.astype(F32), SSM_HEADS, 1),
                      jnp.where(lane < 3 * SSM_HEADS,
                                pltpu.roll(q3.astype(F32), 2 * SSM_HEADS, 1), 0.0)))
        return packed.astype(BF16)

    pk_all = jnp.concatenate([pack(dt), pack(cs), pack(tot)], axis=0)
    pk_ct = pk_all[L:]

    for s in range(D_INNER // W):
        sl = slice(s * W, (s + 1) * W)
        ex = _dot(pk_all, expand_ref[:, sl])
        xd = xs_ref[:, sl] * ex[0:L]
        xd16_ref[:, sl] = xd.astype(BF16)
        xdd_ref[:, sl] = (xd * jnp.exp2(ex[2 * L:3 * L] - ex[L:2 * L])).astype(BF16)

    low_head = lane < SSM_HEAD_DIM
    heads_per_strip = W // SSM_HEAD_DIM
    strips_per_group = HEADS_PER_GROUP // heads_per_strip
    ssq = jnp.zeros((L, LANES), F32)

    for g in range(SSM_GROUPS):
        nsl = slice(g * D_STATE, (g + 1) * D_STATE)
        bg = b16_ref[:, nsl]
        cg = c16_ref[:, nsl]
        bg_t = bg.astype(F32).T.astype(BF16)
        cb = jnp.where(causal, _dot_nt(cg, bg), 0.0)
        for k in range(strips_per_group):
            s = g * strips_per_group + k
            sl = slice(s * W, (s + 1) * W)
            ex = _dot(pk_ct, expand_ref[:, sl])
            st_prev = state_ref[:, sl]
            y_off = _dot(cg, st_prev.astype(BF16))
            state_ref[:, sl] = st_prev * jnp.exp2(ex[L:2 * L]) + _dot(bg_t, xdd_ref[:, sl])
            y_pairs = []
            for pair in range(W // LANES):
                psl = slice(s * W + pair * LANES, s * W + (pair + 1) * LANES)
                xp = xd16_ref[:, psl]
                mhs, xps = [], []
                for sub in range(2):
                    h = s * heads_per_strip + pair * 2 + sub
                    diff = cs[:, h:h + 1] - cs_t[h:h + 1, :]
                    decay = jnp.exp2(jnp.minimum(diff, 0.0))
                    mhs.append((cb * decay).astype(BF16))
                    keep = low_head if sub == 0 else jnp.logical_not(low_head)
                    xps.append(jnp.where(keep, xp, jnp.zeros_like(xp)))
                y_pairs.append(_dot(jnp.concatenate(mhs, axis=1), jnp.concatenate(xps, axis=0)))
            y = (jnp.concatenate(y_pairs, axis=1) + y_off * jnp.exp2(ex[0:L])
                 + dskip_ref[:, sl] * xs_ref[:, sl])
            yz = y * _silu(z_ref[:, sl].astype(F32))
            yz_ref[:, sl] = yz
            for pair in range(W // LANES):
                part = yz[:, pair * LANES:(pair + 1) * LANES]
                ssq = ssq + part * part

    var = jnp.sum(ssq, axis=-1, keepdims=True) * (1.0 / D_INNER)
    rs = lax.rsqrt(var + SSM_NORM_EPS)
    for s in range(D_INNER // W):
        sl = slice(s * W, (s + 1) * W)
        o_ref[:, sl] = (yz_ref[:, sl] * rs * g_ref[:, sl]).astype(o_ref.dtype)


def _ssd(zx, dtp, shift, conv_w, conv_b, dtb_p, alog_p, dskip_x, g_ssd, expand, batch, seq):
    t = zx.shape[0]
    nc = seq // CHUNK
    bc_dim = 2 * SSM_GROUPS * D_STATE
    row = lambda b, c: (b * nc + c, 0)
    cur = lambda b, c: b * nc + c
    prev = lambda b, c: jnp.maximum(b * nc + c - 1, 0)
    x_col = 1
    bc_col = 2 * D_INNER // bc_dim
    return pl.pallas_call(
        _ssd_kernel,
        out_shape=jax.ShapeDtypeStruct((t, D_INNER), BF16),
        grid=(batch, nc),
        in_specs=[pl.BlockSpec((CHUNK, D_INNER), row),
                  pl.BlockSpec((CHUNK, D_INNER), lambda b, c: (cur(b, c), x_col)),
                  pl.BlockSpec((CHUNK, bc_dim), lambda b, c: (cur(b, c), bc_col)),
                  pl.BlockSpec((CHUNK, D_INNER), lambda b, c: (prev(b, c), x_col)),
                  pl.BlockSpec((CHUNK, bc_dim), lambda b, c: (prev(b, c), bc_col)),
                  pl.BlockSpec((CHUNK, LANES), row),
                  _resident(((CONV_WIDTH - 1) * CHUNK, 2 * CHUNK)),
                  _resident((CONV_WIDTH, CONV_DIM)),
                  _resident((1, CONV_DIM)),
                  _resident((1, LANES)),
                  _resident((1, LANES)),
                  _resident((1, D_INNER)),
                  _resident((1, D_INNER)),
                  _resident((LANES, D_INNER))],
        out_specs=pl.BlockSpec((CHUNK, D_INNER), row),
        scratch_shapes=[pltpu.VMEM((D_STATE, D_INNER), F32),
                        pltpu.VMEM((CHUNK, D_INNER), F32),
                        pltpu.VMEM((CHUNK, SSM_GROUPS * D_STATE), BF16),
                        pltpu.VMEM((CHUNK, SSM_GROUPS * D_STATE), BF16),
                        pltpu.VMEM((CHUNK, D_INNER), BF16),
                        pltpu.VMEM((CHUNK, D_INNER), BF16),
                        pltpu.VMEM((CHUNK, D_INNER), F32)],
        compiler_params=_cparams(("arbitrary", "arbitrary")),
        name="ssd",
    )(zx, zx, zx, zx, zx, dtp, shift, conv_w, conv_b, dtb_p, alog_p, dskip_x, g_ssd, expand)


def _merge_kernel(attn_ref, y_ref, ga_ref, gs_ref, wa_ref, ws_ref, o_ref, wa16_ref, ws16_ref):
    @pl.when(pl.program_id(1) == 0)
    def _():
        _cast_rows(wa16_ref, wa_ref, True)
        ws16_ref[...] = ws_ref[...].astype(BF16)

    out_a = _dot(attn_ref[...], wa16_ref[...])
    out_s = _dot(y_ref[...], ws16_ref[...])
    merged = (_sigmoid(ga_ref[...].astype(F32)) * out_a
              + _sigmoid(gs_ref[...].astype(F32)) * out_s)
    o_ref[...] = merged.astype(o_ref.dtype)


def _merge(attn, y, gates, gate_col0, wa, ws, tm=512, tn=1024):
    t = attn.shape[0]
    nj = D_MODEL // tn
    assert gate_col0 % tn == 0
    g0 = gate_col0 // tn
    return pl.pallas_call(
        _merge_kernel,
        out_shape=jax.ShapeDtypeStruct((t, D_MODEL), BF16),
        grid=(nj, t // tm),
        in_specs=[pl.BlockSpec((tm, Q_DIM), lambda j, i: (i, 0)),
                  pl.BlockSpec((tm, D_INNER), lambda j, i: (i, 0)),
                  pl.BlockSpec((tm, tn), lambda j, i: (i, g0 + j)),
                  pl.BlockSpec((tm, tn), lambda j, i: (i, g0 + nj + j)),
                  pl.BlockSpec((Q_DIM, tn), lambda j, i: (0, j)),
                  pl.BlockSpec((D_INNER, tn), lambda j, i: (0, j))],
        out_specs=pl.BlockSpec((tm, tn), lambda j, i: (i, j)),
        scratch_shapes=[pltpu.VMEM((Q_DIM, tn), BF16),
                        pltpu.VMEM((D_INNER, tn), BF16)],
        compiler_params=_cparams(("arbitrary", "arbitrary")),
        name="merge",
    )(attn, y, gates, gates, wa, ws)


def _oproj_kernel(m_ref, x_ref, w_ref, g_ref, h_ref, f_ref, w16_ref):
    @pl.when(pl.program_id(0) == 0)
    def _():
        w16_ref[...] = w_ref[...].astype(BF16)

    h = x_ref[...] + _dot(m_ref[...], w16_ref[...])
    h_ref[...] = h
    f_ref[...] = _rms(h, g_ref[...], NORM_EPS).astype(BF16)


def _oproj(merged, x2, wo, g_ffn, tm=512):
    t = merged.shape[0]
    row = lambda i: (i, 0)
    return pl.pallas_call(
        _oproj_kernel,
        out_shape=(jax.ShapeDtypeStruct((t, D_MODEL), F32),
                   jax.ShapeDtypeStruct((t, D_MODEL), BF16)),
        grid=(t // tm,),
        in_specs=[pl.BlockSpec((tm, D_MODEL), row),
                  pl.BlockSpec((tm, D_MODEL), row),
                  _resident((D_MODEL, D_MODEL)),
                  _resident((1, D_MODEL))],
        out_specs=(pl.BlockSpec((tm, D_MODEL), row),
                   pl.BlockSpec((tm, D_MODEL), row)),
        scratch_shapes=[pltpu.VMEM((D_MODEL, D_MODEL), BF16)],
        compiler_params=_cparams(("arbitrary",)),
        name="oproj",
    )(merged, x2, wo, g_ffn)


def _ffn_up_kernel(f_ref, wg_ref, wu_ref, o_ref, wg16_ref, wu16_ref):
    @pl.when(pl.program_id(1) == 0)
    def _():
        wg16_ref[...] = wg_ref[...].astype(BF16)
        wu16_ref[...] = wu_ref[...].astype(BF16)

    f = f_ref[...]
    o_ref[...] = (_silu(_dot(f, wg16_ref[...])) * _dot(f, wu16_ref[...])).astype(o_ref.dtype)


def _ffn_up(f, wg, wu, tm=2048, th=512):
    t = f.shape[0]
    return pl.pallas_call(
        _ffn_up_kernel,
        out_shape=jax.ShapeDtypeStruct((t, FFN_HIDDEN), BF16),
        grid=(FFN_HIDDEN // th, t // tm),
        in_specs=[pl.BlockSpec((tm, D_MODEL), lambda j, i: (i, 0)),
                  pl.BlockSpec((D_MODEL, th), lambda j, i: (0, j)),
                  pl.BlockSpec((D_MODEL, th), lambda j, i: (0, j))],
        out_specs=pl.BlockSpec((tm, th), lambda j, i: (i, j)),
        scratch_shapes=[pltpu.VMEM((D_MODEL, th), BF16),
                        pltpu.VMEM((D_MODEL, th), BF16)],
        compiler_params=_cparams(("arbitrary", "arbitrary")),
        name="ffn_up",
    )(f, wg, wu)


def _ffn_down_kernel(a_ref, h_ref, wd_hbm, o_ref, wd16_ref, wf32_ref, sem, *, n_col_tiles):
    j = pl.program_id(0)
    i = pl.program_id(1)
    tn = wf32_ref.shape[1]

    def copy(col_tile):
        return pltpu.make_async_copy(wd_hbm.at[:, pl.ds(col_tile * tn, tn)], wf32_ref, sem)

    @pl.when((j == 0) & (i == 0))
    def _():
        copy(0).start()

    @pl.when(i == 0)
    def _():
        copy(j).wait()
        wd16_ref[...] = wf32_ref[...].astype(BF16)

    @pl.when((i == 0) & (j + 1 < n_col_tiles))
    def _():
        copy(j + 1).start()

    o_ref[...] = h_ref[...] + _dot(a_ref[...], wd16_ref[...])


def _ffn_down(act, h1, wd, tm=1024, tn=512):
    t = act.shape[0]
    n_col_tiles = D_MODEL // tn
    return pl.pallas_call(
        partial(_ffn_down_kernel, n_col_tiles=n_col_tiles),
        out_shape=jax.ShapeDtypeStruct((t, D_MODEL), F32),
        grid=(n_col_tiles, t // tm),
        in_specs=[pl.BlockSpec((tm, FFN_HIDDEN), lambda j, i: (i, 0)),
                  pl.BlockSpec((tm, tn), lambda j, i: (i, j)),
                  pl.BlockSpec(memory_space=pl.ANY)],
        out_specs=pl.BlockSpec((tm, tn), lambda j, i: (i, j)),
        scratch_shapes=[pltpu.VMEM((FFN_HIDDEN, tn), BF16),
                        pltpu.VMEM((FFN_HIDDEN, tn), F32),
                        pltpu.SemaphoreType.DMA],
        compiler_params=_cparams(("arbitrary", "arbitrary")),
        name="ffn_down",
    )(act, h1, wd)


def _ple_kernel(h_ref, p_ref, gp_ref, gf_ref, wg_ref, wp_ref, o_ref, wg16_ref, wp16_ref):
    @pl.when(pl.program_id(0) == 0)
    def _():
        wg16_ref[...] = wg_ref[...].astype(BF16)
        wp16_ref[...] = wp_ref[...].astype(BF16)

    h = h_ref[...]
    r = _rms(h, gp_ref[...], NORM_EPS).astype(BF16)
    gate = _sigmoid(_dot(r, wg16_ref[...]))
    h3 = h + gate * _dot(p_ref[...].astype(BF16), wp16_ref[...])
    o_ref[...] = _rms(h3, gf_ref[...], NORM_EPS)


def _ple(h2, p2, g_ple, g_final, wpg, wpp, tm=512):
    t = h2.shape[0]
    row = lambda i: (i, 0)
    return pl.pallas_call(
        _ple_kernel,
        out_shape=jax.ShapeDtypeStruct((t, D_MODEL), F32),
        grid=(t // tm,),
        in_specs=[pl.BlockSpec((tm, D_MODEL), row),
                  pl.BlockSpec((tm, PLE_DIM), row),
                  _resident((1, D_MODEL)),
                  _resident((1, D_MODEL)),
                  _resident((D_MODEL, D_MODEL)),
                  _resident((PLE_DIM, D_MODEL))],
        out_specs=pl.BlockSpec((tm, D_MODEL), row),
        scratch_shapes=[pltpu.VMEM((D_MODEL, D_MODEL), BF16),
                        pltpu.VMEM((PLE_DIM, D_MODEL), BF16)],
        compiler_params=_cparams(("arbitrary",)),
        name="ple",
    )(h2, p2, g_ple, g_final, wpg, wpp)


def _rope_constants():
    half = HEAD_DIM // 2
    lane = np.arange(LANES)
    inv_freq = ROPE_THETA ** (-jnp.arange(half, dtype=F32) * 2.0 / HEAD_DIM)
    invf = inv_freq[lane % half][None, :]
    sgn = jnp.asarray(np.where((lane % HEAD_DIM) < half, -1.0, 1.0), F32)[None, :]
    return invf, sgn


def _expand_matrix():
    rows = np.arange(LANES)[:, None]
    cols = np.arange(D_INNER)[None, :]
    hit = (rows < 3 * SSM_HEADS) & ((rows % SSM_HEADS) == (cols // SSM_HEAD_DIM))
    return jnp.asarray(hit, BF16)


def _shift_matrix():
    rows = np.arange((CONV_WIDTH - 1) * CHUNK)[:, None]
    cols = np.arange(2 * CHUNK)[None, :]
    w, t = rows // CHUNK, rows % CHUNK
    return jnp.asarray(cols == CHUNK + t - (CONV_WIDTH - 1) + w, BF16)


def _pad_lanes(v):
    return jnp.pad(v.astype(F32), (0, LANES - v.shape[0]))[None, :]


def kernel(x, p, positions, g_mix, w_in, conv_w, conv_b, dt_bias, a_log, d_skip, g_ssd,
           sinks, w_attn_br, w_ssd_br, w_o, g_ffn, w_gate, w_up, w_down, g_ple,
           w_ple_gate, w_ple_proj, g_final):
    b, s, d = x.shape
    t = b * s
    assert d == D_MODEL and s % CHUNK == 0 and p.shape[0] == 1
    i = 0
    x2 = x.reshape(t, d)
    p2 = p[i].reshape(t, PLE_DIM)
    pos_c = jnp.repeat(positions.reshape(t // POS_PER_ROW, POS_PER_ROW), ROPE_FREQS, axis=1)

    o_k = Q_DIM
    o_z = Q_DIM + 2 * KV_DIM
    o_xbc = o_z + D_INNER
    o_dt = o_xbc + CONV_DIM
    o_ga = o_dt + SSM_HEADS
    wt = jnp.swapaxes(w_in[i], 0, 1)

    invf, sgn = _rope_constants()
    u, cos_t, sin_t, dtp = _prep(x2, g_mix[i][None, :], pos_c, invf, sgn, wt, o_dt)

    qkv = _inproj(u, wt, [(0, o_z)], BF16, INPROJ_ROW_TILE, PAIR_ROWS, "proj_qkv",
                  paired_tiles=Q_DIM // PAIR_ROWS)
    zxg = _inproj(u, wt, [(o_z, D_INNER + CONV_DIM), (o_ga, 2 * D_MODEL)], BF16,
                  INPROJ_ROW_TILE, INPROJ_COL_TILE, "proj_zxg")
    gate_col0 = D_INNER + CONV_DIM

    attn = _attention(qkv, cos_t, sin_t, sinks[i][None, :].astype(F32), b, s)

    y = _ssd(zxg, dtp, _shift_matrix(), conv_w[i], conv_b[i][None, :], _pad_lanes(dt_bias[i]),
             _pad_lanes(a_log[i]), jnp.repeat(d_skip[i].astype(F32), SSM_HEAD_DIM)[None, :],
             g_ssd[i][None, :], _expand_matrix(), b, s)

    merged = _merge(attn, y, zxg, gate_col0, w_attn_br[i], w_ssd_br[i])
    h1, f = _oproj(merged, x2, w_o[i], g_ffn[i][None, :])
    act = _ffn_up(f, w_gate[i], w_up[i])
    h2 = _ffn_down(act, h1, w_down[i])
    out = _ple(h2, p2, g_ple[i][None, :], g_final[None, :], w_ple_gate[i], w_ple_proj[i])
    return out.reshape(b, s, d)
```

```python
from functools import partial

import numpy as np
import jax
import jax.numpy as jnp
from jax import lax
from jax.experimental import pallas as pl
from jax.experimental.pallas import tpu as pltpu

F32 = jnp.float32
BF16 = jnp.bfloat16

D_MODEL = 2048
HEAD_DIM = 64
ATTN_HEADS = 16
KV_HEADS = 4
GROUP = ATTN_HEADS // KV_HEADS
Q_DIM = ATTN_HEADS * HEAD_DIM
KV_DIM = KV_HEADS * HEAD_DIM
ATTN_BLOCK = 128
ROPE_THETA = 10000.0
D_INNER = 2048
SSM_HEAD_DIM = 64
SSM_HEADS = 32
SSM_GROUPS = 4
HEADS_PER_GROUP = SSM_HEADS // SSM_GROUPS
D_STATE = 128
CONV_WIDTH = 4
CHUNK = 128
CONV_DIM = D_INNER + 2 * SSM_GROUPS * D_STATE
FFN_HIDDEN = 5632
PLE_DIM = 256
NORM_EPS = 1e-6
SSM_NORM_EPS = 1e-5

LANES = 128
SUBLANES = 8
VMEM_LIMIT_BYTES = 56 * 1024 * 1024

NEG_BIG = -1e30


def _cparams(semantics):
    return pltpu.CompilerParams(dimension_semantics=semantics,
                                vmem_limit_bytes=VMEM_LIMIT_BYTES)


def _resident(shape):
    return pl.BlockSpec(shape, lambda *_: (0,) * len(shape),
                        pipeline_mode=pl.Buffered(1))


def _rms(xf, g, eps):
    var = jnp.mean(xf * xf, axis=-1, keepdims=True)
    return xf * lax.rsqrt(var + eps) * g


NEG_LOG2E = -1.4426950408889634


def _sigmoid(x):
    return 1.0 / (1.0 + jnp.exp2(x * NEG_LOG2E))


def _silu(x):
    return x * _sigmoid(x)


def _dot(a, b):
    return jnp.dot(a, b, preferred_element_type=F32)


def _dot_nt(a, b):
    return lax.dot_general(a, b, (((1,), (1,)), ((), ())),
                           preferred_element_type=F32)


def _split3(v):
    v1 = v.astype(BF16)
    r1 = v - v1.astype(F32)
    v2 = r1.astype(BF16)
    v3 = (r1 - v2.astype(F32)).astype(BF16)
    return v1, v2, v3


ROPE_FREQS = HEAD_DIM // 2
POS_PER_ROW = LANES // ROPE_FREQS


def _prep_kernel(x_ref, g_ref, pos_ref, invf_ref, sgn_ref, wdt_ref, u_ref, cos_ref, sin_ref, dt_ref):
    u = _rms(x_ref[...], g_ref[...], NORM_EPS).astype(BF16)
    u_ref[...] = u
    rows = pos_ref.shape[0]
    ang = pos_ref[...].astype(F32) * invf_ref[...]
    cos_c = jnp.cos(ang)
    sin_c = jnp.sin(ang)
    group = lax.broadcasted_iota(jnp.int32, ang.shape, 1) // ROPE_FREQS
    for q in range(POS_PER_ROW):
        for src, dst_ref, sign in ((cos_c, cos_ref, None), (sin_c, sin_ref, sgn_ref[...])):
            only = jnp.where(group == q, src, 0.0)
            rep = only
            for k in range(1, POS_PER_ROW):
                rep = rep + pltpu.roll(only, k * ROPE_FREQS, 1)
            if sign is not None:
                rep = rep * sign
            dst_ref[pl.ds(q, rows, stride=POS_PER_ROW), :] = rep
    dt_ref[...] = _dot_nt(u, wdt_ref[...].astype(BF16))


def _prep(x2, g_mix, pos_c, invf, sgn, wt, dt_row0, tm=1024):
    t = x2.shape[0]
    row = lambda i: (i, 0)
    assert dt_row0 % SUBLANES == 0 and tm % (POS_PER_ROW * SUBLANES) == 0
    return pl.pallas_call(
        _prep_kernel,
        out_shape=(jax.ShapeDtypeStruct((t, D_MODEL), BF16),
                   jax.ShapeDtypeStruct((t, LANES), F32),
                   jax.ShapeDtypeStruct((t, LANES), F32),
                   jax.ShapeDtypeStruct((t, LANES), F32)),
        grid=(t // tm,),
        in_specs=[pl.BlockSpec((tm, D_MODEL), row),
                  _resident((1, D_MODEL)),
                  pl.BlockSpec((tm // POS_PER_ROW, LANES), row),
                  _resident((1, LANES)),
                  _resident((1, LANES)),
                  pl.BlockSpec((pl.Element(LANES), pl.Element(D_MODEL)),
                               lambda i: (dt_row0, 0), pipeline_mode=pl.Buffered(1))],
        out_specs=(pl.BlockSpec((tm, D_MODEL), row),
                   pl.BlockSpec((tm, LANES), row),
                   pl.BlockSpec((tm, LANES), row),
                   pl.BlockSpec((tm, LANES), row)),
        compiler_params=_cparams(("arbitrary",)),
        name="prep",
    )(x2, g_mix, pos_c, invf, sgn, wt)


PAIR_ROWS = 2 * GROUP * HEAD_DIM
INPROJ_ROW_TILE = 2048
INPROJ_COL_TILE = 1024


def _paired_head_blocks():
    return [(g * 2 + sub, sub * GROUP + g) for g in range(GROUP) for sub in range(2)]


def _cast_rows(dst_ref, src_ref, paired_heads):
    if not paired_heads:
        dst_ref[...] = src_ref[...].astype(BF16)
        return
    assert dst_ref.shape[0] % PAIR_ROWS == 0
    for base in range(0, dst_ref.shape[0], PAIR_ROWS):
        for dst, src in _paired_head_blocks():
            dst_ref[base + dst * HEAD_DIM:base + (dst + 1) * HEAD_DIM, :] = (
                src_ref[base + src * HEAD_DIM:base + (src + 1) * HEAD_DIM, :].astype(BF16))


def _inproj_kernel(a_ref, wt_ref, o_ref, wbf_ref, *, paired_tiles):
    j = pl.program_id(0)

    @pl.when((pl.program_id(1) == 0) & (j >= paired_tiles))
    def _():
        _cast_rows(wbf_ref, wt_ref, False)

    if paired_tiles:
        @pl.when((pl.program_id(1) == 0) & (j < paired_tiles))
        def _():
            _cast_rows(wbf_ref, wt_ref, True)

    o_ref[...] = _dot_nt(a_ref[...], wbf_ref[...]).astype(o_ref.dtype)


def _inproj(a, wt, segments, out_dtype, tm, tn, name, paired_tiles=0):
    m, k = a.shape
    assert all(seg_n % tn == 0 and row0 % SUBLANES == 0 for row0, seg_n in segments)
    assert not paired_tiles or tn % PAIR_ROWS == 0
    n = sum(seg_n for _, seg_n in segments)

    def weight_row(j):
        row, first_tile = None, 0
        for row0, seg_n in segments:
            here = row0 + (j - first_tile) * tn
            row = here if row is None else jnp.where(j >= first_tile, here, row)
            first_tile += seg_n // tn
        return pl.multiple_of(row, SUBLANES)

    return pl.pallas_call(
        partial(_inproj_kernel, paired_tiles=paired_tiles),
        out_shape=jax.ShapeDtypeStruct((m, n), out_dtype),
        grid=(n // tn, m // tm),
        in_specs=[pl.BlockSpec((tm, k), lambda j, i: (i, 0)),
                  pl.BlockSpec((pl.Element(tn), pl.Element(k)), lambda j, i: (weight_row(j), 0))],
        out_specs=pl.BlockSpec((tm, tn), lambda j, i: (i, j)),
        scratch_shapes=[pltpu.VMEM((tn, k), BF16)],
        compiler_params=_cparams(("arbitrary", "arbitrary")),
        name=name,
    )(a, wt)


ATTN_STEP_BLOCKS = 2


def _attn_kernel(sinks_ref, q_ref, kc_ref, kp_ref, vc_ref, vp_ref,
                 cc_ref, sc_ref, cp_ref, sp_ref, o_ref):
    n = pl.program_id(1)
    blk = ATTN_BLOCK
    nblk = ATTN_STEP_BLOCKS
    lane = lax.broadcasted_iota(jnp.int32, (blk, LANES), 1)
    first_half = (lane % HEAD_DIM) < (HEAD_DIM // 2)
    low_head = lane < HEAD_DIM

    def rope(t, c, s):
        partner = jnp.where(first_half,
                            pltpu.roll(t, LANES - HEAD_DIM // 2, 1),
                            pltpu.roll(t, HEAD_DIM // 2, 1))
        return t * c + partner * s

    def rows(j):
        return slice(j * blk, (j + 1) * blk)

    cp, sp = cp_ref[...], sp_ref[...]
    cc = [cc_ref[rows(j), :] for j in range(nblk)]
    sc = [sc_ref[rows(j), :] for j in range(nblk)]

    qi = lax.broadcasted_iota(jnp.int32, (blk, 2 * blk), 0)
    kj = lax.broadcasted_iota(jnp.int32, (blk, 2 * blk), 1)
    rel = kj - qi
    band = (rel >= 1) & (rel <= blk)
    valid = [band & ((kj >= blk) | (n > 0))] + [band] * (nblk - 1)

    scale = HEAD_DIM ** -0.5
    n_half = KV_DIM // LANES
    rows_per_half = 2 * GROUP * blk
    s_rows, sink_rows, valid_rows, v_wins = [], [], [], []
    for a in range(n_half):
        ksl = slice(a * LANES, (a + 1) * LANES)
        k_blocks = [rope(kp_ref[:, ksl].astype(F32), cp, sp).astype(BF16)]
        v_blocks = [vp_ref[:, ksl]]
        for j in range(nblk):
            k_blocks.append(rope(kc_ref[rows(j), ksl].astype(F32), cc[j], sc[j]).astype(BF16))
            v_blocks.append(vc_ref[rows(j), ksl])
        for j in range(nblk):
            k_win = jnp.concatenate(k_blocks[j:j + 2], axis=0)
            v_wins.append(jnp.concatenate(v_blocks[j:j + 2], axis=0))
            qc = [rope(q_ref[rows(j), (a * GROUP + g) * LANES:(a * GROUP + g + 1) * LANES]
                       .astype(F32), cc[j] * scale, sc[j] * scale) for g in range(GROUP)]
            q_rows = []
            for sub in range(2):
                keep = low_head if sub == 0 else jnp.logical_not(low_head)
                for g in range(GROUP):
                    q_rows.append(jnp.where(keep, qc[g], 0.0).astype(BF16))
                    sink_rows.append(
                        jnp.full((blk, LANES), sinks_ref[0, (2 * a + sub) * GROUP + g], F32))
            s_rows.append(_dot_nt(jnp.concatenate(q_rows, axis=0), k_win))
            valid_rows.extend([valid[j]] * (2 * GROUP))
    s = jnp.where(jnp.concatenate(valid_rows, axis=0), jnp.concatenate(s_rows, axis=0), NEG_BIG)
    sink = jnp.concatenate(sink_rows, axis=0)
    s0, s1 = s[:, :LANES], s[:, LANES:]
    m = jnp.maximum(jnp.max(jnp.maximum(s0, s1), axis=-1, keepdims=True), sink)
    e0 = jnp.exp(s0 - m)
    e1 = jnp.exp(s1 - m)
    den = jnp.sum(e0 + e1, axis=-1, keepdims=True) + jnp.exp(sink - m)
    r = 1.0 / den
    p = jnp.concatenate([e0 * r, e1 * r], axis=1).astype(BF16)
    for a in range(n_half):
        for j in range(nblk):
            w = a * nblk + j
            pv = _dot(p[w * rows_per_half:(w + 1) * rows_per_half], v_wins[w])
            for g in range(GROUP):
                lo = pv[g * blk:(g + 1) * blk]
                hi = pv[(GROUP + g) * blk:(GROUP + g + 1) * blk]
                chunk = a * GROUP + g
                o_ref[rows(j), chunk * LANES:(chunk + 1) * LANES] = (
                    jnp.where(low_head, lo, hi).astype(o_ref.dtype))


def _attention(qkv, cos_t, sin_t, sinks, batch, seq):
    t = qkv.shape[0]
    step = ATTN_STEP_BLOCKS * ATTN_BLOCK
    assert seq % step == 0
    ns = seq // step
    kcol = Q_DIM // KV_DIM
    vcol = kcol + 1
    cur = lambda b, n: b * ns + n
    prev = lambda b, n: jnp.maximum((b * ns + n) * ATTN_STEP_BLOCKS - 1, 0)
    return pl.pallas_call(
        _attn_kernel,
        out_shape=jax.ShapeDtypeStruct((t, Q_DIM), BF16),
        grid=(batch, ns),
        in_specs=[pl.BlockSpec(memory_space=pltpu.SMEM),
                  pl.BlockSpec((step, Q_DIM), lambda b, n: (cur(b, n), 0)),
                  pl.BlockSpec((step, KV_DIM), lambda b, n: (cur(b, n), kcol)),
                  pl.BlockSpec((ATTN_BLOCK, KV_DIM), lambda b, n: (prev(b, n), kcol)),
                  pl.BlockSpec((step, KV_DIM), lambda b, n: (cur(b, n), vcol)),
                  pl.BlockSpec((ATTN_BLOCK, KV_DIM), lambda b, n: (prev(b, n), vcol)),
                  pl.BlockSpec((step, LANES), lambda b, n: (cur(b, n), 0)),
                  pl.BlockSpec((step, LANES), lambda b, n: (cur(b, n), 0)),
                  pl.BlockSpec((ATTN_BLOCK, LANES), lambda b, n: (prev(b, n), 0)),
                  pl.BlockSpec((ATTN_BLOCK, LANES), lambda b, n: (prev(b, n), 0))],
        out_specs=pl.BlockSpec((step, Q_DIM), lambda b, n: (cur(b, n), 0)),
        compiler_params=_cparams(("arbitrary", "arbitrary")),
        name="attention",
    )(sinks, qkv, qkv, qkv, qkv, qkv, cos_t, sin_t, cos_t, sin_t)


SSD_STRIP = 256


def _ssd_kernel(z_ref, xin_ref, bcin_ref, xin_prev_ref, bcin_prev_ref, dt_ref, shift_ref, cw_ref,
                cb_ref, dtb_ref, alog_ref, dskip_ref, g_ref, expand_ref, o_ref,
                state_ref, xs_ref, b16_ref, c16_ref, xdd_ref, xd16_ref, yz_ref):
    c = pl.program_id(1)
    L = CHUNK
    W = SSD_STRIP
    bc_dim = SSM_GROUPS * D_STATE

    @pl.when(c == 0)
    def _():
        state_ref[...] = jnp.zeros_like(state_ref)

    n_shift = CONV_WIDTH - 1
    sel_col = lax.broadcasted_iota(jnp.int32, (n_shift * L, 2 * L), 1)
    sel = shift_ref[...]
    sel = jnp.where((sel_col < L) & (c == 0), jnp.zeros_like(sel), sel)
    for s in range(CONV_DIM // W):
        sl = slice(s * W, (s + 1) * W)
        if s * W < D_INNER:
            cur_ref, prev_ref, src = xin_ref, xin_prev_ref, sl
        else:
            cur_ref, prev_ref = bcin_ref, bcin_prev_ref
            src = slice(s * W - D_INNER, (s + 1) * W - D_INNER)
        x_cur = cur_ref[:, src]
        xwin = jnp.concatenate([prev_ref[:, src], x_cur], axis=0)
        taps = _dot(sel, xwin)
        acc = cb_ref[:, sl] + x_cur.astype(F32) * cw_ref[n_shift:CONV_WIDTH, sl]
        for w in range(n_shift):
            acc = acc + taps[w * L:(w + 1) * L] * cw_ref[w:w + 1, sl]
        xc = _silu(acc)
        if s * W < D_INNER:
            xs_ref[:, sl] = xc
        elif s * W < D_INNER + bc_dim:
            b16_ref[:, s * W - D_INNER:(s + 1) * W - D_INNER] = xc.astype(BF16)
        else:
            c16_ref[:, s * W - D_INNER - bc_dim:(s + 1) * W - D_INNER - bc_dim] = xc.astype(BF16)

    v = dt_ref[...] + dtb_ref[...]
    dt = jnp.maximum(v, 0.0) + jnp.log1p(jnp.exp(-jnp.abs(v)))
    a = dt * (jnp.exp(alog_ref[...]) * NEG_LOG2E)

    ri = lax.broadcasted_iota(jnp.int32, (L, L), 0)
    ci = lax.broadcasted_iota(jnp.int32, (L, L), 1)
    causal = ri >= ci
    tril = jnp.where(causal, 1.0, 0.0).astype(BF16)
    R = 2 * SUBLANES
    ones = jnp.ones((L, L), BF16)
    a1, a2, a3 = _split3(a)
    cs = _dot(tril, a1) + _dot(tril, a2) + _dot(tril, a3)
    tot = (_dot(ones, a1) + _dot(ones, a2) + _dot(ones, a3))[0:R]
    cs_t = cs.T[0:SSM_HEADS]

    lane = lax.broadcasted_iota(jnp.int32, (L, LANES), 1)

    def pack(q):
        ln = lax.broadcasted_iota(jnp.int32, q.shape, 1)
        q1, q2, q3 = _split3(q)
        packed = jnp.where(
            ln < SSM_HEADS, q1.astype(F32),
            jnp.where(ln < 2 * SSM_HEADS, pltpu.roll(q2.astype(F32), SSM_HEADS, 1),
                      jnp.where(ln < 3 * SSM_HEADS,
                                pltpu.roll(q3.astype(F32), 2 * SSM_HEADS, 1), 0.0)))
        return packed.astype(BF16)

    def tile_rows(v):
        return jnp.concatenate([v] * (L // SUBLANES), axis=0)

    pk_all = jnp.concatenate([pack(dt), pack(cs), pack(tot)], axis=0)
    pk_ct = pk_all[L:]

    for s in range(D_INNER // W):
        sl = slice(s * W, (s + 1) * W)
        ex = _dot(pk_all, expand_ref[:, sl])
        xd = xs_ref[:, sl] * ex[0:L]
        xd16_ref[:, sl] = xd.astype(BF16)
        tot_x = tile_rows(ex[2 * L:2 * L + SUBLANES])
        xdd_ref[:, sl] = (xd * jnp.exp2(tot_x - ex[L:2 * L])).astype(BF16)

    low_head = lane < SSM_HEAD_DIM
    heads_per_strip = W // SSM_HEAD_DIM
    strips_per_group = HEADS_PER_GROUP // heads_per_strip
    ssq = jnp.zeros((L, LANES), F32)

    for g in range(SSM_GROUPS):
        nsl = slice(g * D_STATE, (g + 1) * D_STATE)
        bg = b16_ref[:, nsl]
        cg = c16_ref[:, nsl]
        bg_t = bg.astype(F32).T.astype(BF16)
        cb = jnp.where(causal, _dot_nt(cg, bg), 0.0)
        for k in range(strips_per_group):
            s = g * strips_per_group + k
            sl = slice(s * W, (s + 1) * W)
            ex = _dot(pk_ct, expand_ref[:, sl])
            st_prev = state_ref[:, sl]
            y_off = _dot(cg, st_prev.astype(BF16))
            chunk_decay = tile_rows(jnp.exp2(ex[L:L + SUBLANES]))
            state_ref[:, sl] = st_prev * chunk_decay + _dot(bg_t, xdd_ref[:, sl])
            y_pairs = []
            for pair in range(W // LANES):
                psl = slice(s * W + pair * LANES, s * W + (pair + 1) * LANES)
                xp = xd16_ref[:, psl]
                mhs, xps = [], []
                for sub in range(2):
                    h = s * heads_per_strip + pair * 2 + sub
                    diff = cs[:, h:h + 1] - cs_t[h:h + 1, :]
                    decay = jnp.exp2(jnp.minimum(diff, 0.0))
                    mhs.append((cb * decay).astype(BF16))
                    keep = low_head if sub == 0 else jnp.logical_not(low_head)
                    xps.append(jnp.where(keep, xp, jnp.zeros_like(xp)))
                y_pairs.append(_dot(jnp.concatenate(mhs, axis=1), jnp.concatenate(xps, axis=0)))
            y = (jnp.concatenate(y_pairs, axis=1) + y_off * jnp.exp2(ex[0:L])
                 + dskip_ref[:, sl] * xs_ref[:, sl])
            yz = y * _silu(z_ref[:, sl].astype(F32))
            yz_ref[:, sl] = yz
            for pair in range(W // LANES):
                part = yz[:, pair * LANES:(pair + 1) * LANES]
                ssq = ssq + part * part

    var = jnp.sum(ssq, axis=-1, keepdims=True) * (1.0 / D_INNER)
    rs = lax.rsqrt(var + SSM_NORM_EPS)
    for s in range(D_INNER // W):
        sl = slice(s * W, (s + 1) * W)
        o_ref[:, sl] = (yz_ref[:, sl] * rs * g_ref[:, sl]).astype(o_ref.dtype)


def _ssd(zx, dtp, shift, conv_w, conv_b, dtb_p, alog_p, dskip_x, g_ssd, expand, batch, seq):
    t = zx.shape[0]
    nc = seq // CHUNK
    bc_dim = 2 * SSM_GROUPS * D_STATE
    row = lambda b, c: (b * nc + c, 0)
    cur = lambda b, c: b * nc + c
    prev = lambda b, c: jnp.maximum(b * nc + c - 1, 0)
    x_col = 1
    bc_col = 2 * D_INNER // bc_dim
    return pl.pallas_call(
        _ssd_kernel,
        out_shape=jax.ShapeDtypeStruct((t, D_INNER), BF16),
        grid=(batch, nc),
        in_specs=[pl.BlockSpec((CHUNK, D_INNER), row),
                  pl.BlockSpec((CHUNK, D_INNER), lambda b, c: (cur(b, c), x_col)),
                  pl.BlockSpec((CHUNK, bc_dim), lambda b, c: (cur(b, c), bc_col)),
                  pl.BlockSpec((CHUNK, D_INNER), lambda b, c: (prev(b, c), x_col)),
                  pl.BlockSpec((CHUNK, bc_dim), lambda b, c: (prev(b, c), bc_col)),
                  pl.BlockSpec((CHUNK, LANES), row),
                  _resident(((CONV_WIDTH - 1) * CHUNK, 2 * CHUNK)),
                  _resident((CONV_WIDTH, CONV_DIM)),
                  _resident((1, CONV_DIM)),
                  _resident((1, LANES)),
                  _resident((1, LANES)),
                  _resident((1, D_INNER)),
                  _resident((1, D_INNER)),
                  _resident((LANES, D_INNER))],
        out_specs=pl.BlockSpec((CHUNK, D_INNER), row),
        scratch_shapes=[pltpu.VMEM((D_STATE, D_INNER), F32),
                        pltpu.VMEM((CHUNK, D_INNER), F32),
                        pltpu.VMEM((CHUNK, SSM_GROUPS * D_STATE), BF16),
                        pltpu.VMEM((CHUNK, SSM_GROUPS * D_STATE), BF16),
                        pltpu.VMEM((CHUNK, D_INNER), BF16),
                        pltpu.VMEM((CHUNK, D_INNER), BF16),
                        pltpu.VMEM((CHUNK, D_INNER), F32)],
        compiler_params=_cparams(("arbitrary", "arbitrary")),
        name="ssd",
    )(zx, zx, zx, zx, zx, dtp, shift, conv_w, conv_b, dtb_p, alog_p, dskip_x, g_ssd, expand)


def _merge_kernel(attn_ref, y_ref, ga_ref, gs_ref, wa_ref, ws_ref, o_ref, wa16_ref, ws16_ref):
    @pl.when(pl.program_id(1) == 0)
    def _():
        _cast_rows(wa16_ref, wa_ref, True)
        ws16_ref[...] = ws_ref[...].astype(BF16)

    out_a = _dot(attn_ref[...], wa16_ref[...])
    out_s = _dot(y_ref[...], ws16_ref[...])
    merged = (_sigmoid(ga_ref[...].astype(F32)) * out_a
              + _sigmoid(gs_ref[...].astype(F32)) * out_s)
    o_ref[...] = merged.astype(o_ref.dtype)


def _merge(attn, y, gates, gate_col0, wa, ws, tm=512, tn=1024):
    t = attn.shape[0]
    nj = D_MODEL // tn
    assert gate_col0 % tn == 0
    g0 = gate_col0 // tn
    return pl.pallas_call(
        _merge_kernel,
        out_shape=jax.ShapeDtypeStruct((t, D_MODEL), BF16),
        grid=(nj, t // tm),
        in_specs=[pl.BlockSpec((tm, Q_DIM), lambda j, i: (i, 0)),
                  pl.BlockSpec((tm, D_INNER), lambda j, i: (i, 0)),
                  pl.BlockSpec((tm, tn), lambda j, i: (i, g0 + j)),
                  pl.BlockSpec((tm, tn), lambda j, i: (i, g0 + nj + j)),
                  pl.BlockSpec((Q_DIM, tn), lambda j, i: (0, j)),
                  pl.BlockSpec((D_INNER, tn), lambda j, i: (0, j))],
        out_specs=pl.BlockSpec((tm, tn), lambda j, i: (i, j)),
        scratch_shapes=[pltpu.VMEM((Q_DIM, tn), BF16),
                        pltpu.VMEM((D_INNER, tn), BF16)],
        compiler_params=_cparams(("arbitrary", "arbitrary")),
        name="merge",
    )(attn, y, gates, gates, wa, ws)


def _oproj_kernel(m_ref, x_ref, w_ref, g_ref, h_ref, f_ref, w16_ref):
    @pl.when(pl.program_id(0) == 0)
    def _():
        w16_ref[...] = w_ref[...].astype(BF16)

    h = x_ref[...] + _dot(m_ref[...], w16_ref[...])
    h_ref[...] = h
    f_ref[...] = _rms(h, g_ref[...], NORM_EPS).astype(BF16)


def _oproj(merged, x2, wo, g_ffn, tm=512):
    t = merged.shape[0]
    row = lambda i: (i, 0)
    return pl.pallas_call(
        _oproj_kernel,
        out_shape=(jax.ShapeDtypeStruct((t, D_MODEL), F32),
                   jax.ShapeDtypeStruct((t, D_MODEL), BF16)),
        grid=(t // tm,),
        in_specs=[pl.BlockSpec((tm, D_MODEL), row),
                  pl.BlockSpec((tm, D_MODEL), row),
                  _resident((D_MODEL, D_MODEL)),
                  _resident((1, D_MODEL))],
        out_specs=(pl.BlockSpec((tm, D_MODEL), row),
                   pl.BlockSpec((tm, D_MODEL), row)),
        scratch_shapes=[pltpu.VMEM((D_MODEL, D_MODEL), BF16)],
        compiler_params=_cparams(("arbitrary",)),
        name="oproj",
    )(merged, x2, wo, g_ffn)


def _ffn_up_kernel(f_ref, wg_ref, wu_ref, o_ref, wg16_ref, wu16_ref):
    @pl.when(pl.program_id(1) == 0)
    def _():
        wg16_ref[...] = wg_ref[...].astype(BF16)
        wu16_ref[...] = wu_ref[...].astype(BF16)

    f = f_ref[...]
    o_ref[...] = (_silu(_dot(f, wg16_ref[...])) * _dot(f, wu16_ref[...])).astype(o_ref.dtype)


def _ffn_up(f, wg, wu, tm=2048, th=512):
    t = f.shape[0]
    return pl.pallas_call(
        _ffn_up_kernel,
        out_shape=jax.ShapeDtypeStruct((t, FFN_HIDDEN), BF16),
        grid=(FFN_HIDDEN // th, t // tm),
        in_specs=[pl.BlockSpec((tm, D_MODEL), lambda j, i: (i, 0)),
                  pl.BlockSpec((D_MODEL, th), lambda j, i: (0, j)),
                  pl.BlockSpec((D_MODEL, th), lambda j, i: (0, j))],
        out_specs=pl.BlockSpec((tm, th), lambda j, i: (i, j)),
        scratch_shapes=[pltpu.VMEM((D_MODEL, th), BF16),
                        pltpu.VMEM((D_MODEL, th), BF16)],
        compiler_params=_cparams(("arbitrary", "arbitrary")),
        name="ffn_up",
    )(f, wg, wu)


def _ffn_down_kernel(a_ref, h_ref, wd_hbm, o_ref, wd16_ref, wf32_ref, sem, *, n_col_tiles):
    j = pl.program_id(0)
    i = pl.program_id(1)
    tn = wf32_ref.shape[1]

    def copy(col_tile):
        return pltpu.make_async_copy(wd_hbm.at[:, pl.ds(col_tile * tn, tn)], wf32_ref, sem)

    @pl.when((j == 0) & (i == 0))
    def _():
        copy(0).start()

    @pl.when(i == 0)
    def _():
        copy(j).wait()
        wd16_ref[...] = wf32_ref[...].astype(BF16)

    @pl.when((i == 0) & (j + 1 < n_col_tiles))
    def _():
        copy(j + 1).start()

    o_ref[...] = h_ref[...] + _dot(a_ref[...], wd16_ref[...])


def _ffn_down(act, h1, wd, tm=1024, tn=512):
    t = act.shape[0]
    n_col_tiles = D_MODEL // tn
    return pl.pallas_call(
        partial(_ffn_down_kernel, n_col_tiles=n_col_tiles),
        out_shape=jax.ShapeDtypeStruct((t, D_MODEL), F32),
        grid=(n_col_tiles, t // tm),
        in_specs=[pl.BlockSpec((tm, FFN_HIDDEN), lambda j, i: (i, 0)),
                  pl.BlockSpec((tm, tn), lambda j, i: (i, j)),
                  pl.BlockSpec(memory_space=pl.ANY)],
        out_specs=pl.BlockSpec((tm, tn), lambda j, i: (i, j)),
        scratch_shapes=[pltpu.VMEM((FFN_HIDDEN, tn), BF16),
                        pltpu.VMEM((FFN_HIDDEN, tn), F32),
                        pltpu.SemaphoreType.DMA],
        compiler_params=_cparams(("arbitrary", "arbitrary")),
        name="ffn_down",
    )(act, h1, wd)


def _ple_kernel(h_ref, p_ref, gp_ref, gf_ref, wg_ref, wp_ref, o_ref, wg16_ref, wp16_ref):
    @pl.when(pl.program_id(0) == 0)
    def _():
        wg16_ref[...] = wg_ref[...].astype(BF16)
        wp16_ref[...] = wp_ref[...].astype(BF16)

    h = h_ref[...]
    r = _rms(h, gp_ref[...], NORM_EPS).astype(BF16)
    gate = _sigmoid(_dot(r, wg16_ref[...]))
    h3 = h + gate * _dot(p_ref[...].astype(BF16), wp16_ref[...])
    o_ref[...] = _rms(h3, gf_ref[...], NORM_EPS)


def _ple(h2, p2, g_ple, g_final, wpg, wpp, tm=512):
    t = h2.shape[0]
    row = lambda i: (i, 0)
    return pl.pallas_call(
        _ple_kernel,
        out_shape=jax.ShapeDtypeStruct((t, D_MODEL), F32),
        grid=(t // tm,),
        in_specs=[pl.BlockSpec((tm, D_MODEL), row),
                  pl.BlockSpec((tm, PLE_DIM), row),
                  _resident((1, D_MODEL)),
                  _resident((1, D_MODEL)),
                  _resident((D_MODEL, D_MODEL)),
                  _resident((PLE_DIM, D_MODEL))],
        out_specs=pl.BlockSpec((tm, D_MODEL), row),
        scratch_shapes=[pltpu.VMEM((D_MODEL, D_MODEL), BF16),
                        pltpu.VMEM((PLE_DIM, D_MODEL), BF16)],
        compiler_params=_cparams(("arbitrary",)),
        name="ple",
    )(h2, p2, g_ple, g_final, wpg, wpp)


def _rope_constants():
    half = HEAD_DIM // 2
    lane = np.arange(LANES)
    inv_freq = ROPE_THETA ** (-jnp.arange(half, dtype=F32) * 2.0 / HEAD_DIM)
    invf = inv_freq[lane % half][None, :]
    sgn = jnp.asarray(np.where((lane % HEAD_DIM) < half, -1.0, 1.0), F32)[None, :]
    return invf, sgn


def _expand_matrix():
    rows = np.arange(LANES)[:, None]
    cols = np.arange(D_INNER)[None, :]
    hit = (rows < 3 * SSM_HEADS) & ((rows % SSM_HEADS) == (cols // SSM_HEAD_DIM))
    return jnp.asarray(hit, BF16)


def _shift_matrix():
    rows = np.arange((CONV_WIDTH - 1) * CHUNK)[:, None]
    cols = np.arange(2 * CHUNK)[None, :]
    w, t = rows // CHUNK, rows % CHUNK
    return jnp.asarray(cols == CHUNK + t - (CONV_WIDTH - 1) + w, BF16)


def _pad_lanes(v):
    return jnp.pad(v.astype(F32), (0, LANES - v.shape[0]))[None, :]


def kernel(x, p, positions, g_mix, w_in, conv_w, conv_b, dt_bias, a_log, d_skip, g_ssd,
           sinks, w_attn_br, w_ssd_br, w_o, g_ffn, w_gate, w_up, w_down, g_ple,
           w_ple_gate, w_ple_proj, g_final):
    b, s, d = x.shape
    t = b * s
    assert d == D_MODEL and s % CHUNK == 0 and p.shape[0] == 1
    i = 0
    x2 = x.reshape(t, d)
    p2 = p[i].reshape(t, PLE_DIM)
    pos_c = jnp.repeat(positions.reshape(t // POS_PER_ROW, POS_PER_ROW), ROPE_FREQS, axis=1)

    o_k = Q_DIM
    o_z = Q_DIM + 2 * KV_DIM
    o_xbc = o_z + D_INNER
    o_dt = o_xbc + CONV_DIM
    o_ga = o_dt + SSM_HEADS
    wt = jnp.swapaxes(w_in[i], 0, 1)

    invf, sgn = _rope_constants()
    u, cos_t, sin_t, dtp = _prep(x2, g_mix[i][None, :], pos_c, invf, sgn, wt, o_dt)

    qkv = _inproj(u, wt, [(0, o_z)], BF16, INPROJ_ROW_TILE, PAIR_ROWS, "proj_qkv",
                  paired_tiles=Q_DIM // PAIR_ROWS)
    zxg = _inproj(u, wt, [(o_z, D_INNER + CONV_DIM), (o_ga, 2 * D_MODEL)], BF16,
                  INPROJ_ROW_TILE, INPROJ_COL_TILE, "proj_zxg")
    gate_col0 = D_INNER + CONV_DIM

    attn = _attention(qkv, cos_t, sin_t, sinks[i][None, :].astype(F32), b, s)

    y = _ssd(zxg, dtp, _shift_matrix(), conv_w[i], conv_b[i][None, :], _pad_lanes(dt_bias[i]),
             _pad_lanes(a_log[i]), jnp.repeat(d_skip[i].astype(F32), SSM_HEAD_DIM)[None, :],
             g_ssd[i][None, :], _expand_matrix(), b, s)

    merged = _merge(attn, y, zxg, gate_col0, w_attn_br[i], w_ssd_br[i])
    h1, f = _oproj(merged, x2, w_o[i], g_ffn[i][None, :])
    act = _ffn_up(f, w_gate[i], w_up[i])
    h2 = _ffn_down(act, h1, w_down[i])
    out = _ple(h2, p2, g_ple[i][None, :], g_final[None, :], w_ple_gate[i], w_ple_proj[i])
    return out.reshape(b, s, d)
```

```python
from functools import partial

import numpy as np
import jax
import jax.numpy as jnp
from jax import lax
from jax.experimental import pallas as pl
from jax.experimental.pallas import tpu as pltpu

F32 = jnp.float32
BF16 = jnp.bfloat16

D_MODEL = 2048
HEAD_DIM = 64
ATTN_HEADS = 16
KV_HEADS = 4
GROUP = ATTN_HEADS // KV_HEADS
Q_DIM = ATTN_HEADS * HEAD_DIM
KV_DIM = KV_HEADS * HEAD_DIM
ATTN_BLOCK = 128
ROPE_THETA = 10000.0
D_INNER = 2048
SSM_HEAD_DIM = 64
SSM_HEADS = 32
SSM_GROUPS = 4
HEADS_PER_GROUP = SSM_HEADS // SSM_GROUPS
D_STATE = 128
CONV_WIDTH = 4
CHUNK = 128
CONV_DIM = D_INNER + 2 * SSM_GROUPS * D_STATE
FFN_HIDDEN = 5632
PLE_DIM = 256
NORM_EPS = 1e-6
SSM_NORM_EPS = 1e-5

LANES = 128
SUBLANES = 8
VMEM_LIMIT_BYTES = 56 * 1024 * 1024

NEG_BIG = -1e30


def _cparams(semantics):
    return pltpu.CompilerParams(dimension_semantics=semantics,
                                vmem_limit_bytes=VMEM_LIMIT_BYTES)


def _resident(shape):
    return pl.BlockSpec(shape, lambda *_: (0,) * len(shape),
                        pipeline_mode=pl.Buffered(1))


def _rms(xf, g, eps):
    var = jnp.mean(xf * xf, axis=-1, keepdims=True)
    return xf * lax.rsqrt(var + eps) * g


NEG_LOG2E = -1.4426950408889634


def _sigmoid(x):
    return 1.0 / (1.0 + jnp.exp2(x * NEG_LOG2E))


def _silu(x):
    return x * _sigmoid(x)


def _dot(a, b):
    return jnp.dot(a, b, preferred_element_type=F32)


def _dot_nt(a, b):
    return lax.dot_general(a, b, (((1,), (1,)), ((), ())),
                           preferred_element_type=F32)


def _split3(v):
    v1 = v.astype(BF16)
    r1 = v - v1.astype(F32)
    v2 = r1.astype(BF16)
    v3 = (r1 - v2.astype(F32)).astype(BF16)
    return v1, v2, v3


ROPE_FREQS = HEAD_DIM // 2
POS_PER_ROW = LANES // ROPE_FREQS


def _prep_kernel(x_ref, g_ref, pos_ref, invf_ref, sgn_ref, wdt_ref, u_ref, cos_ref, sin_ref, dt_ref):
    u = _rms(x_ref[...], g_ref[...], NORM_EPS).astype(BF16)
    u_ref[...] = u
    rows = pos_ref.shape[0]
    ang = pos_ref[...].astype(F32) * invf_ref[...]
    cos_c = jnp.cos(ang)
    sin_c = jnp.sin(ang)
    group = lax.broadcasted_iota(jnp.int32, ang.shape, 1) // ROPE_FREQS
    for q in range(POS_PER_ROW):
        for src, dst_ref, sign in ((cos_c, cos_ref, None), (sin_c, sin_ref, sgn_ref[...])):
            only = jnp.where(group == q, src, 0.0)
            rep = only
            for k in range(1, POS_PER_ROW):
                rep = rep + pltpu.roll(only, k * ROPE_FREQS, 1)
            if sign is not None:
                rep = rep * sign
            dst_ref[pl.ds(q, rows, stride=POS_PER_ROW), :] = rep
    dt_ref[...] = _dot_nt(u, wdt_ref[...].astype(BF16))


def _prep(x2, g_mix, pos_c, invf, sgn, wt, dt_row0, tm=1024):
    t = x2.shape[0]
    row = lambda i: (i, 0)
    assert dt_row0 % SUBLANES == 0 and tm % (POS_PER_ROW * SUBLANES) == 0
    return pl.pallas_call(
        _prep_kernel,
        out_shape=(jax.ShapeDtypeStruct((t, D_MODEL), BF16),
                   jax.ShapeDtypeStruct((t, LANES), F32),
                   jax.ShapeDtypeStruct((t, LANES), F32),
                   jax.ShapeDtypeStruct((t, LANES), F32)),
        grid=(t // tm,),
        in_specs=[pl.BlockSpec((tm, D_MODEL), row),
                  _resident((1, D_MODEL)),
                  pl.BlockSpec((tm // POS_PER_ROW, LANES), row),
                  _resident((1, LANES)),
                  _resident((1, LANES)),
                  pl.BlockSpec((pl.Element(LANES), pl.Element(D_MODEL)),
                               lambda i: (dt_row0, 0), pipeline_mode=pl.Buffered(1))],
        out_specs=(pl.BlockSpec((tm, D_MODEL), row),
                   pl.BlockSpec((tm, LANES), row),
                   pl.BlockSpec((tm, LANES), row),
                   pl.BlockSpec((tm, LANES), row)),
        compiler_params=_cparams(("arbitrary",)),
        name="prep",
    )(x2, g_mix, pos_c, invf, sgn, wt)


PAIR_ROWS = 2 * GROUP * HEAD_DIM
INPROJ_ROW_TILE = 2048
INPROJ_COL_TILE = 1024


def _paired_head_blocks():
    return [(g * 2 + sub, sub * GROUP + g) for g in range(GROUP) for sub in range(2)]


def _cast_rows(dst_ref, src_ref, paired_heads):
    if not paired_heads:
        dst_ref[...] = src_ref[...].astype(BF16)
        return
    assert dst_ref.shape[0] % PAIR_ROWS == 0
    for base in range(0, dst_ref.shape[0], PAIR_ROWS):
        for dst, src in _paired_head_blocks():
            dst_ref[base + dst * HEAD_DIM:base + (dst + 1) * HEAD_DIM, :] = (
                src_ref[base + src * HEAD_DIM:base + (src + 1) * HEAD_DIM, :].astype(BF16))


def _inproj_kernel(a_ref, wt_ref, o_ref, wbf_ref, *, paired_tiles):
    j = pl.program_id(0)

    @pl.when((pl.program_id(1) == 0) & (j >= paired_tiles))
    def _():
        _cast_rows(wbf_ref, wt_ref, False)

    if paired_tiles:
        @pl.when((pl.program_id(1) == 0) & (j < paired_tiles))
        def _():
            _cast_rows(wbf_ref, wt_ref, True)

    o_ref[...] = _dot_nt(a_ref[...], wbf_ref[...]).astype(o_ref.dtype)


def _inproj(a, wt, segments, out_dtype, tm, tn, name, paired_tiles=0):
    m, k = a.shape
    assert all(seg_n % tn == 0 and row0 % SUBLANES == 0 for row0, seg_n in segments)
    assert not paired_tiles or tn % PAIR_ROWS == 0
    n = sum(seg_n for _, seg_n in segments)

    def weight_row(j):
        row, first_tile = None, 0
        for row0, seg_n in segments:
            here = row0 + (j - first_tile) * tn
            row = here if row is None else jnp.where(j >= first_tile, here, row)
            first_tile += seg_n // tn
        return pl.multiple_of(row, SUBLANES)

    return pl.pallas_call(
        partial(_inproj_kernel, paired_tiles=paired_tiles),
        out_shape=jax.ShapeDtypeStruct((m, n), out_dtype),
        grid=(n // tn, m // tm),
        in_specs=[pl.BlockSpec((tm, k), lambda j, i: (i, 0)),
                  pl.BlockSpec((pl.Element(tn), pl.Element(k)), lambda j, i: (weight_row(j), 0))],
        out_specs=pl.BlockSpec((tm, tn), lambda j, i: (i, j)),
        scratch_shapes=[pltpu.VMEM((tn, k), BF16)],
        compiler_params=_cparams(("arbitrary", "arbitrary")),
        name=name,
    )(a, wt)


ATTN_STEP_BLOCKS = 2


def _attn_kernel(sinks_ref, q_ref, kc_ref, kp_ref, vc_ref, vp_ref,
                 cc_ref, sc_ref, cp_ref, sp_ref, o_ref):
    n = pl.program_id(1)
    blk = ATTN_BLOCK
    nblk = ATTN_STEP_BLOCKS
    lane = lax.broadcasted_iota(jnp.int32, (blk, LANES), 1)
    first_half = (lane % HEAD_DIM) < (HEAD_DIM // 2)
    low_head = lane < HEAD_DIM

    def rope(t, c, s):
        partner = jnp.where(first_half,
                            pltpu.roll(t, LANES - HEAD_DIM // 2, 1),
                            pltpu.roll(t, HEAD_DIM // 2, 1))
        return t * c + partner * s

    def rows(j):
        return slice(j * blk, (j + 1) * blk)

    cp, sp = cp_ref[...], sp_ref[...]
    cc = [cc_ref[rows(j), :] for j in range(nblk)]
    sc = [sc_ref[rows(j), :] for j in range(nblk)]

    qi = lax.broadcasted_iota(jnp.int32, (blk, 2 * blk), 0)
    kj = lax.broadcasted_iota(jnp.int32, (blk, 2 * blk), 1)
    rel = kj - qi
    band = (rel >= 1) & (rel <= blk)
    valid = [band & ((kj >= blk) | (n > 0))] + [band] * (nblk - 1)

    scale = HEAD_DIM ** -0.5
    n_half = KV_DIM // LANES
    rows_per_half = 2 * GROUP * blk
    s_rows, sink_rows, valid_rows, v_wins = [], [], [], []
    for a in range(n_half):
        ksl = slice(a * LANES, (a + 1) * LANES)
        k_blocks = [rope(kp_ref[:, ksl].astype(F32), cp, sp).astype(BF16)]
        v_blocks = [vp_ref[:, ksl]]
        for j in range(nblk):
            k_blocks.append(rope(kc_ref[rows(j), ksl].astype(F32), cc[j], sc[j]).astype(BF16))
            v_blocks.append(vc_ref[rows(j), ksl])
        for j in range(nblk):
            k_win = jnp.concatenate(k_blocks[j:j + 2], axis=0)
            v_wins.append(jnp.concatenate(v_blocks[j:j + 2], axis=0))
            qc = [rope(q_ref[rows(j), (a * GROUP + g) * LANES:(a * GROUP + g + 1) * LANES]
                       .astype(F32), cc[j] * scale, sc[j] * scale) for g in range(GROUP)]
            q_rows = []
            for sub in range(2):
                keep = low_head if sub == 0 else jnp.logical_not(low_head)
                for g in range(GROUP):
                    q_rows.append(jnp.where(keep, qc[g], 0.0).astype(BF16))
                    sink_rows.append(
                        jnp.full((blk, LANES), sinks_ref[0, (2 * a + sub) * GROUP + g], F32))
            s_rows.append(_dot_nt(jnp.concatenate(q_rows, axis=0), k_win))
            valid_rows.extend([valid[j]] * (2 * GROUP))
    s = jnp.where(jnp.concatenate(valid_rows, axis=0), jnp.concatenate(s_rows, axis=0), NEG_BIG)
    sink = jnp.concatenate(sink_rows, axis=0)
    s0, s1 = s[:, :LANES], s[:, LANES:]
    m = jnp.maximum(jnp.max(jnp.maximum(s0, s1), axis=-1, keepdims=True), sink)
    e0 = jnp.exp(s0 - m)
    e1 = jnp.exp(s1 - m)
    den = jnp.sum(e0 + e1, axis=-1, keepdims=True) + jnp.exp(sink - m)
    r = 1.0 / den
    p = jnp.concatenate([e0 * r, e1 * r], axis=1).astype(BF16)
    for a in range(n_half):
        for j in range(nblk):
            w = a * nblk + j
            pv = _dot(p[w * rows_per_half:(w + 1) * rows_per_half], v_wins[w])
            for g in range(GROUP):
                lo = pv[g * blk:(g + 1) * blk]
                hi = pv[(GROUP + g) * blk:(GROUP + g + 1) * blk]
                chunk = a * GROUP + g
                o_ref[rows(j), chunk * LANES:(chunk + 1) * LANES] = (
                    jnp.where(low_head, lo, hi).astype(o_ref.dtype))


def _attention(qkv, cos_t, sin_t, sinks, batch, seq):
    t = qkv.shape[0]
    step = ATTN_STEP_BLOCKS * ATTN_BLOCK
    assert seq % step == 0
    ns = seq // step
    kcol = Q_DIM // KV_DIM
    vcol = kcol + 1
    cur = lambda b, n: b * ns + n
    prev = lambda b, n: jnp.maximum((b * ns + n) * ATTN_STEP_BLOCKS - 1, 0)
    return pl.pallas_call(
        _attn_kernel,
        out_shape=jax.ShapeDtypeStruct((t, Q_DIM), BF16),
        grid=(batch, ns),
        in_specs=[pl.BlockSpec(memory_space=pltpu.SMEM),
                  pl.BlockSpec((step, Q_DIM), lambda b, n: (cur(b, n), 0)),
                  pl.BlockSpec((step, KV_DIM), lambda b, n: (cur(b, n), kcol)),
                  pl.BlockSpec((ATTN_BLOCK, KV_DIM), lambda b, n: (prev(b, n), kcol)),
                  pl.BlockSpec((step, KV_DIM), lambda b, n: (cur(b, n), vcol)),
                  pl.BlockSpec((ATTN_BLOCK, KV_DIM), lambda b, n: (prev(b, n), vcol)),
                  pl.BlockSpec((step, LANES), lambda b, n: (cur(b, n), 0)),
                  pl.BlockSpec((step, LANES), lambda b, n: (cur(b, n), 0)),
                  pl.BlockSpec((ATTN_BLOCK, LANES), lambda b, n: (prev(b, n), 0)),
                  pl.BlockSpec((ATTN_BLOCK, LANES), lambda b, n: (prev(b, n), 0))],
        out_specs=pl.BlockSpec((step, Q_DIM), lambda b, n: (cur(b, n), 0)),
        compiler_params=_cparams(("arbitrary", "arbitrary")),
        name="attention",
    )(sinks, qkv, qkv, qkv, qkv, qkv, cos_t, sin_t, cos_t, sin_t)


SSD_STRIP = 256


def _ssd_kernel(z_ref, xin_ref, bcin_ref, xin_prev_ref, bcin_prev_ref, dt_ref, shift_ref, cw_ref,
                cb_ref, dtb_ref, alog_ref, dskip_ref, g_ref, expand_ref, o_ref,
                state_ref, xs_ref, b16_ref, c16_ref, xdd_ref, xd16_ref, yz_ref):
    c = pl.program_id(1)
    L = CHUNK
    W = SSD_STRIP
    bc_dim = SSM_GROUPS * D_STATE

    @pl.when(c == 0)
    def _():
        state_ref[...] = jnp.zeros_like(state_ref)

    n_shift = CONV_WIDTH - 1
    sel_col = lax.broadcasted_iota(jnp.int32, (n_shift * L, 2 * L), 1)
    sel = shift_ref[...]
    sel = jnp.where((sel_col < L) & (c == 0), jnp.zeros_like(sel), sel)
    for s in range(CONV_DIM // W):
        sl = slice(s * W, (s + 1) * W)
        if s * W < D_INNER:
            cur_ref, prev_ref, src = xin_ref, xin_prev_ref, sl
        else:
            cur_ref, prev_ref = bcin_ref, bcin_prev_ref
            src = slice(s * W - D_INNER, (s + 1) * W - D_INNER)
        x_cur = cur_ref[:, src]
        xwin = jnp.concatenate([prev_ref[:, src], x_cur], axis=0)
        taps = _dot(sel, xwin)
        acc = cb_ref[:, sl] + x_cur.astype(F32) * cw_ref[n_shift:CONV_WIDTH, sl]
        for w in range(n_shift):
            acc = acc + taps[w * L:(w + 1) * L] * cw_ref[w:w + 1, sl]
        xc = _silu(acc)
        if s * W < D_INNER:
            xs_ref[:, sl] = xc
        elif s * W < D_INNER + bc_dim:
            b16_ref[:, s * W - D_INNER:(s + 1) * W - D_INNER] = xc.astype(BF16)
        else:
            c16_ref[:, s * W - D_INNER - bc_dim:(s + 1) * W - D_INNER - bc_dim] = xc.astype(BF16)

    v = dt_ref[...] + dtb_ref[...]
    dt = jnp.maximum(v, 0.0) + jnp.log1p(jnp.exp(-jnp.abs(v)))
    a = dt * (jnp.exp(alog_ref[...]) * NEG_LOG2E)

    ri = lax.broadcasted_iota(jnp.int32, (L, L), 0)
    ci = lax.broadcasted_iota(jnp.int32, (L, L), 1)
    causal = ri >= ci
    tril = jnp.where(causal, 1.0, 0.0).astype(BF16)
    R = 2 * SUBLANES
    ones = jnp.ones((L, L), BF16)
    a1, a2, a3 = _split3(a)
    cs = _dot(tril, a1) + _dot(tril, a2) + _dot(tril, a3)
    tot = (_dot(ones, a1) + _dot(ones, a2) + _dot(ones, a3))[0:R]
    cs_t = cs.T[0:SSM_HEADS]

    lane = lax.broadcasted_iota(jnp.int32, (L, LANES), 1)

    def pack(q):
        ln = lax.broadcasted_iota(jnp.int32, q.shape, 1)
        q1, q2, q3 = _split3(q)
        packed = jnp.where(
            ln < SSM_HEADS, q1.astype(F32),
            jnp.where(ln < 2 * SSM_HEADS, pltpu.roll(q2.astype(F32), SSM_HEADS, 1),
                      jnp.where(ln < 3 * SSM_HEADS,
                                pltpu.roll(q3.astype(F32), 2 * SSM_HEADS, 1), 0.0)))
        return packed.astype(BF16)

    def tile_rows(v):
        return jnp.concatenate([v] * (L // SUBLANES), axis=0)

    pk_all = jnp.concatenate([pack(dt), pack(cs), pack(tot)], axis=0)
    pk_ct = pk_all[L:]

    for s in range(D_INNER // W):
        sl = slice(s * W, (s + 1) * W)
        ex = _dot(pk_all, expand_ref[:, sl])
        xd = xs_ref[:, sl] * ex[0:L]
        xd16_ref[:, sl] = xd.astype(BF16)
        tot_x = tile_rows(ex[2 * L:2 * L + SUBLANES])
        xdd_ref[:, sl] = (xd * jnp.exp2(tot_x - ex[L:2 * L])).astype(BF16)

    low_head = lane < SSM_HEAD_DIM
    heads_per_strip = W // SSM_HEAD_DIM
    strips_per_group = HEADS_PER_GROUP // heads_per_strip
    ssq = jnp.zeros((L, LANES), F32)

    for g in range(SSM_GROUPS):
        nsl = slice(g * D_STATE, (g + 1) * D_STATE)
        bg = b16_ref[:, nsl]
        cg = c16_ref[:, nsl]
        bg_t = bg.astype(F32).T.astype(BF16)
        cb = jnp.where(causal, _dot_nt(cg, bg), 0.0)
        for k in range(strips_per_group):
            s = g * strips_per_group + k
            sl = slice(s * W, (s + 1) * W)
            ex = _dot(pk_ct, expand_ref[:, sl])
            st_prev = state_ref[:, sl]
            y_off = _dot(cg, st_prev.astype(BF16))
            chunk_decay = tile_rows(jnp.exp2(ex[L:L + SUBLANES]))
            state_ref[:, sl] = st_prev * chunk_decay + _dot(bg_t, xdd_ref[:, sl])
            y_pairs = []
            for pair in range(W // LANES):
                psl = slice(s * W + pair * LANES, s * W + (pair + 1) * LANES)
                xp = xd16_ref[:, psl]
                mhs, xps = [], []
                for sub in range(2):
                    h = s * heads_per_strip + pair * 2 + sub
                    diff = cs[:, h:h + 1] - cs_t[h:h + 1, :]
                    decay = jnp.exp2(jnp.minimum(diff, 0.0))
                    mhs.append((cb * decay).astype(BF16))
                    keep = low_head if sub == 0 else jnp.logical_not(low_head)
                    xps.append(jnp.where(keep, xp, jnp.zeros_like(xp)))
                y_pairs.append(_dot(jnp.concatenate(mhs, axis=1), jnp.concatenate(xps, axis=0)))
            y = (jnp.concatenate(y_pairs, axis=1) + y_off * jnp.exp2(ex[0:L])
                 + dskip_ref[:, sl] * xs_ref[:, sl])
            yz = y * _silu(z_ref[:, sl].astype(F32))
            yz_ref[:, sl] = yz
            for pair in range(W // LANES):
                part = yz[:, pair * LANES:(pair + 1) * LANES]
                ssq = ssq + part * part

    var = jnp.sum(ssq, axis=-1, keepdims=True) * (1.0 / D_INNER)
    rs = lax.rsqrt(var + SSM_NORM_EPS)
    for s in range(D_INNER // W):
        sl = slice(s * W, (s + 1) * W)
        o_ref[:, sl] = (yz_ref[:, sl] * rs * g_ref[:, sl]).astype(o_ref.dtype)


def _ssd(zx, dtp, shift, conv_w, conv_b, dtb_p, alog_p, dskip_x, g_ssd, expand, batch, seq):
    t = zx.shape[0]
    nc = seq // CHUNK
    bc_dim = 2 * SSM_GROUPS * D_STATE
    row = lambda b, c: (b * nc + c, 0)
    cur = lambda b, c: b * nc + c
    prev = lambda b, c: jnp.maximum(b * nc + c - 1, 0)
    x_col = 1
    bc_col = 2 * D_INNER // bc_dim
    return pl.pallas_call(
        _ssd_kernel,
        out_shape=jax.ShapeDtypeStruct((t, D_INNER), BF16),
        grid=(batch, nc),
        in_specs=[pl.BlockSpec((CHUNK, D_INNER), row),
                  pl.BlockSpec((CHUNK, D_INNER), lambda b, c: (cur(b, c), x_col)),
                  pl.BlockSpec((CHUNK, bc_dim), lambda b, c: (cur(b, c), bc_col)),
                  pl.BlockSpec((CHUNK, D_INNER), lambda b, c: (prev(b, c), x_col)),
                  pl.BlockSpec((CHUNK, bc_dim), lambda b, c: (prev(b, c), bc_col)),
                  pl.BlockSpec((CHUNK, LANES), row),
                  _resident(((CONV_WIDTH - 1) * CHUNK, 2 * CHUNK)),
                  _resident((CONV_WIDTH, CONV_DIM)),
                  _resident((1, CONV_DIM)),
                  _resident((1, LANES)),
                  _resident((1, LANES)),
                  _resident((1, D_INNER)),
                  _resident((1, D_INNER)),
                  _resident((LANES, D_INNER))],
        out_specs=pl.BlockSpec((CHUNK, D_INNER), row),
        scratch_shapes=[pltpu.VMEM((D_STATE, D_INNER), F32),
                        pltpu.VMEM((CHUNK, D_INNER), F32),
                        pltpu.VMEM((CHUNK, SSM_GROUPS * D_STATE), BF16),
                        pltpu.VMEM((CHUNK, SSM_GROUPS * D_STATE), BF16),
                        pltpu.VMEM((CHUNK, D_INNER), BF16),
                        pltpu.VMEM((CHUNK, D_INNER), BF16),
                        pltpu.VMEM((CHUNK, D_INNER), F32)],
        compiler_params=_cparams(("arbitrary", "arbitrary")),
        name="ssd",
    )(zx, zx, zx, zx, zx, dtp, shift, conv_w, conv_b, dtb_p, alog_p, dskip_x, g_ssd, expand)


def _merge_kernel(attn_ref, y_ref, ga_ref, gs_ref, wa_ref, ws_ref, o_ref, wa16_ref, ws16_ref):
    @pl.when(pl.program_id(1) == 0)
    def _():
        _cast_rows(wa16_ref, wa_ref, True)
        ws16_ref[...] = ws_ref[...].astype(BF16)

    out_a = _dot(attn_ref[...], wa16_ref[...])
    out_s = _dot(y_ref[...], ws16_ref[...])
    merged = (_sigmoid(ga_ref[...].astype(F32)) * out_a
              + _sigmoid(gs_ref[...].astype(F32)) * out_s)
    o_ref[...] = merged.astype(o_ref.dtype)


def _merge(attn, y, gates, gate_col0, wa, ws, tm=512, tn=1024):
    t = attn.shape[0]
    nj = D_MODEL // tn
    assert gate_col0 % tn == 0
    g0 = gate_col0 // tn
    return pl.pallas_call(
        _merge_kernel,
        out_shape=jax.ShapeDtypeStruct((t, D_MODEL), BF16),
        grid=(nj, t // tm),
        in_specs=[pl.BlockSpec((tm, Q_DIM), lambda j, i: (i, 0)),
                  pl.BlockSpec((tm, D_INNER), lambda j, i: (i, 0)),
                  pl.BlockSpec((tm, tn), lambda j, i: (i, g0 + j)),
                  pl.BlockSpec((tm, tn), lambda j, i: (i, g0 + nj + j)),
                  pl.BlockSpec((Q_DIM, tn), lambda j, i: (0, j)),
                  pl.BlockSpec((D_INNER, tn), lambda j, i: (0, j))],
        out_specs=pl.BlockSpec((tm, tn), lambda j, i: (i, j)),
        scratch_shapes=[pltpu.VMEM((Q_DIM, tn), BF16),
                        pltpu.VMEM((D_INNER, tn), BF16)],
        compiler_params=_cparams(("arbitrary", "arbitrary")),
        name="merge",
    )(attn, y, gates, gates, wa, ws)


def _oproj_kernel(m_ref, x_ref, w_ref, g_ref, h_ref, f_ref, w16_ref):
    @pl.when(pl.program_id(0) == 0)
    def _():
        w16_ref[...] = w_ref[...].astype(BF16)

    h = x_ref[...] + _dot(m_ref[...], w16_ref[...])
    h_ref[...] = h
    f_ref[...] = _rms(h, g_ref[...], NORM_EPS).astype(BF16)


def _oproj(merged, x2, wo, g_ffn, tm=512):
    t = merged.shape[0]
    row = lambda i: (i, 0)
    return pl.pallas_call(
        _oproj_kernel,
        out_shape=(jax.ShapeDtypeStruct((t, D_MODEL), F32),
                   jax.ShapeDtypeStruct((t, D_MODEL), BF16)),
        grid=(t // tm,),
        in_specs=[pl.BlockSpec((tm, D_MODEL), row),
                  pl.BlockSpec((tm, D_MODEL), row),
                  _resident((D_MODEL, D_MODEL)),
                  _resident((1, D_MODEL))],
        out_specs=(pl.BlockSpec((tm, D_MODEL), row),
                   pl.BlockSpec((tm, D_MODEL), row)),
        scratch_shapes=[pltpu.VMEM((D_MODEL, D_MODEL), BF16)],
        compiler_params=_cparams(("arbitrary",)),
        name="oproj",
    )(merged, x2, wo, g_ffn)


def _ffn_up_kernel(f_ref, wg_ref, wu_ref, o_ref, wg16_ref, wu16_ref):
    @pl.when(pl.program_id(1) == 0)
    def _():
        wg16_ref[...] = wg_ref[...].astype(BF16)
        wu16_ref[...] = wu_ref[...].astype(BF16)

    f = f_ref[...]
    half = o_ref.shape[1] // 2
    for c in range(2):
        sl = slice(c * half, (c + 1) * half)
        o_ref[:, sl] = (_silu(_dot(f, wg16_ref[:, sl])) * _dot(f, wu16_ref[:, sl])).astype(o_ref.dtype)


def _ffn_up(f, wg, wu, tm=2048, th=512):
    t = f.shape[0]
    return pl.pallas_call(
        _ffn_up_kernel,
        out_shape=jax.ShapeDtypeStruct((t, FFN_HIDDEN), BF16),
        grid=(FFN_HIDDEN // th, t // tm),
        in_specs=[pl.BlockSpec((tm, D_MODEL), lambda j, i: (i, 0)),
                  pl.BlockSpec((D_MODEL, th), lambda j, i: (0, j)),
                  pl.BlockSpec((D_MODEL, th), lambda j, i: (0, j))],
        out_specs=pl.BlockSpec((tm, th), lambda j, i: (i, j)),
        scratch_shapes=[pltpu.VMEM((D_MODEL, th), BF16),
                        pltpu.VMEM((D_MODEL, th), BF16)],
        compiler_params=_cparams(("arbitrary", "arbitrary")),
        name="ffn_up",
    )(f, wg, wu)


def _ffn_down_kernel(a_ref, h_ref, wd_hbm, o_ref, wd16_ref, wf32_ref, sem, *, n_col_tiles):
    j = pl.program_id(0)
    i = pl.program_id(1)
    tn = wf32_ref.shape[1]

    def copy(col_tile):
        return pltpu.make_async_copy(wd_hbm.at[:, pl.ds(col_tile * tn, tn)], wf32_ref, sem)

    @pl.when((j == 0) & (i == 0))
    def _():
        copy(0).start()

    @pl.when(i == 0)
    def _():
        copy(j).wait()
        wd16_ref[...] = wf32_ref[...].astype(BF16)

    @pl.when((i == 0) & (j + 1 < n_col_tiles))
    def _():
        copy(j + 1).start()

    o_ref[...] = h_ref[...] + _dot(a_ref[...], wd16_ref[...])


def _ffn_down(act, h1, wd, tm=1024, tn=512):
    t = act.shape[0]
    n_col_tiles = D_MODEL // tn
    return pl.pallas_call(
        partial(_ffn_down_kernel, n_col_tiles=n_col_tiles),
        out_shape=jax.ShapeDtypeStruct((t, D_MODEL), F32),
        grid=(n_col_tiles, t // tm),
        in_specs=[pl.BlockSpec((tm, FFN_HIDDEN), lambda j, i: (i, 0)),
                  pl.BlockSpec((tm, tn), lambda j, i: (i, j)),
                  pl.BlockSpec(memory_space=pl.ANY)],
        out_specs=pl.BlockSpec((tm, tn), lambda j, i: (i, j)),
        scratch_shapes=[pltpu.VMEM((FFN_HIDDEN, tn), BF16),
                        pltpu.VMEM((FFN_HIDDEN, tn), F32),
                        pltpu.SemaphoreType.DMA],
        compiler_params=_cparams(("arbitrary", "arbitrary")),
        name="ffn_down",
    )(act, h1, wd)


def _ple_kernel(h_ref, p_ref, gp_ref, gf_ref, wg_ref, wp_ref, o_ref, wg16_ref, wp16_ref):
    @pl.when(pl.program_id(0) == 0)
    def _():
        wg16_ref[...] = wg_ref[...].astype(BF16)
        wp16_ref[...] = wp_ref[...].astype(BF16)

    h = h_ref[...]
    r = _rms(h, gp_ref[...], NORM_EPS).astype(BF16)
    gate = _sigmoid(_dot(r, wg16_ref[...]))
    h3 = h + gate * _dot(p_ref[...].astype(BF16), wp16_ref[...])
    o_ref[...] = _rms(h3, gf_ref[...], NORM_EPS)


def _ple(h2, p2, g_ple, g_final, wpg, wpp, tm=512):
    t = h2.shape[0]
    row = lambda i: (i, 0)
    return pl.pallas_call(
        _ple_kernel,
        out_shape=jax.ShapeDtypeStruct((t, D_MODEL), F32),
        grid=(t // tm,),
        in_specs=[pl.BlockSpec((tm, D_MODEL), row),
                  pl.BlockSpec((tm, PLE_DIM), row),
                  _resident((1, D_MODEL)),
                  _resident((1, D_MODEL)),
                  _resident((D_MODEL, D_MODEL)),
                  _resident((PLE_DIM, D_MODEL))],
        out_specs=pl.BlockSpec((tm, D_MODEL), row),
        scratch_shapes=[pltpu.VMEM((D_MODEL, D_MODEL), BF16),
                        pltpu.VMEM((PLE_DIM, D_MODEL), BF16)],
        compiler_params=_cparams(("arbitrary",)),
        name="ple",
    )(h2, p2, g_ple, g_final, wpg, wpp)


def _rope_constants():
    half = HEAD_DIM // 2
    lane = np.arange(LANES)
    inv_freq = ROPE_THETA ** (-jnp.arange(half, dtype=F32) * 2.0 / HEAD_DIM)
    invf = inv_freq[lane % half][None, :]
    sgn = jnp.asarray(np.where((lane % HEAD_DIM) < half, -1.0, 1.0), F32)[None, :]
    return invf, sgn


def _expand_matrix():
    rows = np.arange(LANES)[:, None]
    cols = np.arange(D_INNER)[None, :]
    hit = (rows < 3 * SSM_HEADS) & ((rows % SSM_HEADS) == (cols // SSM_HEAD_DIM))
    return jnp.asarray(hit, BF16)


def _shift_matrix():
    rows = np.arange((CONV_WIDTH - 1) * CHUNK)[:, None]
    cols = np.arange(2 * CHUNK)[None, :]
    w, t = rows // CHUNK, rows % CHUNK
    return jnp.asarray(cols == CHUNK + t - (CONV_WIDTH - 1) + w, BF16)


def _pad_lanes(v):
    return jnp.pad(v.astype(F32), (0, LANES - v.shape[0]))[None, :]


def kernel(x, p, positions, g_mix, w_in, conv_w, conv_b, dt_bias, a_log, d_skip, g_ssd,
           sinks, w_attn_br, w_ssd_br, w_o, g_ffn, w_gate, w_up, w_down, g_ple,
           w_ple_gate, w_ple_proj, g_final):
    b, s, d = x.shape
    t = b * s
    assert d == D_MODEL and s % CHUNK == 0 and p.shape[0] == 1
    i = 0
    x2 = x.reshape(t, d)
    p2 = p[i].reshape(t, PLE_DIM)
    pos_c = jnp.repeat(positions.reshape(t // POS_PER_ROW, POS_PER_ROW), ROPE_FREQS, axis=1)

    o_k = Q_DIM
    o_z = Q_DIM + 2 * KV_DIM
    o_xbc = o_z + D_INNER
    o_dt = o_xbc + CONV_DIM
    o_ga = o_dt + SSM_HEADS
    wt = jnp.swapaxes(w_in[i], 0, 1)

    invf, sgn = _rope_constants()
    u, cos_t, sin_t, dtp = _prep(x2, g_mix[i][None, :], pos_c, invf, sgn, wt, o_dt)

    qkv = _inproj(u, wt, [(0, o_z)], BF16, INPROJ_ROW_TILE, PAIR_ROWS, "proj_qkv",
                  paired_tiles=Q_DIM // PAIR_ROWS)
    zxg = _inproj(u, wt, [(o_z, D_INNER + CONV_DIM), (o_ga, 2 * D_MODEL)], BF16,
                  INPROJ_ROW_TILE, INPROJ_COL_TILE, "proj_zxg")
    gate_col0 = D_INNER + CONV_DIM

    attn = _attention(qkv, cos_t, sin_t, sinks[i][None, :].astype(F32), b, s)

    y = _ssd(zxg, dtp, _shift_matrix(), conv_w[i], conv_b[i][None, :], _pad_lanes(dt_bias[i]),
             _pad_lanes(a_log[i]), jnp.repeat(d_skip[i].astype(F32), SSM_HEAD_DIM)[None, :],
             g_ssd[i][None, :], _expand_matrix(), b, s)

    merged = _merge(attn, y, zxg, gate_col0, w_attn_br[i], w_ssd_br[i])
    h1, f = _oproj(merged, x2, w_o[i], g_ffn[i][None, :])
    act = _ffn_up(f, w_gate[i], w_up[i])
    h2 = _ffn_down(act, h1, w_down[i])
    out = _ple(h2, p2, g_ple[i][None, :], g_final[None, :], w_ple_gate[i], w_ple_proj[i])
    return out.reshape(b, s, d)
```

```python
from functools import partial

import numpy as np
import jax
import jax.numpy as jnp
from jax import lax
from jax.experimental import pallas as pl
from jax.experimental.pallas import tpu as pltpu

F32 = jnp.float32
BF16 = jnp.bfloat16

D_MODEL = 2048
HEAD_DIM = 64
ATTN_HEADS = 16
KV_HEADS = 4
GROUP = ATTN_HEADS // KV_HEADS
Q_DIM = ATTN_HEADS * HEAD_DIM
KV_DIM = KV_HEADS * HEAD_DIM
ATTN_BLOCK = 128
ROPE_THETA = 10000.0
D_INNER = 2048
SSM_HEAD_DIM = 64
SSM_HEADS = 32
SSM_GROUPS = 4
HEADS_PER_GROUP = SSM_HEADS // SSM_GROUPS
D_STATE = 128
CONV_WIDTH = 4
CHUNK = 128
CONV_DIM = D_INNER + 2 * SSM_GROUPS * D_STATE
FFN_HIDDEN = 5632
PLE_DIM = 256
NORM_EPS = 1e-6
SSM_NORM_EPS = 1e-5

LANES = 128
SUBLANES = 8
VMEM_LIMIT_BYTES = 56 * 1024 * 1024

NEG_BIG = -1e30


def _cparams(semantics):
    return pltpu.CompilerParams(dimension_semantics=semantics,
                                vmem_limit_bytes=VMEM_LIMIT_BYTES)


def _resident(shape):
    return pl.BlockSpec(shape, lambda *_: (0,) * len(shape),
                        pipeline_mode=pl.Buffered(1))


def _rms(xf, g, eps):
    var = jnp.mean(xf * xf, axis=-1, keepdims=True)
    return xf * lax.rsqrt(var + eps) * g


NEG_LOG2E = -1.4426950408889634


def _sigmoid(x):
    return 1.0 / (1.0 + jnp.exp2(x * NEG_LOG2E))


def _silu(x):
    return x * _sigmoid(x)


def _dot(a, b):
    return jnp.dot(a, b, preferred_element_type=F32)


def _dot_nt(a, b):
    return lax.dot_general(a, b, (((1,), (1,)), ((), ())),
                           preferred_element_type=F32)


def _split3(v):
    v1 = v.astype(BF16)
    r1 = v - v1.astype(F32)
    v2 = r1.astype(BF16)
    v3 = (r1 - v2.astype(F32)).astype(BF16)
    return v1, v2, v3


ROPE_FREQS = HEAD_DIM // 2
POS_PER_ROW = LANES // ROPE_FREQS


def _prep_kernel(x_ref, g_ref, pos_ref, invf_ref, sgn_ref, wdt_ref, u_ref, cos_ref, sin_ref, dt_ref):
    u = _rms(x_ref[...], g_ref[...], NORM_EPS).astype(BF16)
    u_ref[...] = u
    rows = pos_ref.shape[0]
    ang = pos_ref[...].astype(F32) * invf_ref[...]
    cos_c = jnp.cos(ang)
    sin_c = jnp.sin(ang)
    group = lax.broadcasted_iota(jnp.int32, ang.shape, 1) // ROPE_FREQS
    for q in range(POS_PER_ROW):
        for src, dst_ref, sign in ((cos_c, cos_ref, None), (sin_c, sin_ref, sgn_ref[...])):
            only = jnp.where(group == q, src, 0.0)
            rep = only
            for k in range(1, POS_PER_ROW):
                rep = rep + pltpu.roll(only, k * ROPE_FREQS, 1)
            if sign is not None:
                rep = rep * sign
            dst_ref[pl.ds(q, rows, stride=POS_PER_ROW), :] = rep
    dt_ref[...] = _dot_nt(u, wdt_ref[...].astype(BF16))


def _prep(x2, g_mix, pos_c, invf, sgn, wt, dt_row0, tm=1024):
    t = x2.shape[0]
    row = lambda i: (i, 0)
    assert dt_row0 % SUBLANES == 0 and tm % (POS_PER_ROW * SUBLANES) == 0
    return pl.pallas_call(
        _prep_kernel,
        out_shape=(jax.ShapeDtypeStruct((t, D_MODEL), BF16),
                   jax.ShapeDtypeStruct((t, LANES), F32),
                   jax.ShapeDtypeStruct((t, LANES), F32),
                   jax.ShapeDtypeStruct((t, LANES), F32)),
        grid=(t // tm,),
        in_specs=[pl.BlockSpec((tm, D_MODEL), row),
                  _resident((1, D_MODEL)),
                  pl.BlockSpec((tm // POS_PER_ROW, LANES), row),
                  _resident((1, LANES)),
                  _resident((1, LANES)),
                  pl.BlockSpec((pl.Element(LANES), pl.Element(D_MODEL)),
                               lambda i: (dt_row0, 0), pipeline_mode=pl.Buffered(1))],
        out_specs=(pl.BlockSpec((tm, D_MODEL), row),
                   pl.BlockSpec((tm, LANES), row),
                   pl.BlockSpec((tm, LANES), row),
                   pl.BlockSpec((tm, LANES), row)),
        compiler_params=_cparams(("arbitrary",)),
        name="prep",
    )(x2, g_mix, pos_c, invf, sgn, wt)


PAIR_ROWS = 2 * GROUP * HEAD_DIM
INPROJ_ROW_TILE = 2048
INPROJ_COL_TILE = 1024


def _paired_head_blocks():
    return [(g * 2 + sub, sub * GROUP + g) for g in range(GROUP) for sub in range(2)]


def _cast_rows(dst_ref, src_ref, paired_heads):
    if not paired_heads:
        dst_ref[...] = src_ref[...].astype(BF16)
        return
    assert dst_ref.shape[0] % PAIR_ROWS == 0
    for base in range(0, dst_ref.shape[0], PAIR_ROWS):
        for dst, src in _paired_head_blocks():
            dst_ref[base + dst * HEAD_DIM:base + (dst + 1) * HEAD_DIM, :] = (
                src_ref[base + src * HEAD_DIM:base + (src + 1) * HEAD_DIM, :].astype(BF16))


def _inproj_kernel(a_ref, wt_ref, o_ref, wbf_ref, *, paired_tiles):
    j = pl.program_id(0)

    @pl.when((pl.program_id(1) == 0) & (j >= paired_tiles))
    def _():
        _cast_rows(wbf_ref, wt_ref, False)

    if paired_tiles:
        @pl.when((pl.program_id(1) == 0) & (j < paired_tiles))
        def _():
            _cast_rows(wbf_ref, wt_ref, True)

    o_ref[...] = _dot_nt(a_ref[...], wbf_ref[...]).astype(o_ref.dtype)


def _inproj(a, wt, segments, out_dtype, tm, tn, name, paired_tiles=0):
    m, k = a.shape
    assert all(seg_n % tn == 0 and row0 % SUBLANES == 0 for row0, seg_n in segments)
    assert not paired_tiles or tn % PAIR_ROWS == 0
    n = sum(seg_n for _, seg_n in segments)

    def weight_row(j):
        row, first_tile = None, 0
        for row0, seg_n in segments:
            here = row0 + (j - first_tile) * tn
            row = here if row is None else jnp.where(j >= first_tile, here, row)
            first_tile += seg_n // tn
        return pl.multiple_of(row, SUBLANES)

    return pl.pallas_call(
        partial(_inproj_kernel, paired_tiles=paired_tiles),
        out_shape=jax.ShapeDtypeStruct((m, n), out_dtype),
        grid=(n // tn, m // tm),
        in_specs=[pl.BlockSpec((tm, k), lambda j, i: (i, 0)),
                  pl.BlockSpec((pl.Element(tn), pl.Element(k)), lambda j, i: (weight_row(j), 0))],
        out_specs=pl.BlockSpec((tm, tn), lambda j, i: (i, j)),
        scratch_shapes=[pltpu.VMEM((tn, k), BF16)],
        compiler_params=_cparams(("arbitrary", "arbitrary")),
        name=name,
    )(a, wt)


ATTN_STEP_BLOCKS = 2


def _attn_kernel(sinks_ref, q_ref, kc_ref, kp_ref, vc_ref, vp_ref,
                 cc_ref, sc_ref, cp_ref, sp_ref, o_ref):
    n = pl.program_id(1)
    blk = ATTN_BLOCK
    nblk = ATTN_STEP_BLOCKS
    lane = lax.broadcasted_iota(jnp.int32, (blk, LANES), 1)
    first_half = (lane % HEAD_DIM) < (HEAD_DIM // 2)
    low_head = lane < HEAD_DIM

    def rope(t, c, s):
        partner = jnp.where(first_half,
                            pltpu.roll(t, LANES - HEAD_DIM // 2, 1),
                            pltpu.roll(t, HEAD_DIM // 2, 1))
        return t * c + partner * s

    def rows(j):
        return slice(j * blk, (j + 1) * blk)

    cp, sp = cp_ref[...], sp_ref[...]
    cc = [cc_ref[rows(j), :] for j in range(nblk)]
    sc = [sc_ref[rows(j), :] for j in range(nblk)]

    qi = lax.broadcasted_iota(jnp.int32, (blk, 2 * blk), 0)
    kj = lax.broadcasted_iota(jnp.int32, (blk, 2 * blk), 1)
    rel = kj - qi
    band = (rel >= 1) & (rel <= blk)
    valid = [band & ((kj >= blk) | (n > 0))] + [band] * (nblk - 1)

    scale = HEAD_DIM ** -0.5
    n_half = KV_DIM // LANES
    rows_per_half = 2 * GROUP * blk
    s_rows, sink_rows, valid_rows, v_wins = [], [], [], []
    for a in range(n_half):
        ksl = slice(a * LANES, (a + 1) * LANES)
        k_blocks = [rope(kp_ref[:, ksl].astype(F32), cp, sp).astype(BF16)]
        v_blocks = [vp_ref[:, ksl]]
        for j in range(nblk):
            k_blocks.append(rope(kc_ref[rows(j), ksl].astype(F32), cc[j], sc[j]).astype(BF16))
            v_blocks.append(vc_ref[rows(j), ksl])
        for j in range(nblk):
            k_win = jnp.concatenate(k_blocks[j:j + 2], axis=0)
            v_wins.append(jnp.concatenate(v_blocks[j:j + 2], axis=0))
            qc = [rope(q_ref[rows(j), (a * GROUP + g) * LANES:(a * GROUP + g + 1) * LANES]
                       .astype(F32), cc[j] * scale, sc[j] * scale) for g in range(GROUP)]
            q_rows = []
            for sub in range(2):
                keep = low_head if sub == 0 else jnp.logical_not(low_head)
                for g in range(GROUP):
                    q_rows.append(jnp.where(keep, qc[g], 0.0).astype(BF16))
                    sink_rows.append(
                        jnp.full((blk, LANES), sinks_ref[0, (2 * a + sub) * GROUP + g], F32))
            s_rows.append(_dot_nt(jnp.concatenate(q_rows, axis=0), k_win))
            valid_rows.extend([valid[j]] * (2 * GROUP))
    s = jnp.where(jnp.concatenate(valid_rows, axis=0), jnp.concatenate(s_rows, axis=0), NEG_BIG)
    sink = jnp.concatenate(sink_rows, axis=0)
    s0, s1 = s[:, :LANES], s[:, LANES:]
    m = jnp.maximum(jnp.max(jnp.maximum(s0, s1), axis=-1, keepdims=True), sink)
    e0 = jnp.exp(s0 - m)
    e1 = jnp.exp(s1 - m)
    den = jnp.sum(e0 + e1, axis=-1, keepdims=True) + jnp.exp(sink - m)
    r = 1.0 / den
    p = jnp.concatenate([e0 * r, e1 * r], axis=1).astype(BF16)
    for a in range(n_half):
        for j in range(nblk):
            w = a * nblk + j
            pv = _dot(p[w * rows_per_half:(w + 1) * rows_per_half], v_wins[w])
            for g in range(GROUP):
                lo = pv[g * blk:(g + 1) * blk]
                hi = pv[(GROUP + g) * blk:(GROUP + g + 1) * blk]
                chunk = a * GROUP + g
                o_ref[rows(j), chunk * LANES:(chunk + 1) * LANES] = (
                    jnp.where(low_head, lo, hi).astype(o_ref.dtype))


def _attention(qkv, cos_t, sin_t, sinks, batch, seq):
    t = qkv.shape[0]
    step = ATTN_STEP_BLOCKS * ATTN_BLOCK
    assert seq % step == 0
    ns = seq // step
    kcol = Q_DIM // KV_DIM
    vcol = kcol + 1
    cur = lambda b, n: b * ns + n
    prev = lambda b, n: jnp.maximum((b * ns + n) * ATTN_STEP_BLOCKS - 1, 0)
    return pl.pallas_call(
        _attn_kernel,
        out_shape=jax.ShapeDtypeStruct((t, Q_DIM), BF16),
        grid=(batch, ns),
        in_specs=[pl.BlockSpec(memory_space=pltpu.SMEM),
                  pl.BlockSpec((step, Q_DIM), lambda b, n: (cur(b, n), 0)),
                  pl.BlockSpec((step, KV_DIM), lambda b, n: (cur(b, n), kcol)),
                  pl.BlockSpec((ATTN_BLOCK, KV_DIM), lambda b, n: (prev(b, n), kcol)),
                  pl.BlockSpec((step, KV_DIM), lambda b, n: (cur(b, n), vcol)),
                  pl.BlockSpec((ATTN_BLOCK, KV_DIM), lambda b, n: (prev(b, n), vcol)),
                  pl.BlockSpec((step, LANES), lambda b, n: (cur(b, n), 0)),
                  pl.BlockSpec((step, LANES), lambda b, n: (cur(b, n), 0)),
                  pl.BlockSpec((ATTN_BLOCK, LANES), lambda b, n: (prev(b, n), 0)),
                  pl.BlockSpec((ATTN_BLOCK, LANES), lambda b, n: (prev(b, n), 0))],
        out_specs=pl.BlockSpec((step, Q_DIM), lambda b, n: (cur(b, n), 0)),
        compiler_params=_cparams(("arbitrary", "arbitrary")),
        name="attention",
    )(sinks, qkv, qkv, qkv, qkv, qkv, cos_t, sin_t, cos_t, sin_t)


SSD_STRIP = 256


def _ssd_kernel(z_ref, xin_ref, bcin_ref, xin_prev_ref, bcin_prev_ref, dt_ref, shift_ref, cw_ref,
                cb_ref, dtb_ref, alog_ref, dskip_ref, g_ref, expand_ref, o_ref,
                state_ref, xs_ref, b16_ref, c16_ref, xdd_ref, xd16_ref, yz_ref):
    c = pl.program_id(1)
    L = CHUNK
    W = SSD_STRIP
    bc_dim = SSM_GROUPS * D_STATE

    @pl.when(c == 0)
    def _():
        state_ref[...] = jnp.zeros_like(state_ref)

    n_shift = CONV_WIDTH - 1
    sel_col = lax.broadcasted_iota(jnp.int32, (n_shift * L, 2 * L), 1)
    sel = shift_ref[...]
    sel = jnp.where((sel_col < L) & (c == 0), jnp.zeros_like(sel), sel)
    for s in range(CONV_DIM // W):
        sl = slice(s * W, (s + 1) * W)
        if s * W < D_INNER:
            cur_ref, prev_ref, src = xin_ref, xin_prev_ref, sl
        else:
            cur_ref, prev_ref = bcin_ref, bcin_prev_ref
            src = slice(s * W - D_INNER, (s + 1) * W - D_INNER)
        x_cur = cur_ref[:, src]
        xwin = jnp.concatenate([prev_ref[:, src], x_cur], axis=0)
        taps = _dot(sel, xwin)
        acc = cb_ref[:, sl] + x_cur.astype(F32) * cw_ref[n_shift:CONV_WIDTH, sl]
        for w in range(n_shift):
            acc = acc + taps[w * L:(w + 1) * L] * cw_ref[w:w + 1, sl]
        xc = _silu(acc)
        if s * W < D_INNER:
            xs_ref[:, sl] = xc
        elif s * W < D_INNER + bc_dim:
            b16_ref[:, s * W - D_INNER:(s + 1) * W - D_INNER] = xc.astype(BF16)
        else:
            c16_ref[:, s * W - D_INNER - bc_dim:(s + 1) * W - D_INNER - bc_dim] = xc.astype(BF16)

    v = dt_ref[...] + dtb_ref[...]
    dt = jnp.maximum(v, 0.0) + jnp.log1p(jnp.exp(-jnp.abs(v)))
    a = dt * (jnp.exp(alog_ref[...]) * NEG_LOG2E)

    ri = lax.broadcasted_iota(jnp.int32, (L, L), 0)
    ci = lax.broadcasted_iota(jnp.int32, (L, L), 1)
    causal = ri >= ci
    tril = jnp.where(causal, 1.0, 0.0).astype(BF16)
    R = 2 * SUBLANES
    ones = jnp.ones((L, L), BF16)
    a1, a2, a3 = _split3(a)
    cs = _dot(tril, a1) + _dot(tril, a2) + _dot(tril, a3)
    tot = (_dot(ones, a1) + _dot(ones, a2) + _dot(ones, a3))[0:R]
    cs_t = cs.T[0:SSM_HEADS]

    lane = lax.broadcasted_iota(jnp.int32, (L, LANES), 1)

    def pack(q):
        ln = lax.broadcasted_iota(jnp.int32, q.shape, 1)
        q1, q2, q3 = _split3(q)
        packed = jnp.where(
            ln < SSM_HEADS, q1.astype(F32),
            jnp.where(ln < 2 * SSM_HEADS, pltpu.roll(q2.astype(F32), SSM_HEADS, 1),
                      jnp.where(ln < 3 * SSM_HEADS,
                                pltpu.roll(q3.astype(F32), 2 * SSM_HEADS, 1), 0.0)))
        return packed.astype(BF16)

    def tile_rows(v):
        return jnp.concatenate([v] * (L // SUBLANES), axis=0)

    pk_all = jnp.concatenate([pack(dt), pack(cs), pack(tot)], axis=0)
    pk_ct = pk_all[L:]

    for s in range(D_INNER // W):
        sl = slice(s * W, (s + 1) * W)
        ex = _dot(pk_all, expand_ref[:, sl])
        xd = xs_ref[:, sl] * ex[0:L]
        xd16_ref[:, sl] = xd.astype(BF16)
        tot_x = tile_rows(ex[2 * L:2 * L + SUBLANES])
        xdd_ref[:, sl] = (xd * jnp.exp2(tot_x - ex[L:2 * L])).astype(BF16)

    low_head = lane < SSM_HEAD_DIM
    heads_per_strip = W // SSM_HEAD_DIM
    strips_per_group = HEADS_PER_GROUP // heads_per_strip
    ssq = jnp.zeros((L, LANES), F32)

    for g in range(SSM_GROUPS):
        nsl = slice(g * D_STATE, (g + 1) * D_STATE)
        bg = b16_ref[:, nsl]
        cg = c16_ref[:, nsl]
        bg_t = bg.astype(F32).T.astype(BF16)
        cb = jnp.where(causal, _dot_nt(cg, bg), 0.0)
        for k in range(strips_per_group):
            s = g * strips_per_group + k
            sl = slice(s * W, (s + 1) * W)
            ex = _dot(pk_ct, expand_ref[:, sl])
            st_prev = state_ref[:, sl]
            y_off = _dot(cg, st_prev.astype(BF16))
            chunk_decay = tile_rows(jnp.exp2(ex[L:L + SUBLANES]))
            state_ref[:, sl] = st_prev * chunk_decay + _dot(bg_t, xdd_ref[:, sl])
            y_pairs = []
            for pair in range(W // LANES):
                psl = slice(s * W + pair * LANES, s * W + (pair + 1) * LANES)
                xp = xd16_ref[:, psl]
                mhs, xps = [], []
                for sub in range(2):
                    h = s * heads_per_strip + pair * 2 + sub
                    diff = cs[:, h:h + 1] - cs_t[h:h + 1, :]
                    decay = jnp.exp2(jnp.minimum(diff, 0.0))
                    mhs.append((cb * decay).astype(BF16))
                    keep = low_head if sub == 0 else jnp.logical_not(low_head)
                    xps.append(jnp.where(keep, xp, jnp.zeros_like(xp)))
                y_pairs.append(_dot(jnp.concatenate(mhs, axis=1), jnp.concatenate(xps, axis=0)))
            y = (jnp.concatenate(y_pairs, axis=1) + y_off * jnp.exp2(ex[0:L])
                 + dskip_ref[:, sl] * xs_ref[:, sl])
            yz = y * _silu(z_ref[:, sl].astype(F32))
            yz_ref[:, sl] = yz
            for pair in range(W // LANES):
                part = yz[:, pair * LANES:(pair + 1) * LANES]
                ssq = ssq + part * part

    var = jnp.sum(ssq, axis=-1, keepdims=True) * (1.0 / D_INNER)
    rs = lax.rsqrt(var + SSM_NORM_EPS)
    for s in range(D_INNER // W):
        sl = slice(s * W, (s + 1) * W)
        o_ref[:, sl] = (yz_ref[:, sl] * rs * g_ref[:, sl]).astype(o_ref.dtype)


def _ssd(zx, dtp, shift, conv_w, conv_b, dtb_p, alog_p, dskip_x, g_ssd, expand, batch, seq):
    t = zx.shape[0]
    nc = seq // CHUNK
    bc_dim = 2 * SSM_GROUPS * D_STATE
    row = lambda b, c: (b * nc + c, 0)
    cur = lambda b, c: b * nc + c
    prev = lambda b, c: jnp.maximum(b * nc + c - 1, 0)
    x_col = 1
    bc_col = 2 * D_INNER // bc_dim
    return pl.pallas_call(
        _ssd_kernel,
        out_shape=jax.ShapeDtypeStruct((t, D_INNER), BF16),
        grid=(batch, nc),
        in_specs=[pl.BlockSpec((CHUNK, D_INNER), row),
                  pl.BlockSpec((CHUNK, D_INNER), lambda b, c: (cur(b, c), x_col)),
                  pl.BlockSpec((CHUNK, bc_dim), lambda b, c: (cur(b, c), bc_col)),
                  pl.BlockSpec((CHUNK, D_INNER), lambda b, c: (prev(b, c), x_col)),
                  pl.BlockSpec((CHUNK, bc_dim), lambda b, c: (prev(b, c), bc_col)),
                  pl.BlockSpec((CHUNK, LANES), row),
                  _resident(((CONV_WIDTH - 1) * CHUNK, 2 * CHUNK)),
                  _resident((CONV_WIDTH, CONV_DIM)),
                  _resident((1, CONV_DIM)),
                  _resident((1, LANES)),
                  _resident((1, LANES)),
                  _resident((1, D_INNER)),
                  _resident((1, D_INNER)),
                  _resident((LANES, D_INNER))],
        out_specs=pl.BlockSpec((CHUNK, D_INNER), row),
        scratch_shapes=[pltpu.VMEM((D_STATE, D_INNER), F32),
                        pltpu.VMEM((CHUNK, D_INNER), F32),
                        pltpu.VMEM((CHUNK, SSM_GROUPS * D_STATE), BF16),
                        pltpu.VMEM((CHUNK, SSM_GROUPS * D_STATE), BF16),
                        pltpu.VMEM((CHUNK, D_INNER), BF16),
                        pltpu.VMEM((CHUNK, D_INNER), BF16),
                        pltpu.VMEM((CHUNK, D_INNER), F32)],
        compiler_params=_cparams(("arbitrary", "arbitrary")),
        name="ssd",
    )(zx, zx, zx, zx, zx, dtp, shift, conv_w, conv_b, dtb_p, alog_p, dskip_x, g_ssd, expand)


def _merge_kernel(attn_ref, y_ref, ga_ref, gs_ref, wa_ref, ws_ref, o_ref, wa16_ref, ws16_ref):
    @pl.when(pl.program_id(1) == 0)
    def _():
        _cast_rows(wa16_ref, wa_ref, True)
        ws16_ref[...] = ws_ref[...].astype(BF16)

    out_a = _dot(attn_ref[...], wa16_ref[...])
    out_s = _dot(y_ref[...], ws16_ref[...])
    merged = (_sigmoid(ga_ref[...].astype(F32)) * out_a
              + _sigmoid(gs_ref[...].astype(F32)) * out_s)
    o_ref[...] = merged.astype(o_ref.dtype)


def _merge(attn, y, gates, gate_col0, wa, ws, tm=512, tn=1024):
    t = attn.shape[0]
    nj = D_MODEL // tn
    assert gate_col0 % tn == 0
    g0 = gate_col0 // tn
    return pl.pallas_call(
        _merge_kernel,
        out_shape=jax.ShapeDtypeStruct((t, D_MODEL), BF16),
        grid=(nj, t // tm),
        in_specs=[pl.BlockSpec((tm, Q_DIM), lambda j, i: (i, 0)),
                  pl.BlockSpec((tm, D_INNER), lambda j, i: (i, 0)),
                  pl.BlockSpec((tm, tn), lambda j, i: (i, g0 + j)),
                  pl.BlockSpec((tm, tn), lambda j, i: (i, g0 + nj + j)),
                  pl.BlockSpec((Q_DIM, tn), lambda j, i: (0, j)),
                  pl.BlockSpec((D_INNER, tn), lambda j, i: (0, j))],
        out_specs=pl.BlockSpec((tm, tn), lambda j, i: (i, j)),
        scratch_shapes=[pltpu.VMEM((Q_DIM, tn), BF16),
                        pltpu.VMEM((D_INNER, tn), BF16)],
        compiler_params=_cparams(("arbitrary", "arbitrary")),
        name="merge",
    )(attn, y, gates, gates, wa, ws)


def _oproj_kernel(m_ref, x_ref, w_ref, g_ref, h_ref, f_ref, w16_ref):
    @pl.when(pl.program_id(0) == 0)
    def _():
        w16_ref[...] = w_ref[...].astype(BF16)

    h = x_ref[...] + _dot(m_ref[...], w16_ref[...])
    h_ref[...] = h
    f_ref[...] = _rms(h, g_ref[...], NORM_EPS).astype(BF16)


def _oproj(merged, x2, wo, g_ffn, tm=512):
    t = merged.shape[0]
    row = lambda i: (i, 0)
    return pl.pallas_call(
        _oproj_kernel,
        out_shape=(jax.ShapeDtypeStruct((t, D_MODEL), F32),
                   jax.ShapeDtypeStruct((t, D_MODEL), BF16)),
        grid=(t // tm,),
        in_specs=[pl.BlockSpec((tm, D_MODEL), row),
                  pl.BlockSpec((tm, D_MODEL), row),
                  _resident((D_MODEL, D_MODEL)),
                  _resident((1, D_MODEL))],
        out_specs=(pl.BlockSpec((tm, D_MODEL), row),
                   pl.BlockSpec((tm, D_MODEL), row)),
        scratch_shapes=[pltpu.VMEM((D_MODEL, D_MODEL), BF16)],
        compiler_params=_cparams(("arbitrary",)),
        name="oproj",
    )(merged, x2, wo, g_ffn)


def _ffn_up_kernel(f_ref, wg_ref, wu_ref, o_ref, wg16_ref, wu16_ref):
    @pl.when(pl.program_id(1) == 0)
    def _():
        wg16_ref[...] = wg_ref[...].astype(BF16)
        wu16_ref[...] = wu_ref[...].astype(BF16)

    half_r = o_ref.shape[0] // 2
    half_c = o_ref.shape[1] // 2
    for r in range(2):
        rs = slice(r * half_r, (r + 1) * half_r)
        f = f_ref[rs, :]
        for c in range(2):
            cs = slice(c * half_c, (c + 1) * half_c)
            o_ref[rs, cs] = (_silu(_dot(f, wg16_ref[:, cs]))
                             * _dot(f, wu16_ref[:, cs])).astype(o_ref.dtype)


def _ffn_up(f, wg, wu, tm=2048, th=512):
    t = f.shape[0]
    return pl.pallas_call(
        _ffn_up_kernel,
        out_shape=jax.ShapeDtypeStruct((t, FFN_HIDDEN), BF16),
        grid=(FFN_HIDDEN // th, t // tm),
        in_specs=[pl.BlockSpec((tm, D_MODEL), lambda j, i: (i, 0)),
                  pl.BlockSpec((D_MODEL, th), lambda j, i: (0, j)),
                  pl.BlockSpec((D_MODEL, th), lambda j, i: (0, j))],
        out_specs=pl.BlockSpec((tm, th), lambda j, i: (i, j)),
        scratch_shapes=[pltpu.VMEM((D_MODEL, th), BF16),
                        pltpu.VMEM((D_MODEL, th), BF16)],
        compiler_params=_cparams(("arbitrary", "arbitrary")),
        name="ffn_up",
    )(f, wg, wu)


def _ffn_down_kernel(a_ref, h_ref, wd_hbm, o_ref, wd16_ref, wf32_ref, sem, *, n_col_tiles):
    j = pl.program_id(0)
    i = pl.program_id(1)
    tn = wf32_ref.shape[1]

    def copy(col_tile):
        return pltpu.make_async_copy(wd_hbm.at[:, pl.ds(col_tile * tn, tn)], wf32_ref, sem)

    @pl.when((j == 0) & (i == 0))
    def _():
        copy(0).start()

    @pl.when(i == 0)
    def _():
        copy(j).wait()
        wd16_ref[...] = wf32_ref[...].astype(BF16)

    @pl.when((i == 0) & (j + 1 < n_col_tiles))
    def _():
        copy(j + 1).start()

    o_ref[...] = h_ref[...] + _dot(a_ref[...], wd16_ref[...])


def _ffn_down(act, h1, wd, tm=1024, tn=512):
    t = act.shape[0]
    n_col_tiles = D_MODEL // tn
    return pl.pallas_call(
        partial(_ffn_down_kernel, n_col_tiles=n_col_tiles),
        out_shape=jax.ShapeDtypeStruct((t, D_MODEL), F32),
        grid=(n_col_tiles, t // tm),
        in_specs=[pl.BlockSpec((tm, FFN_HIDDEN), lambda j, i: (i, 0)),
                  pl.BlockSpec((tm, tn), lambda j, i: (i, j)),
                  pl.BlockSpec(memory_space=pl.ANY)],
        out_specs=pl.BlockSpec((tm, tn), lambda j, i: (i, j)),
        scratch_shapes=[pltpu.VMEM((FFN_HIDDEN, tn), BF16),
                        pltpu.VMEM((FFN_HIDDEN, tn), F32),
                        pltpu.SemaphoreType.DMA],
        compiler_params=_cparams(("arbitrary", "arbitrary")),
        name="ffn_down",
    )(act, h1, wd)


def _ple_kernel(h_ref, p_ref, gp_ref, gf_ref, wg_ref, wp_ref, o_ref, wg16_ref, wp16_ref):
    @pl.when(pl.program_id(0) == 0)
    def _():
        wg16_ref[...] = wg_ref[...].astype(BF16)
        wp16_ref[...] = wp_ref[...].astype(BF16)

    h = h_ref[...]
    r = _rms(h, gp_ref[...], NORM_EPS).astype(BF16)
    gate = _sigmoid(_dot(r, wg16_ref[...]))
    h3 = h + gate * _dot(p_ref[...].astype(BF16), wp16_ref[...])
    o_ref[...] = _rms(h3, gf_ref[...], NORM_EPS)


def _ple(h2, p2, g_ple, g_final, wpg, wpp, tm=512):
    t = h2.shape[0]
    row = lambda i: (i, 0)
    return pl.pallas_call(
        _ple_kernel,
        out_shape=jax.ShapeDtypeStruct((t, D_MODEL), F32),
        grid=(t // tm,),
        in_specs=[pl.BlockSpec((tm, D_MODEL), row),
                  pl.BlockSpec((tm, PLE_DIM), row),
                  _resident((1, D_MODEL)),
                  _resident((1, D_MODEL)),
                  _resident((D_MODEL, D_MODEL)),
                  _resident((PLE_DIM, D_MODEL))],
        out_specs=pl.BlockSpec((tm, D_MODEL), row),
        scratch_shapes=[pltpu.VMEM((D_MODEL, D_MODEL), BF16),
                        pltpu.VMEM((PLE_DIM, D_MODEL), BF16)],
        compiler_params=_cparams(("arbitrary",)),
        name="ple",
    )(h2, p2, g_ple, g_final, wpg, wpp)


def _rope_constants():
    half = HEAD_DIM // 2
    lane = np.arange(LANES)
    inv_freq = ROPE_THETA ** (-jnp.arange(half, dtype=F32) * 2.0 / HEAD_DIM)
    invf = inv_freq[lane % half][None, :]
    sgn = jnp.asarray(np.where((lane % HEAD_DIM) < half, -1.0, 1.0), F32)[None, :]
    return invf, sgn


def _expand_matrix():
    rows = np.arange(LANES)[:, None]
    cols = np.arange(D_INNER)[None, :]
    hit = (rows < 3 * SSM_HEADS) & ((rows % SSM_HEADS) == (cols // SSM_HEAD_DIM))
    return jnp.asarray(hit, BF16)


def _shift_matrix():
    rows = np.arange((CONV_WIDTH - 1) * CHUNK)[:, None]
    cols = np.arange(2 * CHUNK)[None, :]
    w, t = rows // CHUNK, rows % CHUNK
    return jnp.asarray(cols == CHUNK + t - (CONV_WIDTH - 1) + w, BF16)


def _pad_lanes(v):
    return jnp.pad(v.astype(F32), (0, LANES - v.shape[0]))[None, :]


def kernel(x, p, positions, g_mix, w_in, conv_w, conv_b, dt_bias, a_log, d_skip, g_ssd,
           sinks, w_attn_br, w_ssd_br, w_o, g_ffn, w_gate, w_up, w_down, g_ple,
           w_ple_gate, w_ple_proj, g_final):
    b, s, d = x.shape
    t = b * s
    assert d == D_MODEL and s % CHUNK == 0 and p.shape[0] == 1
    i = 0
    x2 = x.reshape(t, d)
    p2 = p[i].reshape(t, PLE_DIM)
    pos_c = jnp.repeat(positions.reshape(t // POS_PER_ROW, POS_PER_ROW), ROPE_FREQS, axis=1)

    o_k = Q_DIM
    o_z = Q_DIM + 2 * KV_DIM
    o_xbc = o_z + D_INNER
    o_dt = o_xbc + CONV_DIM
    o_ga = o_dt + SSM_HEADS
    wt = jnp.swapaxes(w_in[i], 0, 1)

    invf, sgn = _rope_constants()
    u, cos_t, sin_t, dtp = _prep(x2, g_mix[i][None, :], pos_c, invf, sgn, wt, o_dt)

    qkv = _inproj(u, wt, [(0, o_z)], BF16, INPROJ_ROW_TILE, PAIR_ROWS, "proj_qkv",
                  paired_tiles=Q_DIM // PAIR_ROWS)
    zxg = _inproj(u, wt, [(o_z, D_INNER + CONV_DIM), (o_ga, 2 * D_MODEL)], BF16,
                  INPROJ_ROW_TILE, INPROJ_COL_TILE, "proj_zxg")
    gate_col0 = D_INNER + CONV_DIM

    attn = _attention(qkv, cos_t, sin_t, sinks[i][None, :].astype(F32), b, s)

    y = _ssd(zxg, dtp, _shift_matrix(), conv_w[i], conv_b[i][None, :], _pad_lanes(dt_bias[i]),
             _pad_lanes(a_log[i]), jnp.repeat(d_skip[i].astype(F32), SSM_HEAD_DIM)[None, :],
             g_ssd[i][None, :], _expand_matrix(), b, s)

    merged = _merge(attn, y, zxg, gate_col0, w_attn_br[i], w_ssd_br[i])
    h1, f = _oproj(merged, x2, w_o[i], g_ffn[i][None, :])
    act = _ffn_up(f, w_gate[i], w_up[i])
    h2 = _ffn_down(act, h1, w_down[i])
    out = _ple(h2, p2, g_ple[i][None, :], g_final[None, :], w_ple_gate[i], w_ple_proj[i])
    return out.reshape(b, s, d)
```

```python
from functools import partial

import numpy as np
import jax
import jax.numpy as jnp
from jax import lax
from jax.experimental import pallas as pl
from jax.experimental.pallas import tpu as pltpu

F32 = jnp.float32
BF16 = jnp.bfloat16

D_MODEL = 2048
HEAD_DIM = 64
ATTN_HEADS = 16
KV_HEADS = 4
GROUP = ATTN_HEADS // KV_HEADS
Q_DIM = ATTN_HEADS * HEAD_DIM
KV_DIM = KV_HEADS * HEAD_DIM
ATTN_BLOCK = 128
ROPE_THETA = 10000.0
D_INNER = 2048
SSM_HEAD_DIM = 64
SSM_HEADS = 32
SSM_GROUPS = 4
HEADS_PER_GROUP = SSM_HEADS // SSM_GROUPS
D_STATE = 128
CONV_WIDTH = 4
CHUNK = 128
CONV_DIM = D_INNER + 2 * SSM_GROUPS * D_STATE
FFN_HIDDEN = 5632
PLE_DIM = 256
NORM_EPS = 1e-6
SSM_NORM_EPS = 1e-5

LANES = 128
SUBLANES = 8
VMEM_LIMIT_BYTES = 56 * 1024 * 1024

NEG_BIG = -1e30


def _cparams(semantics):
    return pltpu.CompilerParams(dimension_semantics=semantics,
                                vmem_limit_bytes=VMEM_LIMIT_BYTES)


def _resident(shape):
    return pl.BlockSpec(shape, lambda *_: (0,) * len(shape),
                        pipeline_mode=pl.Buffered(1))


def _rms(xf, g, eps):
    var = jnp.mean(xf * xf, axis=-1, keepdims=True)
    return xf * lax.rsqrt(var + eps) * g


NEG_LOG2E = -1.4426950408889634


def _sigmoid(x):
    return 1.0 / (1.0 + jnp.exp2(x * NEG_LOG2E))


def _silu(x):
    return x * _sigmoid(x)


def _dot(a, b):
    return jnp.dot(a, b, preferred_element_type=F32)


def _dot_nt(a, b):
    return lax.dot_general(a, b, (((1,), (1,)), ((), ())),
                           preferred_element_type=F32)


def _split3(v):
    v1 = v.astype(BF16)
    r1 = v - v1.astype(F32)
    v2 = r1.astype(BF16)
    v3 = (r1 - v2.astype(F32)).astype(BF16)
    return v1, v2, v3


ROPE_FREQS = HEAD_DIM // 2
POS_PER_ROW = LANES // ROPE_FREQS


def _prep_kernel(x_ref, g_ref, pos_ref, invf_ref, sgn_ref, wdt_ref, u_ref, cos_ref, sin_ref, dt_ref):
    u = _rms(x_ref[...], g_ref[...], NORM_EPS).astype(BF16)
    u_ref[...] = u
    rows = pos_ref.shape[0]
    ang = pos_ref[...].astype(F32) * invf_ref[...]
    cos_c = jnp.cos(ang)
    sin_c = jnp.sin(ang)
    group = lax.broadcasted_iota(jnp.int32, ang.shape, 1) // ROPE_FREQS
    for q in range(POS_PER_ROW):
        for src, dst_ref, sign in ((cos_c, cos_ref, None), (sin_c, sin_ref, sgn_ref[...])):
            only = jnp.where(group == q, src, 0.0)
            rep = only
            for k in range(1, POS_PER_ROW):
                rep = rep + pltpu.roll(only, k * ROPE_FREQS, 1)
            if sign is not None:
                rep = rep * sign
            dst_ref[pl.ds(q, rows, stride=POS_PER_ROW), :] = rep
    dt_ref[...] = _dot_nt(u, wdt_ref[...].astype(BF16))


def _prep(x2, g_mix, pos_c, invf, sgn, wt, dt_row0, tm=1024):
    t = x2.shape[0]
    row = lambda i: (i, 0)
    assert dt_row0 % SUBLANES == 0 and tm % (POS_PER_ROW * SUBLANES) == 0
    return pl.pallas_call(
        _prep_kernel,
        out_shape=(jax.ShapeDtypeStruct((t, D_MODEL), BF16),
                   jax.ShapeDtypeStruct((t, LANES), F32),
                   jax.ShapeDtypeStruct((t, LANES), F32),
                   jax.ShapeDtypeStruct((t, LANES), F32)),
        grid=(t // tm,),
        in_specs=[pl.BlockSpec((tm, D_MODEL), row),
                  _resident((1, D_MODEL)),
                  pl.BlockSpec((tm // POS_PER_ROW, LANES), row),
                  _resident((1, LANES)),
                  _resident((1, LANES)),
                  pl.BlockSpec((pl.Element(LANES), pl.Element(D_MODEL)),
                               lambda i: (dt_row0, 0), pipeline_mode=pl.Buffered(1))],
        out_specs=(pl.BlockSpec((tm, D_MODEL), row),
                   pl.BlockSpec((tm, LANES), row),
                   pl.BlockSpec((tm, LANES), row),
                   pl.BlockSpec((tm, LANES), row)),
        compiler_params=_cparams(("arbitrary",)),
        name="prep",
    )(x2, g_mix, pos_c, invf, sgn, wt)


PAIR_ROWS = 2 * GROUP * HEAD_DIM
INPROJ_ROW_TILE = 2048
INPROJ_COL_TILE = 1024


def _paired_head_blocks():
    return [(g * 2 + sub, sub * GROUP + g) for g in range(GROUP) for sub in range(2)]


def _cast_rows(dst_ref, src_ref, paired_heads):
    if not paired_heads:
        dst_ref[...] = src_ref[...].astype(BF16)
        return
    assert dst_ref.shape[0] % PAIR_ROWS == 0
    for base in range(0, dst_ref.shape[0], PAIR_ROWS):
        for dst, src in _paired_head_blocks():
            dst_ref[base + dst * HEAD_DIM:base + (dst + 1) * HEAD_DIM, :] = (
                src_ref[base + src * HEAD_DIM:base + (src + 1) * HEAD_DIM, :].astype(BF16))


def _inproj_kernel(a_ref, wt_ref, o_ref, wbf_ref, *, paired_tiles):
    j = pl.program_id(0)

    @pl.when((pl.program_id(1) == 0) & (j >= paired_tiles))
    def _():
        _cast_rows(wbf_ref, wt_ref, False)

    if paired_tiles:
        @pl.when((pl.program_id(1) == 0) & (j < paired_tiles))
        def _():
            _cast_rows(wbf_ref, wt_ref, True)

    o_ref[...] = _dot_nt(a_ref[...], wbf_ref[...]).astype(o_ref.dtype)


def _inproj(a, wt, segments, out_dtype, tm, tn, name, paired_tiles=0):
    m, k = a.shape
    assert all(seg_n % tn == 0 and row0 % SUBLANES == 0 for row0, seg_n in segments)
    assert not paired_tiles or tn % PAIR_ROWS == 0
    n = sum(seg_n for _, seg_n in segments)

    def weight_row(j):
        row, first_tile = None, 0
        for row0, seg_n in segments:
            here = row0 + (j - first_tile) * tn
            row = here if row is None else jnp.where(j >= first_tile, here, row)
            first_tile += seg_n // tn
        return pl.multiple_of(row, SUBLANES)

    return pl.pallas_call(
        partial(_inproj_kernel, paired_tiles=paired_tiles),
        out_shape=jax.ShapeDtypeStruct((m, n), out_dtype),
        grid=(n // tn, m // tm),
        in_specs=[pl.BlockSpec((tm, k), lambda j, i: (i, 0)),
                  pl.BlockSpec((pl.Element(tn), pl.Element(k)), lambda j, i: (weight_row(j), 0))],
        out_specs=pl.BlockSpec((tm, tn), lambda j, i: (i, j)),
        scratch_shapes=[pltpu.VMEM((tn, k), BF16)],
        compiler_params=_cparams(("arbitrary", "arbitrary")),
        name=name,
    )(a, wt)


ATTN_STEP_BLOCKS = 2


def _attn_kernel(sinks_ref, q_ref, kc_ref, kp_ref, vc_ref, vp_ref,
                 cc_ref, sc_ref, cp_ref, sp_ref, o_ref):
    n = pl.program_id(1)
    blk = ATTN_BLOCK
    nblk = ATTN_STEP_BLOCKS
    lane = lax.broadcasted_iota(jnp.int32, (blk, LANES), 1)
    first_half = (lane % HEAD_DIM) < (HEAD_DIM // 2)
    low_head = lane < HEAD_DIM

    def rope(t, c, s):
        partner = jnp.where(first_half,
                            pltpu.roll(t, LANES - HEAD_DIM // 2, 1),
                            pltpu.roll(t, HEAD_DIM // 2, 1))
        return t * c + partner * s

    def rows(j):
        return slice(j * blk, (j + 1) * blk)

    cp, sp = cp_ref[...], sp_ref[...]
    cc = [cc_ref[rows(j), :] for j in range(nblk)]
    sc = [sc_ref[rows(j), :] for j in range(nblk)]

    qi = lax.broadcasted_iota(jnp.int32, (blk, 2 * blk), 0)
    kj = lax.broadcasted_iota(jnp.int32, (blk, 2 * blk), 1)
    rel = kj - qi
    band = (rel >= 1) & (rel <= blk)
    valid = [band & ((kj >= blk) | (n > 0))] + [band] * (nblk - 1)

    scale = HEAD_DIM ** -0.5
    n_half = KV_DIM // LANES
    rows_per_half = 2 * GROUP * blk
    s_rows, sink_rows, valid_rows, v_wins = [], [], [], []
    for a in range(n_half):
        ksl = slice(a * LANES, (a + 1) * LANES)
        k_blocks = [rope(kp_ref[:, ksl].astype(F32), cp, sp).astype(BF16)]
        v_blocks = [vp_ref[:, ksl]]
        for j in range(nblk):
            k_blocks.append(rope(kc_ref[rows(j), ksl].astype(F32), cc[j], sc[j]).astype(BF16))
            v_blocks.append(vc_ref[rows(j), ksl])
        for j in range(nblk):
            k_win = jnp.concatenate(k_blocks[j:j + 2], axis=0)
            v_wins.append(jnp.concatenate(v_blocks[j:j + 2], axis=0))
            qc = [rope(q_ref[rows(j), (a * GROUP + g) * LANES:(a * GROUP + g + 1) * LANES]
                       .astype(F32), cc[j] * scale, sc[j] * scale) for g in range(GROUP)]
            q_rows = []
            for sub in range(2):
                keep = low_head if sub == 0 else jnp.logical_not(low_head)
                for g in range(GROUP):
                    q_rows.append(jnp.where(keep, qc[g], 0.0).astype(BF16))
                    sink_rows.append(
                        jnp.full((blk, LANES), sinks_ref[0, (2 * a + sub) * GROUP + g], F32))
            s_rows.append(_dot_nt(jnp.concatenate(q_rows, axis=0), k_win))
            valid_rows.extend([valid[j]] * (2 * GROUP))
    s = jnp.where(jnp.concatenate(valid_rows, axis=0), jnp.concatenate(s_rows, axis=0), NEG_BIG)
    sink = jnp.concatenate(sink_rows, axis=0)
    s0, s1 = s[:, :LANES], s[:, LANES:]
    m = jnp.maximum(jnp.max(jnp.maximum(s0, s1), axis=-1, keepdims=True), sink)
    e0 = jnp.exp(s0 - m)
    e1 = jnp.exp(s1 - m)
    den = jnp.sum(e0 + e1, axis=-1, keepdims=True) + jnp.exp(sink - m)
    r = 1.0 / den
    p = jnp.concatenate([e0 * r, e1 * r], axis=1).astype(BF16)
    for a in range(n_half):
        for j in range(nblk):
            w = a * nblk + j
            pv = _dot(p[w * rows_per_half:(w + 1) * rows_per_half], v_wins[w])
            for g in range(GROUP):
                lo = pv[g * blk:(g + 1) * blk]
                hi = pv[(GROUP + g) * blk:(GROUP + g + 1) * blk]
                chunk = a * GROUP + g
                o_ref[rows(j), chunk * LANES:(chunk + 1) * LANES] = (
                    jnp.where(low_head, lo, hi).astype(o_ref.dtype))


def _attention(qkv, cos_t, sin_t, sinks, batch, seq):
    t = qkv.shape[0]
    step = ATTN_STEP_BLOCKS * ATTN_BLOCK
    assert seq % step == 0
    ns = seq // step
    kcol = Q_DIM // KV_DIM
    vcol = kcol + 1
    cur = lambda b, n: b * ns + n
    prev = lambda b, n: jnp.maximum((b * ns + n) * ATTN_STEP_BLOCKS - 1, 0)
    return pl.pallas_call(
        _attn_kernel,
        out_shape=jax.ShapeDtypeStruct((t, Q_DIM), BF16),
        grid=(batch, ns),
        in_specs=[pl.BlockSpec(memory_space=pltpu.SMEM),
                  pl.BlockSpec((step, Q_DIM), lambda b, n: (cur(b, n), 0)),
                  pl.BlockSpec((step, KV_DIM), lambda b, n: (cur(b, n), kcol)),
                  pl.BlockSpec((ATTN_BLOCK, KV_DIM), lambda b, n: (prev(b, n), kcol)),
                  pl.BlockSpec((step, KV_DIM), lambda b, n: (cur(b, n), vcol)),
                  pl.BlockSpec((ATTN_BLOCK, KV_DIM), lambda b, n: (prev(b, n), vcol)),
                  pl.BlockSpec((step, LANES), lambda b, n: (cur(b, n), 0)),
                  pl.BlockSpec((step, LANES), lambda b, n: (cur(b, n), 0)),
                  pl.BlockSpec((ATTN_BLOCK, LANES), lambda b, n: (prev(b, n), 0)),
                  pl.BlockSpec((ATTN_BLOCK, LANES), lambda b, n: (prev(b, n), 0))],
        out_specs=pl.BlockSpec((step, Q_DIM), lambda b, n: (cur(b, n), 0)),
        compiler_params=_cparams(("arbitrary", "arbitrary")),
        name="attention",
    )(sinks, qkv, qkv, qkv, qkv, qkv, cos_t, sin_t, cos_t, sin_t)


SSD_STRIP = 256


def _ssd_kernel(z_ref, xin_ref, bcin_ref, xin_prev_ref, bcin_prev_ref, dt_ref, shift_ref, cw_ref,
                cb_ref, dtb_ref, alog_ref, dskip_ref, g_ref, expand_ref, o_ref,
                state_ref, xs_ref, b16_ref, c16_ref, xdd_ref, xd16_ref, yz_ref):
    c = pl.program_id(1)
    L = CHUNK
    W = SSD_STRIP
    bc_dim = SSM_GROUPS * D_STATE

    @pl.when(c == 0)
    def _():
        state_ref[...] = jnp.zeros_like(state_ref)

    n_shift = CONV_WIDTH - 1
    sel_col = lax.broadcasted_iota(jnp.int32, (n_shift * L, 2 * L), 1)
    sel = shift_ref[...]
    sel = jnp.where((sel_col < L) & (c == 0), jnp.zeros_like(sel), sel)
    for s in range(CONV_DIM // W):
        sl = slice(s * W, (s + 1) * W)
        if s * W < D_INNER:
            cur_ref, prev_ref, src = xin_ref, xin_prev_ref, sl
        else:
            cur_ref, prev_ref = bcin_ref, bcin_prev_ref
            src = slice(s * W - D_INNER, (s + 1) * W - D_INNER)
        x_cur = cur_ref[:, src]
        xwin = jnp.concatenate([prev_ref[:, src], x_cur], axis=0)
        taps = _dot(sel, xwin)
        acc = cb_ref[:, sl] + x_cur.astype(F32) * cw_ref[n_shift:CONV_WIDTH, sl]
        for w in range(n_shift):
            acc = acc + taps[w * L:(w + 1) * L] * cw_ref[w:w + 1, sl]
        xc = _silu(acc)
        if s * W < D_INNER:
            xs_ref[:, sl] = xc
        elif s * W < D_INNER + bc_dim:
            b16_ref[:, s * W - D_INNER:(s + 1) * W - D_INNER] = xc.astype(BF16)
        else:
            c16_ref[:, s * W - D_INNER - bc_dim:(s + 1) * W - D_INNER - bc_dim] = xc.astype(BF16)

    v = dt_ref[...] + dtb_ref[...]
    dt = jnp.maximum(v, 0.0) + jnp.log1p(jnp.exp(-jnp.abs(v)))
    a = dt * (jnp.exp(alog_ref[...]) * NEG_LOG2E)

    ri = lax.broadcasted_iota(jnp.int32, (L, L), 0)
    ci = lax.broadcasted_iota(jnp.int32, (L, L), 1)
    causal = ri >= ci
    tril = jnp.where(causal, 1.0, 0.0).astype(BF16)
    R = 2 * SUBLANES
    ones = jnp.ones((L, L), BF16)
    a1, a2, a3 = _split3(a)
    cs = _dot(tril, a1) + _dot(tril, a2) + _dot(tril, a3)
    tot = (_dot(ones, a1) + _dot(ones, a2) + _dot(ones, a3))[0:R]
    cs_t = cs.T[0:SSM_HEADS]

    lane = lax.broadcasted_iota(jnp.int32, (L, LANES), 1)

    def pack(q):
        ln = lax.broadcasted_iota(jnp.int32, q.shape, 1)
        q1, q2, q3 = _split3(q)
        packed = jnp.where(
            ln < SSM_HEADS, q1.astype(F32),
            jnp.where(ln < 2 * SSM_HEADS, pltpu.roll(q2.astype(F32), SSM_HEADS, 1),
                      jnp.where(ln < 3 * SSM_HEADS,
                                pltpu.roll(q3.astype(F32), 2 * SSM_HEADS, 1), 0.0)))
        return packed.astype(BF16)

    def tile_rows(v):
        return jnp.concatenate([v] * (L // SUBLANES), axis=0)

    pk_all = jnp.concatenate([pack(dt), pack(cs), pack(tot)], axis=0)
    pk_ct = pk_all[L:]

    for s in range(D_INNER // W):
        sl = slice(s * W, (s + 1) * W)
        ex = _dot(pk_all, expand_ref[:, sl])
        xd = xs_ref[:, sl] * ex[0:L]
        xd16_ref[:, sl] = xd.astype(BF16)
        tot_x = tile_rows(ex[2 * L:2 * L + SUBLANES])
        xdd_ref[:, sl] = (xd * jnp.exp2(tot_x - ex[L:2 * L])).astype(BF16)

    low_head = lane < SSM_HEAD_DIM
    heads_per_strip = W // SSM_HEAD_DIM
    strips_per_group = HEADS_PER_GROUP // heads_per_strip
    ssq = jnp.zeros((L, LANES), F32)

    for g in range(SSM_GROUPS):
        nsl = slice(g * D_STATE, (g + 1) * D_STATE)
        bg = b16_ref[:, nsl]
        cg = c16_ref[:, nsl]
        bg_t = bg.astype(F32).T.astype(BF16)
        cb = jnp.where(causal, _dot_nt(cg, bg), 0.0)
        for k in range(strips_per_group):
            s = g * strips_per_group + k
            sl = slice(s * W, (s + 1) * W)
            ex = _dot(pk_ct, expand_ref[:, sl])
            st_prev = state_ref[:, sl]
            y_off = _dot(cg, st_prev.astype(BF16))
            chunk_decay = tile_rows(jnp.exp2(ex[L:L + SUBLANES]))
            state_ref[:, sl] = st_prev * chunk_decay + _dot(bg_t, xdd_ref[:, sl])
            y_pairs = []
            for pair in range(W // LANES):
                psl = slice(s * W + pair * LANES, s * W + (pair + 1) * LANES)
                xp = xd16_ref[:, psl]
                mhs, xps = [], []
                for sub in range(2):
                    h = s * heads_per_strip + pair * 2 + sub
                    diff = cs[:, h:h + 1] - cs_t[h:h + 1, :]
                    decay = jnp.exp2(jnp.minimum(diff, 0.0))
                    mhs.append((cb * decay).astype(BF16))
                    keep = low_head if sub == 0 else jnp.logical_not(low_head)
                    xps.append(jnp.where(keep, xp, jnp.zeros_like(xp)))
                y_pairs.append(_dot(jnp.concatenate(mhs, axis=1), jnp.concatenate(xps, axis=0)))
            y = (jnp.concatenate(y_pairs, axis=1) + y_off * jnp.exp2(ex[0:L])
                 + dskip_ref[:, sl] * xs_ref[:, sl])
            yz = y * _silu(z_ref[:, sl].astype(F32))
            yz_ref[:, sl] = yz
            for pair in range(W // LANES):
                part = yz[:, pair * LANES:(pair + 1) * LANES]
                ssq = ssq + part * part

    var = jnp.sum(ssq, axis=-1, keepdims=True) * (1.0 / D_INNER)
    rs = lax.rsqrt(var + SSM_NORM_EPS)
    for s in range(D_INNER // W):
        sl = slice(s * W, (s + 1) * W)
        o_ref[:, sl] = (yz_ref[:, sl] * rs * g_ref[:, sl]).astype(o_ref.dtype)


def _ssd(zx, dtp, shift, conv_w, conv_b, dtb_p, alog_p, dskip_x, g_ssd, expand, batch, seq):
    t = zx.shape[0]
    nc = seq // CHUNK
    bc_dim = 2 * SSM_GROUPS * D_STATE
    row = lambda b, c: (b * nc + c, 0)
    cur = lambda b, c: b * nc + c
    prev = lambda b, c: jnp.maximum(b * nc + c - 1, 0)
    x_col = 1
    bc_col = 2 * D_INNER // bc_dim
    return pl.pallas_call(
        _ssd_kernel,
        out_shape=jax.ShapeDtypeStruct((t, D_INNER), BF16),
        grid=(batch, nc),
        in_specs=[pl.BlockSpec((CHUNK, D_INNER), row),
                  pl.BlockSpec((CHUNK, D_INNER), lambda b, c: (cur(b, c), x_col)),
                  pl.BlockSpec((CHUNK, bc_dim), lambda b, c: (cur(b, c), bc_col)),
                  pl.BlockSpec((CHUNK, D_INNER), lambda b, c: (prev(b, c), x_col)),
                  pl.BlockSpec((CHUNK, bc_dim), lambda b, c: (prev(b, c), bc_col)),
                  pl.BlockSpec((CHUNK, LANES), row),
                  _resident(((CONV_WIDTH - 1) * CHUNK, 2 * CHUNK)),
                  _resident((CONV_WIDTH, CONV_DIM)),
                  _resident((1, CONV_DIM)),
                  _resident((1, LANES)),
                  _resident((1, LANES)),
                  _resident((1, D_INNER)),
                  _resident((1, D_INNER)),
                  _resident((LANES, D_INNER))],
        out_specs=pl.BlockSpec((CHUNK, D_INNER), row),
        scratch_shapes=[pltpu.VMEM((D_STATE, D_INNER), F32),
                        pltpu.VMEM((CHUNK, D_INNER), F32),
                        pltpu.VMEM((CHUNK, SSM_GROUPS * D_STATE), BF16),
                        pltpu.VMEM((CHUNK, SSM_GROUPS * D_STATE), BF16),
                        pltpu.VMEM((CHUNK, D_INNER), BF16),
                        pltpu.VMEM((CHUNK, D_INNER), BF16),
                        pltpu.VMEM((CHUNK, D_INNER), F32)],
        compiler_params=_cparams(("arbitrary", "arbitrary")),
        name="ssd",
    )(zx, zx, zx, zx, zx, dtp, shift, conv_w, conv_b, dtb_p, alog_p, dskip_x, g_ssd, expand)


def _merge_kernel(attn_ref, y_ref, ga_ref, gs_ref, wa_ref, ws_ref, o_ref, wa16_ref, ws16_ref):
    @pl.when(pl.program_id(1) == 0)
    def _():
        _cast_rows(wa16_ref, wa_ref, True)
        ws16_ref[...] = ws_ref[...].astype(BF16)

    out_a = _dot(attn_ref[...], wa16_ref[...])
    out_s = _dot(y_ref[...], ws16_ref[...])
    merged = (_sigmoid(ga_ref[...].astype(F32)) * out_a
              + _sigmoid(gs_ref[...].astype(F32)) * out_s)
    o_ref[...] = merged.astype(o_ref.dtype)


def _merge(attn, y, gates, gate_col0, wa, ws, tm=512, tn=1024):
    t = attn.shape[0]
    nj = D_MODEL // tn
    assert gate_col0 % tn == 0
    g0 = gate_col0 // tn
    return pl.pallas_call(
        _merge_kernel,
        out_shape=jax.ShapeDtypeStruct((t, D_MODEL), BF16),
        grid=(nj, t // tm),
        in_specs=[pl.BlockSpec((tm, Q_DIM), lambda j, i: (i, 0)),
                  pl.BlockSpec((tm, D_INNER), lambda j, i: (i, 0)),
                  pl.BlockSpec((tm, tn), lambda j, i: (i, g0 + j)),
                  pl.BlockSpec((tm, tn), lambda j, i: (i, g0 + nj + j)),
                  pl.BlockSpec((Q_DIM, tn), lambda j, i: (0, j)),
                  pl.BlockSpec((D_INNER, tn), lambda j, i: (0, j))],
        out_specs=pl.BlockSpec((tm, tn), lambda j, i: (i, j)),
        scratch_shapes=[pltpu.VMEM((Q_DIM, tn), BF16),
                        pltpu.VMEM((D_INNER, tn), BF16)],
        compiler_params=_cparams(("arbitrary", "arbitrary")),
        name="merge",
    )(attn, y, gates, gates, wa, ws)


def _oproj_kernel(m_ref, x_ref, w_ref, g_ref, h_ref, f_ref, w16_ref):
    @pl.when(pl.program_id(0) == 0)
    def _():
        w16_ref[...] = w_ref[...].astype(BF16)

    h = x_ref[...] + _dot(m_ref[...], w16_ref[...])
    h_ref[...] = h
    f_ref[...] = _rms(h, g_ref[...], NORM_EPS).astype(BF16)


def _oproj(merged, x2, wo, g_ffn, tm=512):
    t = merged.shape[0]
    row = lambda i: (i, 0)
    return pl.pallas_call(
        _oproj_kernel,
        out_shape=(jax.ShapeDtypeStruct((t, D_MODEL), F32),
                   jax.ShapeDtypeStruct((t, D_MODEL), BF16)),
        grid=(t // tm,),
        in_specs=[pl.BlockSpec((tm, D_MODEL), row),
                  pl.BlockSpec((tm, D_MODEL), row),
                  _resident((D_MODEL, D_MODEL)),
                  _resident((1, D_MODEL))],
        out_specs=(pl.BlockSpec((tm, D_MODEL), row),
                   pl.BlockSpec((tm, D_MODEL), row)),
        scratch_shapes=[pltpu.VMEM((D_MODEL, D_MODEL), BF16)],
        compiler_params=_cparams(("arbitrary",)),
        name="oproj",
    )(merged, x2, wo, g_ffn)


def _ffn_up_kernel(f_ref, wg_ref, wu_ref, o_ref, wg16_ref, wu16_ref):
    @pl.when(pl.program_id(1) == 0)
    def _():
        wg16_ref[...] = wg_ref[...].astype(BF16)
        wu16_ref[...] = wu_ref[...].astype(BF16)

    half_r = o_ref.shape[0] // 2
    half_c = o_ref.shape[1] // 2
    for r in range(2):
        rs = slice(r * half_r, (r + 1) * half_r)
        f = f_ref[rs, :]
        for c in range(2):
            cs = slice(c * half_c, (c + 1) * half_c)
            o_ref[rs, cs] = (_silu(_dot(f, wg16_ref[:, cs]))
                             * _dot(f, wu16_ref[:, cs])).astype(o_ref.dtype)


def _ffn_up(f, wg, wu, tm=2048, th=512):
    t = f.shape[0]
    return pl.pallas_call(
        _ffn_up_kernel,
        out_shape=jax.ShapeDtypeStruct((t, FFN_HIDDEN), BF16),
        grid=(FFN_HIDDEN // th, t // tm),
        in_specs=[pl.BlockSpec((tm, D_MODEL), lambda j, i: (i, 0)),
                  pl.BlockSpec((D_MODEL, th), lambda j, i: (0, j)),
                  pl.BlockSpec((D_MODEL, th), lambda j, i: (0, j))],
        out_specs=pl.BlockSpec((tm, th), lambda j, i: (i, j)),
        scratch_shapes=[pltpu.VMEM((D_MODEL, th), BF16),
                        pltpu.VMEM((D_MODEL, th), BF16)],
        compiler_params=_cparams(("arbitrary", "arbitrary")),
        name="ffn_up",
    )(f, wg, wu)


def _ffn_down_kernel(a_ref, h_ref, wd_hbm, o_ref, wd16_ref, wf32_ref, sem, *, n_col_tiles):
    j = pl.program_id(0)
    i = pl.program_id(1)
    tn = wf32_ref.shape[1]

    def copy(col_tile):
        return pltpu.make_async_copy(wd_hbm.at[:, pl.ds(col_tile * tn, tn)], wf32_ref, sem)

    @pl.when((j == 0) & (i == 0))
    def _():
        copy(0).start()

    @pl.when(i == 0)
    def _():
        copy(j).wait()
        wd16_ref[...] = wf32_ref[...].astype(BF16)

    @pl.when((i == 0) & (j + 1 < n_col_tiles))
    def _():
        copy(j + 1).start()

    o_ref[...] = h_ref[...] + _dot(a_ref[...], wd16_ref[...])


def _ffn_down(act, h1, wd, tm=512, tn=1024):
    t = act.shape[0]
    n_col_tiles = D_MODEL // tn
    return pl.pallas_call(
        partial(_ffn_down_kernel, n_col_tiles=n_col_tiles),
        out_shape=jax.ShapeDtypeStruct((t, D_MODEL), F32),
        grid=(n_col_tiles, t // tm),
        in_specs=[pl.BlockSpec((tm, FFN_HIDDEN), lambda j, i: (i, 0)),
                  pl.BlockSpec((tm, tn), lambda j, i: (i, j)),
                  pl.BlockSpec(memory_space=pl.ANY)],
        out_specs=pl.BlockSpec((tm, tn), lambda j, i: (i, j)),
        scratch_shapes=[pltpu.VMEM((FFN_HIDDEN, tn), BF16),
                        pltpu.VMEM((FFN_HIDDEN, tn), F32),
                        pltpu.SemaphoreType.DMA],
        compiler_params=_cparams(("arbitrary", "arbitrary")),
        name="ffn_down",
    )(act, h1, wd)


def _ple_kernel(h_ref, p_ref, gp_ref, gf_ref, wg_ref, wp_ref, o_ref, wg16_ref, wp16_ref):
    @pl.when(pl.program_id(0) == 0)
    def _():
        wg16_ref[...] = wg_ref[...].astype(BF16)
        wp16_ref[...] = wp_ref[...].astype(BF16)

    h = h_ref[...]
    r = _rms(h, gp_ref[...], NORM_EPS).astype(BF16)
    gate = _sigmoid(_dot(r, wg16_ref[...]))
    h3 = h + gate * _dot(p_ref[...].astype(BF16), wp16_ref[...])
    o_ref[...] = _rms(h3, gf_ref[...], NORM_EPS)


def _ple(h2, p2, g_ple, g_final, wpg, wpp, tm=512):
    t = h2.shape[0]
    row = lambda i: (i, 0)
    return pl.pallas_call(
        _ple_kernel,
        out_shape=jax.ShapeDtypeStruct((t, D_MODEL), F32),
        grid=(t // tm,),
        in_specs=[pl.BlockSpec((tm, D_MODEL), row),
                  pl.BlockSpec((tm, PLE_DIM), row),
                  _resident((1, D_MODEL)),
                  _resident((1, D_MODEL)),
                  _resident((D_MODEL, D_MODEL)),
                  _resident((PLE_DIM, D_MODEL))],
        out_specs=pl.BlockSpec((tm, D_MODEL), row),
        scratch_shapes=[pltpu.VMEM((D_MODEL, D_MODEL), BF16),
                        pltpu.VMEM((PLE_DIM, D_MODEL), BF16)],
        compiler_params=_cparams(("arbitrary",)),
        name="ple",
    )(h2, p2, g_ple, g_final, wpg, wpp)


def _rope_constants():
    half = HEAD_DIM // 2
    lane = np.arange(LANES)
    inv_freq = ROPE_THETA ** (-jnp.arange(half, dtype=F32) * 2.0 / HEAD_DIM)
    invf = inv_freq[lane % half][None, :]
    sgn = jnp.asarray(np.where((lane % HEAD_DIM) < half, -1.0, 1.0), F32)[None, :]
    return invf, sgn


def _expand_matrix():
    rows = np.arange(LANES)[:, None]
    cols = np.arange(D_INNER)[None, :]
    hit = (rows < 3 * SSM_HEADS) & ((rows % SSM_HEADS) == (cols // SSM_HEAD_DIM))
    return jnp.asarray(hit, BF16)


def _shift_matrix():
    rows = np.arange((CONV_WIDTH - 1) * CHUNK)[:, None]
    cols = np.arange(2 * CHUNK)[None, :]
    w, t = rows // CHUNK, rows % CHUNK
    return jnp.asarray(cols == CHUNK + t - (CONV_WIDTH - 1) + w, BF16)


def _pad_lanes(v):
    return jnp.pad(v.astype(F32), (0, LANES - v.shape[0]))[None, :]


def kernel(x, p, positions, g_mix, w_in, conv_w, conv_b, dt_bias, a_log, d_skip, g_ssd,
           sinks, w_attn_br, w_ssd_br, w_o, g_ffn, w_gate, w_up, w_down, g_ple,
           w_ple_gate, w_ple_proj, g_final):
    b, s, d = x.shape
    t = b * s
    assert d == D_MODEL and s % CHUNK == 0 and p.shape[0] == 1
    i = 0
    x2 = x.reshape(t, d)
    p2 = p[i].reshape(t, PLE_DIM)
    pos_c = jnp.repeat(positions.reshape(t // POS_PER_ROW, POS_PER_ROW), ROPE_FREQS, axis=1)

    o_k = Q_DIM
    o_z = Q_DIM + 2 * KV_DIM
    o_xbc = o_z + D_INNER
    o_dt = o_xbc + CONV_DIM
    o_ga = o_dt + SSM_HEADS
    wt = jnp.swapaxes(w_in[i], 0, 1)

    invf, sgn = _rope_constants()
    u, cos_t, sin_t, dtp = _prep(x2, g_mix[i][None, :], pos_c, invf, sgn, wt, o_dt)

    qkv = _inproj(u, wt, [(0, o_z)], BF16, INPROJ_ROW_TILE, PAIR_ROWS, "proj_qkv",
                  paired_tiles=Q_DIM // PAIR_ROWS)
    zxg = _inproj(u, wt, [(o_z, D_INNER + CONV_DIM), (o_ga, 2 * D_MODEL)], BF16,
                  INPROJ_ROW_TILE, INPROJ_COL_TILE, "proj_zxg")
    gate_col0 = D_INNER + CONV_DIM

    attn = _attention(qkv, cos_t, sin_t, sinks[i][None, :].astype(F32), b, s)

    y = _ssd(zxg, dtp, _shift_matrix(), conv_w[i], conv_b[i][None, :], _pad_lanes(dt_bias[i]),
             _pad_lanes(a_log[i]), jnp.repeat(d_skip[i].astype(F32), SSM_HEAD_DIM)[None, :],
             g_ssd[i][None, :], _expand_matrix(), b, s)

    merged = _merge(attn, y, zxg, gate_col0, w_attn_br[i], w_ssd_br[i])
    h1, f = _oproj(merged, x2, w_o[i], g_ffn[i][None, :])
    act = _ffn_up(f, w_gate[i], w_up[i])
    h2 = _ffn_down(act, h1, w_down[i])
    out = _ple(h2, p2, g_ple[i][None, :], g_final[None, :], w_ple_gate[i], w_ple_proj[i])
    return out.reshape(b, s, d)
```

```python
from functools import partial

import numpy as np
import jax
import jax.numpy as jnp
from jax import lax
from jax.experimental import pallas as pl
from jax.experimental.pallas import tpu as pltpu

F32 = jnp.float32
BF16 = jnp.bfloat16

D_MODEL = 2048
HEAD_DIM = 64
ATTN_HEADS = 16
KV_HEADS = 4
GROUP = ATTN_HEADS // KV_HEADS
Q_DIM = ATTN_HEADS * HEAD_DIM
KV_DIM = KV_HEADS * HEAD_DIM
ATTN_BLOCK = 128
ROPE_THETA = 10000.0
D_INNER = 2048
SSM_HEAD_DIM = 64
SSM_HEADS = 32
SSM_GROUPS = 4
HEADS_PER_GROUP = SSM_HEADS // SSM_GROUPS
D_STATE = 128
CONV_WIDTH = 4
CHUNK = 128
CONV_DIM = D_INNER + 2 * SSM_GROUPS * D_STATE
FFN_HIDDEN = 5632
PLE_DIM = 256
NORM_EPS = 1e-6
SSM_NORM_EPS = 1e-5

LANES = 128
SUBLANES = 8
VMEM_LIMIT_BYTES = 56 * 1024 * 1024

NEG_BIG = -1e30


def _cparams(semantics):
    return pltpu.CompilerParams(dimension_semantics=semantics,
                                vmem_limit_bytes=VMEM_LIMIT_BYTES)


def _resident(shape):
    return pl.BlockSpec(shape, lambda *_: (0,) * len(shape),
                        pipeline_mode=pl.Buffered(1))


def _rms(xf, g, eps):
    var = jnp.mean(xf * xf, axis=-1, keepdims=True)
    return xf * lax.rsqrt(var + eps) * g


NEG_LOG2E = -1.4426950408889634


def _sigmoid(x):
    return 1.0 / (1.0 + jnp.exp2(x * NEG_LOG2E))


def _silu(x):
    return x * _sigmoid(x)


def _dot(a, b):
    return jnp.dot(a, b, preferred_element_type=F32)


def _dot_nt(a, b):
    return lax.dot_general(a, b, (((1,), (1,)), ((), ())),
                           preferred_element_type=F32)


def _split3(v):
    v1 = v.astype(BF16)
    r1 = v - v1.astype(F32)
    v2 = r1.astype(BF16)
    v3 = (r1 - v2.astype(F32)).astype(BF16)
    return v1, v2, v3


ROPE_FREQS = HEAD_DIM // 2
POS_PER_ROW = LANES // ROPE_FREQS


def _prep_kernel(x_ref, g_ref, pos_ref, invf_ref, sgn_ref, wdt_ref, u_ref, cos_ref, sin_ref, dt_ref):
    u = _rms(x_ref[...], g_ref[...], NORM_EPS).astype(BF16)
    u_ref[...] = u
    rows = pos_ref.shape[0]
    ang = pos_ref[...].astype(F32) * invf_ref[...]
    cos_c = jnp.cos(ang)
    sin_c = jnp.sin(ang)
    group = lax.broadcasted_iota(jnp.int32, ang.shape, 1) // ROPE_FREQS
    for q in range(POS_PER_ROW):
        for src, dst_ref, sign in ((cos_c, cos_ref, None), (sin_c, sin_ref, sgn_ref[...])):
            only = jnp.where(group == q, src, 0.0)
            rep = only
            for k in range(1, POS_PER_ROW):
                rep = rep + pltpu.roll(only, k * ROPE_FREQS, 1)
            if sign is not None:
                rep = rep * sign
            dst_ref[pl.ds(q, rows, stride=POS_PER_ROW), :] = rep
    dt_ref[...] = _dot_nt(u, wdt_ref[...].astype(BF16))


def _prep(x2, g_mix, pos_c, invf, sgn, wt, dt_row0, tm=1024):
    t = x2.shape[0]
    row = lambda i: (i, 0)
    assert dt_row0 % SUBLANES == 0 and tm % (POS_PER_ROW * SUBLANES) == 0
    return pl.pallas_call(
        _prep_kernel,
        out_shape=(jax.ShapeDtypeStruct((t, D_MODEL), BF16),
                   jax.ShapeDtypeStruct((t, LANES), F32),
                   jax.ShapeDtypeStruct((t, LANES), F32),
                   jax.ShapeDtypeStruct((t, LANES), F32)),
        grid=(t // tm,),
        in_specs=[pl.BlockSpec((tm, D_MODEL), row),
                  _resident((1, D_MODEL)),
                  pl.BlockSpec((tm // POS_PER_ROW, LANES), row),
                  _resident((1, LANES)),
                  _resident((1, LANES)),
                  pl.BlockSpec((pl.Element(LANES), pl.Element(D_MODEL)),
                               lambda i: (dt_row0, 0), pipeline_mode=pl.Buffered(1))],
        out_specs=(pl.BlockSpec((tm, D_MODEL), row),
                   pl.BlockSpec((tm, LANES), row),
                   pl.BlockSpec((tm, LANES), row),
                   pl.BlockSpec((tm, LANES), row)),
        compiler_params=_cparams(("arbitrary",)),
        name="prep",
    )(x2, g_mix, pos_c, invf, sgn, wt)


PAIR_ROWS = 2 * GROUP * HEAD_DIM
INPROJ_ROW_TILE = 2048
INPROJ_COL_TILE = 1024
QKV_ROW_TILE = 1024


def _paired_head_blocks():
    return [(g * 2 + sub, sub * GROUP + g) for g in range(GROUP) for sub in range(2)]


def _cast_rows(dst_ref, src_ref, paired_rows):
    rows = dst_ref.shape[0]
    assert paired_rows % PAIR_ROWS == 0 and paired_rows <= rows
    for base in range(0, paired_rows, PAIR_ROWS):
        for dst, src in _paired_head_blocks():
            dst_ref[base + dst * HEAD_DIM:base + (dst + 1) * HEAD_DIM, :] = (
                src_ref[base + src * HEAD_DIM:base + (src + 1) * HEAD_DIM, :].astype(BF16))
    if paired_rows < rows:
        dst_ref[paired_rows:rows, :] = src_ref[paired_rows:rows, :].astype(BF16)


def _inproj_kernel(a_ref, wt_ref, o_ref, wbf_ref, *, paired_rows):
    @pl.when(pl.program_id(1) == 0)
    def _():
        _cast_rows(wbf_ref, wt_ref, paired_rows)

    o_ref[...] = _dot_nt(a_ref[...], wbf_ref[...]).astype(o_ref.dtype)


def _inproj(a, wt, segments, out_dtype, tm, tn, name, paired_rows=0):
    m, k = a.shape
    assert all(seg_n % tn == 0 and row0 % SUBLANES == 0 for row0, seg_n in segments)
    n = sum(seg_n for _, seg_n in segments)
    assert not paired_rows or n == tn

    def weight_row(j):
        row, first_tile = None, 0
        for row0, seg_n in segments:
            here = row0 + (j - first_tile) * tn
            row = here if row is None else jnp.where(j >= first_tile, here, row)
            first_tile += seg_n // tn
        return pl.multiple_of(row, SUBLANES)

    return pl.pallas_call(
        partial(_inproj_kernel, paired_rows=paired_rows),
        out_shape=jax.ShapeDtypeStruct((m, n), out_dtype),
        grid=(n // tn, m // tm),
        in_specs=[pl.BlockSpec((tm, k), lambda j, i: (i, 0)),
                  pl.BlockSpec((pl.Element(tn), pl.Element(k)), lambda j, i: (weight_row(j), 0))],
        out_specs=pl.BlockSpec((tm, tn), lambda j, i: (i, j)),
        scratch_shapes=[pltpu.VMEM((tn, k), BF16)],
        compiler_params=_cparams(("arbitrary", "arbitrary")),
        name=name,
    )(a, wt)


ATTN_STEP_BLOCKS = 2


def _attn_kernel(sinks_ref, q_ref, kc_ref, kp_ref, vc_ref, vp_ref,
                 cc_ref, sc_ref, cp_ref, sp_ref, o_ref):
    n = pl.program_id(1)
    blk = ATTN_BLOCK
    nblk = ATTN_STEP_BLOCKS
    lane = lax.broadcasted_iota(jnp.int32, (blk, LANES), 1)
    first_half = (lane % HEAD_DIM) < (HEAD_DIM // 2)
    low_head = lane < HEAD_DIM

    def rope(t, c, s):
        partner = jnp.where(first_half,
                            pltpu.roll(t, LANES - HEAD_DIM // 2, 1),
                            pltpu.roll(t, HEAD_DIM // 2, 1))
        return t * c + partner * s

    def rows(j):
        return slice(j * blk, (j + 1) * blk)

    cp, sp = cp_ref[...], sp_ref[...]
    cc = [cc_ref[rows(j), :] for j in range(nblk)]
    sc = [sc_ref[rows(j), :] for j in range(nblk)]

    qi = lax.broadcasted_iota(jnp.int32, (blk, 2 * blk), 0)
    kj = lax.broadcasted_iota(jnp.int32, (blk, 2 * blk), 1)
    rel = kj - qi
    band = (rel >= 1) & (rel <= blk)
    valid = [band & ((kj >= blk) | (n > 0))] + [band] * (nblk - 1)

    scale = HEAD_DIM ** -0.5
    n_half = KV_DIM // LANES
    rows_per_half = 2 * GROUP * blk
    s_rows, sink_rows, valid_rows, v_wins = [], [], [], []
    for a in range(n_half):
        ksl = slice(a * LANES, (a + 1) * LANES)
        k_blocks = [rope(kp_ref[:, ksl].astype(F32), cp, sp).astype(BF16)]
        v_blocks = [vp_ref[:, ksl]]
        for j in range(nblk):
            k_blocks.append(rope(kc_ref[rows(j), ksl].astype(F32), cc[j], sc[j]).astype(BF16))
            v_blocks.append(vc_ref[rows(j), ksl])
        for j in range(nblk):
            k_win = jnp.concatenate(k_blocks[j:j + 2], axis=0)
            v_wins.append(jnp.concatenate(v_blocks[j:j + 2], axis=0))
            qc = [rope(q_ref[rows(j), (a * GROUP + g) * LANES:(a * GROUP + g + 1) * LANES]
                       .astype(F32), cc[j] * scale, sc[j] * scale) for g in range(GROUP)]
            q_rows = []
            for sub in range(2):
                keep = low_head if sub == 0 else jnp.logical_not(low_head)
                for g in range(GROUP):
                    q_rows.append(jnp.where(keep, qc[g], 0.0).astype(BF16))
                    sink_rows.append(
                        jnp.full((blk, LANES), sinks_ref[0, (2 * a + sub) * GROUP + g], F32))
            s_rows.append(_dot_nt(jnp.concatenate(q_rows, axis=0), k_win))
            valid_rows.extend([valid[j]] * (2 * GROUP))
    s = jnp.where(jnp.concatenate(valid_rows, axis=0), jnp.concatenate(s_rows, axis=0), NEG_BIG)
    sink = jnp.concatenate(sink_rows, axis=0)
    s0, s1 = s[:, :LANES], s[:, LANES:]
    m = jnp.maximum(jnp.max(jnp.maximum(s0, s1), axis=-1, keepdims=True), sink)
    e0 = jnp.exp(s0 - m)
    e1 = jnp.exp(s1 - m)
    den = jnp.sum(e0 + e1, axis=-1, keepdims=True) + jnp.exp(sink - m)
    r = 1.0 / den
    p = jnp.concatenate([e0 * r, e1 * r], axis=1).astype(BF16)
    for a in range(n_half):
        for j in range(nblk):
            w = a * nblk + j
            pv = _dot(p[w * rows_per_half:(w + 1) * rows_per_half], v_wins[w])
            for g in range(GROUP):
                lo = pv[g * blk:(g + 1) * blk]
                hi = pv[(GROUP + g) * blk:(GROUP + g + 1) * blk]
                chunk = a * GROUP + g
                o_ref[rows(j), chunk * LANES:(chunk + 1) * LANES] = (
                    jnp.where(low_head, lo, hi).astype(o_ref.dtype))


def _attention(qkv, cos_t, sin_t, sinks, batch, seq):
    t = qkv.shape[0]
    step = ATTN_STEP_BLOCKS * ATTN_BLOCK
    assert seq % step == 0
    ns = seq // step
    kcol = Q_DIM // KV_DIM
    vcol = kcol + 1
    cur = lambda b, n: b * ns + n
    prev = lambda b, n: jnp.maximum((b * ns + n) * ATTN_STEP_BLOCKS - 1, 0)
    return pl.pallas_call(
        _attn_kernel,
        out_shape=jax.ShapeDtypeStruct((t, Q_DIM), BF16),
        grid=(batch, ns),
        in_specs=[pl.BlockSpec(memory_space=pltpu.SMEM),
                  pl.BlockSpec((step, Q_DIM), lambda b, n: (cur(b, n), 0)),
                  pl.BlockSpec((step, KV_DIM), lambda b, n: (cur(b, n), kcol)),
                  pl.BlockSpec((ATTN_BLOCK, KV_DIM), lambda b, n: (prev(b, n), kcol)),
                  pl.BlockSpec((step, KV_DIM), lambda b, n: (cur(b, n), vcol)),
                  pl.BlockSpec((ATTN_BLOCK, KV_DIM), lambda b, n: (prev(b, n), vcol)),
                  pl.BlockSpec((step, LANES), lambda b, n: (cur(b, n), 0)),
                  pl.BlockSpec((step, LANES), lambda b, n: (cur(b, n), 0)),
                  pl.BlockSpec((ATTN_BLOCK, LANES), lambda b, n: (prev(b, n), 0)),
                  pl.BlockSpec((ATTN_BLOCK, LANES), lambda b, n: (prev(b, n), 0))],
        out_specs=pl.BlockSpec((step, Q_DIM), lambda b, n: (cur(b, n), 0)),
        compiler_params=_cparams(("arbitrary", "arbitrary")),
        name="attention",
    )(sinks, qkv, qkv, qkv, qkv, qkv, cos_t, sin_t, cos_t, sin_t)


SSD_STRIP = 256


def _ssd_kernel(z_ref, xin_ref, bcin_ref, xin_prev_ref, bcin_prev_ref, dt_ref, shift_ref, cw_ref,
                cb_ref, dtb_ref, alog_ref, dskip_ref, g_ref, expand_ref, o_ref,
                state_ref, xs_ref, b16_ref, c16_ref, xdd_ref, xd16_ref, yz_ref):
    c = pl.program_id(1)
    L = CHUNK
    W = SSD_STRIP
    bc_dim = SSM_GROUPS * D_STATE

    @pl.when(c == 0)
    def _():
        state_ref[...] = jnp.zeros_like(state_ref)

    n_shift = CONV_WIDTH - 1
    sel_col = lax.broadcasted_iota(jnp.int32, (n_shift * L, 2 * L), 1)
    sel = shift_ref[...]
    sel = jnp.where((sel_col < L) & (c == 0), jnp.zeros_like(sel), sel)
    for s in range(CONV_DIM // W):
        sl = slice(s * W, (s + 1) * W)
        if s * W < D_INNER:
            cur_ref, prev_ref, src = xin_ref, xin_prev_ref, sl
        else:
            cur_ref, prev_ref = bcin_ref, bcin_prev_ref
            src = slice(s * W - D_INNER, (s + 1) * W - D_INNER)
        x_cur = cur_ref[:, src]
        xwin = jnp.concatenate([prev_ref[:, src], x_cur], axis=0)
        taps = _dot(sel, xwin)
        acc = cb_ref[:, sl] + x_cur.astype(F32) * cw_ref[n_shift:CONV_WIDTH, sl]
        for w in range(n_shift):
            acc = acc + taps[w * L:(w + 1) * L] * cw_ref[w:w + 1, sl]
        xc = _silu(acc)
        if s * W < D_INNER:
            xs_ref[:, sl] = xc
        elif s * W < D_INNER + bc_dim:
            b16_ref[:, s * W - D_INNER:(s + 1) * W - D_INNER] = xc.astype(BF16)
        else:
            c16_ref[:, s * W - D_INNER - bc_dim:(s + 1) * W - D_INNER - bc_dim] = xc.astype(BF16)

    v = dt_ref[...] + dtb_ref[...]
    dt = jnp.maximum(v, 0.0) + jnp.log1p(jnp.exp(-jnp.abs(v)))
    a = dt * (jnp.exp(alog_ref[...]) * NEG_LOG2E)

    ri = lax.broadcasted_iota(jnp.int32, (L, L), 0)
    ci = lax.broadcasted_iota(jnp.int32, (L, L), 1)
    causal = ri >= ci
    tril = jnp.where(causal, 1.0, 0.0).astype(BF16)
    R = 2 * SUBLANES
    ones = jnp.ones((L, L), BF16)
    a1, a2, a3 = _split3(a)
    cs = _dot(tril, a1) + _dot(tril, a2) + _dot(tril, a3)
    tot = (_dot(ones, a1) + _dot(ones, a2) + _dot(ones, a3))[0:R]
    cs_t = cs.T[0:SSM_HEADS]

    lane = lax.broadcasted_iota(jnp.int32, (L, LANES), 1)

    def pack(q):
        ln = lax.broadcasted_iota(jnp.int32, q.shape, 1)
        q1, q2, q3 = _split3(q)
        packed = jnp.where(
            ln < SSM_HEADS, q1.astype(F32),
            jnp.where(ln < 2 * SSM_HEADS, pltpu.roll(q2.astype(F32), SSM_HEADS, 1),
                      jnp.where(ln < 3 * SSM_HEADS,
                                pltpu.roll(q3.astype(F32), 2 * SSM_HEADS, 1), 0.0)))
        return packed.astype(BF16)

    def tile_rows(v):
        return jnp.concatenate([v] * (L // SUBLANES), axis=0)

    pk_all = jnp.concatenate([pack(dt), pack(cs), pack(tot)], axis=0)
    pk_ct = pk_all[L:]

    for s in range(D_INNER // W):
        sl = slice(s * W, (s + 1) * W)
        ex = _dot(pk_all, expand_ref[:, sl])
        xd = xs_ref[:, sl] * ex[0:L]
        xd16_ref[:, sl] = xd.astype(BF16)
        tot_x = tile_rows(ex[2 * L:2 * L + SUBLANES])
        xdd_ref[:, sl] = (xd * jnp.exp2(tot_x - ex[L:2 * L])).astype(BF16)

    low_head = lane < SSM_HEAD_DIM
    heads_per_strip = W // SSM_HEAD_DIM
    strips_per_group = HEADS_PER_GROUP // heads_per_strip
    ssq = jnp.zeros((L, LANES), F32)

    for g in range(SSM_GROUPS):
        nsl = slice(g * D_STATE, (g + 1) * D_STATE)
        bg = b16_ref[:, nsl]
        cg = c16_ref[:, nsl]
        bg_t = bg.astype(F32).T.astype(BF16)
        cb = jnp.where(causal, _dot_nt(cg, bg), 0.0)
        for k in range(strips_per_group):
            s = g * strips_per_group + k
            sl = slice(s * W, (s + 1) * W)
            ex = _dot(pk_ct, expand_ref[:, sl])
            st_prev = state_ref[:, sl]
            y_off = _dot(cg, st_prev.astype(BF16))
            chunk_decay = tile_rows(jnp.exp2(ex[L:L + SUBLANES]))
            state_ref[:, sl] = st_prev * chunk_decay + _dot(bg_t, xdd_ref[:, sl])
            y_pairs = []
            for pair in range(W // LANES):
                psl = slice(s * W + pair * LANES, s * W + (pair + 1) * LANES)
                xp = xd16_ref[:, psl]
                mhs, xps = [], []
                for sub in range(2):
                    h = s * heads_per_strip + pair * 2 + sub
                    diff = cs[:, h:h + 1] - cs_t[h:h + 1, :]
                    decay = jnp.exp2(jnp.minimum(diff, 0.0))
                    mhs.append((cb * decay).astype(BF16))
                    keep = low_head if sub == 0 else jnp.logical_not(low_head)
                    xps.append(jnp.where(keep, xp, jnp.zeros_like(xp)))
                y_pairs.append(_dot(jnp.concatenate(mhs, axis=1), jnp.concatenate(xps, axis=0)))
            y = (jnp.concatenate(y_pairs, axis=1) + y_off * jnp.exp2(ex[0:L])
                 + dskip_ref[:, sl] * xs_ref[:, sl])
            yz = y * _silu(z_ref[:, sl].astype(F32))
            yz_ref[:, sl] = yz
            for pair in range(W // LANES):
                part = yz[:, pair * LANES:(pair + 1) * LANES]
                ssq = ssq + part * part

    var = jnp.sum(ssq, axis=-1, keepdims=True) * (1.0 / D_INNER)
    rs = lax.rsqrt(var + SSM_NORM_EPS)
    for s in range(D_INNER // W):
        sl = slice(s * W, (s + 1) * W)
        o_ref[:, sl] = (yz_ref[:, sl] * rs * g_ref[:, sl]).astype(o_ref.dtype)


def _ssd(zx, dtp, shift, conv_w, conv_b, dtb_p, alog_p, dskip_x, g_ssd, expand, batch, seq):
    t = zx.shape[0]
    nc = seq // CHUNK
    bc_dim = 2 * SSM_GROUPS * D_STATE
    row = lambda b, c: (b * nc + c, 0)
    cur = lambda b, c: b * nc + c
    prev = lambda b, c: jnp.maximum(b * nc + c - 1, 0)
    x_col = 1
    bc_col = 2 * D_INNER // bc_dim
    return pl.pallas_call(
        _ssd_kernel,
        out_shape=jax.ShapeDtypeStruct((t, D_INNER), BF16),
        grid=(batch, nc),
        in_specs=[pl.BlockSpec((CHUNK, D_INNER), row),
                  pl.BlockSpec((CHUNK, D_INNER), lambda b, c: (cur(b, c), x_col)),
                  pl.BlockSpec((CHUNK, bc_dim), lambda b, c: (cur(b, c), bc_col)),
                  pl.BlockSpec((CHUNK, D_INNER), lambda b, c: (prev(b, c), x_col)),
                  pl.BlockSpec((CHUNK, bc_dim), lambda b, c: (prev(b, c), bc_col)),
                  pl.BlockSpec((CHUNK, LANES), row),
                  _resident(((CONV_WIDTH - 1) * CHUNK, 2 * CHUNK)),
                  _resident((CONV_WIDTH, CONV_DIM)),
                  _resident((1, CONV_DIM)),
                  _resident((1, LANES)),
                  _resident((1, LANES)),
                  _resident((1, D_INNER)),
                  _resident((1, D_INNER)),
                  _resident((LANES, D_INNER))],
        out_specs=pl.BlockSpec((CHUNK, D_INNER), row),
        scratch_shapes=[pltpu.VMEM((D_STATE, D_INNER), F32),
                        pltpu.VMEM((CHUNK, D_INNER), F32),
                        pltpu.VMEM((CHUNK, SSM_GROUPS * D_STATE), BF16),
                        pltpu.VMEM((CHUNK, SSM_GROUPS * D_STATE), BF16),
                        pltpu.VMEM((CHUNK, D_INNER), BF16),
                        pltpu.VMEM((CHUNK, D_INNER), BF16),
                        pltpu.VMEM((CHUNK, D_INNER), F32)],
        compiler_params=_cparams(("arbitrary", "arbitrary")),
        name="ssd",
    )(zx, zx, zx, zx, zx, dtp, shift, conv_w, conv_b, dtb_p, alog_p, dskip_x, g_ssd, expand)


def _merge_kernel(attn_ref, y_ref, ga_ref, gs_ref, wa_ref, ws_ref, o_ref, wa16_ref, ws16_ref):
    @pl.when(pl.program_id(1) == 0)
    def _():
        _cast_rows(wa16_ref, wa_ref, Q_DIM)
        ws16_ref[...] = ws_ref[...].astype(BF16)

    out_a = _dot(attn_ref[...], wa16_ref[...])
    out_s = _dot(y_ref[...], ws16_ref[...])
    merged = (_sigmoid(ga_ref[...].astype(F32)) * out_a
              + _sigmoid(gs_ref[...].astype(F32)) * out_s)
    o_ref[...] = merged.astype(o_ref.dtype)


def _merge(attn, y, gates, gate_col0, wa, ws, tm=512, tn=1024):
    t = attn.shape[0]
    nj = D_MODEL // tn
    assert gate_col0 % tn == 0
    g0 = gate_col0 // tn
    return pl.pallas_call(
        _merge_kernel,
        out_shape=jax.ShapeDtypeStruct((t, D_MODEL), BF16),
        grid=(nj, t // tm),
        in_specs=[pl.BlockSpec((tm, Q_DIM), lambda j, i: (i, 0)),
                  pl.BlockSpec((tm, D_INNER), lambda j, i: (i, 0)),
                  pl.BlockSpec((tm, tn), lambda j, i: (i, g0 + j)),
                  pl.BlockSpec((tm, tn), lambda j, i: (i, g0 + nj + j)),
                  pl.BlockSpec((Q_DIM, tn), lambda j, i: (0, j)),
                  pl.BlockSpec((D_INNER, tn), lambda j, i: (0, j))],
        out_specs=pl.BlockSpec((tm, tn), lambda j, i: (i, j)),
        scratch_shapes=[pltpu.VMEM((Q_DIM, tn), BF16),
                        pltpu.VMEM((D_INNER, tn), BF16)],
        compiler_params=_cparams(("arbitrary", "arbitrary")),
        name="merge",
    )(attn, y, gates, gates, wa, ws)


def _oproj_kernel(m_ref, x_ref, w_ref, g_ref, h_ref, f_ref, w16_ref):
    @pl.when(pl.program_id(0) == 0)
    def _():
        w16_ref[...] = w_ref[...].astype(BF16)

    h = x_ref[...] + _dot(m_ref[...], w16_ref[...])
    h_ref[...] = h
    f_ref[...] = _rms(h, g_ref[...], NORM_EPS).astype(BF16)


def _oproj(merged, x2, wo, g_ffn, tm=512):
    t = merged.shape[0]
    row = lambda i: (i, 0)
    return pl.pallas_call(
        _oproj_kernel,
        out_shape=(jax.ShapeDtypeStruct((t, D_MODEL), F32),
                   jax.ShapeDtypeStruct((t, D_MODEL), BF16)),
        grid=(t // tm,),
        in_specs=[pl.BlockSpec((tm, D_MODEL), row),
                  pl.BlockSpec((tm, D_MODEL), row),
                  _resident((D_MODEL, D_MODEL)),
                  _resident((1, D_MODEL))],
        out_specs=(pl.BlockSpec((tm, D_MODEL), row),
                   pl.BlockSpec((tm, D_MODEL), row)),
        scratch_shapes=[pltpu.VMEM((D_MODEL, D_MODEL), BF16)],
        compiler_params=_cparams(("arbitrary",)),
        name="oproj",
    )(merged, x2, wo, g_ffn)


def _ffn_up_kernel(f_ref, wg_ref, wu_ref, o_ref, wg16_ref, wu16_ref):
    @pl.when(pl.program_id(1) == 0)
    def _():
        wg16_ref[...] = wg_ref[...].astype(BF16)
        wu16_ref[...] = wu_ref[...].astype(BF16)

    half_r = o_ref.shape[0] // 2
    half_c = o_ref.shape[1] // 2
    for r in range(2):
        rs = slice(r * half_r, (r + 1) * half_r)
        f = f_ref[rs, :]
        for c in range(2):
            cs = slice(c * half_c, (c + 1) * half_c)
            o_ref[rs, cs] = (_silu(_dot(f, wg16_ref[:, cs]))
                             * _dot(f, wu16_ref[:, cs])).astype(o_ref.dtype)


def _ffn_up(f, wg, wu, tm=2048, th=512):
    t = f.shape[0]
    return pl.pallas_call(
        _ffn_up_kernel,
        out_shape=jax.ShapeDtypeStruct((t, FFN_HIDDEN), BF16),
        grid=(FFN_HIDDEN // th, t // tm),
        in_specs=[pl.BlockSpec((tm, D_MODEL), lambda j, i: (i, 0)),
                  pl.BlockSpec((D_MODEL, th), lambda j, i: (0, j)),
                  pl.BlockSpec((D_MODEL, th), lambda j, i: (0, j))],
        out_specs=pl.BlockSpec((tm, th), lambda j, i: (i, j)),
        scratch_shapes=[pltpu.VMEM((D_MODEL, th), BF16),
                        pltpu.VMEM((D_MODEL, th), BF16)],
        compiler_params=_cparams(("arbitrary", "arbitrary")),
        name="ffn_up",
    )(f, wg, wu)


def _ffn_down_kernel(a_ref, h_ref, wd_hbm, o_ref, wd16_ref, wf32_ref, sem, *, n_col_tiles):
    j = pl.program_id(0)
    i = pl.program_id(1)
    tn = wf32_ref.shape[1]

    def copy(col_tile):
        return pltpu.make_async_copy(wd_hbm.at[:, pl.ds(col_tile * tn, tn)], wf32_ref, sem)

    @pl.when((j == 0) & (i == 0))
    def _():
        copy(0).start()

    @pl.when(i == 0)
    def _():
        copy(j).wait()
        wd16_ref[...] = wf32_ref[...].astype(BF16)

    @pl.when((i == 0) & (j + 1 < n_col_tiles))
    def _():
        copy(j + 1).start()

    o_ref[...] = h_ref[...] + _dot(a_ref[...], wd16_ref[...])


def _ffn_down(act, h1, wd, tm=512, tn=1024):
    t = act.shape[0]
    n_col_tiles = D_MODEL // tn
    return pl.pallas_call(
        partial(_ffn_down_kernel, n_col_tiles=n_col_tiles),
        out_shape=jax.ShapeDtypeStruct((t, D_MODEL), F32),
        grid=(n_col_tiles, t // tm),
        in_specs=[pl.BlockSpec((tm, FFN_HIDDEN), lambda j, i: (i, 0)),
                  pl.BlockSpec((tm, tn), lambda j, i: (i, j)),
                  pl.BlockSpec(memory_space=pl.ANY)],
        out_specs=pl.BlockSpec((tm, tn), lambda j, i: (i, j)),
        scratch_shapes=[pltpu.VMEM((FFN_HIDDEN, tn), BF16),
                        pltpu.VMEM((FFN_HIDDEN, tn), F32),
                        pltpu.SemaphoreType.DMA],
        compiler_params=_cparams(("arbitrary", "arbitrary")),
        name="ffn_down",
    )(act, h1, wd)


def _ple_kernel(h_ref, p_ref, gp_ref, gf_ref, wg_ref, wp_ref, o_ref, wg16_ref, wp16_ref):
    @pl.when(pl.program_id(0) == 0)
    def _():
        wg16_ref[...] = wg_ref[...].astype(BF16)
        wp16_ref[...] = wp_ref[...].astype(BF16)

    h = h_ref[...]
    r = _rms(h, gp_ref[...], NORM_EPS).astype(BF16)
    gate = _sigmoid(_dot(r, wg16_ref[...]))
    h3 = h + gate * _dot(p_ref[...].astype(BF16), wp16_ref[...])
    o_ref[...] = _rms(h3, gf_ref[...], NORM_EPS)


def _ple(h2, p2, g_ple, g_final, wpg, wpp, tm=512):
    t = h2.shape[0]
    row = lambda i: (i, 0)
    return pl.pallas_call(
        _ple_kernel,
        out_shape=jax.ShapeDtypeStruct((t, D_MODEL), F32),
        grid=(t // tm,),
        in_specs=[pl.BlockSpec((tm, D_MODEL), row),
                  pl.BlockSpec((tm, PLE_DIM), row),
                  _resident((1, D_MODEL)),
                  _resident((1, D_MODEL)),
                  _resident((D_MODEL, D_MODEL)),
                  _resident((PLE_DIM, D_MODEL))],
        out_specs=pl.BlockSpec((tm, D_MODEL), row),
        scratch_shapes=[pltpu.VMEM((D_MODEL, D_MODEL), BF16),
                        pltpu.VMEM((PLE_DIM, D_MODEL), BF16)],
        compiler_params=_cparams(("arbitrary",)),
        name="ple",
    )(h2, p2, g_ple, g_final, wpg, wpp)


def _rope_constants():
    half = HEAD_DIM // 2
    lane = np.arange(LANES)
    inv_freq = ROPE_THETA ** (-jnp.arange(half, dtype=F32) * 2.0 / HEAD_DIM)
    invf = inv_freq[lane % half][None, :]
    sgn = jnp.asarray(np.where((lane % HEAD_DIM) < half, -1.0, 1.0), F32)[None, :]
    return invf, sgn


def _expand_matrix():
    rows = np.arange(LANES)[:, None]
    cols = np.arange(D_INNER)[None, :]
    hit = (rows < 3 * SSM_HEADS) & ((rows % SSM_HEADS) == (cols // SSM_HEAD_DIM))
    return jnp.asarray(hit, BF16)


def _shift_matrix():
    rows = np.arange((CONV_WIDTH - 1) * CHUNK)[:, None]
    cols = np.arange(2 * CHUNK)[None, :]
    w, t = rows // CHUNK, rows % CHUNK
    return jnp.asarray(cols == CHUNK + t - (CONV_WIDTH - 1) + w, BF16)


def _pad_lanes(v):
    return jnp.pad(v.astype(F32), (0, LANES - v.shape[0]))[None, :]


def kernel(x, p, positions, g_mix, w_in, conv_w, conv_b, dt_bias, a_log, d_skip, g_ssd,
           sinks, w_attn_br, w_ssd_br, w_o, g_ffn, w_gate, w_up, w_down, g_ple,
           w_ple_gate, w_ple_proj, g_final):
    b, s, d = x.shape
    t = b * s
    assert d == D_MODEL and s % CHUNK == 0 and p.shape[0] == 1
    i = 0
    x2 = x.reshape(t, d)
    p2 = p[i].reshape(t, PLE_DIM)
    pos_c = jnp.repeat(positions.reshape(t // POS_PER_ROW, POS_PER_ROW), ROPE_FREQS, axis=1)

    o_k = Q_DIM
    o_z = Q_DIM + 2 * KV_DIM
    o_xbc = o_z + D_INNER
    o_dt = o_xbc + CONV_DIM
    o_ga = o_dt + SSM_HEADS
    wt = jnp.swapaxes(w_in[i], 0, 1)

    invf, sgn = _rope_constants()
    u, cos_t, sin_t, dtp = _prep(x2, g_mix[i][None, :], pos_c, invf, sgn, wt, o_dt)

    qkv = _inproj(u, wt, [(0, o_z)], BF16, QKV_ROW_TILE, o_z, "proj_qkv", paired_rows=Q_DIM)
    zxg = _inproj(u, wt, [(o_z, D_INNER + CONV_DIM), (o_ga, 2 * D_MODEL)], BF16,
                  INPROJ_ROW_TILE, INPROJ_COL_TILE, "proj_zxg")
    gate_col0 = D_INNER + CONV_DIM

    attn = _attention(qkv, cos_t, sin_t, sinks[i][None, :].astype(F32), b, s)

    y = _ssd(zxg, dtp, _shift_matrix(), conv_w[i], conv_b[i][None, :], _pad_lanes(dt_bias[i]),
             _pad_lanes(a_log[i]), jnp.repeat(d_skip[i].astype(F32), SSM_HEAD_DIM)[None, :],
             g_ssd[i][None, :], _expand_matrix(), b, s)

    merged = _merge(attn, y, zxg, gate_col0, w_attn_br[i], w_ssd_br[i])
    h1, f = _oproj(merged, x2, w_o[i], g_ffn[i][None, :])
    act = _ffn_up(f, w_gate[i], w_up[i])
    h2 = _ffn_down(act, h1, w_down[i])
    out = _ple(h2, p2, g_ple[i][None, :], g_final[None, :], w_ple_gate[i], w_ple_proj[i])
    return out.reshape(b, s, d)
```

```python
from functools import partial

import numpy as np
import jax
import jax.numpy as jnp
from jax import lax
from jax.experimental import pallas as pl
from jax.experimental.pallas import tpu as pltpu

F32 = jnp.float32
BF16 = jnp.bfloat16

D_MODEL = 2048
HEAD_DIM = 64
ATTN_HEADS = 16
KV_HEADS = 4
GROUP = ATTN_HEADS // KV_HEADS
Q_DIM = ATTN_HEADS * HEAD_DIM
KV_DIM = KV_HEADS * HEAD_DIM
ATTN_BLOCK = 128
ROPE_THETA = 10000.0
D_INNER = 2048
SSM_HEAD_DIM = 64
SSM_HEADS = 32
SSM_GROUPS = 4
HEADS_PER_GROUP = SSM_HEADS // SSM_GROUPS
D_STATE = 128
CONV_WIDTH = 4
CHUNK = 128
CONV_DIM = D_INNER + 2 * SSM_GROUPS * D_STATE
FFN_HIDDEN = 5632
PLE_DIM = 256
NORM_EPS = 1e-6
SSM_NORM_EPS = 1e-5

LANES = 128
SUBLANES = 8
VMEM_LIMIT_BYTES = 56 * 1024 * 1024

NEG_BIG = -1e30


def _cparams(semantics):
    return pltpu.CompilerParams(dimension_semantics=semantics,
                                vmem_limit_bytes=VMEM_LIMIT_BYTES)


def _resident(shape):
    return pl.BlockSpec(shape, lambda *_: (0,) * len(shape),
                        pipeline_mode=pl.Buffered(1))


def _rms(xf, g, eps):
    var = jnp.mean(xf * xf, axis=-1, keepdims=True)
    return xf * lax.rsqrt(var + eps) * g


NEG_LOG2E = -1.4426950408889634


def _sigmoid(x):
    return 1.0 / (1.0 + jnp.exp2(x * NEG_LOG2E))


def _silu(x):
    return x * _sigmoid(x)


def _dot(a, b):
    return jnp.dot(a, b, preferred_element_type=F32)


def _dot_nt(a, b):
    return lax.dot_general(a, b, (((1,), (1,)), ((), ())),
                           preferred_element_type=F32)


def _split3(v):
    v1 = v.astype(BF16)
    r1 = v - v1.astype(F32)
    v2 = r1.astype(BF16)
    v3 = (r1 - v2.astype(F32)).astype(BF16)
    return v1, v2, v3


ROPE_FREQS = HEAD_DIM // 2
POS_PER_ROW = LANES // ROPE_FREQS


def _prep_kernel(x_ref, g_ref, pos_ref, invf_ref, sgn_ref, wdt_ref, wqkv_ref,
                 u_ref, cos_ref, sin_ref, dt_ref, qkv_ref, wqkv16_ref):
    @pl.when(pl.program_id(0) == 0)
    def _():
        _cast_rows(wqkv16_ref, wqkv_ref, Q_DIM)

    u = _rms(x_ref[...], g_ref[...], NORM_EPS).astype(BF16)
    u_ref[...] = u
    qkv_ref[...] = _dot_nt(u, wqkv16_ref[...]).astype(qkv_ref.dtype)
    rows = pos_ref.shape[0]
    ang = pos_ref[...].astype(F32) * invf_ref[...]
    cos_c = jnp.cos(ang)
    sin_c = jnp.sin(ang)
    group = lax.broadcasted_iota(jnp.int32, ang.shape, 1) // ROPE_FREQS
    for q in range(POS_PER_ROW):
        for src, dst_ref, sign in ((cos_c, cos_ref, None), (sin_c, sin_ref, sgn_ref[...])):
            only = jnp.where(group == q, src, 0.0)
            rep = only
            for k in range(1, POS_PER_ROW):
                rep = rep + pltpu.roll(only, k * ROPE_FREQS, 1)
            if sign is not None:
                rep = rep * sign
            dst_ref[pl.ds(q, rows, stride=POS_PER_ROW), :] = rep
    dt_ref[...] = _dot_nt(u, wdt_ref[...].astype(BF16))


def _prep(x2, g_mix, pos_c, invf, sgn, wt, dt_row0, qkv_rows, tm=512):
    t = x2.shape[0]
    row = lambda i: (i, 0)
    assert dt_row0 % SUBLANES == 0 and tm % (POS_PER_ROW * SUBLANES) == 0
    return pl.pallas_call(
        _prep_kernel,
        out_shape=(jax.ShapeDtypeStruct((t, D_MODEL), BF16),
                   jax.ShapeDtypeStruct((t, LANES), F32),
                   jax.ShapeDtypeStruct((t, LANES), F32),
                   jax.ShapeDtypeStruct((t, LANES), F32),
                   jax.ShapeDtypeStruct((t, qkv_rows), BF16)),
        grid=(t // tm,),
        in_specs=[pl.BlockSpec((tm, D_MODEL), row),
                  _resident((1, D_MODEL)),
                  pl.BlockSpec((tm // POS_PER_ROW, LANES), row),
                  _resident((1, LANES)),
                  _resident((1, LANES)),
                  pl.BlockSpec((pl.Element(LANES), pl.Element(D_MODEL)),
                               lambda i: (dt_row0, 0), pipeline_mode=pl.Buffered(1)),
                  pl.BlockSpec((pl.Element(qkv_rows), pl.Element(D_MODEL)),
                               lambda i: (0, 0), pipeline_mode=pl.Buffered(1))],
        out_specs=(pl.BlockSpec((tm, D_MODEL), row),
                   pl.BlockSpec((tm, LANES), row),
                   pl.BlockSpec((tm, LANES), row),
                   pl.BlockSpec((tm, LANES), row),
                   pl.BlockSpec((tm, qkv_rows), row)),
        scratch_shapes=[pltpu.VMEM((qkv_rows, D_MODEL), BF16)],
        compiler_params=_cparams(("arbitrary",)),
        name="prep_qkv",
    )(x2, g_mix, pos_c, invf, sgn, wt, wt)


PAIR_ROWS = 2 * GROUP * HEAD_DIM
INPROJ_ROW_TILE = 2048
INPROJ_COL_TILE = 1024


def _paired_head_blocks():
    return [(g * 2 + sub, sub * GROUP + g) for g in range(GROUP) for sub in range(2)]


def _cast_rows(dst_ref, src_ref, paired_rows):
    rows = dst_ref.shape[0]
    assert paired_rows % PAIR_ROWS == 0 and paired_rows <= rows
    for base in range(0, paired_rows, PAIR_ROWS):
        for dst, src in _paired_head_blocks():
            dst_ref[base + dst * HEAD_DIM:base + (dst + 1) * HEAD_DIM, :] = (
                src_ref[base + src * HEAD_DIM:base + (src + 1) * HEAD_DIM, :].astype(BF16))
    if paired_rows < rows:
        dst_ref[paired_rows:rows, :] = src_ref[paired_rows:rows, :].astype(BF16)


def _inproj_kernel(a_ref, wt_ref, o_ref, wbf_ref):
    @pl.when(pl.program_id(1) == 0)
    def _():
        wbf_ref[...] = wt_ref[...].astype(BF16)

    o_ref[...] = _dot_nt(a_ref[...], wbf_ref[...]).astype(o_ref.dtype)


def _inproj(a, wt, segments, out_dtype, tm, tn, name):
    m, k = a.shape
    assert all(seg_n % tn == 0 and row0 % SUBLANES == 0 for row0, seg_n in segments)
    n = sum(seg_n for _, seg_n in segments)

    def weight_row(j):
        row, first_tile = None, 0
        for row0, seg_n in segments:
            here = row0 + (j - first_tile) * tn
            row = here if row is None else jnp.where(j >= first_tile, here, row)
            first_tile += seg_n // tn
        return pl.multiple_of(row, SUBLANES)

    return pl.pallas_call(
        _inproj_kernel,
        out_shape=jax.ShapeDtypeStruct((m, n), out_dtype),
        grid=(n // tn, m // tm),
        in_specs=[pl.BlockSpec((tm, k), lambda j, i: (i, 0)),
                  pl.BlockSpec((pl.Element(tn), pl.Element(k)), lambda j, i: (weight_row(j), 0))],
        out_specs=pl.BlockSpec((tm, tn), lambda j, i: (i, j)),
        scratch_shapes=[pltpu.VMEM((tn, k), BF16)],
        compiler_params=_cparams(("arbitrary", "arbitrary")),
        name=name,
    )(a, wt)


ATTN_STEP_BLOCKS = 2


def _attn_kernel(sinks_ref, q_ref, kc_ref, kp_ref, vc_ref, vp_ref,
                 cc_ref, sc_ref, cp_ref, sp_ref, o_ref):
    n = pl.program_id(1)
    blk = ATTN_BLOCK
    nblk = ATTN_STEP_BLOCKS
    lane = lax.broadcasted_iota(jnp.int32, (blk, LANES), 1)
    first_half = (lane % HEAD_DIM) < (HEAD_DIM // 2)
    low_head = lane < HEAD_DIM

    def rope(t, c, s):
        partner = jnp.where(first_half,
                            pltpu.roll(t, LANES - HEAD_DIM // 2, 1),
                            pltpu.roll(t, HEAD_DIM // 2, 1))
        return t * c + partner * s

    def rows(j):
        return slice(j * blk, (j + 1) * blk)

    cp, sp = cp_ref[...], sp_ref[...]
    cc = [cc_ref[rows(j), :] for j in range(nblk)]
    sc = [sc_ref[rows(j), :] for j in range(nblk)]

    qi = lax.broadcasted_iota(jnp.int32, (blk, 2 * blk), 0)
    kj = lax.broadcasted_iota(jnp.int32, (blk, 2 * blk), 1)
    rel = kj - qi
    band = (rel >= 1) & (rel <= blk)
    valid = [band & ((kj >= blk) | (n > 0))] + [band] * (nblk - 1)

    scale = HEAD_DIM ** -0.5
    n_half = KV_DIM // LANES
    rows_per_half = 2 * GROUP * blk
    s_rows, sink_rows, valid_rows, v_wins = [], [], [], []
    for a in range(n_half):
        ksl = slice(a * LANES, (a + 1) * LANES)
        k_blocks = [rope(kp_ref[:, ksl].astype(F32), cp, sp).astype(BF16)]
        v_blocks = [vp_ref[:, ksl]]
        for j in range(nblk):
            k_blocks.append(rope(kc_ref[rows(j), ksl].astype(F32), cc[j], sc[j]).astype(BF16))
            v_blocks.append(vc_ref[rows(j), ksl])
        for j in range(nblk):
            k_win = jnp.concatenate(k_blocks[j:j + 2], axis=0)
            v_wins.append(jnp.concatenate(v_blocks[j:j + 2], axis=0))
            qc = [rope(q_ref[rows(j), (a * GROUP + g) * LANES:(a * GROUP + g + 1) * LANES]
                       .astype(F32), cc[j] * scale, sc[j] * scale) for g in range(GROUP)]
            q_rows = []
            for sub in range(2):
                keep = low_head if sub == 0 else jnp.logical_not(low_head)
                for g in range(GROUP):
                    q_rows.append(jnp.where(keep, qc[g], 0.0).astype(BF16))
                    sink_rows.append(
                        jnp.full((blk, LANES), sinks_ref[0, (2 * a + sub) * GROUP + g], F32))
            s_rows.append(_dot_nt(jnp.concatenate(q_rows, axis=0), k_win))
            valid_rows.extend([valid[j]] * (2 * GROUP))
    s = jnp.where(jnp.concatenate(valid_rows, axis=0), jnp.concatenate(s_rows, axis=0), NEG_BIG)
    sink = jnp.concatenate(sink_rows, axis=0)
    s0, s1 = s[:, :LANES], s[:, LANES:]
    m = jnp.maximum(jnp.max(jnp.maximum(s0, s1), axis=-1, keepdims=True), sink)
    e0 = jnp.exp(s0 - m)
    e1 = jnp.exp(s1 - m)
    den = jnp.sum(e0 + e1, axis=-1, keepdims=True) + jnp.exp(sink - m)
    r = 1.0 / den
    p = jnp.concatenate([e0 * r, e1 * r], axis=1).astype(BF16)
    for a in range(n_half):
        for j in range(nblk):
            w = a * nblk + j
            pv = _dot(p[w * rows_per_half:(w + 1) * rows_per_half], v_wins[w])
            for g in range(GROUP):
                lo = pv[g * blk:(g + 1) * blk]
                hi = pv[(GROUP + g) * blk:(GROUP + g + 1) * blk]
                chunk = a * GROUP + g
                o_ref[rows(j), chunk * LANES:(chunk + 1) * LANES] = (
                    jnp.where(low_head, lo, hi).astype(o_ref.dtype))


def _attention(qkv, cos_t, sin_t, sinks, batch, seq):
    t = qkv.shape[0]
    step = ATTN_STEP_BLOCKS * ATTN_BLOCK
    assert seq % step == 0
    ns = seq // step
    kcol = Q_DIM // KV_DIM
    vcol = kcol + 1
    cur = lambda b, n: b * ns + n
    prev = lambda b, n: jnp.maximum((b * ns + n) * ATTN_STEP_BLOCKS - 1, 0)
    return pl.pallas_call(
        _attn_kernel,
        out_shape=jax.ShapeDtypeStruct((t, Q_DIM), BF16),
        grid=(batch, ns),
        in_specs=[pl.BlockSpec(memory_space=pltpu.SMEM),
                  pl.BlockSpec((step, Q_DIM), lambda b, n: (cur(b, n), 0)),
                  pl.BlockSpec((step, KV_DIM), lambda b, n: (cur(b, n), kcol)),
                  pl.BlockSpec((ATTN_BLOCK, KV_DIM), lambda b, n: (prev(b, n), kcol)),
                  pl.BlockSpec((step, KV_DIM), lambda b, n: (cur(b, n), vcol)),
                  pl.BlockSpec((ATTN_BLOCK, KV_DIM), lambda b, n: (prev(b, n), vcol)),
                  pl.BlockSpec((step, LANES), lambda b, n: (cur(b, n), 0)),
                  pl.BlockSpec((step, LANES), lambda b, n: (cur(b, n), 0)),
                  pl.BlockSpec((ATTN_BLOCK, LANES), lambda b, n: (prev(b, n), 0)),
                  pl.BlockSpec((ATTN_BLOCK, LANES), lambda b, n: (prev(b, n), 0))],
        out_specs=pl.BlockSpec((step, Q_DIM), lambda b, n: (cur(b, n), 0)),
        compiler_params=_cparams(("arbitrary", "arbitrary")),
        name="attention",
    )(sinks, qkv, qkv, qkv, qkv, qkv, cos_t, sin_t, cos_t, sin_t)


SSD_STRIP = 256
CONV_MXU_STRIP_PERIOD = 3


def _ssd_kernel(z_ref, xin_ref, bcin_ref, xin_prev_ref, bcin_prev_ref, dt_ref, shift_ref, cw_ref,
                cb_ref, dtb_ref, alog_ref, dskip_ref, g_ref, expand_ref, o_ref,
                state_ref, xs_ref, b16_ref, c16_ref, xdd_ref, xd16_ref, yz_ref):
    c = pl.program_id(1)
    L = CHUNK
    W = SSD_STRIP
    bc_dim = SSM_GROUPS * D_STATE

    @pl.when(c == 0)
    def _():
        state_ref[...] = jnp.zeros_like(state_ref)

    n_shift = CONV_WIDTH - 1
    sel_col = lax.broadcasted_iota(jnp.int32, (n_shift * L, 2 * L), 1)
    sel = shift_ref[...]
    sel = jnp.where((sel_col < L) & (c == 0), jnp.zeros_like(sel), sel)
    for s in range(CONV_DIM // W):
        sl = slice(s * W, (s + 1) * W)
        if s * W < D_INNER:
            cur_ref, prev_ref, src = xin_ref, xin_prev_ref, sl
        else:
            cur_ref, prev_ref = bcin_ref, bcin_prev_ref
            src = slice(s * W - D_INNER, (s + 1) * W - D_INNER)
        x_cur = cur_ref[:, src]
        x_f32 = x_cur.astype(F32)
        acc = cb_ref[:, sl] + x_f32 * cw_ref[n_shift:CONV_WIDTH, sl]
        if s % CONV_MXU_STRIP_PERIOD == 0:
            xwin = jnp.concatenate([prev_ref[:, src], x_cur], axis=0)
            taps = _dot(sel, xwin)
            for w in range(n_shift):
                acc = acc + taps[w * L:(w + 1) * L] * cw_ref[w:w + 1, sl]
        else:
            tail = prev_ref[L - 2 * SUBLANES:L, src].astype(F32)[SUBLANES:]
            tail = jnp.where(c == 0, 0.0, tail)
            xw = jnp.concatenate([tail, x_f32], axis=0)
            for w in range(n_shift):
                off = SUBLANES - n_shift + w
                acc = acc + xw[off:off + L] * cw_ref[w:w + 1, sl]
        xc = _silu(acc)
        if s * W < D_INNER:
            xs_ref[:, sl] = xc
        elif s * W < D_INNER + bc_dim:
            b16_ref[:, s * W - D_INNER:(s + 1) * W - D_INNER] = xc.astype(BF16)
        else:
            c16_ref[:, s * W - D_INNER - bc_dim:(s + 1) * W - D_INNER - bc_dim] = xc.astype(BF16)

    v = dt_ref[...] + dtb_ref[...]
    dt = jnp.maximum(v, 0.0) + jnp.log1p(jnp.exp(-jnp.abs(v)))
    a = dt * (jnp.exp(alog_ref[...]) * NEG_LOG2E)

    ri = lax.broadcasted_iota(jnp.int32, (L, L), 0)
    ci = lax.broadcasted_iota(jnp.int32, (L, L), 1)
    causal = ri >= ci
    tril = jnp.where(causal, 1.0, 0.0).astype(BF16)
    R = 2 * SUBLANES
    ones = jnp.ones((L, L), BF16)
    a1, a2, a3 = _split3(a)
    cs = _dot(tril, a1) + _dot(tril, a2) + _dot(tril, a3)
    tot = (_dot(ones, a1) + _dot(ones, a2) + _dot(ones, a3))[0:R]
    cs_t = cs.T[0:SSM_HEADS]

    lane = lax.broadcasted_iota(jnp.int32, (L, LANES), 1)

    def pack(q):
        ln = lax.broadcasted_iota(jnp.int32, q.shape, 1)
        q1, q2, q3 = _split3(q)
        packed = jnp.where(
            ln < SSM_HEADS, q1.astype(F32),
            jnp.where(ln < 2 * SSM_HEADS, pltpu.roll(q2.astype(F32), SSM_HEADS, 1),
                      jnp.where(ln < 3 * SSM_HEADS,
                                pltpu.roll(q3.astype(F32), 2 * SSM_HEADS, 1), 0.0)))
        return packed.astype(BF16)

    def tile_rows(v):
        return jnp.concatenate([v] * (L // SUBLANES), axis=0)

    pk_all = jnp.concatenate([pack(dt), pack(cs), pack(tot)], axis=0)
    pk_ct = pk_all[L:]

    for s in range(D_INNER // W):
        sl = slice(s * W, (s + 1) * W)
        ex = _dot(pk_all, expand_ref[:, sl])
        xd = xs_ref[:, sl] * ex[0:L]
        xd16_ref[:, sl] = xd.astype(BF16)
        tot_x = tile_rows(ex[2 * L:2 * L + SUBLANES])
        xdd_ref[:, sl] = (xd * jnp.exp2(tot_x - ex[L:2 * L])).astype(BF16)

    low_head = lane < SSM_HEAD_DIM
    heads_per_strip = W // SSM_HEAD_DIM
    strips_per_group = HEADS_PER_GROUP // heads_per_strip
    ssq = jnp.zeros((L, LANES), F32)

    for g in range(SSM_GROUPS):
        nsl = slice(g * D_STATE, (g + 1) * D_STATE)
        bg = b16_ref[:, nsl]
        cg = c16_ref[:, nsl]
        bg_t = bg.astype(F32).T.astype(BF16)
        cb = jnp.where(causal, _dot_nt(cg, bg), 0.0)
        for k in range(strips_per_group):
            s = g * strips_per_group + k
            sl = slice(s * W, (s + 1) * W)
            ex = _dot(pk_ct, expand_ref[:, sl])
            st_prev = state_ref[:, sl]
            y_off = _dot(cg, st_prev.astype(BF16))
            chunk_decay = tile_rows(jnp.exp2(ex[L:L + SUBLANES]))
            state_ref[:, sl] = st_prev * chunk_decay + _dot(bg_t, xdd_ref[:, sl])
            y_pairs = []
            for pair in range(W // LANES):
                psl = slice(s * W + pair * LANES, s * W + (pair + 1) * LANES)
                xp = xd16_ref[:, psl]
                mhs, xps = [], []
                for sub in range(2):
                    h = s * heads_per_strip + pair * 2 + sub
                    diff = cs[:, h:h + 1] - cs_t[h:h + 1, :]
                    decay = jnp.exp2(jnp.minimum(diff, 0.0))
                    mhs.append((cb * decay).astype(BF16))
                    keep = low_head if sub == 0 else jnp.logical_not(low_head)
                    xps.append(jnp.where(keep, xp, jnp.zeros_like(xp)))
                y_pairs.append(_dot(jnp.concatenate(mhs, axis=1), jnp.concatenate(xps, axis=0)))
            y = (jnp.concatenate(y_pairs, axis=1) + y_off * jnp.exp2(ex[0:L])
                 + dskip_ref[:, sl] * xs_ref[:, sl])
            yz = y * _silu(z_ref[:, sl].astype(F32))
            yz_ref[:, sl] = yz
            for pair in range(W // LANES):
                part = yz[:, pair * LANES:(pair + 1) * LANES]
                ssq = ssq + part * part

    var = jnp.sum(ssq, axis=-1, keepdims=True) * (1.0 / D_INNER)
    rs = lax.rsqrt(var + SSM_NORM_EPS)
    for s in range(D_INNER // W):
        sl = slice(s * W, (s + 1) * W)
        o_ref[:, sl] = (yz_ref[:, sl] * rs * g_ref[:, sl]).astype(o_ref.dtype)


def _ssd(zx, dtp, shift, conv_w, conv_b, dtb_p, alog_p, dskip_x, g_ssd, expand, batch, seq):
    t = zx.shape[0]
    nc = seq // CHUNK
    bc_dim = 2 * SSM_GROUPS * D_STATE
    row = lambda b, c: (b * nc + c, 0)
    cur = lambda b, c: b * nc + c
    prev = lambda b, c: jnp.maximum(b * nc + c - 1, 0)
    x_col = 1
    bc_col = 2 * D_INNER // bc_dim
    return pl.pallas_call(
        _ssd_kernel,
        out_shape=jax.ShapeDtypeStruct((t, D_INNER), BF16),
        grid=(batch, nc),
        in_specs=[pl.BlockSpec((CHUNK, D_INNER), row),
                  pl.BlockSpec((CHUNK, D_INNER), lambda b, c: (cur(b, c), x_col)),
                  pl.BlockSpec((CHUNK, bc_dim), lambda b, c: (cur(b, c), bc_col)),
                  pl.BlockSpec((CHUNK, D_INNER), lambda b, c: (prev(b, c), x_col)),
                  pl.BlockSpec((CHUNK, bc_dim), lambda b, c: (prev(b, c), bc_col)),
                  pl.BlockSpec((CHUNK, LANES), row),
                  _resident(((CONV_WIDTH - 1) * CHUNK, 2 * CHUNK)),
                  _resident((CONV_WIDTH, CONV_DIM)),
                  _resident((1, CONV_DIM)),
                  _resident((1, LANES)),
                  _resident((1, LANES)),
                  _resident((1, D_INNER)),
                  _resident((1, D_INNER)),
                  _resident((LANES, D_INNER))],
        out_specs=pl.BlockSpec((CHUNK, D_INNER), row),
        scratch_shapes=[pltpu.VMEM((D_STATE, D_INNER), F32),
                        pltpu.VMEM((CHUNK, D_INNER), F32),
                        pltpu.VMEM((CHUNK, SSM_GROUPS * D_STATE), BF16),
                        pltpu.VMEM((CHUNK, SSM_GROUPS * D_STATE), BF16),
                        pltpu.VMEM((CHUNK, D_INNER), BF16),
                        pltpu.VMEM((CHUNK, D_INNER), BF16),
                        pltpu.VMEM((CHUNK, D_INNER), F32)],
        compiler_params=_cparams(("arbitrary", "arbitrary")),
        name="ssd",
    )(zx, zx, zx, zx, zx, dtp, shift, conv_w, conv_b, dtb_p, alog_p, dskip_x, g_ssd, expand)


def _merge_kernel(attn_ref, y_ref, ga_ref, gs_ref, wa_ref, ws_ref, o_ref, wa16_ref, ws16_ref):
    @pl.when(pl.program_id(1) == 0)
    def _():
        _cast_rows(wa16_ref, wa_ref, Q_DIM)
        ws16_ref[...] = ws_ref[...].astype(BF16)

    out_a = _dot(attn_ref[...], wa16_ref[...])
    out_s = _dot(y_ref[...], ws16_ref[...])
    merged = (_sigmoid(ga_ref[...].astype(F32)) * out_a
              + _sigmoid(gs_ref[...].astype(F32)) * out_s)
    o_ref[...] = merged.astype(o_ref.dtype)


def _merge(attn, y, gates, gate_col0, wa, ws, tm=512, tn=1024):
    t = attn.shape[0]
    nj = D_MODEL // tn
    assert gate_col0 % tn == 0
    g0 = gate_col0 // tn
    return pl.pallas_call(
        _merge_kernel,
        out_shape=jax.ShapeDtypeStruct((t, D_MODEL), BF16),
        grid=(nj, t // tm),
        in_specs=[pl.BlockSpec((tm, Q_DIM), lambda j, i: (i, 0)),
                  pl.BlockSpec((tm, D_INNER), lambda j, i: (i, 0)),
                  pl.BlockSpec((tm, tn), lambda j, i: (i, g0 + j)),
                  pl.BlockSpec((tm, tn), lambda j, i: (i, g0 + nj + j)),
                  pl.BlockSpec((Q_DIM, tn), lambda j, i: (0, j)),
                  pl.BlockSpec((D_INNER, tn), lambda j, i: (0, j))],
        out_specs=pl.BlockSpec((tm, tn), lambda j, i: (i, j)),
        scratch_shapes=[pltpu.VMEM((Q_DIM, tn), BF16),
                        pltpu.VMEM((D_INNER, tn), BF16)],
        compiler_params=_cparams(("arbitrary", "arbitrary")),
        name="merge",
    )(attn, y, gates, gates, wa, ws)


def _oproj_kernel(m_ref, x_ref, w_ref, g_ref, h_ref, f_ref, w16_ref):
    @pl.when(pl.program_id(0) == 0)
    def _():
        w16_ref[...] = w_ref[...].astype(BF16)

    h = x_ref[...] + _dot(m_ref[...], w16_ref[...])
    h_ref[...] = h
    f_ref[...] = _rms(h, g_ref[...], NORM_EPS).astype(BF16)


def _oproj(merged, x2, wo, g_ffn, tm=512):
    t = merged.shape[0]
    row = lambda i: (i, 0)
    return pl.pallas_call(
        _oproj_kernel,
        out_shape=(jax.ShapeDtypeStruct((t, D_MODEL), F32),
                   jax.ShapeDtypeStruct((t, D_MODEL), BF16)),
        grid=(t // tm,),
        in_specs=[pl.BlockSpec((tm, D_MODEL), row),
                  pl.BlockSpec((tm, D_MODEL), row),
                  _resident((D_MODEL, D_MODEL)),
                  _resident((1, D_MODEL))],
        out_specs=(pl.BlockSpec((tm, D_MODEL), row),
                   pl.BlockSpec((tm, D_MODEL), row)),
        scratch_shapes=[pltpu.VMEM((D_MODEL, D_MODEL), BF16)],
        compiler_params=_cparams(("arbitrary",)),
        name="oproj",
    )(merged, x2, wo, g_ffn)


def _ffn_up_kernel(f_ref, wg_ref, wu_ref, o_ref, wg16_ref, wu16_ref):
    @pl.when(pl.program_id(1) == 0)
    def _():
        wg16_ref[...] = wg_ref[...].astype(BF16)
        wu16_ref[...] = wu_ref[...].astype(BF16)

    half_r = o_ref.shape[0] // 2
    half_c = o_ref.shape[1] // 2
    for r in range(2):
        rs = slice(r * half_r, (r + 1) * half_r)
        f = f_ref[rs, :]
        for c in range(2):
            cs = slice(c * half_c, (c + 1) * half_c)
            o_ref[rs, cs] = (_silu(_dot(f, wg16_ref[:, cs]))
                             * _dot(f, wu16_ref[:, cs])).astype(o_ref.dtype)


def _ffn_up(f, wg, wu, tm=2048, th=512):
    t = f.shape[0]
    return pl.pallas_call(
        _ffn_up_kernel,
        out_shape=jax.ShapeDtypeStruct((t, FFN_HIDDEN), BF16),
        grid=(FFN_HIDDEN // th, t // tm),
        in_specs=[pl.BlockSpec((tm, D_MODEL), lambda j, i: (i, 0)),
                  pl.BlockSpec((D_MODEL, th), lambda j, i: (0, j)),
                  pl.BlockSpec((D_MODEL, th), lambda j, i: (0, j))],
        out_specs=pl.BlockSpec((tm, th), lambda j, i: (i, j)),
        scratch_shapes=[pltpu.VMEM((D_MODEL, th), BF16),
                        pltpu.VMEM((D_MODEL, th), BF16)],
        compiler_params=_cparams(("arbitrary", "arbitrary")),
        name="ffn_up",
    )(f, wg, wu)


def _ffn_down_kernel(a_ref, h_ref, wd_hbm, o_ref, wd16_ref, wf32_ref, sem, *, n_col_tiles):
    j = pl.program_id(0)
    i = pl.program_id(1)
    tn = wf32_ref.shape[1]

    def copy(col_tile):
        return pltpu.make_async_copy(wd_hbm.at[:, pl.ds(col_tile * tn, tn)], wf32_ref, sem)

    @pl.when((j == 0) & (i == 0))
    def _():
        copy(0).start()

    @pl.when(i == 0)
    def _():
        copy(j).wait()
        wd16_ref[...] = wf32_ref[...].astype(BF16)

    @pl.when((i == 0) & (j + 1 < n_col_tiles))
    def _():
        copy(j + 1).start()

    o_ref[...] = h_ref[...] + _dot(a_ref[...], wd16_ref[...])


def _ffn_down(act, h1, wd, tm=512, tn=1024):
    t = act.shape[0]
    n_col_tiles = D_MODEL // tn
    return pl.pallas_call(
        partial(_ffn_down_kernel, n_col_tiles=n_col_tiles),
        out_shape=jax.ShapeDtypeStruct((t, D_MODEL), F32),
        grid=(n_col_tiles, t // tm),
        in_specs=[pl.BlockSpec((tm, FFN_HIDDEN), lambda j, i: (i, 0)),
                  pl.BlockSpec((tm, tn), lambda j, i: (i, j)),
                  pl.BlockSpec(memory_space=pl.ANY)],
        out_specs=pl.BlockSpec((tm, tn), lambda j, i: (i, j)),
        scratch_shapes=[pltpu.VMEM((FFN_HIDDEN, tn), BF16),
                        pltpu.VMEM((FFN_HIDDEN, tn), F32),
                        pltpu.SemaphoreType.DMA],
        compiler_params=_cparams(("arbitrary", "arbitrary")),
        name="ffn_down",
    )(act, h1, wd)


def _ple_kernel(h_ref, p_ref, gp_ref, gf_ref, wg_ref, wp_ref, o_ref, wg16_ref, wp16_ref):
    @pl.when(pl.program_id(0) == 0)
    def _():
        wg16_ref[...] = wg_ref[...].astype(BF16)
        wp16_ref[...] = wp_ref[...].astype(BF16)

    h = h_ref[...]
    r = _rms(h, gp_ref[...], NORM_EPS).astype(BF16)
    gate = _sigmoid(_dot(r, wg16_ref[...]))
    h3 = h + gate * _dot(p_ref[...].astype(BF16), wp16_ref[...])
    o_ref[...] = _rms(h3, gf_ref[...], NORM_EPS)


def _ple(h2, p2, g_ple, g_final, wpg, wpp, tm=512):
    t = h2.shape[0]
    row = lambda i: (i, 0)
    return pl.pallas_call(
        _ple_kernel,
        out_shape=jax.ShapeDtypeStruct((t, D_MODEL), F32),
        grid=(t // tm,),
        in_specs=[pl.BlockSpec((tm, D_MODEL), row),
                  pl.BlockSpec((tm, PLE_DIM), row),
                  _resident((1, D_MODEL)),
                  _resident((1, D_MODEL)),
                  _resident((D_MODEL, D_MODEL)),
                  _resident((PLE_DIM, D_MODEL))],
        out_specs=pl.BlockSpec((tm, D_MODEL), row),
        scratch_shapes=[pltpu.VMEM((D_MODEL, D_MODEL), BF16),
                        pltpu.VMEM((PLE_DIM, D_MODEL), BF16)],
        compiler_params=_cparams(("arbitrary",)),
        name="ple",
    )(h2, p2, g_ple, g_final, wpg, wpp)


def _rope_constants():
    half = HEAD_DIM // 2
    lane = np.arange(LANES)
    inv_freq = ROPE_THETA ** (-jnp.arange(half, dtype=F32) * 2.0 / HEAD_DIM)
    invf = inv_freq[lane % half][None, :]
    sgn = jnp.asarray(np.where((lane % HEAD_DIM) < half, -1.0, 1.0), F32)[None, :]
    return invf, sgn


def _expand_matrix():
    rows = np.arange(LANES)[:, None]
    cols = np.arange(D_INNER)[None, :]
    hit = (rows < 3 * SSM_HEADS) & ((rows % SSM_HEADS) == (cols // SSM_HEAD_DIM))
    return jnp.asarray(hit, BF16)


def _shift_matrix():
    rows = np.arange((CONV_WIDTH - 1) * CHUNK)[:, None]
    cols = np.arange(2 * CHUNK)[None, :]
    w, t = rows // CHUNK, rows % CHUNK
    return jnp.asarray(cols == CHUNK + t - (CONV_WIDTH - 1) + w, BF16)


def _pad_lanes(v):
    return jnp.pad(v.astype(F32), (0, LANES - v.shape[0]))[None, :]


def kernel(x, p, positions, g_mix, w_in, conv_w, conv_b, dt_bias, a_log, d_skip, g_ssd,
           sinks, w_attn_br, w_ssd_br, w_o, g_ffn, w_gate, w_up, w_down, g_ple,
           w_ple_gate, w_ple_proj, g_final):
    b, s, d = x.shape
    t = b * s
    assert d == D_MODEL and s % CHUNK == 0 and p.shape[0] == 1
    i = 0
    x2 = x.reshape(t, d)
    p2 = p[i].reshape(t, PLE_DIM)
    pos_c = jnp.repeat(positions.reshape(t // POS_PER_ROW, POS_PER_ROW), ROPE_FREQS, axis=1)

    o_z = Q_DIM + 2 * KV_DIM
    o_xbc = o_z + D_INNER
    o_dt = o_xbc + CONV_DIM
    o_ga = o_dt + SSM_HEADS
    wt = jnp.swapaxes(w_in[i], 0, 1)

    invf, sgn = _rope_constants()
    u, cos_t, sin_t, dtp, qkv = _prep(x2, g_mix[i][None, :], pos_c, invf, sgn, wt, o_dt, o_z)

    zxg = _inproj(u, wt, [(o_z, D_INNER + CONV_DIM), (o_ga, 2 * D_MODEL)], BF16,
                  INPROJ_ROW_TILE, INPROJ_COL_TILE, "proj_zxg")
    gate_col0 = D_INNER + CONV_DIM

    attn = _attention(qkv, cos_t, sin_t, sinks[i][None, :].astype(F32), b, s)

    y = _ssd(zxg, dtp, _shift_matrix(), conv_w[i], conv_b[i][None, :], _pad_lanes(dt_bias[i]),
             _pad_lanes(a_log[i]), jnp.repeat(d_skip[i].astype(F32), SSM_HEAD_DIM)[None, :],
             g_ssd[i][None, :], _expand_matrix(), b, s)

    merged = _merge(attn, y, zxg, gate_col0, w_attn_br[i], w_ssd_br[i])
    h1, f = _oproj(merged, x2, w_o[i], g_ffn[i][None, :])
    act = _ffn_up(f, w_gate[i], w_up[i])
    h2 = _ffn_down(act, h1, w_down[i])
    out = _ple(h2, p2, g_ple[i][None, :], g_final[None, :], w_ple_gate[i], w_ple_proj[i])
    return out.reshape(b, s, d)
```

```python
from functools import partial

import numpy as np
import jax
import jax.numpy as jnp
from jax import lax
from jax.experimental import pallas as pl
from jax.experimental.pallas import tpu as pltpu

F32 = jnp.float32
BF16 = jnp.bfloat16

D_MODEL = 2048
HEAD_DIM = 64
ATTN_HEADS = 16
KV_HEADS = 4
GROUP = ATTN_HEADS // KV_HEADS
Q_DIM = ATTN_HEADS * HEAD_DIM
KV_DIM = KV_HEADS * HEAD_DIM
ATTN_BLOCK = 128
ROPE_THETA = 10000.0
D_INNER = 2048
SSM_HEAD_DIM = 64
SSM_HEADS = 32
SSM_GROUPS = 4
HEADS_PER_GROUP = SSM_HEADS // SSM_GROUPS
D_STATE = 128
CONV_WIDTH = 4
CHUNK = 128
CONV_DIM = D_INNER + 2 * SSM_GROUPS * D_STATE
FFN_HIDDEN = 5632
PLE_DIM = 256
NORM_EPS = 1e-6
SSM_NORM_EPS = 1e-5

LANES = 128
SUBLANES = 8
VMEM_LIMIT_BYTES = 56 * 1024 * 1024

NEG_BIG = -1e30


def _cparams(semantics):
    return pltpu.CompilerParams(dimension_semantics=semantics,
                                vmem_limit_bytes=VMEM_LIMIT_BYTES)


def _resident(shape):
    return pl.BlockSpec(shape, lambda *_: (0,) * len(shape),
                        pipeline_mode=pl.Buffered(1))


def _rms(xf, g, eps):
    var = jnp.mean(xf * xf, axis=-1, keepdims=True)
    return xf * lax.rsqrt(var + eps) * g


NEG_LOG2E = -1.4426950408889634


def _sigmoid(x):
    return 1.0 / (1.0 + jnp.exp2(x * NEG_LOG2E))


def _silu(x):
    return x * _sigmoid(x)


def _dot(a, b):
    return jnp.dot(a, b, preferred_element_type=F32)


def _dot_nt(a, b):
    return lax.dot_general(a, b, (((1,), (1,)), ((), ())),
                           preferred_element_type=F32)


def _split3(v):
    v1 = v.astype(BF16)
    r1 = v - v1.astype(F32)
    v2 = r1.astype(BF16)
    v3 = (r1 - v2.astype(F32)).astype(BF16)
    return v1, v2, v3


ROPE_FREQS = HEAD_DIM // 2
POS_PER_ROW = LANES // ROPE_FREQS


def _prep_kernel(x_ref, g_ref, pos_ref, invf_ref, sgn_ref, wdt_ref, wqkv_ref,
                 u_ref, cos_ref, sin_ref, dt_ref, qkv_ref, wqkv16_ref):
    @pl.when(pl.program_id(0) == 0)
    def _():
        _cast_rows(wqkv16_ref, wqkv_ref, Q_DIM)

    u = _rms(x_ref[...], g_ref[...], NORM_EPS).astype(BF16)
    u_ref[...] = u
    qkv_ref[...] = _dot_nt(u, wqkv16_ref[...]).astype(qkv_ref.dtype)
    rows = pos_ref.shape[0]
    ang = pos_ref[...].astype(F32) * invf_ref[...]
    cos_c = jnp.cos(ang)
    sin_c = jnp.sin(ang)
    group = lax.broadcasted_iota(jnp.int32, ang.shape, 1) // ROPE_FREQS
    for q in range(POS_PER_ROW):
        for src, dst_ref, sign in ((cos_c, cos_ref, None), (sin_c, sin_ref, sgn_ref[...])):
            only = jnp.where(group == q, src, 0.0)
            rep = only
            for k in range(1, POS_PER_ROW):
                rep = rep + pltpu.roll(only, k * ROPE_FREQS, 1)
            if sign is not None:
                rep = rep * sign
            dst_ref[pl.ds(q, rows, stride=POS_PER_ROW), :] = rep
    dt_ref[...] = _dot_nt(u, wdt_ref[...].astype(BF16))


def _prep(x2, g_mix, pos_c, invf, sgn, wt, dt_row0, qkv_rows, tm=512):
    t = x2.shape[0]
    row = lambda i: (i, 0)
    assert dt_row0 % SUBLANES == 0 and tm % (POS_PER_ROW * SUBLANES) == 0
    return pl.pallas_call(
        _prep_kernel,
        out_shape=(jax.ShapeDtypeStruct((t, D_MODEL), BF16),
                   jax.ShapeDtypeStruct((t, LANES), F32),
                   jax.ShapeDtypeStruct((t, LANES), F32),
                   jax.ShapeDtypeStruct((t, LANES), F32),
                   jax.ShapeDtypeStruct((t, qkv_rows), BF16)),
        grid=(t // tm,),
        in_specs=[pl.BlockSpec((tm, D_MODEL), row),
                  _resident((1, D_MODEL)),
                  pl.BlockSpec((tm // POS_PER_ROW, LANES), row),
                  _resident((1, LANES)),
                  _resident((1, LANES)),
                  pl.BlockSpec((pl.Element(LANES), pl.Element(D_MODEL)),
                               lambda i: (dt_row0, 0), pipeline_mode=pl.Buffered(1)),
                  pl.BlockSpec((pl.Element(qkv_rows), pl.Element(D_MODEL)),
                               lambda i: (0, 0), pipeline_mode=pl.Buffered(1))],
        out_specs=(pl.BlockSpec((tm, D_MODEL), row),
                   pl.BlockSpec((tm, LANES), row),
                   pl.BlockSpec((tm, LANES), row),
                   pl.BlockSpec((tm, LANES), row),
                   pl.BlockSpec((tm, qkv_rows), row)),
        scratch_shapes=[pltpu.VMEM((qkv_rows, D_MODEL), BF16)],
        compiler_params=_cparams(("arbitrary",)),
        name="prep_qkv",
    )(x2, g_mix, pos_c, invf, sgn, wt, wt)


PAIR_ROWS = 2 * GROUP * HEAD_DIM
INPROJ_ROW_TILE = 2048
INPROJ_COL_TILE = 1024


def _paired_head_blocks():
    return [(g * 2 + sub, sub * GROUP + g) for g in range(GROUP) for sub in range(2)]


def _cast_rows(dst_ref, src_ref, paired_rows):
    rows = dst_ref.shape[0]
    assert paired_rows % PAIR_ROWS == 0 and paired_rows <= rows
    for base in range(0, paired_rows, PAIR_ROWS):
        for dst, src in _paired_head_blocks():
            dst_ref[base + dst * HEAD_DIM:base + (dst + 1) * HEAD_DIM, :] = (
                src_ref[base + src * HEAD_DIM:base + (src + 1) * HEAD_DIM, :].astype(BF16))
    if paired_rows < rows:
        dst_ref[paired_rows:rows, :] = src_ref[paired_rows:rows, :].astype(BF16)


def _inproj_kernel(a_ref, wt_ref, o_ref, wbf_ref):
    @pl.when(pl.program_id(1) == 0)
    def _():
        wbf_ref[...] = wt_ref[...].astype(BF16)

    o_ref[...] = _dot_nt(a_ref[...], wbf_ref[...]).astype(o_ref.dtype)


def _inproj(a, wt, segments, out_dtype, tm, tn, name):
    m, k = a.shape
    assert all(seg_n % tn == 0 and row0 % SUBLANES == 0 for row0, seg_n in segments)
    n = sum(seg_n for _, seg_n in segments)

    def weight_row(j):
        row, first_tile = None, 0
        for row0, seg_n in segments:
            here = row0 + (j - first_tile) * tn
            row = here if row is None else jnp.where(j >= first_tile, here, row)
            first_tile += seg_n // tn
        return pl.multiple_of(row, SUBLANES)

    return pl.pallas_call(
        _inproj_kernel,
        out_shape=jax.ShapeDtypeStruct((m, n), out_dtype),
        grid=(n // tn, m // tm),
        in_specs=[pl.BlockSpec((tm, k), lambda j, i: (i, 0)),
                  pl.BlockSpec((pl.Element(tn), pl.Element(k)), lambda j, i: (weight_row(j), 0))],
        out_specs=pl.BlockSpec((tm, tn), lambda j, i: (i, j)),
        scratch_shapes=[pltpu.VMEM((tn, k), BF16)],
        compiler_params=_cparams(("arbitrary", "arbitrary")),
        name=name,
    )(a, wt)


ATTN_STEP_BLOCKS = 2


def _attn_kernel(sinks_ref, q_ref, kc_ref, kp_ref, vc_ref, vp_ref,
                 cc_ref, sc_ref, cp_ref, sp_ref, o_ref):
    n = pl.program_id(1)
    blk = ATTN_BLOCK
    nblk = ATTN_STEP_BLOCKS
    lane = lax.broadcasted_iota(jnp.int32, (blk, LANES), 1)
    first_half = (lane % HEAD_DIM) < (HEAD_DIM // 2)
    low_head = lane < HEAD_DIM

    def rope(t, c, s):
        partner = jnp.where(first_half,
                            pltpu.roll(t, LANES - HEAD_DIM // 2, 1),
                            pltpu.roll(t, HEAD_DIM // 2, 1))
        return t * c + partner * s

    def rows(j):
        return slice(j * blk, (j + 1) * blk)

    cp, sp = cp_ref[...], sp_ref[...]
    cc = [cc_ref[rows(j), :] for j in range(nblk)]
    sc = [sc_ref[rows(j), :] for j in range(nblk)]

    qi = lax.broadcasted_iota(jnp.int32, (blk, 2 * blk), 0)
    kj = lax.broadcasted_iota(jnp.int32, (blk, 2 * blk), 1)
    rel = kj - qi
    band = (rel >= 1) & (rel <= blk)
    valid = [band & ((kj >= blk) | (n > 0))] + [band] * (nblk - 1)

    scale = HEAD_DIM ** -0.5
    n_half = KV_DIM // LANES
    rows_per_half = 2 * GROUP * blk
    s_rows, sink_rows, valid_rows, v_wins = [], [], [], []
    for a in range(n_half):
        ksl = slice(a * LANES, (a + 1) * LANES)
        k_blocks = [rope(kp_ref[:, ksl].astype(F32), cp, sp).astype(BF16)]
        v_blocks = [vp_ref[:, ksl]]
        for j in range(nblk):
            k_blocks.append(rope(kc_ref[rows(j), ksl].astype(F32), cc[j], sc[j]).astype(BF16))
            v_blocks.append(vc_ref[rows(j), ksl])
        for j in range(nblk):
            k_win = jnp.concatenate(k_blocks[j:j + 2], axis=0)
            v_wins.append(jnp.concatenate(v_blocks[j:j + 2], axis=0))
            qc = [rope(q_ref[rows(j), (a * GROUP + g) * LANES:(a * GROUP + g + 1) * LANES]
                       .astype(F32), cc[j] * scale, sc[j] * scale) for g in range(GROUP)]
            q_rows = []
            for sub in range(2):
                keep = low_head if sub == 0 else jnp.logical_not(low_head)
                for g in range(GROUP):
                    q_rows.append(jnp.where(keep, qc[g], 0.0).astype(BF16))
                    sink_rows.append(
                        jnp.full((blk, LANES), sinks_ref[0, (2 * a + sub) * GROUP + g], F32))
            s_rows.append(_dot_nt(jnp.concatenate(q_rows, axis=0), k_win))
            valid_rows.extend([valid[j]] * (2 * GROUP))
    s = jnp.where(jnp.concatenate(valid_rows, axis=0), jnp.concatenate(s_rows, axis=0), NEG_BIG)
    sink = jnp.concatenate(sink_rows, axis=0)
    s0, s1 = s[:, :LANES], s[:, LANES:]
    m = jnp.maximum(jnp.max(jnp.maximum(s0, s1), axis=-1, keepdims=True), sink)
    e0 = jnp.exp(s0 - m)
    e1 = jnp.exp(s1 - m)
    den = jnp.sum(e0 + e1, axis=-1, keepdims=True) + jnp.exp(sink - m)
    r = 1.0 / den
    p = jnp.concatenate([e0 * r, e1 * r], axis=1).astype(BF16)
    for a in range(n_half):
        for j in range(nblk):
            w = a * nblk + j
            pv = _dot(p[w * rows_per_half:(w + 1) * rows_per_half], v_wins[w])
            for g in range(GROUP):
                lo = pv[g * blk:(g + 1) * blk]
                hi = pv[(GROUP + g) * blk:(GROUP + g + 1) * blk]
                chunk = a * GROUP + g
                o_ref[rows(j), chunk * LANES:(chunk + 1) * LANES] = (
                    jnp.where(low_head, lo, hi).astype(o_ref.dtype))


def _attention(qkv, cos_t, sin_t, sinks, batch, seq):
    t = qkv.shape[0]
    step = ATTN_STEP_BLOCKS * ATTN_BLOCK
    assert seq % step == 0
    ns = seq // step
    kcol = Q_DIM // KV_DIM
    vcol = kcol + 1
    cur = lambda b, n: b * ns + n
    prev = lambda b, n: jnp.maximum((b * ns + n) * ATTN_STEP_BLOCKS - 1, 0)
    return pl.pallas_call(
        _attn_kernel,
        out_shape=jax.ShapeDtypeStruct((t, Q_DIM), BF16),
        grid=(batch, ns),
        in_specs=[pl.BlockSpec(memory_space=pltpu.SMEM),
                  pl.BlockSpec((step, Q_DIM), lambda b, n: (cur(b, n), 0)),
                  pl.BlockSpec((step, KV_DIM), lambda b, n: (cur(b, n), kcol)),
                  pl.BlockSpec((ATTN_BLOCK, KV_DIM), lambda b, n: (prev(b, n), kcol)),
                  pl.BlockSpec((step, KV_DIM), lambda b, n: (cur(b, n), vcol)),
                  pl.BlockSpec((ATTN_BLOCK, KV_DIM), lambda b, n: (prev(b, n), vcol)),
                  pl.BlockSpec((step, LANES), lambda b, n: (cur(b, n), 0)),
                  pl.BlockSpec((step, LANES), lambda b, n: (cur(b, n), 0)),
                  pl.BlockSpec((ATTN_BLOCK, LANES), lambda b, n: (prev(b, n), 0)),
                  pl.BlockSpec((ATTN_BLOCK, LANES), lambda b, n: (prev(b, n), 0))],
        out_specs=pl.BlockSpec((step, Q_DIM), lambda b, n: (cur(b, n), 0)),
        compiler_params=_cparams(("arbitrary", "arbitrary")),
        name="attention",
    )(sinks, qkv, qkv, qkv, qkv, qkv, cos_t, sin_t, cos_t, sin_t)


SSD_STRIP = 256


def _ssd_kernel(z_ref, xin_ref, bcin_ref, xin_prev_ref, bcin_prev_ref, dt_ref, shift_ref, cw_ref,
                cb_ref, dtb_ref, alog_ref, dskip_ref, g_ref, expand_ref, o_ref,
                state_ref, xs_ref, b16_ref, c16_ref, xdd_ref, xd16_ref, yz_ref):
    c = pl.program_id(1)
    L = CHUNK
    W = SSD_STRIP
    bc_dim = SSM_GROUPS * D_STATE

    @pl.when(c == 0)
    def _():
        state_ref[...] = jnp.zeros_like(state_ref)

    v = dt_ref[...] + dtb_ref[...]
    dt = jnp.maximum(v, 0.0) + jnp.log1p(jnp.exp(-jnp.abs(v)))
    a = dt * (jnp.exp(alog_ref[...]) * NEG_LOG2E)

    ri = lax.broadcasted_iota(jnp.int32, (L, L), 0)
    ci = lax.broadcasted_iota(jnp.int32, (L, L), 1)
    causal = ri >= ci
    tril = jnp.where(causal, 1.0, 0.0).astype(BF16)
    R = 2 * SUBLANES
    ones = jnp.ones((L, L), BF16)
    a1, a2, a3 = _split3(a)
    cs = _dot(tril, a1) + _dot(tril, a2) + _dot(tril, a3)
    tot = (_dot(ones, a1) + _dot(ones, a2) + _dot(ones, a3))[0:R]
    cs_t = cs.T[0:SSM_HEADS]

    lane = lax.broadcasted_iota(jnp.int32, (L, LANES), 1)

    def pack(q):
        ln = lax.broadcasted_iota(jnp.int32, q.shape, 1)
        q1, q2, q3 = _split3(q)
        packed = jnp.where(
            ln < SSM_HEADS, q1.astype(F32),
            jnp.where(ln < 2 * SSM_HEADS, pltpu.roll(q2.astype(F32), SSM_HEADS, 1),
                      jnp.where(ln < 3 * SSM_HEADS,
                                pltpu.roll(q3.astype(F32), 2 * SSM_HEADS, 1), 0.0)))
        return packed.astype(BF16)

    def tile_rows(v):
        return jnp.concatenate([v] * (L // SUBLANES), axis=0)

    pk_all = jnp.concatenate([pack(dt), pack(cs), pack(tot)], axis=0)
    pk_ct = pk_all[L:]

    n_shift = CONV_WIDTH - 1
    sel_col = lax.broadcasted_iota(jnp.int32, (n_shift * L, 2 * L), 1)
    sel = shift_ref[...]
    sel = jnp.where((sel_col < L) & (c == 0), jnp.zeros_like(sel), sel)
    for s in range(CONV_DIM // W):
        sl = slice(s * W, (s + 1) * W)
        if s * W < D_INNER:
            cur_ref, prev_ref, src = xin_ref, xin_prev_ref, sl
        else:
            cur_ref, prev_ref = bcin_ref, bcin_prev_ref
            src = slice(s * W - D_INNER, (s + 1) * W - D_INNER)
        x_cur = cur_ref[:, src]
        xwin = jnp.concatenate([prev_ref[:, src], x_cur], axis=0)
        taps = _dot(sel, xwin)
        acc = cb_ref[:, sl] + x_cur.astype(F32) * cw_ref[n_shift:CONV_WIDTH, sl]
        for w in range(n_shift):
            acc = acc + taps[w * L:(w + 1) * L] * cw_ref[w:w + 1, sl]
        xc = _silu(acc)
        if s * W < D_INNER:
            xs_ref[:, sl] = xc
            ex = _dot(pk_all, expand_ref[:, sl])
            xd = xc * ex[0:L]
            xd16_ref[:, sl] = xd.astype(BF16)
            tot_x = tile_rows(ex[2 * L:2 * L + SUBLANES])
            xdd_ref[:, sl] = (xd * jnp.exp2(tot_x - ex[L:2 * L])).astype(BF16)
        elif s * W < D_INNER + bc_dim:
            b16_ref[:, s * W - D_INNER:(s + 1) * W - D_INNER] = xc.astype(BF16)
        else:
            c16_ref[:, s * W - D_INNER - bc_dim:(s + 1) * W - D_INNER - bc_dim] = xc.astype(BF16)

    low_head = lane < SSM_HEAD_DIM
    heads_per_strip = W // SSM_HEAD_DIM
    strips_per_group = HEADS_PER_GROUP // heads_per_strip
    ssq = jnp.zeros((L, LANES), F32)

    for g in range(SSM_GROUPS):
        nsl = slice(g * D_STATE, (g + 1) * D_STATE)
        bg = b16_ref[:, nsl]
        cg = c16_ref[:, nsl]
        bg_t = bg.astype(F32).T.astype(BF16)
        cb = jnp.where(causal, _dot_nt(cg, bg), 0.0)
        for k in range(strips_per_group):
            s = g * strips_per_group + k
            sl = slice(s * W, (s + 1) * W)
            ex = _dot(pk_ct, expand_ref[:, sl])
            st_prev = state_ref[:, sl]
            y_off = _dot(cg, st_prev.astype(BF16))
            chunk_decay = tile_rows(jnp.exp2(ex[L:L + SUBLANES]))
            state_ref[:, sl] = st_prev * chunk_decay + _dot(bg_t, xdd_ref[:, sl])
            y_pairs = []
            for pair in range(W // LANES):
                psl = slice(s * W + pair * LANES, s * W + (pair + 1) * LANES)
                xp = xd16_ref[:, psl]
                mhs, xps = [], []
                for sub in range(2):
                    h = s * heads_per_strip + pair * 2 + sub
                    diff = cs[:, h:h + 1] - cs_t[h:h + 1, :]
                    decay = jnp.exp2(jnp.minimum(diff, 0.0))
                    mhs.append((cb * decay).astype(BF16))
                    keep = low_head if sub == 0 else jnp.logical_not(low_head)
                    xps.append(jnp.where(keep, xp, jnp.zeros_like(xp)))
                y_pairs.append(_dot(jnp.concatenate(mhs, axis=1), jnp.concatenate(xps, axis=0)))
            y = (jnp.concatenate(y_pairs, axis=1) + y_off * jnp.exp2(ex[0:L])
                 + dskip_ref[:, sl] * xs_ref[:, sl])
            yz = y * _silu(z_ref[:, sl].astype(F32))
            yz_ref[:, sl] = yz
            for pair in range(W // LANES):
                part = yz[:, pair * LANES:(pair + 1) * LANES]
                ssq = ssq + part * part

    var = jnp.sum(ssq, axis=-1, keepdims=True) * (1.0 / D_INNER)
    rs = lax.rsqrt(var + SSM_NORM_EPS)
    for s in range(D_INNER // W):
        sl = slice(s * W, (s + 1) * W)
        o_ref[:, sl] = (yz_ref[:, sl] * rs * g_ref[:, sl]).astype(o_ref.dtype)


def _ssd(zx, dtp, shift, conv_w, conv_b, dtb_p, alog_p, dskip_x, g_ssd, expand, batch, seq):
    t = zx.shape[0]
    nc = seq // CHUNK
    bc_dim = 2 * SSM_GROUPS * D_STATE
    row = lambda b, c: (b * nc + c, 0)
    cur = lambda b, c: b * nc + c
    prev = lambda b, c: jnp.maximum(b * nc + c - 1, 0)
    x_col = 1
    bc_col = 2 * D_INNER // bc_dim
    return pl.pallas_call(
        _ssd_kernel,
        out_shape=jax.ShapeDtypeStruct((t, D_INNER), BF16),
        grid=(batch, nc),
        in_specs=[pl.BlockSpec((CHUNK, D_INNER), row),
                  pl.BlockSpec((CHUNK, D_INNER), lambda b, c: (cur(b, c), x_col)),
                  pl.BlockSpec((CHUNK, bc_dim), lambda b, c: (cur(b, c), bc_col)),
                  pl.BlockSpec((CHUNK, D_INNER), lambda b, c: (prev(b, c), x_col)),
                  pl.BlockSpec((CHUNK, bc_dim), lambda b, c: (prev(b, c), bc_col)),
                  pl.BlockSpec((CHUNK, LANES), row),
                  _resident(((CONV_WIDTH - 1) * CHUNK, 2 * CHUNK)),
                  _resident((CONV_WIDTH, CONV_DIM)),
                  _resident((1, CONV_DIM)),
                  _resident((1, LANES)),
                  _resident((1, LANES)),
                  _resident((1, D_INNER)),
                  _resident((1, D_INNER)),
                  _resident((LANES, D_INNER))],
        out_specs=pl.BlockSpec((CHUNK, D_INNER), row),
        scratch_shapes=[pltpu.VMEM((D_STATE, D_INNER), F32),
                        pltpu.VMEM((CHUNK, D_INNER), F32),
                        pltpu.VMEM((CHUNK, SSM_GROUPS * D_STATE), BF16),
                        pltpu.VMEM((CHUNK, SSM_GROUPS * D_STATE), BF16),
                        pltpu.VMEM((CHUNK, D_INNER), BF16),
                        pltpu.VMEM((CHUNK, D_INNER), BF16),
                        pltpu.VMEM((CHUNK, D_INNER), F32)],
        compiler_params=_cparams(("arbitrary", "arbitrary")),
        name="ssd",
    )(zx, zx, zx, zx, zx, dtp, shift, conv_w, conv_b, dtb_p, alog_p, dskip_x, g_ssd, expand)


def _merge_kernel(attn_ref, y_ref, ga_ref, gs_ref, wa_ref, ws_ref, o_ref, wa16_ref, ws16_ref):
    @pl.when(pl.program_id(1) == 0)
    def _():
        _cast_rows(wa16_ref, wa_ref, Q_DIM)
        ws16_ref[...] = ws_ref[...].astype(BF16)

    out_a = _dot(attn_ref[...], wa16_ref[...])
    out_s = _dot(y_ref[...], ws16_ref[...])
    merged = (_sigmoid(ga_ref[...].astype(F32)) * out_a
              + _sigmoid(gs_ref[...].astype(F32)) * out_s)
    o_ref[...] = merged.astype(o_ref.dtype)


def _merge(attn, y, gates, gate_col0, wa, ws, tm=512, tn=1024):
    t = attn.shape[0]
    nj = D_MODEL // tn
    assert gate_col0 % tn == 0
    g0 = gate_col0 // tn
    return pl.pallas_call(
        _merge_kernel,
        out_shape=jax.ShapeDtypeStruct((t, D_MODEL), BF16),
        grid=(nj, t // tm),
        in_specs=[pl.BlockSpec((tm, Q_DIM), lambda j, i: (i, 0)),
                  pl.BlockSpec((tm, D_INNER), lambda j, i: (i, 0)),
                  pl.BlockSpec((tm, tn), lambda j, i: (i, g0 + j)),
                  pl.BlockSpec((tm, tn), lambda j, i: (i, g0 + nj + j)),
                  pl.BlockSpec((Q_DIM, tn), lambda j, i: (0, j)),
                  pl.BlockSpec((D_INNER, tn), lambda j, i: (0, j))],
        out_specs=pl.BlockSpec((tm, tn), lambda j, i: (i, j)),
        scratch_shapes=[pltpu.VMEM((Q_DIM, tn), BF16),
                        pltpu.VMEM((D_INNER, tn), BF16)],
        compiler_params=_cparams(("arbitrary", "arbitrary")),
        name="merge",
    )(attn, y, gates, gates, wa, ws)


def _oproj_kernel(m_ref, x_ref, w_ref, g_ref, h_ref, f_ref, w16_ref):
    @pl.when(pl.program_id(0) == 0)
    def _():
        w16_ref[...] = w_ref[...].astype(BF16)

    h = x_ref[...] + _dot(m_ref[...], w16_ref[...])
    h_ref[...] = h
    f_ref[...] = _rms(h, g_ref[...], NORM_EPS).astype(BF16)


def _oproj(merged, x2, wo, g_ffn, tm=512):
    t = merged.shape[0]
    row = lambda i: (i, 0)
    return pl.pallas_call(
        _oproj_kernel,
        out_shape=(jax.ShapeDtypeStruct((t, D_MODEL), F32),
                   jax.ShapeDtypeStruct((t, D_MODEL), BF16)),
        grid=(t // tm,),
        in_specs=[pl.BlockSpec((tm, D_MODEL), row),
                  pl.BlockSpec((tm, D_MODEL), row),
                  _resident((D_MODEL, D_MODEL)),
                  _resident((1, D_MODEL))],
        out_specs=(pl.BlockSpec((tm, D_MODEL), row),
                   pl.BlockSpec((tm, D_MODEL), row)),
        scratch_shapes=[pltpu.VMEM((D_MODEL, D_MODEL), BF16)],
        compiler_params=_cparams(("arbitrary",)),
        name="oproj",
    )(merged, x2, wo, g_ffn)


def _ffn_up_kernel(f_ref, wg_ref, wu_ref, o_ref, wg16_ref, wu16_ref):
    @pl.when(pl.program_id(1) == 0)
    def _():
        wg16_ref[...] = wg_ref[...].astype(BF16)
        wu16_ref[...] = wu_ref[...].astype(BF16)

    half_r = o_ref.shape[0] // 2
    half_c = o_ref.shape[1] // 2
    for r in range(2):
        rs = slice(r * half_r, (r + 1) * half_r)
        f = f_ref[rs, :]
        for c in range(2):
            cs = slice(c * half_c, (c + 1) * half_c)
            o_ref[rs, cs] = (_silu(_dot(f, wg16_ref[:, cs]))
                             * _dot(f, wu16_ref[:, cs])).astype(o_ref.dtype)


def _ffn_up(f, wg, wu, tm=2048, th=512):
    t = f.shape[0]
    return pl.pallas_call(
        _ffn_up_kernel,
        out_shape=jax.ShapeDtypeStruct((t, FFN_HIDDEN), BF16),
        grid=(FFN_HIDDEN // th, t // tm),
        in_specs=[pl.BlockSpec((tm, D_MODEL), lambda j, i: (i, 0)),
                  pl.BlockSpec((D_MODEL, th), lambda j, i: (0, j)),
                  pl.BlockSpec((D_MODEL, th), lambda j, i: (0, j))],
        out_specs=pl.BlockSpec((tm, th), lambda j, i: (i, j)),
        scratch_shapes=[pltpu.VMEM((D_MODEL, th), BF16),
                        pltpu.VMEM((D_MODEL, th), BF16)],
        compiler_params=_cparams(("arbitrary", "arbitrary")),
        name="ffn_up",
    )(f, wg, wu)


def _ffn_down_kernel(a_ref, h_ref, wd_hbm, o_ref, wd16_ref, wf32_ref, sem, *, n_col_tiles):
    j = pl.program_id(0)
    i = pl.program_id(1)
    tn = wf32_ref.shape[1]

    def copy(col_tile):
        return pltpu.make_async_copy(wd_hbm.at[:, pl.ds(col_tile * tn, tn)], wf32_ref, sem)

    @pl.when((j == 0) & (i == 0))
    def _():
        copy(0).start()

    @pl.when(i == 0)
    def _():
        copy(j).wait()
        wd16_ref[...] = wf32_ref[...].astype(BF16)

    @pl.when((i == 0) & (j + 1 < n_col_tiles))
    def _():
        copy(j + 1).start()

    o_ref[...] = h_ref[...] + _dot(a_ref[...], wd16_ref[...])


def _ffn_down(act, h1, wd, tm=512, tn=1024):
    t = act.shape[0]
    n_col_tiles = D_MODEL // tn
    return pl.pallas_call(
        partial(_ffn_down_kernel, n_col_tiles=n_col_tiles),
        out_shape=jax.ShapeDtypeStruct((t, D_MODEL), F32),
        grid=(n_col_tiles, t // tm),
        in_specs=[pl.BlockSpec((tm, FFN_HIDDEN), lambda j, i: (i, 0)),
                  pl.BlockSpec((tm, tn), lambda j, i: (i, j)),
                  pl.BlockSpec(memory_space=pl.ANY)],
        out_specs=pl.BlockSpec((tm, tn), lambda j, i: (i, j)),
        scratch_shapes=[pltpu.VMEM((FFN_HIDDEN, tn), BF16),
                        pltpu.VMEM((FFN_HIDDEN, tn), F32),
                        pltpu.SemaphoreType.DMA],
        compiler_params=_cparams(("arbitrary", "arbitrary")),
        name="ffn_down",
    )(act, h1, wd)


def _ple_kernel(h_ref, p_ref, gp_ref, gf_ref, wg_ref, wp_ref, o_ref, wg16_ref, wp16_ref):
    @pl.when(pl.program_id(0) == 0)
    def _():
        wg16_ref[...] = wg_ref[...].astype(BF16)
        wp16_ref[...] = wp_ref[...].astype(BF16)

    h = h_ref[...]
    r = _rms(h, gp_ref[...], NORM_EPS).astype(BF16)
    gate = _sigmoid(_dot(r, wg16_ref[...]))
    h3 = h + gate * _dot(p_ref[...].astype(BF16), wp16_ref[...])
    o_ref[...] = _rms(h3, gf_ref[...], NORM_EPS)


def _ple(h2, p2, g_ple, g_final, wpg, wpp, tm=512):
    t = h2.shape[0]
    row = lambda i: (i, 0)
    return pl.pallas_call(
        _ple_kernel,
        out_shape=jax.ShapeDtypeStruct((t, D_MODEL), F32),
        grid=(t // tm,),
        in_specs=[pl.BlockSpec((tm, D_MODEL), row),
                  pl.BlockSpec((tm, PLE_DIM), row),
                  _resident((1, D_MODEL)),
                  _resident((1, D_MODEL)),
                  _resident((D_MODEL, D_MODEL)),
                  _resident((PLE_DIM, D_MODEL))],
        out_specs=pl.BlockSpec((tm, D_MODEL), row),
        scratch_shapes=[pltpu.VMEM((D_MODEL, D_MODEL), BF16),
                        pltpu.VMEM((PLE_DIM, D_MODEL), BF16)],
        compiler_params=_cparams(("arbitrary",)),
        name="ple",
    )(h2, p2, g_ple, g_final, wpg, wpp)


def _rope_constants():
    half = HEAD_DIM // 2
    lane = np.arange(LANES)
    inv_freq = ROPE_THETA ** (-jnp.arange(half, dtype=F32) * 2.0 / HEAD_DIM)
    invf = inv_freq[lane % half][None, :]
    sgn = jnp.asarray(np.where((lane % HEAD_DIM) < half, -1.0, 1.0), F32)[None, :]
    return invf, sgn


def _expand_matrix():
    rows = np.arange(LANES)[:, None]
    cols = np.arange(D_INNER)[None, :]
    hit = (rows < 3 * SSM_HEADS) & ((rows % SSM_HEADS) == (cols // SSM_HEAD_DIM))
    return jnp.asarray(hit, BF16)


def _shift_matrix():
    rows = np.arange((CONV_WIDTH - 1) * CHUNK)[:, None]
    cols = np.arange(2 * CHUNK)[None, :]
    w, t = rows // CHUNK, rows % CHUNK
    return jnp.asarray(cols == CHUNK + t - (CONV_WIDTH - 1) + w, BF16)


def _pad_lanes(v):
    return jnp.pad(v.astype(F32), (0, LANES - v.shape[0]))[None, :]


def kernel(x, p, positions, g_mix, w_in, conv_w, conv_b, dt_bias, a_log, d_skip, g_ssd,
           sinks, w_attn_br, w_ssd_br, w_o, g_ffn, w_gate, w_up, w_down, g_ple,
           w_ple_gate, w_ple_proj, g_final):
    b, s, d = x.shape
    t = b * s
    assert d == D_MODEL and s % CHUNK == 0 and p.shape[0] == 1
    i = 0
    x2 = x.reshape(t, d)
    p2 = p[i].reshape(t, PLE_DIM)
    pos_c = jnp.repeat(positions.reshape(t // POS_PER_ROW, POS_PER_ROW), ROPE_FREQS, axis=1)

    o_z = Q_DIM + 2 * KV_DIM
    o_xbc = o_z + D_INNER
    o_dt = o_xbc + CONV_DIM
    o_ga = o_dt + SSM_HEADS
    wt = jnp.swapaxes(w_in[i], 0, 1)

    invf, sgn = _rope_constants()
    u, cos_t, sin_t, dtp, qkv = _prep(x2, g_mix[i][None, :], pos_c, invf, sgn, wt, o_dt, o_z)

    zxg = _inproj(u, wt, [(o_z, D_INNER + CONV_DIM), (o_ga, 2 * D_MODEL)], BF16,
                  INPROJ_ROW_TILE, INPROJ_COL_TILE, "proj_zxg")
    gate_col0 = D_INNER + CONV_DIM

    attn = _attention(qkv, cos_t, sin_t, sinks[i][None, :].astype(F32), b, s)

    y = _ssd(zxg, dtp, _shift_matrix(), conv_w[i], conv_b[i][None, :], _pad_lanes(dt_bias[i]),
             _pad_lanes(a_log[i]), jnp.repeat(d_skip[i].astype(F32), SSM_HEAD_DIM)[None, :],
             g_ssd[i][None, :], _expand_matrix(), b, s)

    merged = _merge(attn, y, zxg, gate_col0, w_attn_br[i], w_ssd_br[i])
    h1, f = _oproj(merged, x2, w_o[i], g_ffn[i][None, :])
    act = _ffn_up(f, w_gate[i], w_up[i])
    h2 = _ffn_down(act, h1, w_down[i])
    out = _ple(h2, p2, g_ple[i][None, :], g_final[None, :], w_ple_gate[i], w_ple_proj[i])
    return out.reshape(b, s, d)
```

```python
from functools import partial

import numpy as np
import jax
import jax.numpy as jnp
from jax import lax
from jax.experimental import pallas as pl
from jax.experimental.pallas import tpu as pltpu

F32 = jnp.float32
BF16 = jnp.bfloat16

D_MODEL = 2048
HEAD_DIM = 64
ATTN_HEADS = 16
KV_HEADS = 4
GROUP = ATTN_HEADS // KV_HEADS
Q_DIM = ATTN_HEADS * HEAD_DIM
KV_DIM = KV_HEADS * HEAD_DIM
ATTN_BLOCK = 128
ROPE_THETA = 10000.0
D_INNER = 2048
SSM_HEAD_DIM = 64
SSM_HEADS = 32
SSM_GROUPS = 4
HEADS_PER_GROUP = SSM_HEADS // SSM_GROUPS
D_STATE = 128
CONV_WIDTH = 4
CHUNK = 128
CONV_DIM = D_INNER + 2 * SSM_GROUPS * D_STATE
FFN_HIDDEN = 5632
PLE_DIM = 256
NORM_EPS = 1e-6
SSM_NORM_EPS = 1e-5

LANES = 128
SUBLANES = 8
VMEM_LIMIT_BYTES = 56 * 1024 * 1024

NEG_BIG = -1e30


def _cparams(semantics):
    return pltpu.CompilerParams(dimension_semantics=semantics,
                                vmem_limit_bytes=VMEM_LIMIT_BYTES)


def _resident(shape):
    return pl.BlockSpec(shape, lambda *_: (0,) * len(shape),
                        pipeline_mode=pl.Buffered(1))


def _rms(xf, g, eps):
    var = jnp.mean(xf * xf, axis=-1, keepdims=True)
    return xf * lax.rsqrt(var + eps) * g


NEG_LOG2E = -1.4426950408889634


def _sigmoid(x):
    return 1.0 / (1.0 + jnp.exp2(x * NEG_LOG2E))


def _silu(x):
    return x * _sigmoid(x)


def _dot(a, b):
    return jnp.dot(a, b, preferred_element_type=F32)


def _dot_nt(a, b):
    return lax.dot_general(a, b, (((1,), (1,)), ((), ())),
                           preferred_element_type=F32)


def _split3(v):
    v1 = v.astype(BF16)
    r1 = v - v1.astype(F32)
    v2 = r1.astype(BF16)
    v3 = (r1 - v2.astype(F32)).astype(BF16)
    return v1, v2, v3


ROPE_FREQS = HEAD_DIM // 2
POS_PER_ROW = LANES // ROPE_FREQS


def _prep_kernel(x_ref, g_ref, pos_ref, invf_ref, sgn_ref, wdt_ref, wqkv_ref,
                 u_ref, cos_ref, sin_ref, dt_ref, qkv_ref, wqkv16_ref):
    @pl.when(pl.program_id(0) == 0)
    def _():
        _cast_rows(wqkv16_ref, wqkv_ref, Q_DIM)

    u = _rms(x_ref[...], g_ref[...], NORM_EPS).astype(BF16)
    u_ref[...] = u
    qkv_ref[...] = _dot_nt(u, wqkv16_ref[...]).astype(qkv_ref.dtype)
    rows = pos_ref.shape[0]
    ang = pos_ref[...].astype(F32) * invf_ref[...]
    cos_c = jnp.cos(ang)
    sin_c = jnp.sin(ang)
    group = lax.broadcasted_iota(jnp.int32, ang.shape, 1) // ROPE_FREQS
    for q in range(POS_PER_ROW):
        for src, dst_ref, sign in ((cos_c, cos_ref, None), (sin_c, sin_ref, sgn_ref[...])):
            only = jnp.where(group == q, src, 0.0)
            rep = only
            for k in range(1, POS_PER_ROW):
                rep = rep + pltpu.roll(only, k * ROPE_FREQS, 1)
            if sign is not None:
                rep = rep * sign
            dst_ref[pl.ds(q, rows, stride=POS_PER_ROW), :] = rep
    dt_ref[...] = _dot_nt(u, wdt_ref[...].astype(BF16))


def _prep(x2, g_mix, pos_c, invf, sgn, wt, dt_row0, qkv_rows, tm=512):
    t = x2.shape[0]
    row = lambda i: (i, 0)
    assert dt_row0 % SUBLANES == 0 and tm % (POS_PER_ROW * SUBLANES) == 0
    return pl.pallas_call(
        _prep_kernel,
        out_shape=(jax.ShapeDtypeStruct((t, D_MODEL), BF16),
                   jax.ShapeDtypeStruct((t, LANES), F32),
                   jax.ShapeDtypeStruct((t, LANES), F32),
                   jax.ShapeDtypeStruct((t, LANES), F32),
                   jax.ShapeDtypeStruct((t, qkv_rows), BF16)),
        grid=(t // tm,),
        in_specs=[pl.BlockSpec((tm, D_MODEL), row),
                  _resident((1, D_MODEL)),
                  pl.BlockSpec((tm // POS_PER_ROW, LANES), row),
                  _resident((1, LANES)),
                  _resident((1, LANES)),
                  pl.BlockSpec((pl.Element(LANES), pl.Element(D_MODEL)),
                               lambda i: (dt_row0, 0), pipeline_mode=pl.Buffered(1)),
                  pl.BlockSpec((pl.Element(qkv_rows), pl.Element(D_MODEL)),
                               lambda i: (0, 0), pipeline_mode=pl.Buffered(1))],
        out_specs=(pl.BlockSpec((tm, D_MODEL), row),
                   pl.BlockSpec((tm, LANES), row),
                   pl.BlockSpec((tm, LANES), row),
                   pl.BlockSpec((tm, LANES), row),
                   pl.BlockSpec((tm, qkv_rows), row)),
        scratch_shapes=[pltpu.VMEM((qkv_rows, D_MODEL), BF16)],
        compiler_params=_cparams(("arbitrary",)),
        name="prep_qkv",
    )(x2, g_mix, pos_c, invf, sgn, wt, wt)


PAIR_ROWS = 2 * GROUP * HEAD_DIM
INPROJ_ROW_TILE = 2048
INPROJ_COL_TILE = 1024


def _paired_head_blocks():
    return [(g * 2 + sub, sub * GROUP + g) for g in range(GROUP) for sub in range(2)]


def _cast_rows(dst_ref, src_ref, paired_rows):
    rows = dst_ref.shape[0]
    assert paired_rows % PAIR_ROWS == 0 and paired_rows <= rows
    for base in range(0, paired_rows, PAIR_ROWS):
        for dst, src in _paired_head_blocks():
            dst_ref[base + dst * HEAD_DIM:base + (dst + 1) * HEAD_DIM, :] = (
                src_ref[base + src * HEAD_DIM:base + (src + 1) * HEAD_DIM, :].astype(BF16))
    if paired_rows < rows:
        dst_ref[paired_rows:rows, :] = src_ref[paired_rows:rows, :].astype(BF16)


def _inproj_kernel(a_ref, wt_ref, o_ref, wbf_ref):
    @pl.when(pl.program_id(1) == 0)
    def _():
        wbf_ref[...] = wt_ref[...].astype(BF16)

    o_ref[...] = _dot_nt(a_ref[...], wbf_ref[...]).astype(o_ref.dtype)


def _inproj(a, wt, segments, out_dtype, tm, tn, name):
    m, k = a.shape
    assert all(seg_n % tn == 0 and row0 % SUBLANES == 0 for row0, seg_n in segments)
    n = sum(seg_n for _, seg_n in segments)

    def weight_row(j):
        row, first_tile = None, 0
        for row0, seg_n in segments:
            here = row0 + (j - first_tile) * tn
            row = here if row is None else jnp.where(j >= first_tile, here, row)
            first_tile += seg_n // tn
        return pl.multiple_of(row, SUBLANES)

    return pl.pallas_call(
        _inproj_kernel,
        out_shape=jax.ShapeDtypeStruct((m, n), out_dtype),
        grid=(n // tn, m // tm),
        in_specs=[pl.BlockSpec((tm, k), lambda j, i: (i, 0)),
                  pl.BlockSpec((pl.Element(tn), pl.Element(k)), lambda j, i: (weight_row(j), 0))],
        out_specs=pl.BlockSpec((tm, tn), lambda j, i: (i, j)),
        scratch_shapes=[pltpu.VMEM((tn, k), BF16)],
        compiler_params=_cparams(("arbitrary", "arbitrary")),
        name=name,
    )(a, wt)


ATTN_STEP_BLOCKS = 2


def _attn_kernel(sinks_ref, q_ref, kc_ref, kp_ref, vc_ref, vp_ref,
                 cc_ref, sc_ref, cp_ref, sp_ref, o_ref):
    n = pl.program_id(1)
    blk = ATTN_BLOCK
    nblk = ATTN_STEP_BLOCKS
    lane = lax.broadcasted_iota(jnp.int32, (blk, LANES), 1)
    first_half = (lane % HEAD_DIM) < (HEAD_DIM // 2)
    low_head = lane < HEAD_DIM

    def rope(t, c, s):
        partner = jnp.where(first_half,
                            pltpu.roll(t, LANES - HEAD_DIM // 2, 1),
                            pltpu.roll(t, HEAD_DIM // 2, 1))
        return t * c + partner * s

    def rows(j):
        return slice(j * blk, (j + 1) * blk)

    cp, sp = cp_ref[...], sp_ref[...]
    cc = [cc_ref[rows(j), :] for j in range(nblk)]
    sc = [sc_ref[rows(j), :] for j in range(nblk)]

    qi = lax.broadcasted_iota(jnp.int32, (blk, 2 * blk), 0)
    kj = lax.broadcasted_iota(jnp.int32, (blk, 2 * blk), 1)
    rel = kj - qi
    band = (rel >= 1) & (rel <= blk)
    valid = [band & ((kj >= blk) | (n > 0))] + [band] * (nblk - 1)

    scale = HEAD_DIM ** -0.5
    n_half = KV_DIM // LANES
    rows_per_half = 2 * GROUP * blk
    s_rows, sink_rows, valid_rows, v_wins = [], [], [], []
    for a in range(n_half):
        ksl = slice(a * LANES, (a + 1) * LANES)
        k_blocks = [rope(kp_ref[:, ksl].astype(F32), cp, sp).astype(BF16)]
        v_blocks = [vp_ref[:, ksl]]
        for j in range(nblk):
            k_blocks.append(rope(kc_ref[rows(j), ksl].astype(F32), cc[j], sc[j]).astype(BF16))
            v_blocks.append(vc_ref[rows(j), ksl])
        for j in range(nblk):
            k_win = jnp.concatenate(k_blocks[j:j + 2], axis=0)
            v_wins.append(jnp.concatenate(v_blocks[j:j + 2], axis=0))
            qc = [rope(q_ref[rows(j), (a * GROUP + g) * LANES:(a * GROUP + g + 1) * LANES]
                       .astype(F32), cc[j] * scale, sc[j] * scale) for g in range(GROUP)]
            q_rows = []
            for sub in range(2):
                keep = low_head if sub == 0 else jnp.logical_not(low_head)
                for g in range(GROUP):
                    q_rows.append(jnp.where(keep, qc[g], 0.0).astype(BF16))
                    sink_rows.append(
                        jnp.full((blk, LANES), sinks_ref[0, (2 * a + sub) * GROUP + g], F32))
            s_rows.append(_dot_nt(jnp.concatenate(q_rows, axis=0), k_win))
            valid_rows.extend([valid[j]] * (2 * GROUP))
    s = jnp.where(jnp.concatenate(valid_rows, axis=0), jnp.concatenate(s_rows, axis=0), NEG_BIG)
    sink = jnp.concatenate(sink_rows, axis=0)
    s0, s1 = s[:, :LANES], s[:, LANES:]
    m = jnp.maximum(jnp.max(jnp.maximum(s0, s1), axis=-1, keepdims=True), sink)
    e0 = jnp.exp(s0 - m)
    e1 = jnp.exp(s1 - m)
    den = jnp.sum(e0 + e1, axis=-1, keepdims=True) + jnp.exp(sink - m)
    r = 1.0 / den
    p = jnp.concatenate([e0 * r, e1 * r], axis=1).astype(BF16)
    for a in range(n_half):
        for j in range(nblk):
            w = a * nblk + j
            pv = _dot(p[w * rows_per_half:(w + 1) * rows_per_half], v_wins[w])
            for g in range(GROUP):
                lo = pv[g * blk:(g + 1) * blk]
                hi = pv[(GROUP + g) * blk:(GROUP + g + 1) * blk]
                chunk = a * GROUP + g
                o_ref[rows(j), chunk * LANES:(chunk + 1) * LANES] = (
                    jnp.where(low_head, lo, hi).astype(o_ref.dtype))


def _attention(qkv, cos_t, sin_t, sinks, batch, seq):
    t = qkv.shape[0]
    step = ATTN_STEP_BLOCKS * ATTN_BLOCK
    assert seq % step == 0
    ns = seq // step
    kcol = Q_DIM // KV_DIM
    vcol = kcol + 1
    cur = lambda b, n: b * ns + n
    prev = lambda b, n: jnp.maximum((b * ns + n) * ATTN_STEP_BLOCKS - 1, 0)
    return pl.pallas_call(
        _attn_kernel,
        out_shape=jax.ShapeDtypeStruct((t, Q_DIM), BF16),
        grid=(batch, ns),
        in_specs=[pl.BlockSpec(memory_space=pltpu.SMEM),
                  pl.BlockSpec((step, Q_DIM), lambda b, n: (cur(b, n), 0)),
                  pl.BlockSpec((step, KV_DIM), lambda b, n: (cur(b, n), kcol)),
                  pl.BlockSpec((ATTN_BLOCK, KV_DIM), lambda b, n: (prev(b, n), kcol)),
                  pl.BlockSpec((step, KV_DIM), lambda b, n: (cur(b, n), vcol)),
                  pl.BlockSpec((ATTN_BLOCK, KV_DIM), lambda b, n: (prev(b, n), vcol)),
                  pl.BlockSpec((step, LANES), lambda b, n: (cur(b, n), 0)),
                  pl.BlockSpec((step, LANES), lambda b, n: (cur(b, n), 0)),
                  pl.BlockSpec((ATTN_BLOCK, LANES), lambda b, n: (prev(b, n), 0)),
                  pl.BlockSpec((ATTN_BLOCK, LANES), lambda b, n: (prev(b, n), 0))],
        out_specs=pl.BlockSpec((step, Q_DIM), lambda b, n: (cur(b, n), 0)),
        compiler_params=_cparams(("arbitrary", "arbitrary")),
        name="attention",
    )(sinks, qkv, qkv, qkv, qkv, qkv, cos_t, sin_t, cos_t, sin_t)


SSD_STRIP = 256


def _ssd_kernel(z_ref, xin_ref, bcin_ref, xin_prev_ref, bcin_prev_ref, dt_ref, shift_ref, cw_ref,
                cb_ref, dtb_ref, alog_ref, dskip_ref, g_ref, expand_ref, o_ref,
                state_ref, xs_ref, b16_ref, c16_ref, xdd_ref, xd16_ref, yz_ref):
    c = pl.program_id(1)
    L = CHUNK
    W = SSD_STRIP
    bc_dim = SSM_GROUPS * D_STATE

    @pl.when(c == 0)
    def _():
        state_ref[...] = jnp.zeros_like(state_ref)

    v = dt_ref[...] + dtb_ref[...]
    dt = jnp.maximum(v, 0.0) + jnp.log1p(jnp.exp(-jnp.abs(v)))
    a = dt * (jnp.exp(alog_ref[...]) * NEG_LOG2E)

    ri = lax.broadcasted_iota(jnp.int32, (L, L), 0)
    ci = lax.broadcasted_iota(jnp.int32, (L, L), 1)
    causal = ri >= ci
    tril = jnp.where(causal, 1.0, 0.0).astype(BF16)
    R = 2 * SUBLANES
    ones = jnp.ones((L, L), BF16)
    a1, a2, a3 = _split3(a)
    cs = _dot(tril, a1) + _dot(tril, a2) + _dot(tril, a3)
    tot = (_dot(ones, a1) + _dot(ones, a2) + _dot(ones, a3))[0:R]
    cs_t = cs.T[0:SSM_HEADS]

    lane = lax.broadcasted_iota(jnp.int32, (L, LANES), 1)

    def pack(q):
        ln = lax.broadcasted_iota(jnp.int32, q.shape, 1)
        q1, q2, q3 = _split3(q)
        packed = jnp.where(
            ln < SSM_HEADS, q1.astype(F32),
            jnp.where(ln < 2 * SSM_HEADS, pltpu.roll(q2.astype(F32), SSM_HEADS, 1),
                      jnp.where(ln < 3 * SSM_HEADS,
                                pltpu.roll(q3.astype(F32), 2 * SSM_HEADS, 1), 0.0)))
        return packed.astype(BF16)

    def tile_rows(v):
        return jnp.concatenate([v] * (L // SUBLANES), axis=0)

    pk_all = jnp.concatenate([pack(dt), pack(cs), pack(tot)], axis=0)
    pk_ct = pk_all[L:]

    n_shift = CONV_WIDTH - 1
    sel_col = lax.broadcasted_iota(jnp.int32, (n_shift * L, 2 * L), 1)
    sel = shift_ref[...]
    sel = jnp.where((sel_col < L) & (c == 0), jnp.zeros_like(sel), sel)
    def conv_strip(s):
        sl = slice(s * W, (s + 1) * W)
        if s * W < D_INNER:
            cur_ref, prev_ref, src = xin_ref, xin_prev_ref, sl
        else:
            cur_ref, prev_ref = bcin_ref, bcin_prev_ref
            src = slice(s * W - D_INNER, (s + 1) * W - D_INNER)
        x_cur = cur_ref[:, src]
        xwin = jnp.concatenate([prev_ref[:, src], x_cur], axis=0)
        taps = _dot(sel, xwin)
        acc = cb_ref[:, sl] + x_cur.astype(F32) * cw_ref[n_shift:CONV_WIDTH, sl]
        for w in range(n_shift):
            acc = acc + taps[w * L:(w + 1) * L] * cw_ref[w:w + 1, sl]
        xc = _silu(acc)
        if s * W < D_INNER:
            xs_ref[:, sl] = xc
            ex = _dot(pk_all, expand_ref[:, sl])
            xd = xc * ex[0:L]
            xd16_ref[:, sl] = xd.astype(BF16)
            tot_x = tile_rows(ex[2 * L:2 * L + SUBLANES])
            xdd_ref[:, sl] = (xd * jnp.exp2(tot_x - ex[L:2 * L])).astype(BF16)
        elif s * W < D_INNER + bc_dim:
            b16_ref[:, s * W - D_INNER:(s + 1) * W - D_INNER] = xc.astype(BF16)
        else:
            c16_ref[:, s * W - D_INNER - bc_dim:(s + 1) * W - D_INNER - bc_dim] = xc.astype(BF16)

    low_head = lane < SSM_HEAD_DIM
    heads_per_strip = W // SSM_HEAD_DIM
    strips_per_group = HEADS_PER_GROUP // heads_per_strip
    ssq = jnp.zeros((L, LANES), F32)

    for s in range(D_INNER // W, CONV_DIM // W):
        conv_strip(s)

    for g in range(SSM_GROUPS):
        for k in range(strips_per_group):
            conv_strip(g * strips_per_group + k)
        nsl = slice(g * D_STATE, (g + 1) * D_STATE)
        bg = b16_ref[:, nsl]
        cg = c16_ref[:, nsl]
        bg_t = bg.astype(F32).T.astype(BF16)
        cb = jnp.where(causal, _dot_nt(cg, bg), 0.0)
        for k in range(strips_per_group):
            s = g * strips_per_group + k
            sl = slice(s * W, (s + 1) * W)
            ex = _dot(pk_ct, expand_ref[:, sl])
            st_prev = state_ref[:, sl]
            y_off = _dot(cg, st_prev.astype(BF16))
            chunk_decay = tile_rows(jnp.exp2(ex[L:L + SUBLANES]))
            state_ref[:, sl] = st_prev * chunk_decay + _dot(bg_t, xdd_ref[:, sl])
            y_pairs = []
            for pair in range(W // LANES):
                psl = slice(s * W + pair * LANES, s * W + (pair + 1) * LANES)
                xp = xd16_ref[:, psl]
                mhs, xps = [], []
                for sub in range(2):
                    h = s * heads_per_strip + pair * 2 + sub
                    diff = cs[:, h:h + 1] - cs_t[h:h + 1, :]
                    decay = jnp.exp2(jnp.minimum(diff, 0.0))
                    mhs.append((cb * decay).astype(BF16))
                    keep = low_head if sub == 0 else jnp.logical_not(low_head)
                    xps.append(jnp.where(keep, xp, jnp.zeros_like(xp)))
                y_pairs.append(_dot(jnp.concatenate(mhs, axis=1), jnp.concatenate(xps, axis=0)))
            y = (jnp.concatenate(y_pairs, axis=1) + y_off * jnp.exp2(ex[0:L])
                 + dskip_ref[:, sl] * xs_ref[:, sl])
            yz = y * _silu(z_ref[:, sl].astype(F32))
            yz_ref[:, sl] = yz
            for pair in range(W // LANES):
                part = yz[:, pair * LANES:(pair + 1) * LANES]
                ssq = ssq + part * part

    var = jnp.sum(ssq, axis=-1, keepdims=True) * (1.0 / D_INNER)
    rs = lax.rsqrt(var + SSM_NORM_EPS)
    for s in range(D_INNER // W):
        sl = slice(s * W, (s + 1) * W)
        o_ref[:, sl] = (yz_ref[:, sl] * rs * g_ref[:, sl]).astype(o_ref.dtype)


def _ssd(zx, dtp, shift, conv_w, conv_b, dtb_p, alog_p, dskip_x, g_ssd, expand, batch, seq):
    t = zx.shape[0]
    nc = seq // CHUNK
    bc_dim = 2 * SSM_GROUPS * D_STATE
    row = lambda b, c: (b * nc + c, 0)
    cur = lambda b, c: b * nc + c
    prev = lambda b, c: jnp.maximum(b * nc + c - 1, 0)
    x_col = 1
    bc_col = 2 * D_INNER // bc_dim
    return pl.pallas_call(
        _ssd_kernel,
        out_shape=jax.ShapeDtypeStruct((t, D_INNER), BF16),
        grid=(batch, nc),
        in_specs=[pl.BlockSpec((CHUNK, D_INNER), row),
                  pl.BlockSpec((CHUNK, D_INNER), lambda b, c: (cur(b, c), x_col)),
                  pl.BlockSpec((CHUNK, bc_dim), lambda b, c: (cur(b, c), bc_col)),
                  pl.BlockSpec((CHUNK, D_INNER), lambda b, c: (prev(b, c), x_col)),
                  pl.BlockSpec((CHUNK, bc_dim), lambda b, c: (prev(b, c), bc_col)),
                  pl.BlockSpec((CHUNK, LANES), row),
                  _resident(((CONV_WIDTH - 1) * CHUNK, 2 * CHUNK)),
                  _resident((CONV_WIDTH, CONV_DIM)),
                  _resident((1, CONV_DIM)),
                  _resident((1, LANES)),
                  _resident((1, LANES)),
                  _resident((1, D_INNER)),
                  _resident((1, D_INNER)),
                  _resident((LANES, D_INNER))],
        out_specs=pl.BlockSpec((CHUNK, D_INNER), row),
        scratch_shapes=[pltpu.VMEM((D_STATE, D_INNER), F32),
                        pltpu.VMEM((CHUNK, D_INNER), F32),
                        pltpu.VMEM((CHUNK, SSM_GROUPS * D_STATE), BF16),
                        pltpu.VMEM((CHUNK, SSM_GROUPS * D_STATE), BF16),
                        pltpu.VMEM((CHUNK, D_INNER), BF16),
                        pltpu.VMEM((CHUNK, D_INNER), BF16),
                        pltpu.VMEM((CHUNK, D_INNER), F32)],
        compiler_params=_cparams(("arbitrary", "arbitrary")),
        name="ssd",
    )(zx, zx, zx, zx, zx, dtp, shift, conv_w, conv_b, dtb_p, alog_p, dskip_x, g_ssd, expand)


def _merge_kernel(attn_ref, y_ref, ga_ref, gs_ref, wa_ref, ws_ref, o_ref, wa16_ref, ws16_ref):
    @pl.when(pl.program_id(1) == 0)
    def _():
        _cast_rows(wa16_ref, wa_ref, Q_DIM)
        ws16_ref[...] = ws_ref[...].astype(BF16)

    out_a = _dot(attn_ref[...], wa16_ref[...])
    out_s = _dot(y_ref[...], ws16_ref[...])
    merged = (_sigmoid(ga_ref[...].astype(F32)) * out_a
              + _sigmoid(gs_ref[...].astype(F32)) * out_s)
    o_ref[...] = merged.astype(o_ref.dtype)


def _merge(attn, y, gates, gate_col0, wa, ws, tm=512, tn=1024):
    t = attn.shape[0]
    nj = D_MODEL // tn
    assert gate_col0 % tn == 0
    g0 = gate_col0 // tn
    return pl.pallas_call(
        _merge_kernel,
        out_shape=jax.ShapeDtypeStruct((t, D_MODEL), BF16),
        grid=(nj, t // tm),
        in_specs=[pl.BlockSpec((tm, Q_DIM), lambda j, i: (i, 0)),
                  pl.BlockSpec((tm, D_INNER), lambda j, i: (i, 0)),
                  pl.BlockSpec((tm, tn), lambda j, i: (i, g0 + j)),
                  pl.BlockSpec((tm, tn), lambda j, i: (i, g0 + nj + j)),
                  pl.BlockSpec((Q_DIM, tn), lambda j, i: (0, j)),
                  pl.BlockSpec((D_INNER, tn), lambda j, i: (0, j))],
        out_specs=pl.BlockSpec((tm, tn), lambda j, i: (i, j)),
        scratch_shapes=[pltpu.VMEM((Q_DIM, tn), BF16),
                        pltpu.VMEM((D_INNER, tn), BF16)],
        compiler_params=_cparams(("arbitrary", "arbitrary")),
        name="merge",
    )(attn, y, gates, gates, wa, ws)


def _oproj_kernel(m_ref, x_ref, w_ref, g_ref, h_ref, f_ref, w16_ref):
    @pl.when(pl.program_id(0) == 0)
    def _():
        w16_ref[...] = w_ref[...].astype(BF16)

    h = x_ref[...] + _dot(m_ref[...], w16_ref[...])
    h_ref[...] = h
    f_ref[...] = _rms(h, g_ref[...], NORM_EPS).astype(BF16)


def _oproj(merged, x2, wo, g_ffn, tm=512):
    t = merged.shape[0]
    row = lambda i: (i, 0)
    return pl.pallas_call(
        _oproj_kernel,
        out_shape=(jax.ShapeDtypeStruct((t, D_MODEL), F32),
                   jax.ShapeDtypeStruct((t, D_MODEL), BF16)),
        grid=(t // tm,),
        in_specs=[pl.BlockSpec((tm, D_MODEL), row),
                  pl.BlockSpec((tm, D_MODEL), row),
                  _resident((D_MODEL, D_MODEL)),
                  _resident((1, D_MODEL))],
        out_specs=(pl.BlockSpec((tm, D_MODEL), row),
                   pl.BlockSpec((tm, D_MODEL), row)),
        scratch_shapes=[pltpu.VMEM((D_MODEL, D_MODEL), BF16)],
        compiler_params=_cparams(("arbitrary",)),
        name="oproj",
    )(merged, x2, wo, g_ffn)


def _ffn_up_kernel(f_ref, wg_ref, wu_ref, o_ref, wg16_ref, wu16_ref):
    @pl.when(pl.program_id(1) == 0)
    def _():
        wg16_ref[...] = wg_ref[...].astype(BF16)
        wu16_ref[...] = wu_ref[...].astype(BF16)

    half_r = o_ref.shape[0] // 2
    half_c = o_ref.shape[1] // 2
    for r in range(2):
        rs = slice(r * half_r, (r + 1) * half_r)
        f = f_ref[rs, :]
        for c in range(2):
            cs = slice(c * half_c, (c + 1) * half_c)
            o_ref[rs, cs] = (_silu(_dot(f, wg16_ref[:, cs]))
                             * _dot(f, wu16_ref[:, cs])).astype(o_ref.dtype)


def _ffn_up(f, wg, wu, tm=2048, th=512):
    t = f.shape[0]
    return pl.pallas_call(
        _ffn_up_kernel,
        out_shape=jax.ShapeDtypeStruct((t, FFN_HIDDEN), BF16),
        grid=(FFN_HIDDEN // th, t // tm),
        in_specs=[pl.BlockSpec((tm, D_MODEL), lambda j, i: (i, 0)),
                  pl.BlockSpec((D_MODEL, th), lambda j, i: (0, j)),
                  pl.BlockSpec((D_MODEL, th), lambda j, i: (0, j))],
        out_specs=pl.BlockSpec((tm, th), lambda j, i: (i, j)),
        scratch_shapes=[pltpu.VMEM((D_MODEL, th), BF16),
                        pltpu.VMEM((D_MODEL, th), BF16)],
        compiler_params=_cparams(("arbitrary", "arbitrary")),
        name="ffn_up",
    )(f, wg, wu)


def _ffn_down_kernel(a_ref, h_ref, wd_hbm, o_ref, wd16_ref, wf32_ref, sem, *, n_col_tiles):
    j = pl.program_id(0)
    i = pl.program_id(1)
    tn = wf32_ref.shape[1]

    def copy(col_tile):
        return pltpu.make_async_copy(wd_hbm.at[:, pl.ds(col_tile * tn, tn)], wf32_ref, sem)

    @pl.when((j == 0) & (i == 0))
    def _():
        copy(0).start()

    @pl.when(i == 0)
    def _():
        copy(j).wait()
        wd16_ref[...] = wf32_ref[...].astype(BF16)

    @pl.when((i == 0) & (j + 1 < n_col_tiles))
    def _():
        copy(j + 1).start()

    o_ref[...] = h_ref[...] + _dot(a_ref[...], wd16_ref[...])


def _ffn_down(act, h1, wd, tm=512, tn=1024):
    t = act.shape[0]
    n_col_tiles = D_MODEL // tn
    return pl.pallas_call(
        partial(_ffn_down_kernel, n_col_tiles=n_col_tiles),
        out_shape=jax.ShapeDtypeStruct((t, D_MODEL), F32),
        grid=(n_col_tiles, t // tm),
        in_specs=[pl.BlockSpec((tm, FFN_HIDDEN), lambda j, i: (i, 0)),
                  pl.BlockSpec((tm, tn), lambda j, i: (i, j)),
                  pl.BlockSpec(memory_space=pl.ANY)],
        out_specs=pl.BlockSpec((tm, tn), lambda j, i: (i, j)),
        scratch_shapes=[pltpu.VMEM((FFN_HIDDEN, tn), BF16),
                        pltpu.VMEM((FFN_HIDDEN, tn), F32),
                        pltpu.SemaphoreType.DMA],
        compiler_params=_cparams(("arbitrary", "arbitrary")),
        name="ffn_down",
    )(act, h1, wd)


def _ple_kernel(h_ref, p_ref, gp_ref, gf_ref, wg_ref, wp_ref, o_ref, wg16_ref, wp16_ref):
    @pl.when(pl.program_id(0) == 0)
    def _():
        wg16_ref[...] = wg_ref[...].astype(BF16)
        wp16_ref[...] = wp_ref[...].astype(BF16)

    h = h_ref[...]
    r = _rms(h, gp_ref[...], NORM_EPS).astype(BF16)
    gate = _sigmoid(_dot(r, wg16_ref[...]))
    h3 = h + gate * _dot(p_ref[...].astype(BF16), wp16_ref[...])
    o_ref[...] = _rms(h3, gf_ref[...], NORM_EPS)


def _ple(h2, p2, g_ple, g_final, wpg, wpp, tm=512):
    t = h2.shape[0]
    row = lambda i: (i, 0)
    return pl.pallas_call(
        _ple_kernel,
        out_shape=jax.ShapeDtypeStruct((t, D_MODEL), F32),
        grid=(t // tm,),
        in_specs=[pl.BlockSpec((tm, D_MODEL), row),
                  pl.BlockSpec((tm, PLE_DIM), row),
                  _resident((1, D_MODEL)),
                  _resident((1, D_MODEL)),
                  _resident((D_MODEL, D_MODEL)),
                  _resident((PLE_DIM, D_MODEL))],
        out_specs=pl.BlockSpec((tm, D_MODEL), row),
        scratch_shapes=[pltpu.VMEM((D_MODEL, D_MODEL), BF16),
                        pltpu.VMEM((PLE_DIM, D_MODEL), BF16)],
        compiler_params=_cparams(("arbitrary",)),
        name="ple",
    )(h2, p2, g_ple, g_final, wpg, wpp)


def _rope_constants():
    half = HEAD_DIM // 2
    lane = np.arange(LANES)
    inv_freq = ROPE_THETA ** (-jnp.arange(half, dtype=F32) * 2.0 / HEAD_DIM)
    invf = inv_freq[lane % half][None, :]
    sgn = jnp.asarray(np.where((lane % HEAD_DIM) < half, -1.0, 1.0), F32)[None, :]
    return invf, sgn


def _expand_matrix():
    rows = np.arange(LANES)[:, None]
    cols = np.arange(D_INNER)[None, :]
    hit = (rows < 3 * SSM_HEADS) & ((rows % SSM_HEADS) == (cols // SSM_HEAD_DIM))
    return jnp.asarray(hit, BF16)


def _shift_matrix():
    rows = np.arange((CONV_WIDTH - 1) * CHUNK)[:, None]
    cols = np.arange(2 * CHUNK)[None, :]
    w, t = rows // CHUNK, rows % CHUNK
    return jnp.asarray(cols == CHUNK + t - (CONV_WIDTH - 1) + w, BF16)


def _pad_lanes(v):
    return jnp.pad(v.astype(F32), (0, LANES - v.shape[0]))[None, :]


def kernel(x, p, positions, g_mix, w_in, conv_w, conv_b, dt_bias, a_log, d_skip, g_ssd,
           sinks, w_attn_br, w_ssd_br, w_o, g_ffn, w_gate, w_up, w_down, g_ple,
           w_ple_gate, w_ple_proj, g_final):
    b, s, d = x.shape
    t = b * s
    assert d == D_MODEL and s % CHUNK == 0 and p.shape[0] == 1
    i = 0
    x2 = x.reshape(t, d)
    p2 = p[i].reshape(t, PLE_DIM)
    pos_c = jnp.repeat(positions.reshape(t // POS_PER_ROW, POS_PER_ROW), ROPE_FREQS, axis=1)

    o_z = Q_DIM + 2 * KV_DIM
    o_xbc = o_z + D_INNER
    o_dt = o_xbc + CONV_DIM
    o_ga = o_dt + SSM_HEADS
    wt = jnp.swapaxes(w_in[i], 0, 1)

    invf, sgn = _rope_constants()
    u, cos_t, sin_t, dtp, qkv = _prep(x2, g_mix[i][None, :], pos_c, invf, sgn, wt, o_dt, o_z)

    zxg = _inproj(u, wt, [(o_z, D_INNER + CONV_DIM), (o_ga, 2 * D_MODEL)], BF16,
                  INPROJ_ROW_TILE, INPROJ_COL_TILE, "proj_zxg")
    gate_col0 = D_INNER + CONV_DIM

    attn = _attention(qkv, cos_t, sin_t, sinks[i][None, :].astype(F32), b, s)

    y = _ssd(zxg, dtp, _shift_matrix(), conv_w[i], conv_b[i][None, :], _pad_lanes(dt_bias[i]),
             _pad_lanes(a_log[i]), jnp.repeat(d_skip[i].astype(F32), SSM_HEAD_DIM)[None, :],
             g_ssd[i][None, :], _expand_matrix(), b, s)

    merged = _merge(attn, y, zxg, gate_col0, w_attn_br[i], w_ssd_br[i])
    h1, f = _oproj(merged, x2, w_o[i], g_ffn[i][None, :])
    act = _ffn_up(f, w_gate[i], w_up[i])
    h2 = _ffn_down(act, h1, w_down[i])
    out = _ple(h2, p2, g_ple[i][None, :], g_final[None, :], w_ple_gate[i], w_ple_proj[i])
    return out.reshape(b, s, d)
```

```python
from functools import partial

import numpy as np
import jax
import jax.numpy as jnp
from jax import lax
from jax.experimental import pallas as pl
from jax.experimental.pallas import tpu as pltpu

F32 = jnp.float32
BF16 = jnp.bfloat16

D_MODEL = 2048
HEAD_DIM = 64
ATTN_HEADS = 16
KV_HEADS = 4
GROUP = ATTN_HEADS // KV_HEADS
Q_DIM = ATTN_HEADS * HEAD_DIM
KV_DIM = KV_HEADS * HEAD_DIM
ATTN_BLOCK = 128
ROPE_THETA = 10000.0
D_INNER = 2048
SSM_HEAD_DIM = 64
SSM_HEADS = 32
SSM_GROUPS = 4
HEADS_PER_GROUP = SSM_HEADS // SSM_GROUPS
D_STATE = 128
CONV_WIDTH = 4
CHUNK = 128
CONV_DIM = D_INNER + 2 * SSM_GROUPS * D_STATE
FFN_HIDDEN = 5632
PLE_DIM = 256
NORM_EPS = 1e-6
SSM_NORM_EPS = 1e-5

LANES = 128
SUBLANES = 8
VMEM_LIMIT_BYTES = 56 * 1024 * 1024

NEG_BIG = -1e30


def _cparams(semantics):
    return pltpu.CompilerParams(dimension_semantics=semantics,
                                vmem_limit_bytes=VMEM_LIMIT_BYTES)


def _resident(shape):
    return pl.BlockSpec(shape, lambda *_: (0,) * len(shape),
                        pipeline_mode=pl.Buffered(1))


def _rms(xf, g, eps):
    var = jnp.mean(xf * xf, axis=-1, keepdims=True)
    return xf * lax.rsqrt(var + eps) * g


NEG_LOG2E = -1.4426950408889634


def _sigmoid(x):
    return 0.5 + 0.5 * jnp.tanh(0.5 * x)


def _silu(x):
    return x * _sigmoid(x)


def _dot(a, b):
    return jnp.dot(a, b, preferred_element_type=F32)


def _dot_nt(a, b):
    return lax.dot_general(a, b, (((1,), (1,)), ((), ())),
                           preferred_element_type=F32)


def _split3(v):
    v1 = v.astype(BF16)
    r1 = v - v1.astype(F32)
    v2 = r1.astype(BF16)
    v3 = (r1 - v2.astype(F32)).astype(BF16)
    return v1, v2, v3


ROPE_FREQS = HEAD_DIM // 2
POS_PER_ROW = LANES // ROPE_FREQS


def _prep_kernel(x_ref, g_ref, pos_ref, invf_ref, sgn_ref, wdt_ref, wqkv_ref,
                 u_ref, cos_ref, sin_ref, dt_ref, qkv_ref, wqkv16_ref):
    @pl.when(pl.program_id(0) == 0)
    def _():
        _cast_rows(wqkv16_ref, wqkv_ref, Q_DIM)

    u = _rms(x_ref[...], g_ref[...], NORM_EPS).astype(BF16)
    u_ref[...] = u
    qkv_ref[...] = _dot_nt(u, wqkv16_ref[...]).astype(qkv_ref.dtype)
    rows = pos_ref.shape[0]
    ang = pos_ref[...].astype(F32) * invf_ref[...]
    cos_c = jnp.cos(ang)
    sin_c = jnp.sin(ang)
    group = lax.broadcasted_iota(jnp.int32, ang.shape, 1) // ROPE_FREQS
    for q in range(POS_PER_ROW):
        for src, dst_ref, sign in ((cos_c, cos_ref, None), (sin_c, sin_ref, sgn_ref[...])):
            only = jnp.where(group == q, src, 0.0)
            rep = only
            for k in range(1, POS_PER_ROW):
                rep = rep + pltpu.roll(only, k * ROPE_FREQS, 1)
            if sign is not None:
                rep = rep * sign
            dst_ref[pl.ds(q, rows, stride=POS_PER_ROW), :] = rep
    dt_ref[...] = _dot_nt(u, wdt_ref[...].astype(BF16))


def _prep(x2, g_mix, pos_c, invf, sgn, wt, dt_row0, qkv_rows, tm=512):
    t = x2.shape[0]
    row = lambda i: (i, 0)
    assert dt_row0 % SUBLANES == 0 and tm % (POS_PER_ROW * SUBLANES) == 0
    return pl.pallas_call(
        _prep_kernel,
        out_shape=(jax.ShapeDtypeStruct((t, D_MODEL), BF16),
                   jax.ShapeDtypeStruct((t, LANES), F32),
                   jax.ShapeDtypeStruct((t, LANES), F32),
                   jax.ShapeDtypeStruct((t, LANES), F32),
                   jax.ShapeDtypeStruct((t, qkv_rows), BF16)),
        grid=(t // tm,),
        in_specs=[pl.BlockSpec((tm, D_MODEL), row),
                  _resident((1, D_MODEL)),
                  pl.BlockSpec((tm // POS_PER_ROW, LANES), row),
                  _resident((1, LANES)),
                  _resident((1, LANES)),
                  pl.BlockSpec((pl.Element(LANES), pl.Element(D_MODEL)),
                               lambda i: (dt_row0, 0), pipeline_mode=pl.Buffered(1)),
                  pl.BlockSpec((pl.Element(qkv_rows), pl.Element(D_MODEL)),
                               lambda i: (0, 0), pipeline_mode=pl.Buffered(1))],
        out_specs=(pl.BlockSpec((tm, D_MODEL), row),
                   pl.BlockSpec((tm, LANES), row),
                   pl.BlockSpec((tm, LANES), row),
                   pl.BlockSpec((tm, LANES), row),
                   pl.BlockSpec((tm, qkv_rows), row)),
        scratch_shapes=[pltpu.VMEM((qkv_rows, D_MODEL), BF16)],
        compiler_params=_cparams(("arbitrary",)),
        name="prep_qkv",
    )(x2, g_mix, pos_c, invf, sgn, wt, wt)


PAIR_ROWS = 2 * GROUP * HEAD_DIM
INPROJ_ROW_TILE = 2048
INPROJ_COL_TILE = 1024


def _paired_head_blocks():
    return [(g * 2 + sub, sub * GROUP + g) for g in range(GROUP) for sub in range(2)]


def _cast_rows(dst_ref, src_ref, paired_rows):
    rows = dst_ref.shape[0]
    assert paired_rows % PAIR_ROWS == 0 and paired_rows <= rows
    for base in range(0, paired_rows, PAIR_ROWS):
        for dst, src in _paired_head_blocks():
            dst_ref[base + dst * HEAD_DIM:base + (dst + 1) * HEAD_DIM, :] = (
                src_ref[base + src * HEAD_DIM:base + (src + 1) * HEAD_DIM, :].astype(BF16))
    if paired_rows < rows:
        dst_ref[paired_rows:rows, :] = src_ref[paired_rows:rows, :].astype(BF16)


def _inproj_kernel(a_ref, wt_ref, o_ref, wbf_ref):
    @pl.when(pl.program_id(1) == 0)
    def _():
        wbf_ref[...] = wt_ref[...].astype(BF16)

    o_ref[...] = _dot_nt(a_ref[...], wbf_ref[...]).astype(o_ref.dtype)


def _inproj(a, wt, segments, out_dtype, tm, tn, name):
    m, k = a.shape
    assert all(seg_n % tn == 0 and row0 % SUBLANES == 0 for row0, seg_n in segments)
    n = sum(seg_n for _, seg_n in segments)

    def weight_row(j):
        row, first_tile = None, 0
        for row0, seg_n in segments:
            here = row0 + (j - first_tile) * tn
            row = here if row is None else jnp.where(j >= first_tile, here, row)
            first_tile += seg_n // tn
        return pl.multiple_of(row, SUBLANES)

    return pl.pallas_call(
        _inproj_kernel,
        out_shape=jax.ShapeDtypeStruct((m, n), out_dtype),
        grid=(n // tn, m // tm),
        in_specs=[pl.BlockSpec((tm, k), lambda j, i: (i, 0)),
                  pl.BlockSpec((pl.Element(tn), pl.Element(k)), lambda j, i: (weight_row(j), 0))],
        out_specs=pl.BlockSpec((tm, tn), lambda j, i: (i, j)),
        scratch_shapes=[pltpu.VMEM((tn, k), BF16)],
        compiler_params=_cparams(("arbitrary", "arbitrary")),
        name=name,
    )(a, wt)


ATTN_STEP_BLOCKS = 2


def _attn_kernel(sinks_ref, q_ref, kc_ref, kp_ref, vc_ref, vp_ref,
                 cc_ref, sc_ref, cp_ref, sp_ref, o_ref):
    n = pl.program_id(1)
    blk = ATTN_BLOCK
    nblk = ATTN_STEP_BLOCKS
    lane = lax.broadcasted_iota(jnp.int32, (blk, LANES), 1)
    first_half = (lane % HEAD_DIM) < (HEAD_DIM // 2)
    low_head = lane < HEAD_DIM

    def rope(t, c, s):
        partner = jnp.where(first_half,
                            pltpu.roll(t, LANES - HEAD_DIM // 2, 1),
                            pltpu.roll(t, HEAD_DIM // 2, 1))
        return t * c + partner * s

    def rows(j):
        return slice(j * blk, (j + 1) * blk)

    cp, sp = cp_ref[...], sp_ref[...]
    cc = [cc_ref[rows(j), :] for j in range(nblk)]
    sc = [sc_ref[rows(j), :] for j in range(nblk)]

    qi = lax.broadcasted_iota(jnp.int32, (blk, 2 * blk), 0)
    kj = lax.broadcasted_iota(jnp.int32, (blk, 2 * blk), 1)
    rel = kj - qi
    band = (rel >= 1) & (rel <= blk)
    valid = [band & ((kj >= blk) | (n > 0))] + [band] * (nblk - 1)

    scale = HEAD_DIM ** -0.5
    n_half = KV_DIM // LANES
    rows_per_half = 2 * GROUP * blk
    s_rows, sink_rows, valid_rows, v_wins = [], [], [], []
    for a in range(n_half):
        ksl = slice(a * LANES, (a + 1) * LANES)
        k_blocks = [rope(kp_ref[:, ksl].astype(F32), cp, sp).astype(BF16)]
        v_blocks = [vp_ref[:, ksl]]
        for j in range(nblk):
            k_blocks.append(rope(kc_ref[rows(j), ksl].astype(F32), cc[j], sc[j]).astype(BF16))
            v_blocks.append(vc_ref[rows(j), ksl])
        for j in range(nblk):
            k_win = jnp.concatenate(k_blocks[j:j + 2], axis=0)
            v_wins.append(jnp.concatenate(v_blocks[j:j + 2], axis=0))
            qc = [rope(q_ref[rows(j), (a * GROUP + g) * LANES:(a * GROUP + g + 1) * LANES]
                       .astype(F32), cc[j] * scale, sc[j] * scale) for g in range(GROUP)]
            q_rows = []
            for sub in range(2):
                keep = low_head if sub == 0 else jnp.logical_not(low_head)
                for g in range(GROUP):
                    q_rows.append(jnp.where(keep, qc[g], 0.0).astype(BF16))
                    sink_rows.append(
                        jnp.full((blk, LANES), sinks_ref[0, (2 * a + sub) * GROUP + g], F32))
            s_rows.append(_dot_nt(jnp.concatenate(q_rows, axis=0), k_win))
            valid_rows.extend([valid[j]] * (2 * GROUP))
    s = jnp.where(jnp.concatenate(valid_rows, axis=0), jnp.concatenate(s_rows, axis=0), NEG_BIG)
    sink = jnp.concatenate(sink_rows, axis=0)
    s0, s1 = s[:, :LANES], s[:, LANES:]
    m = jnp.maximum(jnp.max(jnp.maximum(s0, s1), axis=-1, keepdims=True), sink)
    e0 = jnp.exp(s0 - m)
    e1 = jnp.exp(s1 - m)
    den = jnp.sum(e0 + e1, axis=-1, keepdims=True) + jnp.exp(sink - m)
    r = 1.0 / den
    p = jnp.concatenate([e0 * r, e1 * r], axis=1).astype(BF16)
    for a in range(n_half):
        for j in range(nblk):
            w = a * nblk + j
            pv = _dot(p[w * rows_per_half:(w + 1) * rows_per_half], v_wins[w])
            for g in range(GROUP):
                lo = pv[g * blk:(g + 1) * blk]
                hi = pv[(GROUP + g) * blk:(GROUP + g + 1) * blk]
                chunk = a * GROUP + g
                o_ref[rows(j), chunk * LANES:(chunk + 1) * LANES] = (
                    jnp.where(low_head, lo, hi).astype(o_ref.dtype))


def _attention(qkv, cos_t, sin_t, sinks, batch, seq):
    t = qkv.shape[0]
    step = ATTN_STEP_BLOCKS * ATTN_BLOCK
    assert seq % step == 0
    ns = seq // step
    kcol = Q_DIM // KV_DIM
    vcol = kcol + 1
    cur = lambda b, n: b * ns + n
    prev = lambda b, n: jnp.maximum((b * ns + n) * ATTN_STEP_BLOCKS - 1, 0)
    return pl.pallas_call(
        _attn_kernel,
        out_shape=jax.ShapeDtypeStruct((t, Q_DIM), BF16),
        grid=(batch, ns),
        in_specs=[pl.BlockSpec(memory_space=pltpu.SMEM),
                  pl.BlockSpec((step, Q_DIM), lambda b, n: (cur(b, n), 0)),
                  pl.BlockSpec((step, KV_DIM), lambda b, n: (cur(b, n), kcol)),
                  pl.BlockSpec((ATTN_BLOCK, KV_DIM), lambda b, n: (prev(b, n), kcol)),
                  pl.BlockSpec((step, KV_DIM), lambda b, n: (cur(b, n), vcol)),
                  pl.BlockSpec((ATTN_BLOCK, KV_DIM), lambda b, n: (prev(b, n), vcol)),
                  pl.BlockSpec((step, LANES), lambda b, n: (cur(b, n), 0)),
                  pl.BlockSpec((step, LANES), lambda b, n: (cur(b, n), 0)),
                  pl.BlockSpec((ATTN_BLOCK, LANES), lambda b, n: (prev(b, n), 0)),
                  pl.BlockSpec((ATTN_BLOCK, LANES), lambda b, n: (prev(b, n), 0))],
        out_specs=pl.BlockSpec((step, Q_DIM), lambda b, n: (cur(b, n), 0)),
        compiler_params=_cparams(("arbitrary", "arbitrary")),
        name="attention",
    )(sinks, qkv, qkv, qkv, qkv, qkv, cos_t, sin_t, cos_t, sin_t)


SSD_STRIP = 256


def _ssd_kernel(z_ref, xin_ref, bcin_ref, xin_prev_ref, bcin_prev_ref, dt_ref, shift_ref, cw_ref,
                cb_ref, dtb_ref, alog_ref, dskip_ref, g_ref, expand_ref, o_ref,
                state_ref, xs_ref, b16_ref, c16_ref, xdd_ref, xd16_ref, yz_ref):
    c = pl.program_id(1)
    L = CHUNK
    W = SSD_STRIP
    bc_dim = SSM_GROUPS * D_STATE

    @pl.when(c == 0)
    def _():
        state_ref[...] = jnp.zeros_like(state_ref)

    v = dt_ref[...] + dtb_ref[...]
    dt = jnp.maximum(v, 0.0) + jnp.log1p(jnp.exp(-jnp.abs(v)))
    a = dt * (jnp.exp(alog_ref[...]) * NEG_LOG2E)

    ri = lax.broadcasted_iota(jnp.int32, (L, L), 0)
    ci = lax.broadcasted_iota(jnp.int32, (L, L), 1)
    causal = ri >= ci
    tril = jnp.where(causal, 1.0, 0.0).astype(BF16)
    R = 2 * SUBLANES
    ones = jnp.ones((L, L), BF16)
    a1, a2, a3 = _split3(a)
    cs = _dot(tril, a1) + _dot(tril, a2) + _dot(tril, a3)
    tot = (_dot(ones, a1) + _dot(ones, a2) + _dot(ones, a3))[0:R]
    cs_t = cs.T[0:SSM_HEADS]

    lane = lax.broadcasted_iota(jnp.int32, (L, LANES), 1)

    def pack(q):
        ln = lax.broadcasted_iota(jnp.int32, q.shape, 1)
        q1, q2, q3 = _split3(q)
        packed = jnp.where(
            ln < SSM_HEADS, q1.astype(F32),
            jnp.where(ln < 2 * SSM_HEADS, pltpu.roll(q2.astype(F32), SSM_HEADS, 1),
                      jnp.where(ln < 3 * SSM_HEADS,
                                pltpu.roll(q3.astype(F32), 2 * SSM_HEADS, 1), 0.0)))
        return packed.astype(BF16)

    def tile_rows(v):
        return jnp.concatenate([v] * (L // SUBLANES), axis=0)

    pk_all = jnp.concatenate([pack(dt), pack(cs), pack(tot)], axis=0)
    pk_ct = pk_all[L:]

    n_shift = CONV_WIDTH - 1
    sel_col = lax.broadcasted_iota(jnp.int32, (n_shift * L, 2 * L), 1)
    sel = shift_ref[...]
    sel = jnp.where((sel_col < L) & (c == 0), jnp.zeros_like(sel), sel)
    def conv_strip(s):
        sl = slice(s * W, (s + 1) * W)
        if s * W < D_INNER:
            cur_ref, prev_ref, src = xin_ref, xin_prev_ref, sl
        else:
            cur_ref, prev_ref = bcin_ref, bcin_prev_ref
            src = slice(s * W - D_INNER, (s + 1) * W - D_INNER)
        x_cur = cur_ref[:, src]
        xwin = jnp.concatenate([prev_ref[:, src], x_cur], axis=0)
        taps = _dot(sel, xwin)
        acc = cb_ref[:, sl] + x_cur.astype(F32) * cw_ref[n_shift:CONV_WIDTH, sl]
        for w in range(n_shift):
            acc = acc + taps[w * L:(w + 1) * L] * cw_ref[w:w + 1, sl]
        xc = _silu(acc)
        if s * W < D_INNER:
            xs_ref[:, sl] = xc
            ex = _dot(pk_all, expand_ref[:, sl])
            xd = xc * ex[0:L]
            xd16_ref[:, sl] = xd.astype(BF16)
            tot_x = tile_rows(ex[2 * L:2 * L + SUBLANES])
            xdd_ref[:, sl] = (xd * jnp.exp2(tot_x - ex[L:2 * L])).astype(BF16)
        elif s * W < D_INNER + bc_dim:
            b16_ref[:, s * W - D_INNER:(s + 1) * W - D_INNER] = xc.astype(BF16)
        else:
            c16_ref[:, s * W - D_INNER - bc_dim:(s + 1) * W - D_INNER - bc_dim] = xc.astype(BF16)

    low_head = lane < SSM_HEAD_DIM
    heads_per_strip = W // SSM_HEAD_DIM
    strips_per_group = HEADS_PER_GROUP // heads_per_strip
    ssq = jnp.zeros((L, LANES), F32)

    for s in range(D_INNER // W, CONV_DIM // W):
        conv_strip(s)

    for g in range(SSM_GROUPS):
        for k in range(strips_per_group):
            conv_strip(g * strips_per_group + k)
        nsl = slice(g * D_STATE, (g + 1) * D_STATE)
        bg = b16_ref[:, nsl]
        cg = c16_ref[:, nsl]
        bg_t = bg.astype(F32).T.astype(BF16)
        cb = jnp.where(causal, _dot_nt(cg, bg), 0.0)
        for k in range(strips_per_group):
            s = g * strips_per_group + k
            sl = slice(s * W, (s + 1) * W)
            ex = _dot(pk_ct, expand_ref[:, sl])
            st_prev = state_ref[:, sl]
            y_off = _dot(cg, st_prev.astype(BF16))
            chunk_decay = tile_rows(jnp.exp2(ex[L:L + SUBLANES]))
            state_ref[:, sl] = st_prev * chunk_decay + _dot(bg_t, xdd_ref[:, sl])
            y_pairs = []
            for pair in range(W // LANES):
                psl = slice(s * W + pair * LANES, s * W + (pair + 1) * LANES)
                xp = xd16_ref[:, psl]
                mhs, xps = [], []
                for sub in range(2):
                    h = s * heads_per_strip + pair * 2 + sub
                    diff = cs[:, h:h + 1] - cs_t[h:h + 1, :]
                    decay = jnp.exp2(jnp.minimum(diff, 0.0))
                    mhs.append((cb * decay).astype(BF16))
                    keep = low_head if sub == 0 else jnp.logical_not(low_head)
                    xps.append(jnp.where(keep, xp, jnp.zeros_like(xp)))
                y_pairs.append(_dot(jnp.concatenate(mhs, axis=1), jnp.concatenate(xps, axis=0)))
            y = (jnp.concatenate(y_pairs, axis=1) + y_off * jnp.exp2(ex[0:L])
                 + dskip_ref[:, sl] * xs_ref[:, sl])
            yz = y * _silu(z_ref[:, sl].astype(F32))
            yz_ref[:, sl] = yz
            for pair in range(W // LANES):
                part = yz[:, pair * LANES:(pair + 1) * LANES]
                ssq = ssq + part * part

    var = jnp.sum(ssq, axis=-1, keepdims=True) * (1.0 / D_INNER)
    rs = lax.rsqrt(var + SSM_NORM_EPS)
    for s in range(D_INNER // W):
        sl = slice(s * W, (s + 1) * W)
        o_ref[:, sl] = (yz_ref[:, sl] * rs * g_ref[:, sl]).astype(o_ref.dtype)


def _ssd(zx, dtp, shift, conv_w, conv_b, dtb_p, alog_p, dskip_x, g_ssd, expand, batch, seq):
    t = zx.shape[0]
    nc = seq // CHUNK
    bc_dim = 2 * SSM_GROUPS * D_STATE
    row = lambda b, c: (b * nc + c, 0)
    cur = lambda b, c: b * nc + c
    prev = lambda b, c: jnp.maximum(b * nc + c - 1, 0)
    x_col = 1
    bc_col = 2 * D_INNER // bc_dim
    return pl.pallas_call(
        _ssd_kernel,
        out_shape=jax.ShapeDtypeStruct((t, D_INNER), BF16),
        grid=(batch, nc),
        in_specs=[pl.BlockSpec((CHUNK, D_INNER), row),
                  pl.BlockSpec((CHUNK, D_INNER), lambda b, c: (cur(b, c), x_col)),
                  pl.BlockSpec((CHUNK, bc_dim), lambda b, c: (cur(b, c), bc_col)),
                  pl.BlockSpec((CHUNK, D_INNER), lambda b, c: (prev(b, c), x_col)),
                  pl.BlockSpec((CHUNK, bc_dim), lambda b, c: (prev(b, c), bc_col)),
                  pl.BlockSpec((CHUNK, LANES), row),
                  _resident(((CONV_WIDTH - 1) * CHUNK, 2 * CHUNK)),
                  _resident((CONV_WIDTH, CONV_DIM)),
                  _resident((1, CONV_DIM)),
                  _resident((1, LANES)),
                  _resident((1, LANES)),
                  _resident((1, D_INNER)),
                  _resident((1, D_INNER)),
                  _resident((LANES, D_INNER))],
        out_specs=pl.BlockSpec((CHUNK, D_INNER), row),
        scratch_shapes=[pltpu.VMEM((D_STATE, D_INNER), F32),
                        pltpu.VMEM((CHUNK, D_INNER), F32),
                        pltpu.VMEM((CHUNK, SSM_GROUPS * D_STATE), BF16),
                        pltpu.VMEM((CHUNK, SSM_GROUPS * D_STATE), BF16),
                        pltpu.VMEM((CHUNK, D_INNER), BF16),
                        pltpu.VMEM((CHUNK, D_INNER), BF16),
                        pltpu.VMEM((CHUNK, D_INNER), F32)],
        compiler_params=_cparams(("arbitrary", "arbitrary")),
        name="ssd",
    )(zx, zx, zx, zx, zx, dtp, shift, conv_w, conv_b, dtb_p, alog_p, dskip_x, g_ssd, expand)


def _merge_kernel(attn_ref, y_ref, ga_ref, gs_ref, wa_ref, ws_ref, o_ref, wa16_ref, ws16_ref):
    @pl.when(pl.program_id(1) == 0)
    def _():
        _cast_rows(wa16_ref, wa_ref, Q_DIM)
        ws16_ref[...] = ws_ref[...].astype(BF16)

    out_a = _dot(attn_ref[...], wa16_ref[...])
    out_s = _dot(y_ref[...], ws16_ref[...])
    merged = (_sigmoid(ga_ref[...].astype(F32)) * out_a
              + _sigmoid(gs_ref[...].astype(F32)) * out_s)
    o_ref[...] = merged.astype(o_ref.dtype)


def _merge(attn, y, gates, gate_col0, wa, ws, tm=512, tn=1024):
    t = attn.shape[0]
    nj = D_MODEL // tn
    assert gate_col0 % tn == 0
    g0 = gate_col0 // tn
    return pl.pallas_call(
        _merge_kernel,
        out_shape=jax.ShapeDtypeStruct((t, D_MODEL), BF16),
        grid=(nj, t // tm),
        in_specs=[pl.BlockSpec((tm, Q_DIM), lambda j, i: (i, 0)),
                  pl.BlockSpec((tm, D_INNER), lambda j, i: (i, 0)),
                  pl.BlockSpec((tm, tn), lambda j, i: (i, g0 + j)),
                  pl.BlockSpec((tm, tn), lambda j, i: (i, g0 + nj + j)),
                  pl.BlockSpec((Q_DIM, tn), lambda j, i: (0, j)),
                  pl.BlockSpec((D_INNER, tn), lambda j, i: (0, j))],
        out_specs=pl.BlockSpec((tm, tn), lambda j, i: (i, j)),
        scratch_shapes=[pltpu.VMEM((Q_DIM, tn), BF16),
                        pltpu.VMEM((D_INNER, tn), BF16)],
        compiler_params=_cparams(("arbitrary", "arbitrary")),
        name="merge",
    )(attn, y, gates, gates, wa, ws)


def _oproj_kernel(m_ref, x_ref, w_ref, g_ref, h_ref, f_ref, w16_ref):
    @pl.when(pl.program_id(0) == 0)
    def _():
        w16_ref[...] = w_ref[...].astype(BF16)

    h = x_ref[...] + _dot(m_ref[...], w16_ref[...])
    h_ref[...] = h
    f_ref[...] = _rms(h, g_ref[...], NORM_EPS).astype(BF16)


def _oproj(merged, x2, wo, g_ffn, tm=512):
    t = merged.shape[0]
    row = lambda i: (i, 0)
    return pl.pallas_call(
        _oproj_kernel,
        out_shape=(jax.ShapeDtypeStruct((t, D_MODEL), F32),
                   jax.ShapeDtypeStruct((t, D_MODEL), BF16)),
        grid=(t // tm,),
        in_specs=[pl.BlockSpec((tm, D_MODEL), row),
                  pl.BlockSpec((tm, D_MODEL), row),
                  _resident((D_MODEL, D_MODEL)),
                  _resident((1, D_MODEL))],
        out_specs=(pl.BlockSpec((tm, D_MODEL), row),
                   pl.BlockSpec((tm, D_MODEL), row)),
        scratch_shapes=[pltpu.VMEM((D_MODEL, D_MODEL), BF16)],
        compiler_params=_cparams(("arbitrary",)),
        name="oproj",
    )(merged, x2, wo, g_ffn)


def _ffn_up_kernel(f_ref, wg_ref, wu_ref, o_ref, wg16_ref, wu16_ref):
    @pl.when(pl.program_id(1) == 0)
    def _():
        wg16_ref[...] = wg_ref[...].astype(BF16)
        wu16_ref[...] = wu_ref[...].astype(BF16)

    half_r = o_ref.shape[0] // 2
    half_c = o_ref.shape[1] // 2
    for r in range(2):
        rs = slice(r * half_r, (r + 1) * half_r)
        f = f_ref[rs, :]
        for c in range(2):
            cs = slice(c * half_c, (c + 1) * half_c)
            o_ref[rs, cs] = (_silu(_dot(f, wg16_ref[:, cs]))
                             * _dot(f, wu16_ref[:, cs])).astype(o_ref.dtype)


def _ffn_up(f, wg, wu, tm=2048, th=512):
    t = f.shape[0]
    return pl.pallas_call(
        _ffn_up_kernel,
        out_shape=jax.ShapeDtypeStruct((t, FFN_HIDDEN), BF16),
        grid=(FFN_HIDDEN // th, t // tm),
        in_specs=[pl.BlockSpec((tm, D_MODEL), lambda j, i: (i, 0)),
                  pl.BlockSpec((D_MODEL, th), lambda j, i: (0, j)),
                  pl.BlockSpec((D_MODEL, th), lambda j, i: (0, j))],
        out_specs=pl.BlockSpec((tm, th), lambda j, i: (i, j)),
        scratch_shapes=[pltpu.VMEM((D_MODEL, th), BF16),
                        pltpu.VMEM((D_MODEL, th), BF16)],
        compiler_params=_cparams(("arbitrary", "arbitrary")),
        name="ffn_up",
    )(f, wg, wu)


def _ffn_down_kernel(a_ref, h_ref, wd_hbm, o_ref, wd16_ref, wf32_ref, sem, *, n_col_tiles):
    j = pl.program_id(0)
    i = pl.program_id(1)
    tn = wf32_ref.shape[1]

    def copy(col_tile):
        return pltpu.make_async_copy(wd_hbm.at[:, pl.ds(col_tile * tn, tn)], wf32_ref, sem)

    @pl.when((j == 0) & (i == 0))
    def _():
        copy(0).start()

    @pl.when(i == 0)
    def _():
        copy(j).wait()
        wd16_ref[...] = wf32_ref[...].astype(BF16)

    @pl.when((i == 0) & (j + 1 < n_col_tiles))
    def _():
        copy(j + 1).start()

    o_ref[...] = h_ref[...] + _dot(a_ref[...], wd16_ref[...])


def _ffn_down(act, h1, wd, tm=512, tn=1024):
    t = act.shape[0]
    n_col_tiles = D_MODEL // tn
    return pl.pallas_call(
        partial(_ffn_down_kernel, n_col_tiles=n_col_tiles),
        out_shape=jax.ShapeDtypeStruct((t, D_MODEL), F32),
        grid=(n_col_tiles, t // tm),
        in_specs=[pl.BlockSpec((tm, FFN_HIDDEN), lambda j, i: (i, 0)),
                  pl.BlockSpec((tm, tn), lambda j, i: (i, j)),
                  pl.BlockSpec(memory_space=pl.ANY)],
        out_specs=pl.BlockSpec((tm, tn), lambda j, i: (i, j)),
        scratch_shapes=[pltpu.VMEM((FFN_HIDDEN, tn), BF16),
                        pltpu.VMEM((FFN_HIDDEN, tn), F32),
                        pltpu.SemaphoreType.DMA],
        compiler_params=_cparams(("arbitrary", "arbitrary")),
        name="ffn_down",
    )(act, h1, wd)


def _ple_kernel(h_ref, p_ref, gp_ref, gf_ref, wg_ref, wp_ref, o_ref, wg16_ref, wp16_ref):
    @pl.when(pl.program_id(0) == 0)
    def _():
        wg16_ref[...] = wg_ref[...].astype(BF16)
        wp16_ref[...] = wp_ref[...].astype(BF16)

    h = h_ref[...]
    r = _rms(h, gp_ref[...], NORM_EPS).astype(BF16)
    gate = _sigmoid(_dot(r, wg16_ref[...]))
    h3 = h + gate * _dot(p_ref[...].astype(BF16), wp16_ref[...])
    o_ref[...] = _rms(h3, gf_ref[...], NORM_EPS)


def _ple(h2, p2, g_ple, g_final, wpg, wpp, tm=512):
    t = h2.shape[0]
    row = lambda i: (i, 0)
    return pl.pallas_call(
        _ple_kernel,
        out_shape=jax.ShapeDtypeStruct((t, D_MODEL), F32),
        grid=(t // tm,),
        in_specs=[pl.BlockSpec((tm, D_MODEL), row),
                  pl.BlockSpec((tm, PLE_DIM), row),
                  _resident((1, D_MODEL)),
                  _resident((1, D_MODEL)),
                  _resident((D_MODEL, D_MODEL)),
                  _resident((PLE_DIM, D_MODEL))],
        out_specs=pl.BlockSpec((tm, D_MODEL), row),
        scratch_shapes=[pltpu.VMEM((D_MODEL, D_MODEL), BF16),
                        pltpu.VMEM((PLE_DIM, D_MODEL), BF16)],
        compiler_params=_cparams(("arbitrary",)),
        name="ple",
    )(h2, p2, g_ple, g_final, wpg, wpp)


def _rope_constants():
    half = HEAD_DIM // 2
    lane = np.arange(LANES)
    inv_freq = ROPE_THETA ** (-jnp.arange(half, dtype=F32) * 2.0 / HEAD_DIM)
    invf = inv_freq[lane % half][None, :]
    sgn = jnp.asarray(np.where((lane % HEAD_DIM) < half, -1.0, 1.0), F32)[None, :]
    return invf, sgn


def _expand_matrix():
    rows = np.arange(LANES)[:, None]
    cols = np.arange(D_INNER)[None, :]
    hit = (rows < 3 * SSM_HEADS) & ((rows % SSM_HEADS) == (cols // SSM_HEAD_DIM))
    return jnp.asarray(hit, BF16)


def _shift_matrix():
    rows = np.arange((CONV_WIDTH - 1) * CHUNK)[:, None]
    cols = np.arange(2 * CHUNK)[None, :]
    w, t = rows // CHUNK, rows % CHUNK
    return jnp.asarray(cols == CHUNK + t - (CONV_WIDTH - 1) + w, BF16)


def _pad_lanes(v):
    return jnp.pad(v.astype(F32), (0, LANES - v.shape[0]))[None, :]


def kernel(x, p, positions, g_mix, w_in, conv_w, conv_b, dt_bias, a_log, d_skip, g_ssd,
           sinks, w_attn_br, w_ssd_br, w_o, g_ffn, w_gate, w_up, w_down, g_ple,
           w_ple_gate, w_ple_proj, g_final):
    b, s, d = x.shape
    t = b * s
    assert d == D_MODEL and s % CHUNK == 0 and p.shape[0] == 1
    i = 0
    x2 = x.reshape(t, d)
    p2 = p[i].reshape(t, PLE_DIM)
    pos_c = jnp.repeat(positions.reshape(t // POS_PER_ROW, POS_PER_ROW), ROPE_FREQS, axis=1)

    o_z = Q_DIM + 2 * KV_DIM
    o_xbc = o_z + D_INNER
    o_dt = o_xbc + CONV_DIM
    o_ga = o_dt + SSM_HEADS
    wt = jnp.swapaxes(w_in[i], 0, 1)

    invf, sgn = _rope_constants()
    u, cos_t, sin_t, dtp, qkv = _prep(x2, g_mix[i][None, :], pos_c, invf, sgn, wt, o_dt, o_z)

    zxg = _inproj(u, wt, [(o_z, D_INNER + CONV_DIM), (o_ga, 2 * D_MODEL)], BF16,
                  INPROJ_ROW_TILE, INPROJ_COL_TILE, "proj_zxg")
    gate_col0 = D_INNER + CONV_DIM

    attn = _attention(qkv, cos_t, sin_t, sinks[i][None, :].astype(F32), b, s)

    y = _ssd(zxg, dtp, _shift_matrix(), conv_w[i], conv_b[i][None, :], _pad_lanes(dt_bias[i]),
             _pad_lanes(a_log[i]), jnp.repeat(d_skip[i].astype(F32), SSM_HEAD_DIM)[None, :],
             g_ssd[i][None, :], _expand_matrix(), b, s)

    merged = _merge(attn, y, zxg, gate_col0, w_attn_br[i], w_ssd_br[i])
    h1, f = _oproj(merged, x2, w_o[i], g_ffn[i][None, :])
    act = _ffn_up(f, w_gate[i], w_up[i])
    h2 = _ffn_down(act, h1, w_down[i])
    out = _ple(h2, p2, g_ple[i][None, :], g_final[None, :], w_ple_gate[i], w_ple_proj[i])
    return out.reshape(b, s, d)
```

```python
from functools import partial

import numpy as np
import jax
import jax.numpy as jnp
from jax import lax
from jax.experimental import pallas as pl
from jax.experimental.pallas import tpu as pltpu

F32 = jnp.float32
BF16 = jnp.bfloat16

D_MODEL = 2048
HEAD_DIM = 64
ATTN_HEADS = 16
KV_HEADS = 4
GROUP = ATTN_HEADS // KV_HEADS
Q_DIM = ATTN_HEADS * HEAD_DIM
KV_DIM = KV_HEADS * HEAD_DIM
ATTN_BLOCK = 128
ROPE_THETA = 10000.0
D_INNER = 2048
SSM_HEAD_DIM = 64
SSM_HEADS = 32
SSM_GROUPS = 4
HEADS_PER_GROUP = SSM_HEADS // SSM_GROUPS
D_STATE = 128
CONV_WIDTH = 4
CHUNK = 128
CONV_DIM = D_INNER + 2 * SSM_GROUPS * D_STATE
FFN_HIDDEN = 5632
PLE_DIM = 256
NORM_EPS = 1e-6
SSM_NORM_EPS = 1e-5

LANES = 128
SUBLANES = 8
VMEM_LIMIT_BYTES = 56 * 1024 * 1024
WEIGHT_STAGE_ROWS = 256

NEG_BIG = -1e30


def _cparams(semantics):
    return pltpu.CompilerParams(dimension_semantics=semantics,
                                vmem_limit_bytes=VMEM_LIMIT_BYTES)


def _resident(shape):
    return pl.BlockSpec(shape, lambda *_: (0,) * len(shape),
                        pipeline_mode=pl.Buffered(1))


def _rms(xf, g, eps):
    var = jnp.mean(xf * xf, axis=-1, keepdims=True)
    return xf * lax.rsqrt(var + eps) * g


NEG_LOG2E = -1.4426950408889634


def _sigmoid(x):
    return 1.0 / (1.0 + jnp.exp2(x * NEG_LOG2E))


def _silu(x):
    return x * _sigmoid(x)


def _dot(a, b):
    return jnp.dot(a, b, preferred_element_type=F32)


def _dot_nt(a, b):
    return lax.dot_general(a, b, (((1,), (1,)), ((), ())),
                           preferred_element_type=F32)


def _split3(v):
    v1 = v.astype(BF16)
    r1 = v - v1.astype(F32)
    v2 = r1.astype(BF16)
    v3 = (r1 - v2.astype(F32)).astype(BF16)
    return v1, v2, v3


ROPE_FREQS = HEAD_DIM // 2
POS_PER_ROW = LANES // ROPE_FREQS


def _prep_kernel(x_ref, g_ref, pos_ref, invf_ref, sgn_ref, wdt_ref, wqkv_ref,
                 u_ref, cos_ref, sin_ref, dt_ref, qkv_ref, wqkv16_ref):
    @pl.when(pl.program_id(0) == 0)
    def _():
        _cast_rows(wqkv16_ref, wqkv_ref, Q_DIM)

    u = _rms(x_ref[...], g_ref[...], NORM_EPS).astype(BF16)
    u_ref[...] = u
    qkv_ref[...] = _dot_nt(u, wqkv16_ref[...]).astype(qkv_ref.dtype)
    rows = pos_ref.shape[0]
    ang = pos_ref[...].astype(F32) * invf_ref[...]
    cos_c = jnp.cos(ang)
    sin_c = jnp.sin(ang)
    group = lax.broadcasted_iota(jnp.int32, ang.shape, 1) // ROPE_FREQS
    for q in range(POS_PER_ROW):
        for src, dst_ref, sign in ((cos_c, cos_ref, None), (sin_c, sin_ref, sgn_ref[...])):
            only = jnp.where(group == q, src, 0.0)
            rep = only
            for k in range(1, POS_PER_ROW):
                rep = rep + pltpu.roll(only, k * ROPE_FREQS, 1)
            if sign is not None:
                rep = rep * sign
            dst_ref[pl.ds(q, rows, stride=POS_PER_ROW), :] = rep
    dt_ref[...] = _dot_nt(u, wdt_ref[...].astype(BF16))


def _prep(x2, g_mix, pos_c, invf, sgn, wt, dt_row0, qkv_rows, tm=512):
    t = x2.shape[0]
    row = lambda i: (i, 0)
    assert dt_row0 % SUBLANES == 0 and tm % (POS_PER_ROW * SUBLANES) == 0
    return pl.pallas_call(
        _prep_kernel,
        out_shape=(jax.ShapeDtypeStruct((t, D_MODEL), BF16),
                   jax.ShapeDtypeStruct((t, LANES), F32),
                   jax.ShapeDtypeStruct((t, LANES), F32),
                   jax.ShapeDtypeStruct((t, LANES), F32),
                   jax.ShapeDtypeStruct((t, qkv_rows), BF16)),
        grid=(t // tm,),
        in_specs=[pl.BlockSpec((tm, D_MODEL), row),
                  _resident((1, D_MODEL)),
                  pl.BlockSpec((tm // POS_PER_ROW, LANES), row),
                  _resident((1, LANES)),
                  _resident((1, LANES)),
                  pl.BlockSpec((pl.Element(LANES), pl.Element(D_MODEL)),
                               lambda i: (dt_row0, 0), pipeline_mode=pl.Buffered(1)),
                  pl.BlockSpec((pl.Element(qkv_rows), pl.Element(D_MODEL)),
                               lambda i: (0, 0), pipeline_mode=pl.Buffered(1))],
        out_specs=(pl.BlockSpec((tm, D_MODEL), row),
                   pl.BlockSpec((tm, LANES), row),
                   pl.BlockSpec((tm, LANES), row),
                   pl.BlockSpec((tm, LANES), row),
                   pl.BlockSpec((tm, qkv_rows), row)),
        scratch_shapes=[pltpu.VMEM((qkv_rows, D_MODEL), BF16)],
        compiler_params=_cparams(("arbitrary",)),
        name="prep_qkv",
    )(x2, g_mix, pos_c, invf, sgn, wt, wt)


PAIR_ROWS = 2 * GROUP * HEAD_DIM
INPROJ_ROW_TILE = 2048
INPROJ_COL_TILE = 1024


def _paired_head_blocks():
    return [(g * 2 + sub, sub * GROUP + g) for g in range(GROUP) for sub in range(2)]


def _cast_rows(dst_ref, src_ref, paired_rows):
    rows = dst_ref.shape[0]
    assert paired_rows % PAIR_ROWS == 0 and paired_rows <= rows
    for base in range(0, paired_rows, PAIR_ROWS):
        for dst, src in _paired_head_blocks():
            dst_ref[base + dst * HEAD_DIM:base + (dst + 1) * HEAD_DIM, :] = (
                src_ref[base + src * HEAD_DIM:base + (src + 1) * HEAD_DIM, :].astype(BF16))
    if paired_rows < rows:
        dst_ref[paired_rows:rows, :] = src_ref[paired_rows:rows, :].astype(BF16)


def _inproj_kernel(a_ref, wt_ref, o_ref, wbf_ref):
    @pl.when(pl.program_id(1) == 0)
    def _():
        wbf_ref[...] = wt_ref[...].astype(BF16)

    o_ref[...] = _dot_nt(a_ref[...], wbf_ref[...]).astype(o_ref.dtype)


def _inproj(a, wt, segments, out_dtype, tm, tn, name):
    m, k = a.shape
    assert all(seg_n % tn == 0 and row0 % SUBLANES == 0 for row0, seg_n in segments)
    n = sum(seg_n for _, seg_n in segments)

    def weight_row(j):
        row, first_tile = None, 0
        for row0, seg_n in segments:
            here = row0 + (j - first_tile) * tn
            row = here if row is None else jnp.where(j >= first_tile, here, row)
            first_tile += seg_n // tn
        return pl.multiple_of(row, SUBLANES)

    return pl.pallas_call(
        _inproj_kernel,
        out_shape=jax.ShapeDtypeStruct((m, n), out_dtype),
        grid=(n // tn, m // tm),
        in_specs=[pl.BlockSpec((tm, k), lambda j, i: (i, 0)),
                  pl.BlockSpec((pl.Element(tn), pl.Element(k)), lambda j, i: (weight_row(j), 0))],
        out_specs=pl.BlockSpec((tm, tn), lambda j, i: (i, j)),
        scratch_shapes=[pltpu.VMEM((tn, k), BF16)],
        compiler_params=_cparams(("arbitrary", "arbitrary")),
        name=name,
    )(a, wt)


ATTN_STEP_BLOCKS = 2


def _attn_kernel(sinks_ref, q_ref, kc_ref, kp_ref, vc_ref, vp_ref,
                 cc_ref, sc_ref, cp_ref, sp_ref, o_ref):
    n = pl.program_id(1)
    blk = ATTN_BLOCK
    nblk = ATTN_STEP_BLOCKS
    lane = lax.broadcasted_iota(jnp.int32, (blk, LANES), 1)
    first_half = (lane % HEAD_DIM) < (HEAD_DIM // 2)
    low_head = lane < HEAD_DIM

    def rope(t, c, s):
        partner = jnp.where(first_half,
                            pltpu.roll(t, LANES - HEAD_DIM // 2, 1),
                            pltpu.roll(t, HEAD_DIM // 2, 1))
        return t * c + partner * s

    def rows(j):
        return slice(j * blk, (j + 1) * blk)

    cp, sp = cp_ref[...], sp_ref[...]
    cc = [cc_ref[rows(j), :] for j in range(nblk)]
    sc = [sc_ref[rows(j), :] for j in range(nblk)]

    qi = lax.broadcasted_iota(jnp.int32, (blk, 2 * blk), 0)
    kj = lax.broadcasted_iota(jnp.int32, (blk, 2 * blk), 1)
    rel = kj - qi
    band = (rel >= 1) & (rel <= blk)
    valid = [band & ((kj >= blk) | (n > 0))] + [band] * (nblk - 1)

    scale = HEAD_DIM ** -0.5
    n_half = KV_DIM // LANES
    rows_per_half = 2 * GROUP * blk
    s_rows, sink_rows, valid_rows, v_wins = [], [], [], []
    for a in range(n_half):
        ksl = slice(a * LANES, (a + 1) * LANES)
        k_blocks = [rope(kp_ref[:, ksl].astype(F32), cp, sp).astype(BF16)]
        v_blocks = [vp_ref[:, ksl]]
        for j in range(nblk):
            k_blocks.append(rope(kc_ref[rows(j), ksl].astype(F32), cc[j], sc[j]).astype(BF16))
            v_blocks.append(vc_ref[rows(j), ksl])
        for j in range(nblk):
            k_win = jnp.concatenate(k_blocks[j:j + 2], axis=0)
            v_wins.append(jnp.concatenate(v_blocks[j:j + 2], axis=0))
            qc = [rope(q_ref[rows(j), (a * GROUP + g) * LANES:(a * GROUP + g + 1) * LANES]
                       .astype(F32), cc[j] * scale, sc[j] * scale) for g in range(GROUP)]
            q_rows = []
            for sub in range(2):
                keep = low_head if sub == 0 else jnp.logical_not(low_head)
                for g in range(GROUP):
                    q_rows.append(jnp.where(keep, qc[g], 0.0).astype(BF16))
                    sink_rows.append(
                        jnp.full((blk, LANES), sinks_ref[0, (2 * a + sub) * GROUP + g], F32))
            s_rows.append(_dot_nt(jnp.concatenate(q_rows, axis=0), k_win))
            valid_rows.extend([valid[j]] * (2 * GROUP))
    s = jnp.where(jnp.concatenate(valid_rows, axis=0), jnp.concatenate(s_rows, axis=0), NEG_BIG)
    sink = jnp.concatenate(sink_rows, axis=0)
    s0, s1 = s[:, :LANES], s[:, LANES:]
    m = jnp.maximum(jnp.max(jnp.maximum(s0, s1), axis=-1, keepdims=True), sink)
    e0 = jnp.exp(s0 - m)
    e1 = jnp.exp(s1 - m)
    den = jnp.sum(e0 + e1, axis=-1, keepdims=True) + jnp.exp(sink - m)
    r = 1.0 / den
    p = jnp.concatenate([e0 * r, e1 * r], axis=1).astype(BF16)
    for a in range(n_half):
        for j in range(nblk):
            w = a * nblk + j
            pv = _dot(p[w * rows_per_half:(w + 1) * rows_per_half], v_wins[w])
            for g in range(GROUP):
                lo = pv[g * blk:(g + 1) * blk]
                hi = pv[(GROUP + g) * blk:(GROUP + g + 1) * blk]
                chunk = a * GROUP + g
                o_ref[rows(j), chunk * LANES:(chunk + 1) * LANES] = (
                    jnp.where(low_head, lo, hi).astype(o_ref.dtype))


def _attention(qkv, cos_t, sin_t, sinks, batch, seq):
    t = qkv.shape[0]
    step = ATTN_STEP_BLOCKS * ATTN_BLOCK
    assert seq % step == 0
    ns = seq // step
    kcol = Q_DIM // KV_DIM
    vcol = kcol + 1
    cur = lambda b, n: b * ns + n
    prev = lambda b, n: jnp.maximum((b * ns + n) * ATTN_STEP_BLOCKS - 1, 0)
    return pl.pallas_call(
        _attn_kernel,
        out_shape=jax.ShapeDtypeStruct((t, Q_DIM), BF16),
        grid=(batch, ns),
        in_specs=[pl.BlockSpec(memory_space=pltpu.SMEM),
                  pl.BlockSpec((step, Q_DIM), lambda b, n: (cur(b, n), 0)),
                  pl.BlockSpec((step, KV_DIM), lambda b, n: (cur(b, n), kcol)),
                  pl.BlockSpec((ATTN_BLOCK, KV_DIM), lambda b, n: (prev(b, n), kcol)),
                  pl.BlockSpec((step, KV_DIM), lambda b, n: (cur(b, n), vcol)),
                  pl.BlockSpec((ATTN_BLOCK, KV_DIM), lambda b, n: (prev(b, n), vcol)),
                  pl.BlockSpec((step, LANES), lambda b, n: (cur(b, n), 0)),
                  pl.BlockSpec((step, LANES), lambda b, n: (cur(b, n), 0)),
                  pl.BlockSpec((ATTN_BLOCK, LANES), lambda b, n: (prev(b, n), 0)),
                  pl.BlockSpec((ATTN_BLOCK, LANES), lambda b, n: (prev(b, n), 0))],
        out_specs=pl.BlockSpec((step, Q_DIM), lambda b, n: (cur(b, n), 0)),
        compiler_params=_cparams(("arbitrary", "arbitrary")),
        name="attention",
    )(sinks, qkv, qkv, qkv, qkv, qkv, cos_t, sin_t, cos_t, sin_t)


SSD_STRIP = 256


def _ssd_kernel(z_ref, xin_ref, bcin_ref, xin_prev_ref, bcin_prev_ref, dt_ref, shift_ref, cw_ref,
                cb_ref, dtb_ref, alog_ref, dskip_ref, g_ref, expand_ref, o_ref,
                state_ref, xs_ref, b16_ref, c16_ref, xdd_ref, xd16_ref, yz_ref):
    c = pl.program_id(1)
    L = CHUNK
    W = SSD_STRIP
    bc_dim = SSM_GROUPS * D_STATE

    @pl.when(c == 0)
    def _():
        state_ref[...] = jnp.zeros_like(state_ref)

    v = dt_ref[...] + dtb_ref[...]
    dt = jnp.maximum(v, 0.0) + jnp.log1p(jnp.exp(-jnp.abs(v)))
    a = dt * (jnp.exp(alog_ref[...]) * NEG_LOG2E)

    ri = lax.broadcasted_iota(jnp.int32, (L, L), 0)
    ci = lax.broadcasted_iota(jnp.int32, (L, L), 1)
    causal = ri >= ci
    tril = jnp.where(causal, 1.0, 0.0).astype(BF16)
    R = 2 * SUBLANES
    ones = jnp.ones((L, L), BF16)
    a1, a2, a3 = _split3(a)
    cs = _dot(tril, a1) + _dot(tril, a2) + _dot(tril, a3)
    tot = (_dot(ones, a1) + _dot(ones, a2) + _dot(ones, a3))[0:R]
    cs_t = cs.T[0:SSM_HEADS]

    lane = lax.broadcasted_iota(jnp.int32, (L, LANES), 1)

    def pack(q):
        ln = lax.broadcasted_iota(jnp.int32, q.shape, 1)
        q1, q2, q3 = _split3(q)
        packed = jnp.where(
            ln < SSM_HEADS, q1.astype(F32),
            jnp.where(ln < 2 * SSM_HEADS, pltpu.roll(q2.astype(F32), SSM_HEADS, 1),
                      jnp.where(ln < 3 * SSM_HEADS,
                                pltpu.roll(q3.astype(F32), 2 * SSM_HEADS, 1), 0.0)))
        return packed.astype(BF16)

    def tile_rows(v):
        return jnp.concatenate([v] * (L // SUBLANES), axis=0)

    pk_all = jnp.concatenate([pack(dt), pack(cs), pack(tot)], axis=0)
    pk_ct = pk_all[L:]

    n_shift = CONV_WIDTH - 1
    sel_col = lax.broadcasted_iota(jnp.int32, (n_shift * L, 2 * L), 1)
    sel = shift_ref[...]
    sel = jnp.where((sel_col < L) & (c == 0), jnp.zeros_like(sel), sel)
    def conv_strip(s):
        sl = slice(s * W, (s + 1) * W)
        if s * W < D_INNER:
            cur_ref, prev_ref, src = xin_ref, xin_prev_ref, sl
        else:
            cur_ref, prev_ref = bcin_ref, bcin_prev_ref
            src = slice(s * W - D_INNER, (s + 1) * W - D_INNER)
        x_cur = cur_ref[:, src]
        xwin = jnp.concatenate([prev_ref[:, src], x_cur], axis=0)
        taps = _dot(sel, xwin)
        acc = cb_ref[:, sl] + x_cur.astype(F32) * cw_ref[n_shift:CONV_WIDTH, sl]
        for w in range(n_shift):
            acc = acc + taps[w * L:(w + 1) * L] * cw_ref[w:w + 1, sl]
        xc = _silu(acc)
        if s * W < D_INNER:
            xs_ref[:, sl] = xc
            ex = _dot(pk_all, expand_ref[:, sl])
            xd = xc * ex[0:L]
            xd16_ref[:, sl] = xd.astype(BF16)
            tot_x = tile_rows(ex[2 * L:2 * L + SUBLANES])
            xdd_ref[:, sl] = (xd * jnp.exp2(tot_x - ex[L:2 * L])).astype(BF16)
        elif s * W < D_INNER + bc_dim:
            b16_ref[:, s * W - D_INNER:(s + 1) * W - D_INNER] = xc.astype(BF16)
        else:
            c16_ref[:, s * W - D_INNER - bc_dim:(s + 1) * W - D_INNER - bc_dim] = xc.astype(BF16)

    low_head = lane < SSM_HEAD_DIM
    heads_per_strip = W // SSM_HEAD_DIM
    strips_per_group = HEADS_PER_GROUP // heads_per_strip
    ssq = jnp.zeros((L, LANES), F32)

    for s in range(D_INNER // W, CONV_DIM // W):
        conv_strip(s)

    for g in range(SSM_GROUPS):
        for k in range(strips_per_group):
            conv_strip(g * strips_per_group + k)
        nsl = slice(g * D_STATE, (g + 1) * D_STATE)
        bg = b16_ref[:, nsl]
        cg = c16_ref[:, nsl]
        bg_t = bg.astype(F32).T.astype(BF16)
        cb = jnp.where(causal, _dot_nt(cg, bg), 0.0)
        for k in range(strips_per_group):
            s = g * strips_per_group + k
            sl = slice(s * W, (s + 1) * W)
            ex = _dot(pk_ct, expand_ref[:, sl])
            st_prev = state_ref[:, sl]
            y_off = _dot(cg, st_prev.astype(BF16))
            chunk_decay = tile_rows(jnp.exp2(ex[L:L + SUBLANES]))
            state_ref[:, sl] = st_prev * chunk_decay + _dot(bg_t, xdd_ref[:, sl])
            y_pairs = []
            for pair in range(W // LANES):
                psl = slice(s * W + pair * LANES, s * W + (pair + 1) * LANES)
                xp = xd16_ref[:, psl]
                mhs, xps = [], []
                for sub in range(2):
                    h = s * heads_per_strip + pair * 2 + sub
                    diff = cs[:, h:h + 1] - cs_t[h:h + 1, :]
                    decay = jnp.exp2(jnp.minimum(diff, 0.0))
                    mhs.append((cb * decay).astype(BF16))
                    keep = low_head if sub == 0 else jnp.logical_not(low_head)
                    xps.append(jnp.where(keep, xp, jnp.zeros_like(xp)))
                y_pairs.append(_dot(jnp.concatenate(mhs, axis=1), jnp.concatenate(xps, axis=0)))
            y = (jnp.concatenate(y_pairs, axis=1) + y_off * jnp.exp2(ex[0:L])
                 + dskip_ref[:, sl] * xs_ref[:, sl])
            yz = y * _silu(z_ref[:, sl].astype(F32))
            yz_ref[:, sl] = yz
            for pair in range(W // LANES):
                part = yz[:, pair * LANES:(pair + 1) * LANES]
                ssq = ssq + part * part

    var = jnp.sum(ssq, axis=-1, keepdims=True) * (1.0 / D_INNER)
    rs = lax.rsqrt(var + SSM_NORM_EPS)
    for s in range(D_INNER // W):
        sl = slice(s * W, (s + 1) * W)
        o_ref[:, sl] = (yz_ref[:, sl] * rs * g_ref[:, sl]).astype(o_ref.dtype)


def _ssd(zx, dtp, shift, conv_w, conv_b, dtb_p, alog_p, dskip_x, g_ssd, expand, batch, seq):
    t = zx.shape[0]
    nc = seq // CHUNK
    bc_dim = 2 * SSM_GROUPS * D_STATE
    row = lambda b, c: (b * nc + c, 0)
    cur = lambda b, c: b * nc + c
    prev = lambda b, c: jnp.maximum(b * nc + c - 1, 0)
    x_col = 1
    bc_col = 2 * D_INNER // bc_dim
    return pl.pallas_call(
        _ssd_kernel,
        out_shape=jax.ShapeDtypeStruct((t, D_INNER), BF16),
        grid=(batch, nc),
        in_specs=[pl.BlockSpec((CHUNK, D_INNER), row),
                  pl.BlockSpec((CHUNK, D_INNER), lambda b, c: (cur(b, c), x_col)),
                  pl.BlockSpec((CHUNK, bc_dim), lambda b, c: (cur(b, c), bc_col)),
                  pl.BlockSpec((CHUNK, D_INNER), lambda b, c: (prev(b, c), x_col)),
                  pl.BlockSpec((CHUNK, bc_dim), lambda b, c: (prev(b, c), bc_col)),
                  pl.BlockSpec((CHUNK, LANES), row),
                  _resident(((CONV_WIDTH - 1) * CHUNK, 2 * CHUNK)),
                  _resident((CONV_WIDTH, CONV_DIM)),
                  _resident((1, CONV_DIM)),
                  _resident((1, LANES)),
                  _resident((1, LANES)),
                  _resident((1, D_INNER)),
                  _resident((1, D_INNER)),
                  _resident((LANES, D_INNER))],
        out_specs=pl.BlockSpec((CHUNK, D_INNER), row),
        scratch_shapes=[pltpu.VMEM((D_STATE, D_INNER), F32),
                        pltpu.VMEM((CHUNK, D_INNER), F32),
                        pltpu.VMEM((CHUNK, SSM_GROUPS * D_STATE), BF16),
                        pltpu.VMEM((CHUNK, SSM_GROUPS * D_STATE), BF16),
                        pltpu.VMEM((CHUNK, D_INNER), BF16),
                        pltpu.VMEM((CHUNK, D_INNER), BF16),
                        pltpu.VMEM((CHUNK, D_INNER), F32)],
        compiler_params=_cparams(("arbitrary", "arbitrary")),
        name="ssd",
    )(zx, zx, zx, zx, zx, dtp, shift, conv_w, conv_b, dtb_p, alog_p, dskip_x, g_ssd, expand)


def _merge_kernel(attn_ref, y_ref, ga_ref, gs_ref, wa_ref, ws_ref, o_ref, wa16_ref, ws16_ref):
    @pl.when(pl.program_id(1) == 0)
    def _():
        _cast_rows(wa16_ref, wa_ref, Q_DIM)
        ws16_ref[...] = ws_ref[...].astype(BF16)

    out_a = _dot(attn_ref[...], wa16_ref[...])
    out_s = _dot(y_ref[...], ws16_ref[...])
    merged = (_sigmoid(ga_ref[...].astype(F32)) * out_a
              + _sigmoid(gs_ref[...].astype(F32)) * out_s)
    o_ref[...] = merged.astype(o_ref.dtype)


def _merge(attn, y, gates, gate_col0, wa, ws, tm=512, tn=1024):
    t = attn.shape[0]
    nj = D_MODEL // tn
    assert gate_col0 % tn == 0
    g0 = gate_col0 // tn
    return pl.pallas_call(
        _merge_kernel,
        out_shape=jax.ShapeDtypeStruct((t, D_MODEL), BF16),
        grid=(nj, t // tm),
        in_specs=[pl.BlockSpec((tm, Q_DIM), lambda j, i: (i, 0)),
                  pl.BlockSpec((tm, D_INNER), lambda j, i: (i, 0)),
                  pl.BlockSpec((tm, tn), lambda j, i: (i, g0 + j)),
                  pl.BlockSpec((tm, tn), lambda j, i: (i, g0 + nj + j)),
                  pl.BlockSpec((Q_DIM, tn), lambda j, i: (0, j)),
                  pl.BlockSpec((D_INNER, tn), lambda j, i: (0, j))],
        out_specs=pl.BlockSpec((tm, tn), lambda j, i: (i, j)),
        scratch_shapes=[pltpu.VMEM((Q_DIM, tn), BF16),
                        pltpu.VMEM((D_INNER, tn), BF16)],
        compiler_params=_cparams(("arbitrary", "arbitrary")),
        name="merge",
    )(attn, y, gates, gates, wa, ws)


def _stream_cast(w_hbm, w16_ref, stage_ref, sem_ref):
    rows = stage_ref.shape[1]
    n_chunks = w_hbm.shape[0] // rows

    def copy(k):
        return pltpu.make_async_copy(w_hbm.at[pl.ds(k * rows, rows), :], stage_ref.at[k % 2],
                                     sem_ref.at[k % 2])

    copy(0).start()
    for k in range(n_chunks):
        if k + 1 < n_chunks:
            copy(k + 1).start()
        copy(k).wait()
        w16_ref[k * rows:(k + 1) * rows, :] = stage_ref[k % 2].astype(BF16)


def _oproj_kernel(m_ref, x_ref, w_hbm, g_ref, h_ref, f_ref, w16_ref, stage_ref, sem_ref):
    @pl.when(pl.program_id(0) == 0)
    def _():
        _stream_cast(w_hbm, w16_ref, stage_ref, sem_ref)

    h = x_ref[...] + _dot(m_ref[...], w16_ref[...])
    h_ref[...] = h
    f_ref[...] = _rms(h, g_ref[...], NORM_EPS).astype(BF16)


def _oproj(merged, x2, wo, g_ffn, tm=512):
    t = merged.shape[0]
    row = lambda i: (i, 0)
    return pl.pallas_call(
        _oproj_kernel,
        out_shape=(jax.ShapeDtypeStruct((t, D_MODEL), F32),
                   jax.ShapeDtypeStruct((t, D_MODEL), BF16)),
        grid=(t // tm,),
        in_specs=[pl.BlockSpec((tm, D_MODEL), row),
                  pl.BlockSpec((tm, D_MODEL), row),
                  pl.BlockSpec(memory_space=pl.ANY),
                  _resident((1, D_MODEL))],
        out_specs=(pl.BlockSpec((tm, D_MODEL), row),
                   pl.BlockSpec((tm, D_MODEL), row)),
        scratch_shapes=[pltpu.VMEM((D_MODEL, D_MODEL), BF16),
                        pltpu.VMEM((2, WEIGHT_STAGE_ROWS, D_MODEL), F32),
                        pltpu.SemaphoreType.DMA((2,))],
        compiler_params=_cparams(("arbitrary",)),
        name="oproj",
    )(merged, x2, wo, g_ffn)


def _ffn_up_kernel(f_ref, wg_ref, wu_ref, o_ref, wg16_ref, wu16_ref):
    @pl.when(pl.program_id(1) == 0)
    def _():
        wg16_ref[...] = wg_ref[...].astype(BF16)
        wu16_ref[...] = wu_ref[...].astype(BF16)

    half_r = o_ref.shape[0] // 2
    half_c = o_ref.shape[1] // 2
    for r in range(2):
        rs = slice(r * half_r, (r + 1) * half_r)
        f = f_ref[rs, :]
        for c in range(2):
            cs = slice(c * half_c, (c + 1) * half_c)
            o_ref[rs, cs] = (_silu(_dot(f, wg16_ref[:, cs]))
                             * _dot(f, wu16_ref[:, cs])).astype(o_ref.dtype)


def _ffn_up(f, wg, wu, tm=2048, th=512):
    t = f.shape[0]
    return pl.pallas_call(
        _ffn_up_kernel,
        out_shape=jax.ShapeDtypeStruct((t, FFN_HIDDEN), BF16),
        grid=(FFN_HIDDEN // th, t // tm),
        in_specs=[pl.BlockSpec((tm, D_MODEL), lambda j, i: (i, 0)),
                  pl.BlockSpec((D_MODEL, th), lambda j, i: (0, j)),
                  pl.BlockSpec((D_MODEL, th), lambda j, i: (0, j))],
        out_specs=pl.BlockSpec((tm, th), lambda j, i: (i, j)),
        scratch_shapes=[pltpu.VMEM((D_MODEL, th), BF16),
                        pltpu.VMEM((D_MODEL, th), BF16)],
        compiler_params=_cparams(("arbitrary", "arbitrary")),
        name="ffn_up",
    )(f, wg, wu)


def _ffn_down_kernel(a_ref, h_ref, wd_hbm, o_ref, wd16_ref, wf32_ref, sem, *, n_col_tiles):
    j = pl.program_id(0)
    i = pl.program_id(1)
    tn = wf32_ref.shape[1]

    def copy(col_tile):
        return pltpu.make_async_copy(wd_hbm.at[:, pl.ds(col_tile * tn, tn)], wf32_ref, sem)

    @pl.when((j == 0) & (i == 0))
    def _():
        copy(0).start()

    @pl.when(i == 0)
    def _():
        copy(j).wait()
        wd16_ref[...] = wf32_ref[...].astype(BF16)

    @pl.when((i == 0) & (j + 1 < n_col_tiles))
    def _():
        copy(j + 1).start()

    o_ref[...] = h_ref[...] + _dot(a_ref[...], wd16_ref[...])


def _ffn_down(act, h1, wd, tm=512, tn=1024):
    t = act.shape[0]
    n_col_tiles = D_MODEL // tn
    return pl.pallas_call(
        partial(_ffn_down_kernel, n_col_tiles=n_col_tiles),
        out_shape=jax.ShapeDtypeStruct((t, D_MODEL), F32),
        grid=(n_col_tiles, t // tm),
        in_specs=[pl.BlockSpec((tm, FFN_HIDDEN), lambda j, i: (i, 0)),
                  pl.BlockSpec((tm, tn), lambda j, i: (i, j)),
                  pl.BlockSpec(memory_space=pl.ANY)],
        out_specs=pl.BlockSpec((tm, tn), lambda j, i: (i, j)),
        scratch_shapes=[pltpu.VMEM((FFN_HIDDEN, tn), BF16),
                        pltpu.VMEM((FFN_HIDDEN, tn), F32),
                        pltpu.SemaphoreType.DMA],
        compiler_params=_cparams(("arbitrary", "arbitrary")),
        name="ffn_down",
    )(act, h1, wd)


def _ple_kernel(h_ref, p_ref, gp_ref, gf_ref, wg_hbm, wp_ref, o_ref, wg16_ref, wp16_ref, stage_ref,
                sem_ref):
    @pl.when(pl.program_id(0) == 0)
    def _():
        _stream_cast(wg_hbm, wg16_ref, stage_ref, sem_ref)
        wp16_ref[...] = wp_ref[...].astype(BF16)

    h = h_ref[...]
    r = _rms(h, gp_ref[...], NORM_EPS).astype(BF16)
    gate = _sigmoid(_dot(r, wg16_ref[...]))
    h3 = h + gate * _dot(p_ref[...].astype(BF16), wp16_ref[...])
    o_ref[...] = _rms(h3, gf_ref[...], NORM_EPS)


def _ple(h2, p2, g_ple, g_final, wpg, wpp, tm=512):
    t = h2.shape[0]
    row = lambda i: (i, 0)
    return pl.pallas_call(
        _ple_kernel,
        out_shape=jax.ShapeDtypeStruct((t, D_MODEL), F32),
        grid=(t // tm,),
        in_specs=[pl.BlockSpec((tm, D_MODEL), row),
                  pl.BlockSpec((tm, PLE_DIM), row),
                  _resident((1, D_MODEL)),
                  _resident((1, D_MODEL)),
                  pl.BlockSpec(memory_space=pl.ANY),
                  _resident((PLE_DIM, D_MODEL))],
        out_specs=pl.BlockSpec((tm, D_MODEL), row),
        scratch_shapes=[pltpu.VMEM((D_MODEL, D_MODEL), BF16),
                        pltpu.VMEM((PLE_DIM, D_MODEL), BF16),
                        pltpu.VMEM((2, WEIGHT_STAGE_ROWS, D_MODEL), F32),
                        pltpu.SemaphoreType.DMA((2,))],
        compiler_params=_cparams(("arbitrary",)),
        name="ple",
    )(h2, p2, g_ple, g_final, wpg, wpp)


def _rope_constants():
    half = HEAD_DIM // 2
    lane = np.arange(LANES)
    inv_freq = ROPE_THETA ** (-jnp.arange(half, dtype=F32) * 2.0 / HEAD_DIM)
    invf = inv_freq[lane % half][None, :]
    sgn = jnp.asarray(np.where((lane % HEAD_DIM) < half, -1.0, 1.0), F32)[None, :]
    return invf, sgn


def _expand_matrix():
    rows = np.arange(LANES)[:, None]
    cols = np.arange(D_INNER)[None, :]
    hit = (rows < 3 * SSM_HEADS) & ((rows % SSM_HEADS) == (cols // SSM_HEAD_DIM))
    return jnp.asarray(hit, BF16)


def _shift_matrix():
    rows = np.arange((CONV_WIDTH - 1) * CHUNK)[:, None]
    cols = np.arange(2 * CHUNK)[None, :]
    w, t = rows // CHUNK, rows % CHUNK
    return jnp.asarray(cols == CHUNK + t - (CONV_WIDTH - 1) + w, BF16)


def _pad_lanes(v):
    return jnp.pad(v.astype(F32), (0, LANES - v.shape[0]))[None, :]


def kernel(x, p, positions, g_mix, w_in, conv_w, conv_b, dt_bias, a_log, d_skip, g_ssd,
           sinks, w_attn_br, w_ssd_br, w_o, g_ffn, w_gate, w_up, w_down, g_ple,
           w_ple_gate, w_ple_proj, g_final):
    b, s, d = x.shape
    t = b * s
    assert d == D_MODEL and s % CHUNK == 0 and p.shape[0] == 1
    i = 0
    x2 = x.reshape(t, d)
    p2 = p[i].reshape(t, PLE_DIM)
    pos_c = jnp.repeat(positions.reshape(t // POS_PER_ROW, POS_PER_ROW), ROPE_FREQS, axis=1)

    o_z = Q_DIM + 2 * KV_DIM
    o_xbc = o_z + D_INNER
    o_dt = o_xbc + CONV_DIM
    o_ga = o_dt + SSM_HEADS
    wt = jnp.swapaxes(w_in[i], 0, 1)

    invf, sgn = _rope_constants()
    u, cos_t, sin_t, dtp, qkv = _prep(x2, g_mix[i][None, :], pos_c, invf, sgn, wt, o_dt, o_z)

    zxg = _inproj(u, wt, [(o_z, D_INNER + CONV_DIM), (o_ga, 2 * D_MODEL)], BF16,
                  INPROJ_ROW_TILE, INPROJ_COL_TILE, "proj_zxg")
    gate_col0 = D_INNER + CONV_DIM

    attn = _attention(qkv, cos_t, sin_t, sinks[i][None, :].astype(F32), b, s)

    y = _ssd(zxg, dtp, _shift_matrix(), conv_w[i], conv_b[i][None, :], _pad_lanes(dt_bias[i]),
             _pad_lanes(a_log[i]), jnp.repeat(d_skip[i].astype(F32), SSM_HEAD_DIM)[None, :],
             g_ssd[i][None, :], _expand_matrix(), b, s)

    merged = _merge(attn, y, zxg, gate_col0, w_attn_br[i], w_ssd_br[i])
    h1, f = _oproj(merged, x2, w_o[i], g_ffn[i][None, :])
    act = _ffn_up(f, w_gate[i], w_up[i])
    h2 = _ffn_down(act, h1, w_down[i])
    out = _ple(h2, p2, g_ple[i][None, :], g_final[None, :], w_ple_gate[i], w_ple_proj[i])
    return out.reshape(b, s, d)
```

```python
from functools import partial

import numpy as np
import jax
import jax.numpy as jnp
from jax import lax
from jax.experimental import pallas as pl
from jax.experimental.pallas import tpu as pltpu

F32 = jnp.float32
BF16 = jnp.bfloat16

D_MODEL = 2048
HEAD_DIM = 64
ATTN_HEADS = 16
KV_HEADS = 4
GROUP = ATTN_HEADS // KV_HEADS
Q_DIM = ATTN_HEADS * HEAD_DIM
KV_DIM = KV_HEADS * HEAD_DIM
ATTN_BLOCK = 128
ROPE_THETA = 10000.0
D_INNER = 2048
SSM_HEAD_DIM = 64
SSM_HEADS = 32
SSM_GROUPS = 4
HEADS_PER_GROUP = SSM_HEADS // SSM_GROUPS
D_STATE = 128
CONV_WIDTH = 4
CHUNK = 128
CONV_DIM = D_INNER + 2 * SSM_GROUPS * D_STATE
FFN_HIDDEN = 5632
PLE_DIM = 256
NORM_EPS = 1e-6
SSM_NORM_EPS = 1e-5

LANES = 128
SUBLANES = 8
VMEM_LIMIT_BYTES = 60 * 1024 * 1024

NEG_BIG = -1e30


def _cparams(semantics):
    return pltpu.CompilerParams(dimension_semantics=semantics,
                                vmem_limit_bytes=VMEM_LIMIT_BYTES)


def _resident(shape):
    return pl.BlockSpec(shape, lambda *_: (0,) * len(shape),
                        pipeline_mode=pl.Buffered(1))


def _rms(xf, g, eps):
    var = jnp.mean(xf * xf, axis=-1, keepdims=True)
    return xf * lax.rsqrt(var + eps) * g


NEG_LOG2E = -1.4426950408889634


def _sigmoid(x):
    return 1.0 / (1.0 + jnp.exp2(x * NEG_LOG2E))


def _silu(x):
    return x * _sigmoid(x)


def _dot(a, b):
    return jnp.dot(a, b, preferred_element_type=F32)


def _dot_nt(a, b):
    return lax.dot_general(a, b, (((1,), (1,)), ((), ())),
                           preferred_element_type=F32)


def _split3(v):
    v1 = v.astype(BF16)
    r1 = v - v1.astype(F32)
    v2 = r1.astype(BF16)
    v3 = (r1 - v2.astype(F32)).astype(BF16)
    return v1, v2, v3


ROPE_FREQS = HEAD_DIM // 2
POS_PER_ROW = LANES // ROPE_FREQS


def _prep_kernel(x_ref, g_ref, pos_ref, invf_ref, sgn_ref, wdt_ref, wqkv_ref,
                 u_ref, cos_ref, sin_ref, dt_ref, qkv_ref, wqkv16_ref):
    @pl.when(pl.program_id(0) == 0)
    def _():
        _cast_rows(wqkv16_ref, wqkv_ref, Q_DIM)

    u = _rms(x_ref[...], g_ref[...], NORM_EPS).astype(BF16)
    u_ref[...] = u
    qkv_ref[...] = _dot_nt(u, wqkv16_ref[...]).astype(qkv_ref.dtype)
    rows = pos_ref.shape[0]
    ang = pos_ref[...].astype(F32) * invf_ref[...]
    cos_c = jnp.cos(ang)
    sin_c = jnp.sin(ang)
    group = lax.broadcasted_iota(jnp.int32, ang.shape, 1) // ROPE_FREQS
    for q in range(POS_PER_ROW):
        for src, dst_ref, sign in ((cos_c, cos_ref, None), (sin_c, sin_ref, sgn_ref[...])):
            only = jnp.where(group == q, src, 0.0)
            rep = only
            for k in range(1, POS_PER_ROW):
                rep = rep + pltpu.roll(only, k * ROPE_FREQS, 1)
            if sign is not None:
                rep = rep * sign
            dst_ref[pl.ds(q, rows, stride=POS_PER_ROW), :] = rep
    dt_ref[...] = _dot_nt(u, wdt_ref[...].astype(BF16))


def _prep(x2, g_mix, pos_c, invf, sgn, wt, dt_row0, qkv_rows, tm=512):
    t = x2.shape[0]
    row = lambda i: (i, 0)
    assert dt_row0 % SUBLANES == 0 and tm % (POS_PER_ROW * SUBLANES) == 0
    return pl.pallas_call(
        _prep_kernel,
        out_shape=(jax.ShapeDtypeStruct((t, D_MODEL), BF16),
                   jax.ShapeDtypeStruct((t, LANES), F32),
                   jax.ShapeDtypeStruct((t, LANES), F32),
                   jax.ShapeDtypeStruct((t, LANES), F32),
                   jax.ShapeDtypeStruct((t, qkv_rows), BF16)),
        grid=(t // tm,),
        in_specs=[pl.BlockSpec((tm, D_MODEL), row),
                  _resident((1, D_MODEL)),
                  pl.BlockSpec((tm // POS_PER_ROW, LANES), row),
                  _resident((1, LANES)),
                  _resident((1, LANES)),
                  pl.BlockSpec((pl.Element(LANES), pl.Element(D_MODEL)),
                               lambda i: (dt_row0, 0), pipeline_mode=pl.Buffered(1)),
                  pl.BlockSpec((pl.Element(qkv_rows), pl.Element(D_MODEL)),
                               lambda i: (0, 0), pipeline_mode=pl.Buffered(1))],
        out_specs=(pl.BlockSpec((tm, D_MODEL), row),
                   pl.BlockSpec((tm, LANES), row),
                   pl.BlockSpec((tm, LANES), row),
                   pl.BlockSpec((tm, LANES), row),
                   pl.BlockSpec((tm, qkv_rows), row)),
        scratch_shapes=[pltpu.VMEM((qkv_rows, D_MODEL), BF16)],
        compiler_params=_cparams(("arbitrary",)),
        name="prep_qkv",
    )(x2, g_mix, pos_c, invf, sgn, wt, wt)


PAIR_ROWS = 2 * GROUP * HEAD_DIM
INPROJ_ROW_TILE = 2048
INPROJ_COL_TILE = 1024


def _paired_head_blocks():
    return [(g * 2 + sub, sub * GROUP + g) for g in range(GROUP) for sub in range(2)]


def _cast_rows(dst_ref, src_ref, paired_rows):
    rows = dst_ref.shape[0]
    assert paired_rows % PAIR_ROWS == 0 and paired_rows <= rows
    for base in range(0, paired_rows, PAIR_ROWS):
        for dst, src in _paired_head_blocks():
            dst_ref[base + dst * HEAD_DIM:base + (dst + 1) * HEAD_DIM, :] = (
                src_ref[base + src * HEAD_DIM:base + (src + 1) * HEAD_DIM, :].astype(BF16))
    if paired_rows < rows:
        dst_ref[paired_rows:rows, :] = src_ref[paired_rows:rows, :].astype(BF16)


SIDE_CAST_ROWS = 64


def _inproj_kernel(a_ref, wt_ref, *refs, n_side):
    side_in, o_ref, side_out, wbf_ref = (refs[:n_side], refs[n_side],
                                         refs[n_side + 1:2 * n_side + 1], refs[-1])
    @pl.when(pl.program_id(1) == 0)
    def _():
        wbf_ref[...] = wt_ref[...].astype(BF16)

    o_ref[...] = _dot_nt(a_ref[...], wbf_ref[...]).astype(o_ref.dtype)
    for src_ref, dst_ref in zip(side_in, side_out):
        dst_ref[...] = src_ref[...].astype(BF16)


def _inproj(a, wt, segments, out_dtype, tm, tn, name, side_weights=()):
    m, k = a.shape
    assert all(seg_n % tn == 0 and row0 % SUBLANES == 0 for row0, seg_n in segments)
    n = sum(seg_n for _, seg_n in segments)
    n_row_tiles = m // tm
    n_steps = (n // tn) * n_row_tiles
    assert all(w.shape[0] % SIDE_CAST_ROWS == 0 and w.shape[0] // SIDE_CAST_ROWS <= n_steps
               for w in side_weights)

    def side_spec(w):
        last = w.shape[0] // SIDE_CAST_ROWS - 1
        return pl.BlockSpec((SIDE_CAST_ROWS, w.shape[1]),
                            lambda j, i: (jnp.minimum(j * n_row_tiles + i, last), 0))

    def weight_row(j):
        row, first_tile = None, 0
        for row0, seg_n in segments:
            here = row0 + (j - first_tile) * tn
            row = here if row is None else jnp.where(j >= first_tile, here, row)
            first_tile += seg_n // tn
        return pl.multiple_of(row, SUBLANES)

    return pl.pallas_call(
        partial(_inproj_kernel, n_side=len(side_weights)),
        out_shape=(jax.ShapeDtypeStruct((m, n), out_dtype),
                   *[jax.ShapeDtypeStruct(w.shape, BF16) for w in side_weights]),
        grid=(n // tn, m // tm),
        in_specs=[pl.BlockSpec((tm, k), lambda j, i: (i, 0)),
                  pl.BlockSpec((pl.Element(tn), pl.Element(k)), lambda j, i: (weight_row(j), 0)),
                  *[side_spec(w) for w in side_weights]],
        out_specs=(pl.BlockSpec((tm, tn), lambda j, i: (i, j)),
                   *[side_spec(w) for w in side_weights]),
        scratch_shapes=[pltpu.VMEM((tn, k), BF16)],
        compiler_params=_cparams(("arbitrary", "arbitrary")),
        name=name,
    )(a, wt, *side_weights)


ATTN_STEP_BLOCKS = 2


def _attn_kernel(sinks_ref, q_ref, kc_ref, kp_ref, vc_ref, vp_ref,
                 cc_ref, sc_ref, cp_ref, sp_ref, o_ref):
    n = pl.program_id(1)
    blk = ATTN_BLOCK
    nblk = ATTN_STEP_BLOCKS
    lane = lax.broadcasted_iota(jnp.int32, (blk, LANES), 1)
    first_half = (lane % HEAD_DIM) < (HEAD_DIM // 2)
    low_head = lane < HEAD_DIM

    def rope(t, c, s):
        partner = jnp.where(first_half,
                            pltpu.roll(t, LANES - HEAD_DIM // 2, 1),
                            pltpu.roll(t, HEAD_DIM // 2, 1))
        return t * c + partner * s

    def rows(j):
        return slice(j * blk, (j + 1) * blk)

    cp, sp = cp_ref[...], sp_ref[...]
    cc = [cc_ref[rows(j), :] for j in range(nblk)]
    sc = [sc_ref[rows(j), :] for j in range(nblk)]

    qi = lax.broadcasted_iota(jnp.int32, (blk, 2 * blk), 0)
    kj = lax.broadcasted_iota(jnp.int32, (blk, 2 * blk), 1)
    rel = kj - qi
    band = (rel >= 1) & (rel <= blk)
    valid = [band & ((kj >= blk) | (n > 0))] + [band] * (nblk - 1)

    scale = HEAD_DIM ** -0.5
    n_half = KV_DIM // LANES
    rows_per_half = 2 * GROUP * blk
    s_rows, sink_rows, valid_rows, v_wins = [], [], [], []
    for a in range(n_half):
        ksl = slice(a * LANES, (a + 1) * LANES)
        k_blocks = [rope(kp_ref[:, ksl].astype(F32), cp, sp).astype(BF16)]
        v_blocks = [vp_ref[:, ksl]]
        for j in range(nblk):
            k_blocks.append(rope(kc_ref[rows(j), ksl].astype(F32), cc[j], sc[j]).astype(BF16))
            v_blocks.append(vc_ref[rows(j), ksl])
        for j in range(nblk):
            k_win = jnp.concatenate(k_blocks[j:j + 2], axis=0)
            v_wins.append(jnp.concatenate(v_blocks[j:j + 2], axis=0))
            qc = [rope(q_ref[rows(j), (a * GROUP + g) * LANES:(a * GROUP + g + 1) * LANES]
                       .astype(F32), cc[j] * scale, sc[j] * scale) for g in range(GROUP)]
            q_rows = []
            for sub in range(2):
                keep = low_head if sub == 0 else jnp.logical_not(low_head)
                for g in range(GROUP):
                    q_rows.append(jnp.where(keep, qc[g], 0.0).astype(BF16))
                    sink_rows.append(
                        jnp.full((blk, LANES), sinks_ref[0, (2 * a + sub) * GROUP + g], F32))
            s_rows.append(_dot_nt(jnp.concatenate(q_rows, axis=0), k_win))
            valid_rows.extend([valid[j]] * (2 * GROUP))
    s = jnp.where(jnp.concatenate(valid_rows, axis=0), jnp.concatenate(s_rows, axis=0), NEG_BIG)
    sink = jnp.concatenate(sink_rows, axis=0)
    s0, s1 = s[:, :LANES], s[:, LANES:]
    m = jnp.maximum(jnp.max(jnp.maximum(s0, s1), axis=-1, keepdims=True), sink)
    e0 = jnp.exp(s0 - m)
    e1 = jnp.exp(s1 - m)
    den = jnp.sum(e0 + e1, axis=-1, keepdims=True) + jnp.exp(sink - m)
    r = 1.0 / den
    p = jnp.concatenate([e0 * r, e1 * r], axis=1).astype(BF16)
    for a in range(n_half):
        for j in range(nblk):
            w = a * nblk + j
            pv = _dot(p[w * rows_per_half:(w + 1) * rows_per_half], v_wins[w])
            for g in range(GROUP):
                lo = pv[g * blk:(g + 1) * blk]
                hi = pv[(GROUP + g) * blk:(GROUP + g + 1) * blk]
                chunk = a * GROUP + g
                o_ref[rows(j), chunk * LANES:(chunk + 1) * LANES] = (
                    jnp.where(low_head, lo, hi).astype(o_ref.dtype))


def _attention(qkv, cos_t, sin_t, sinks, batch, seq):
    t = qkv.shape[0]
    step = ATTN_STEP_BLOCKS * ATTN_BLOCK
    assert seq % step == 0
    ns = seq // step
    kcol = Q_DIM // KV_DIM
    vcol = kcol + 1
    cur = lambda b, n: b * ns + n
    prev = lambda b, n: jnp.maximum((b * ns + n) * ATTN_STEP_BLOCKS - 1, 0)
    return pl.pallas_call(
        _attn_kernel,
        out_shape=jax.ShapeDtypeStruct((t, Q_DIM), BF16),
        grid=(batch, ns),
        in_specs=[pl.BlockSpec(memory_space=pltpu.SMEM),
                  pl.BlockSpec((step, Q_DIM), lambda b, n: (cur(b, n), 0)),
                  pl.BlockSpec((step, KV_DIM), lambda b, n: (cur(b, n), kcol)),
                  pl.BlockSpec((ATTN_BLOCK, KV_DIM), lambda b, n: (prev(b, n), kcol)),
                  pl.BlockSpec((step, KV_DIM), lambda b, n: (cur(b, n), vcol)),
                  pl.BlockSpec((ATTN_BLOCK, KV_DIM), lambda b, n: (prev(b, n), vcol)),
                  pl.BlockSpec((step, LANES), lambda b, n: (cur(b, n), 0)),
                  pl.BlockSpec((step, LANES), lambda b, n: (cur(b, n), 0)),
                  pl.BlockSpec((ATTN_BLOCK, LANES), lambda b, n: (prev(b, n), 0)),
                  pl.BlockSpec((ATTN_BLOCK, LANES), lambda b, n: (prev(b, n), 0))],
        out_specs=pl.BlockSpec((step, Q_DIM), lambda b, n: (cur(b, n), 0)),
        compiler_params=_cparams(("arbitrary", "arbitrary")),
        name="attention",
    )(sinks, qkv, qkv, qkv, qkv, qkv, cos_t, sin_t, cos_t, sin_t)


SSD_STRIP = 256


def _ssd_kernel(z_ref, xin_ref, bcin_ref, xin_prev_ref, bcin_prev_ref, dt_ref, shift_ref, cw_ref,
                cb_ref, dtb_ref, alog_ref, dskip_ref, g_ref, expand_ref, o_ref,
                state_ref, xs_ref, b16_ref, c16_ref, xdd_ref, xd16_ref, yz_ref):
    c = pl.program_id(1)
    L = CHUNK
    W = SSD_STRIP
    bc_dim = SSM_GROUPS * D_STATE

    @pl.when(c == 0)
    def _():
        state_ref[...] = jnp.zeros_like(state_ref)

    v = dt_ref[...] + dtb_ref[...]
    dt = jnp.maximum(v, 0.0) + jnp.log1p(jnp.exp(-jnp.abs(v)))
    a = dt * (jnp.exp(alog_ref[...]) * NEG_LOG2E)

    ri = lax.broadcasted_iota(jnp.int32, (L, L), 0)
    ci = lax.broadcasted_iota(jnp.int32, (L, L), 1)
    causal = ri >= ci
    tril = jnp.where(causal, 1.0, 0.0).astype(BF16)
    R = 2 * SUBLANES
    ones = jnp.ones((L, L), BF16)
    a1, a2, a3 = _split3(a)
    cs = _dot(tril, a1) + _dot(tril, a2) + _dot(tril, a3)
    tot = (_dot(ones, a1) + _dot(ones, a2) + _dot(ones, a3))[0:R]
    cs_t = cs.T[0:SSM_HEADS]

    lane = lax.broadcasted_iota(jnp.int32, (L, LANES), 1)

    def pack(q):
        ln = lax.broadcasted_iota(jnp.int32, q.shape, 1)
        q1, q2, q3 = _split3(q)
        packed = jnp.where(
            ln < SSM_HEADS, q1.astype(F32),
            jnp.where(ln < 2 * SSM_HEADS, pltpu.roll(q2.astype(F32), SSM_HEADS, 1),
                      jnp.where(ln < 3 * SSM_HEADS,
                                pltpu.roll(q3.astype(F32), 2 * SSM_HEADS, 1), 0.0)))
        return packed.astype(BF16)

    def tile_rows(v):
        return jnp.concatenate([v] * (L // SUBLANES), axis=0)

    pk_all = jnp.concatenate([pack(dt), pack(cs), pack(tot)], axis=0)
    pk_ct = pk_all[L:]

    n_shift = CONV_WIDTH - 1
    sel_col = lax.broadcasted_iota(jnp.int32, (n_shift * L, 2 * L), 1)
    sel = shift_ref[...]
    sel = jnp.where((sel_col < L) & (c == 0), jnp.zeros_like(sel), sel)
    def conv_strip(s):
        sl = slice(s * W, (s + 1) * W)
        if s * W < D_INNER:
            cur_ref, prev_ref, src = xin_ref, xin_prev_ref, sl
        else:
            cur_ref, prev_ref = bcin_ref, bcin_prev_ref
            src = slice(s * W - D_INNER, (s + 1) * W - D_INNER)
        x_cur = cur_ref[:, src]
        xwin = jnp.concatenate([prev_ref[:, src], x_cur], axis=0)
        taps = _dot(sel, xwin)
        acc = cb_ref[:, sl] + x_cur.astype(F32) * cw_ref[n_shift:CONV_WIDTH, sl]
        for w in range(n_shift):
            acc = acc + taps[w * L:(w + 1) * L] * cw_ref[w:w + 1, sl]
        xc = _silu(acc)
        if s * W < D_INNER:
            xs_ref[:, sl] = xc
            ex = _dot(pk_all, expand_ref[:, sl])
            xd = xc * ex[0:L]
            xd16_ref[:, sl] = xd.astype(BF16)
            tot_x = tile_rows(ex[2 * L:2 * L + SUBLANES])
            xdd_ref[:, sl] = (xd * jnp.exp2(tot_x - ex[L:2 * L])).astype(BF16)
        elif s * W < D_INNER + bc_dim:
            b16_ref[:, s * W - D_INNER:(s + 1) * W - D_INNER] = xc.astype(BF16)
        else:
            c16_ref[:, s * W - D_INNER - bc_dim:(s + 1) * W - D_INNER - bc_dim] = xc.astype(BF16)

    low_head = lane < SSM_HEAD_DIM
    heads_per_strip = W // SSM_HEAD_DIM
    strips_per_group = HEADS_PER_GROUP // heads_per_strip
    ssq = jnp.zeros((L, LANES), F32)

    for s in range(D_INNER // W, CONV_DIM // W):
        conv_strip(s)

    for g in range(SSM_GROUPS):
        for k in range(strips_per_group):
            conv_strip(g * strips_per_group + k)
        nsl = slice(g * D_STATE, (g + 1) * D_STATE)
        bg = b16_ref[:, nsl]
        cg = c16_ref[:, nsl]
        bg_t = bg.astype(F32).T.astype(BF16)
        cb = jnp.where(causal, _dot_nt(cg, bg), 0.0)
        for k in range(strips_per_group):
            s = g * strips_per_group + k
            sl = slice(s * W, (s + 1) * W)
            ex = _dot(pk_ct, expand_ref[:, sl])
            st_prev = state_ref[:, sl]
            y_off = _dot(cg, st_prev.astype(BF16))
            chunk_decay = tile_rows(jnp.exp2(ex[L:L + SUBLANES]))
            state_ref[:, sl] = st_prev * chunk_decay + _dot(bg_t, xdd_ref[:, sl])
            y_pairs = []
            for pair in range(W // LANES):
                psl = slice(s * W + pair * LANES, s * W + (pair + 1) * LANES)
                xp = xd16_ref[:, psl]
                mhs, xps = [], []
                for sub in range(2):
                    h = s * heads_per_strip + pair * 2 + sub
                    diff = cs[:, h:h + 1] - cs_t[h:h + 1, :]
                    decay = jnp.exp2(jnp.minimum(diff, 0.0))
                    mhs.append((cb * decay).astype(BF16))
                    keep = low_head if sub == 0 else jnp.logical_not(low_head)
                    xps.append(jnp.where(keep, xp, jnp.zeros_like(xp)))
                y_pairs.append(_dot(jnp.concatenate(mhs, axis=1), jnp.concatenate(xps, axis=0)))
            y = (jnp.concatenate(y_pairs, axis=1) + y_off * jnp.exp2(ex[0:L])
                 + dskip_ref[:, sl] * xs_ref[:, sl])
            yz = y * _silu(z_ref[:, sl].astype(F32))
            yz_ref[:, sl] = yz
            for pair in range(W // LANES):
                part = yz[:, pair * LANES:(pair + 1) * LANES]
                ssq = ssq + part * part

    var = jnp.sum(ssq, axis=-1, keepdims=True) * (1.0 / D_INNER)
    rs = lax.rsqrt(var + SSM_NORM_EPS)
    for s in range(D_INNER // W):
        sl = slice(s * W, (s + 1) * W)
        o_ref[:, sl] = (yz_ref[:, sl] * rs * g_ref[:, sl]).astype(o_ref.dtype)


def _ssd(zx, dtp, shift, conv_w, conv_b, dtb_p, alog_p, dskip_x, g_ssd, expand, batch, seq):
    t = zx.shape[0]
    nc = seq // CHUNK
    bc_dim = 2 * SSM_GROUPS * D_STATE
    row = lambda b, c: (b * nc + c, 0)
    cur = lambda b, c: b * nc + c
    prev = lambda b, c: jnp.maximum(b * nc + c - 1, 0)
    x_col = 1
    bc_col = 2 * D_INNER // bc_dim
    return pl.pallas_call(
        _ssd_kernel,
        out_shape=jax.ShapeDtypeStruct((t, D_INNER), BF16),
        grid=(batch, nc),
        in_specs=[pl.BlockSpec((CHUNK, D_INNER), row),
                  pl.BlockSpec((CHUNK, D_INNER), lambda b, c: (cur(b, c), x_col)),
                  pl.BlockSpec((CHUNK, bc_dim), lambda b, c: (cur(b, c), bc_col)),
                  pl.BlockSpec((CHUNK, D_INNER), lambda b, c: (prev(b, c), x_col)),
                  pl.BlockSpec((CHUNK, bc_dim), lambda b, c: (prev(b, c), bc_col)),
                  pl.BlockSpec((CHUNK, LANES), row),
                  _resident(((CONV_WIDTH - 1) * CHUNK, 2 * CHUNK)),
                  _resident((CONV_WIDTH, CONV_DIM)),
                  _resident((1, CONV_DIM)),
                  _resident((1, LANES)),
                  _resident((1, LANES)),
                  _resident((1, D_INNER)),
                  _resident((1, D_INNER)),
                  _resident((LANES, D_INNER))],
        out_specs=pl.BlockSpec((CHUNK, D_INNER), row),
        scratch_shapes=[pltpu.VMEM((D_STATE, D_INNER), F32),
                        pltpu.VMEM((CHUNK, D_INNER), F32),
                        pltpu.VMEM((CHUNK, SSM_GROUPS * D_STATE), BF16),
                        pltpu.VMEM((CHUNK, SSM_GROUPS * D_STATE), BF16),
                        pltpu.VMEM((CHUNK, D_INNER), BF16),
                        pltpu.VMEM((CHUNK, D_INNER), BF16),
                        pltpu.VMEM((CHUNK, D_INNER), F32)],
        compiler_params=_cparams(("arbitrary", "arbitrary")),
        name="ssd",
    )(zx, zx, zx, zx, zx, dtp, shift, conv_w, conv_b, dtb_p, alog_p, dskip_x, g_ssd, expand)


def _merge_kernel(attn_ref, y_ref, ga_ref, gs_ref, wa_ref, ws_ref, o_ref, wa16_ref, ws16_ref):
    @pl.when(pl.program_id(1) == 0)
    def _():
        _cast_rows(wa16_ref, wa_ref, Q_DIM)
        ws16_ref[...] = ws_ref[...].astype(BF16)

    out_a = _dot(attn_ref[...], wa16_ref[...])
    out_s = _dot(y_ref[...], ws16_ref[...])
    merged = (_sigmoid(ga_ref[...].astype(F32)) * out_a
              + _sigmoid(gs_ref[...].astype(F32)) * out_s)
    o_ref[...] = merged.astype(o_ref.dtype)


def _merge(attn, y, gates, gate_col0, wa, ws, tm=512, tn=1024):
    t = attn.shape[0]
    nj = D_MODEL // tn
    assert gate_col0 % tn == 0
    g0 = gate_col0 // tn
    return pl.pallas_call(
        _merge_kernel,
        out_shape=jax.ShapeDtypeStruct((t, D_MODEL), BF16),
        grid=(nj, t // tm),
        in_specs=[pl.BlockSpec((tm, Q_DIM), lambda j, i: (i, 0)),
                  pl.BlockSpec((tm, D_INNER), lambda j, i: (i, 0)),
                  pl.BlockSpec((tm, tn), lambda j, i: (i, g0 + j)),
                  pl.BlockSpec((tm, tn), lambda j, i: (i, g0 + nj + j)),
                  pl.BlockSpec((Q_DIM, tn), lambda j, i: (0, j)),
                  pl.BlockSpec((D_INNER, tn), lambda j, i: (0, j))],
        out_specs=pl.BlockSpec((tm, tn), lambda j, i: (i, j)),
        scratch_shapes=[pltpu.VMEM((Q_DIM, tn), BF16),
                        pltpu.VMEM((D_INNER, tn), BF16)],
        compiler_params=_cparams(("arbitrary", "arbitrary")),
        name="merge",
    )(attn, y, gates, gates, wa, ws)


def _oproj_kernel(m_ref, x_ref, w16_ref, g_ref, h_ref, f_ref):
    h = x_ref[...] + _dot(m_ref[...], w16_ref[...])
    h_ref[...] = h
    f_ref[...] = _rms(h, g_ref[...], NORM_EPS).astype(BF16)


def _oproj(merged, x2, wo, g_ffn, tm=512):
    t = merged.shape[0]
    row = lambda i: (i, 0)
    return pl.pallas_call(
        _oproj_kernel,
        out_shape=(jax.ShapeDtypeStruct((t, D_MODEL), F32),
                   jax.ShapeDtypeStruct((t, D_MODEL), BF16)),
        grid=(t // tm,),
        in_specs=[pl.BlockSpec((tm, D_MODEL), row),
                  pl.BlockSpec((tm, D_MODEL), row),
                  _resident((D_MODEL, D_MODEL)),
                  _resident((1, D_MODEL))],
        out_specs=(pl.BlockSpec((tm, D_MODEL), row),
                   pl.BlockSpec((tm, D_MODEL), row)),
        compiler_params=_cparams(("arbitrary",)),
        name="oproj",
    )(merged, x2, wo, g_ffn)


def _ffn_up_kernel(f_ref, wg_ref, wu_ref, o_ref, wg16_ref, wu16_ref):
    @pl.when(pl.program_id(1) == 0)
    def _():
        wg16_ref[...] = wg_ref[...].astype(BF16)
        wu16_ref[...] = wu_ref[...].astype(BF16)

    half_r = o_ref.shape[0] // 2
    half_c = o_ref.shape[1] // 2
    for r in range(2):
        rs = slice(r * half_r, (r + 1) * half_r)
        f = f_ref[rs, :]
        for c in range(2):
            cs = slice(c * half_c, (c + 1) * half_c)
            o_ref[rs, cs] = (_silu(_dot(f, wg16_ref[:, cs]))
                             * _dot(f, wu16_ref[:, cs])).astype(o_ref.dtype)


def _ffn_up(f, wg, wu, tm=2048, th=512):
    t = f.shape[0]
    return pl.pallas_call(
        _ffn_up_kernel,
        out_shape=jax.ShapeDtypeStruct((t, FFN_HIDDEN), BF16),
        grid=(FFN_HIDDEN // th, t // tm),
        in_specs=[pl.BlockSpec((tm, D_MODEL), lambda j, i: (i, 0)),
                  pl.BlockSpec((D_MODEL, th), lambda j, i: (0, j)),
                  pl.BlockSpec((D_MODEL, th), lambda j, i: (0, j))],
        out_specs=pl.BlockSpec((tm, th), lambda j, i: (i, j)),
        scratch_shapes=[pltpu.VMEM((D_MODEL, th), BF16),
                        pltpu.VMEM((D_MODEL, th), BF16)],
        compiler_params=_cparams(("arbitrary", "arbitrary")),
        name="ffn_up",
    )(f, wg, wu)


def _ffn_down_kernel(a_ref, h_ref, wd_hbm, o_ref, wd16_ref, wf32_ref, sem, *, n_col_tiles):
    j = pl.program_id(0)
    i = pl.program_id(1)
    tn = wf32_ref.shape[1]

    def copy(col_tile):
        return pltpu.make_async_copy(wd_hbm.at[:, pl.ds(col_tile * tn, tn)], wf32_ref, sem)

    @pl.when((j == 0) & (i == 0))
    def _():
        copy(0).start()

    @pl.when(i == 0)
    def _():
        copy(j).wait()
        wd16_ref[...] = wf32_ref[...].astype(BF16)

    @pl.when((i == 0) & (j + 1 < n_col_tiles))
    def _():
        copy(j + 1).start()

    o_ref[...] = h_ref[...] + _dot(a_ref[...], wd16_ref[...])


def _ffn_down(act, h1, wd, tm=512, tn=1024):
    t = act.shape[0]
    n_col_tiles = D_MODEL // tn
    return pl.pallas_call(
        partial(_ffn_down_kernel, n_col_tiles=n_col_tiles),
        out_shape=jax.ShapeDtypeStruct((t, D_MODEL), F32),
        grid=(n_col_tiles, t // tm),
        in_specs=[pl.BlockSpec((tm, FFN_HIDDEN), lambda j, i: (i, 0)),
                  pl.BlockSpec((tm, tn), lambda j, i: (i, j)),
                  pl.BlockSpec(memory_space=pl.ANY)],
        out_specs=pl.BlockSpec((tm, tn), lambda j, i: (i, j)),
        scratch_shapes=[pltpu.VMEM((FFN_HIDDEN, tn), BF16),
                        pltpu.VMEM((FFN_HIDDEN, tn), F32),
                        pltpu.SemaphoreType.DMA],
        compiler_params=_cparams(("arbitrary", "arbitrary")),
        name="ffn_down",
    )(act, h1, wd)


def _ple_kernel(h_ref, p_ref, gp_ref, gf_ref, wg16_ref, wp_ref, o_ref, wp16_ref):
    @pl.when(pl.program_id(0) == 0)
    def _():
        wp16_ref[...] = wp_ref[...].astype(BF16)

    h = h_ref[...]
    r = _rms(h, gp_ref[...], NORM_EPS).astype(BF16)
    gate = _sigmoid(_dot(r, wg16_ref[...]))
    h3 = h + gate * _dot(p_ref[...].astype(BF16), wp16_ref[...])
    o_ref[...] = _rms(h3, gf_ref[...], NORM_EPS)


def _ple(h2, p2, g_ple, g_final, wpg, wpp, tm=512):
    t = h2.shape[0]
    row = lambda i: (i, 0)
    return pl.pallas_call(
        _ple_kernel,
        out_shape=jax.ShapeDtypeStruct((t, D_MODEL), F32),
        grid=(t // tm,),
        in_specs=[pl.BlockSpec((tm, D_MODEL), row),
                  pl.BlockSpec((tm, PLE_DIM), row),
                  _resident((1, D_MODEL)),
                  _resident((1, D_MODEL)),
                  _resident((D_MODEL, D_MODEL)),
                  _resident((PLE_DIM, D_MODEL))],
        out_specs=pl.BlockSpec((tm, D_MODEL), row),
        scratch_shapes=[pltpu.VMEM((PLE_DIM, D_MODEL), BF16)],
        compiler_params=_cparams(("arbitrary",)),
        name="ple",
    )(h2, p2, g_ple, g_final, wpg, wpp)


def _rope_constants():
    half = HEAD_DIM // 2
    lane = np.arange(LANES)
    inv_freq = ROPE_THETA ** (-jnp.arange(half, dtype=F32) * 2.0 / HEAD_DIM)
    invf = inv_freq[lane % half][None, :]
    sgn = jnp.asarray(np.where((lane % HEAD_DIM) < half, -1.0, 1.0), F32)[None, :]
    return invf, sgn


def _expand_matrix():
    rows = np.arange(LANES)[:, None]
    cols = np.arange(D_INNER)[None, :]
    hit = (rows < 3 * SSM_HEADS) & ((rows % SSM_HEADS) == (cols // SSM_HEAD_DIM))
    return jnp.asarray(hit, BF16)


def _shift_matrix():
    rows = np.arange((CONV_WIDTH - 1) * CHUNK)[:, None]
    cols = np.arange(2 * CHUNK)[None, :]
    w, t = rows // CHUNK, rows % CHUNK
    return jnp.asarray(cols == CHUNK + t - (CONV_WIDTH - 1) + w, BF16)


def _pad_lanes(v):
    return jnp.pad(v.astype(F32), (0, LANES - v.shape[0]))[None, :]


def kernel(x, p, positions, g_mix, w_in, conv_w, conv_b, dt_bias, a_log, d_skip, g_ssd,
           sinks, w_attn_br, w_ssd_br, w_o, g_ffn, w_gate, w_up, w_down, g_ple,
           w_ple_gate, w_ple_proj, g_final):
    b, s, d = x.shape
    t = b * s
    assert d == D_MODEL and s % CHUNK == 0 and p.shape[0] == 1
    i = 0
    x2 = x.reshape(t, d)
    p2 = p[i].reshape(t, PLE_DIM)
    pos_c = jnp.repeat(positions.reshape(t // POS_PER_ROW, POS_PER_ROW), ROPE_FREQS, axis=1)

    o_z = Q_DIM + 2 * KV_DIM
    o_xbc = o_z + D_INNER
    o_dt = o_xbc + CONV_DIM
    o_ga = o_dt + SSM_HEADS
    wt = jnp.swapaxes(w_in[i], 0, 1)

    invf, sgn = _rope_constants()
    u, cos_t, sin_t, dtp, qkv = _prep(x2, g_mix[i][None, :], pos_c, invf, sgn, wt, o_dt, o_z)

    zxg, wo16, wpg16 = _inproj(u, wt, [(o_z, D_INNER + CONV_DIM), (o_ga, 2 * D_MODEL)], BF16,
                               INPROJ_ROW_TILE, INPROJ_COL_TILE, "proj_zxg",
                               side_weights=(w_o[i], w_ple_gate[i]))
    gate_col0 = D_INNER + CONV_DIM

    attn = _attention(qkv, cos_t, sin_t, sinks[i][None, :].astype(F32), b, s)

    y = _ssd(zxg, dtp, _shift_matrix(), conv_w[i], conv_b[i][None, :], _pad_lanes(dt_bias[i]),
             _pad_lanes(a_log[i]), jnp.repeat(d_skip[i].astype(F32), SSM_HEAD_DIM)[None, :],
             g_ssd[i][None, :], _expand_matrix(), b, s)

    merged = _merge(attn, y, zxg, gate_col0, w_attn_br[i], w_ssd_br[i])
    h1, f = _oproj(merged, x2, wo16, g_ffn[i][None, :])
    act = _ffn_up(f, w_gate[i], w_up[i])
    h2 = _ffn_down(act, h1, w_down[i])
    out = _ple(h2, p2, g_ple[i][None, :], g_final[None, :], wpg16, w_ple_proj[i])
    return out.reshape(b, s, d)
```
